```python
import jax, jax.numpy as jnp
from jax import lax
import numpy as np


D_MODEL = 1024
BATCH = 1
SEQ = 16384
DEPTH = 1
DEC_BATCH = 128
DEC_SEQ = 4
PAST_LEN = 16384
PAGE_SIZE = 128

MIX_WIDTH = D_MODEL
GLA_WIDTH = MIX_WIDTH // 2
GLA_HEADS = 4
GLA_HEAD_V = GLA_WIDTH // GLA_HEADS
GLA_HEAD_K = GLA_HEAD_V // 2
GLA_KEY_WIDTH = GLA_HEADS * GLA_HEAD_K
GLA_GATE_RANK = 16
GLA_GATE_TAU = 16.0
GLA_CHUNK = 64
SWA_WIDTH = MIX_WIDTH - GLA_WIDTH
SWA_HEAD_DIM = 64
SWA_Q_HEADS = SWA_WIDTH // SWA_HEAD_DIM
SWA_KV_HEADS = 2
SWA_GROUP = SWA_Q_HEADS // SWA_KV_HEADS
SWA_KV_WIDTH = SWA_KV_HEADS * SWA_HEAD_DIM
WINDOW = 128
D_IN = 2 * GLA_KEY_WIDTH + 2 * GLA_WIDTH + GLA_GATE_RANK + SWA_WIDTH + 2 * SWA_KV_WIDTH
D_FF = 2816
NORM_EPS = 1e-6

kernel_name = 'hymba_gla_swa_macaron_step'


def _rms_norm(x, g):
    xf = x.astype(jnp.float32)
    y = xf * lax.rsqrt(jnp.mean(xf * xf, axis=-1, keepdims=True) + NORM_EPS)
    return (y * g.astype(jnp.float32)).astype(x.dtype)


def _swiglu(h, w_gate, w_up, w_down):
    return (jax.nn.silu(h @ w_gate) * (h @ w_up)) @ w_down


def _split_proj(proj):
    widths = (GLA_KEY_WIDTH, GLA_KEY_WIDTH, GLA_WIDTH, GLA_WIDTH, GLA_GATE_RANK,
              SWA_WIDTH, SWA_KV_WIDTH, SWA_KV_WIDTH)
    offsets = []
    acc = 0
    for w in widths[:-1]:
        acc += w
        offsets.append(acc)
    return jnp.split(proj, offsets, axis=-1)


def _gla_chunked(q, k, v, log_a, s0):
    b, t, h, dk = q.shape
    dv = v.shape[-1]
    c = min(GLA_CHUNK, t)
    n = t // c

    def to_chunks(a):
        return jnp.moveaxis(a.reshape(b, n, c, h, a.shape[-1]), 1, 0)

    causal = jnp.tril(jnp.ones((c, c), dtype=bool))[None, :, :, None, None]

    def step(s, inp):
        qc, kc, vc, gc = inp
        cum = jnp.cumsum(gc, axis=1)
        diff = cum[:, :, None] - cum[:, None, :]
        decay = jnp.exp(jnp.where(causal, diff, -jnp.inf))
        attn = jnp.einsum('bihd,bjhd,bijhd->bhij', qc, kc, decay)
        o = (jnp.einsum('bhij,bjhv->bihv', attn, vc)
             + jnp.einsum('bihd,bhdv->bihv', qc * jnp.exp(cum), s))
        last = cum[:, -1]
        s = (s * jnp.exp(last)[..., None]
             + jnp.einsum('bjhd,bjhv->bhdv', kc * jnp.exp(last[:, None] - cum), vc))
        return s, o

    s, o = lax.scan(step, s0, (to_chunks(q), to_chunks(k), to_chunks(v), to_chunks(log_a)))
    return jnp.moveaxis(o, 0, 1).reshape(b, t, h, dv), s


def _sink_softmax(scores, mask, sinks):
    s = jnp.where(mask, scores, -jnp.inf)
    sink = sinks[:, :, None, None]
    m = jnp.maximum(jnp.max(s, axis=-1, keepdims=True), sink)
    p = jnp.exp(s - m)
    denom = jnp.sum(p, axis=-1, keepdims=True) + jnp.exp(sink - m)
    return p / denom


def _swa_prompt(q, k, v, sinks):
    b, s, _, hd = q.shape
    w = WINDOW
    nb = s // w
    qb = q.reshape(b, nb, w, SWA_KV_HEADS, SWA_GROUP, hd)
    pad = ((0, 0), (1, 0), (0, 0), (0, 0), (0, 0))
    kb = k.reshape(b, nb, w, SWA_KV_HEADS, hd)
    vb = v.reshape(b, nb, w, SWA_KV_HEADS, hd)
    kk = jnp.concatenate([jnp.pad(kb[:, :-1], pad), kb], axis=2)
    vv = jnp.concatenate([jnp.pad(vb[:, :-1], pad), vb], axis=2)
    scores = jnp.einsum('bnqhgd,bnkhd->bnhgqk', qb, kk,
                        preferred_element_type=jnp.float32) * (hd ** -0.5)
    qi = jnp.arange(w)[:, None] + w
    kj = jnp.arange(2 * w)[None, :]
    band = (kj <= qi) & (qi - kj < w)
    first = (jnp.arange(nb)[:, None, None] > 0) | (kj[None] >= w)
    mask = (band[None] & first)[None, :, None, None]
    p = _sink_softmax(scores, mask, sinks)
    o = jnp.einsum('bnhgqk,bnkhd->bnqhgd', p.astype(v.dtype), vv).reshape(b, s, SWA_WIDTH)
    cw = min(WINDOW, s)
    return o, k[:, s - cw:], v[:, s - cw:]


def _swa_sample(q, k, v, k_cache, v_cache, sinks):
    b, t, _, hd = q.shape
    cw = k_cache.shape[1]
    kk = jnp.concatenate([k_cache, k], axis=1)
    vv = jnp.concatenate([v_cache, v], axis=1)
    qg = q.reshape(b, t, SWA_KV_HEADS, SWA_GROUP, hd)
    scores = jnp.einsum('bqhgd,bkhd->bhgqk', qg, kk,
                        preferred_element_type=jnp.float32) * (hd ** -0.5)
    rel = jnp.arange(cw + t)[None, :] - cw
    qi = jnp.arange(t)[:, None]
    mask = ((rel <= qi) & (qi - rel < WINDOW))[None, None, None]
    p = _sink_softmax(scores, mask, sinks)
    o = jnp.einsum('bhgqk,bkhd->bqhgd', p.astype(v.dtype), vv).reshape(b, t, SWA_WIDTH)
    return o, kk[:, -cw:], vv[:, -cw:]


def _layer(x, gla_s0, k_cache, v_cache, p):
    b, t, _ = x.shape
    f32 = jnp.float32
    x = x + 0.5 * _swiglu(_rms_norm(x, p['ffn1_norm']), p['ffn1_w_gate'], p['ffn1_w_up'], p['ffn1_w_down'])
    h = _rms_norm(x, p['mix_norm'])
    q_g, k_g, v_g, r_g, a_lr, q_s, k_s, v_s = _split_proj(h @ p['w_in'])

    def heads(a, n):
        return a.reshape(b, t, n, -1)

    log_a = jax.nn.log_sigmoid((a_lr @ p['w_gate_up'] + p['b_gate']).astype(f32)) / GLA_GATE_TAU
    o_g, s_new = _gla_chunked(heads(q_g.astype(f32) * GLA_HEAD_K ** -0.5, GLA_HEADS),
                              heads(k_g.astype(f32), GLA_HEADS),
                              heads(v_g.astype(f32), GLA_HEADS),
                              heads(log_a, GLA_HEADS),
                              gla_s0.astype(f32))
    o_g = _rms_norm(o_g.astype(x.dtype), p['gla_head_norm'].reshape(GLA_HEADS, GLA_HEAD_V))
    o_g = o_g.reshape(b, t, GLA_WIDTH) * jax.nn.silu(r_g)
    sinks = p['swa_sinks'].astype(f32).reshape(SWA_KV_HEADS, SWA_GROUP)
    q_s = heads(q_s, SWA_Q_HEADS)
    k_s = heads(k_s, SWA_KV_HEADS)
    v_s = heads(v_s, SWA_KV_HEADS)
    if k_cache is None:
        o_s, k_buf, v_buf = _swa_prompt(q_s, k_s, v_s, sinks)
    else:
        o_s, k_buf, v_buf = _swa_sample(q_s, k_s, v_s, k_cache, v_cache, sinks)
    o_s = _rms_norm(o_s, p['swa_out_norm'])
    x = x + jnp.concatenate([o_g, o_s], axis=-1) @ p['w_out']
    x = x + 0.5 * _swiglu(_rms_norm(x, p['ffn2_norm']), p['ffn2_w_gate'], p['ffn2_w_up'], p['ffn2_w_down'])
    return x, s_new.astype(x.dtype), k_buf, v_buf


def setup_inputs(seed: int = 0) -> dict:
    key = jax.random.key(seed)
    ks = jax.random.split(key, 24)
    f32 = jnp.float32

    def nrm(k, shape, scale):
        return scale * jax.random.normal(k, shape, f32)

    def gain(k, n):
        return 1.0 + 0.02 * jax.random.normal(k, (DEPTH, n), f32)

    cache_w = min(WINDOW, PAST_LEN)
    return {
        'x_prompt': nrm(ks[0], (BATCH, SEQ, D_MODEL), 1.0),
        'x_sample': nrm(ks[1], (DEC_BATCH, DEC_SEQ, D_MODEL), 1.0),
        'state_gla': nrm(ks[2], (DEPTH, DEC_BATCH, GLA_HEADS, GLA_HEAD_K, GLA_HEAD_V), 0.5),
        'cache_swa_k': nrm(ks[3], (DEPTH, DEC_BATCH, cache_w, SWA_KV_HEADS, SWA_HEAD_DIM), 1.0),
        'cache_swa_v': nrm(ks[4], (DEPTH, DEC_BATCH, cache_w, SWA_KV_HEADS, SWA_HEAD_DIM), 1.0),
        'ffn1_norm': gain(ks[5], D_MODEL),
        'ffn1_w_gate': nrm(ks[6], (DEPTH, D_MODEL, D_FF), D_MODEL ** -0.5),
        'ffn1_w_up': nrm(ks[7], (DEPTH, D_MODEL, D_FF), D_MODEL ** -0.5),
        'ffn1_w_down': nrm(ks[8], (DEPTH, D_FF, D_MODEL), D_FF ** -0.5),
        'mix_norm': gain(ks[9], D_MODEL),
        'w_in': nrm(ks[10], (DEPTH, D_MODEL, D_IN), D_MODEL ** -0.5),
        'w_gate_up': nrm(ks[11], (DEPTH, GLA_GATE_RANK, GLA_KEY_WIDTH), GLA_GATE_RANK ** -0.5),
        'b_gate': nrm(ks[12], (DEPTH, GLA_KEY_WIDTH), 0.1),
        'gla_head_norm': gain(ks[13], GLA_WIDTH),
        'swa_out_norm': gain(ks[14], SWA_WIDTH),
        'swa_sinks': nrm(ks[15], (DEPTH, SWA_Q_HEADS), 1.0),
        'w_out': nrm(ks[16], (DEPTH, MIX_WIDTH, D_MODEL), MIX_WIDTH ** -0.5),
        'ffn2_norm': gain(ks[17], D_MODEL),
        'ffn2_w_gate': nrm(ks[18], (DEPTH, D_MODEL, D_FF), D_MODEL ** -0.5),
        'ffn2_w_up': nrm(ks[19], (DEPTH, D_MODEL, D_FF), D_MODEL ** -0.5),
        'ffn2_w_down': nrm(ks[20], (DEPTH, D_FF, D_MODEL), D_FF ** -0.5),
        'final_norm': 1.0 + 0.02 * jax.random.normal(ks[21], (D_MODEL,), f32),
    }


def reference(x_prompt, x_sample, state_gla, cache_swa_k, cache_swa_v,
              ffn1_norm, ffn1_w_gate, ffn1_w_up, ffn1_w_down,
              mix_norm, w_in, w_gate_up, b_gate, gla_head_norm, swa_out_norm, swa_sinks, w_out,
              ffn2_norm, ffn2_w_gate, ffn2_w_up, ffn2_w_down, final_norm):
    yp = x_prompt
    ys = x_sample
    sp, kp, vp, ss, ksm, vsm = [], [], [], [], [], []
    for l in range(DEPTH):
        p = {
            'ffn1_norm': ffn1_norm[l], 'ffn1_w_gate': ffn1_w_gate[l],
            'ffn1_w_up': ffn1_w_up[l], 'ffn1_w_down': ffn1_w_down[l],
            'mix_norm': mix_norm[l], 'w_in': w_in[l], 'w_gate_up': w_gate_up[l], 'b_gate': b_gate[l],
            'gla_head_norm': gla_head_norm[l], 'swa_out_norm': swa_out_norm[l],
            'swa_sinks': swa_sinks[l], 'w_out': w_out[l],
            'ffn2_norm': ffn2_norm[l], 'ffn2_w_gate': ffn2_w_gate[l],
            'ffn2_w_up': ffn2_w_up[l], 'ffn2_w_down': ffn2_w_down[l],
        }
        gla0 = jnp.zeros((yp.shape[0], GLA_HEADS, GLA_HEAD_K, GLA_HEAD_V), yp.dtype)
        yp, s_p, k_p, v_p = _layer(yp, gla0, None, None, p)
        ys, s_s, k_s, v_s = _layer(ys, state_gla[l], cache_swa_k[l], cache_swa_v[l], p)
        sp.append(s_p)
        kp.append(k_p)
        vp.append(v_p)
        ss.append(s_s)
        ksm.append(k_s)
        vsm.append(v_s)
    y_prompt = _rms_norm(yp, final_norm)
    y_sample = _rms_norm(ys, final_norm)
    return (y_prompt, y_sample, jnp.stack(sp), jnp.stack(kp), jnp.stack(vp),
            jnp.stack(ss), jnp.stack(ksm), jnp.stack(vsm))
```

```python
import functools

import jax
import jax.numpy as jnp
from jax import lax
from jax.experimental import pallas as pl
from jax.experimental.pallas import tpu as pltpu

F32 = jnp.float32
BF16 = jnp.bfloat16

D_MODEL = 1024
D_FF = 2816
GLA_HEADS = 4
GLA_HEAD_K = 64
GLA_HEAD_V = 128
GLA_KEY_WIDTH = GLA_HEADS * GLA_HEAD_K
GLA_WIDTH = GLA_HEADS * GLA_HEAD_V
GLA_GATE_RANK = 16
GLA_GATE_TAU = 16.0
SWA_HEAD_DIM = 64
SWA_Q_HEADS = 8
SWA_KV_HEADS = 2
SWA_GROUP = SWA_Q_HEADS // SWA_KV_HEADS
SWA_WIDTH = SWA_Q_HEADS * SWA_HEAD_DIM
SWA_KV_WIDTH = SWA_KV_HEADS * SWA_HEAD_DIM
WINDOW = 128
DEC_SEQ = 4
NORM_EPS = 1e-6
HEAD_SCALE = 0.125

LANES = 128
SUBLANES = 8
VMEM_LIMIT_BYTES = 56 * 1024 * 1024

ROW_TILE = 512
FF_CHUNK = 256
GLA_CHUNK = 128
GLA_BLOCK = 512
SAMPLE_SEQS = 32
SAMPLE_ROWS = SAMPLE_SEQS * DEC_SEQ
DECAY_CLAMP = 60.0

PROJ_Q_G = 0
PROJ_K_G = 256
PROJ_V_G = 512
PROJ_R_G = 1024
PROJ_Q_S = 1536
PROJ_KV_S = 2048
PROJ_A = 2304
PROJ_WIDTH = 2432


def _dot(a, b):
    return jnp.dot(a, b, preferred_element_type=F32)


def _dot_tb(a, b):
    return lax.dot_general(a, b, (((1,), (1,)), ((), ())), preferred_element_type=F32)


def _dot_ta(a, b):
    return lax.dot_general(a, b, (((0,), (0,)), ((), ())), preferred_element_type=F32)


def _rms(x, g):
    return x * lax.rsqrt(jnp.mean(x * x, axis=-1, keepdims=True) + NORM_EPS) * g


def _swiglu(h, wg_ref, wu_ref, wd_ref, act_ref):
    for c0 in range(0, D_FF, FF_CHUNK):
        g = _dot(h, wg_ref[:, c0:c0 + FF_CHUNK])
        u = _dot(h, wu_ref[:, c0:c0 + FF_CHUNK])
        act_ref[:, c0:c0 + FF_CHUNK] = (g * jax.nn.sigmoid(g) * u).astype(BF16)
    return _dot(act_ref[...], wd_ref[...])


def _head_norm_gate(o, r, gn):
    parts = []
    for h in range(GLA_HEADS):
        sl = slice(h * GLA_HEAD_V, (h + 1) * GLA_HEAD_V)
        parts.append(_rms(o[:, sl], gn[:, sl]))
    return (jnp.concatenate(parts, axis=1) * (r * jax.nn.sigmoid(r))).astype(BF16)


def _stage_a_body(x_ref, n1_ref, wg_ref, wu_ref, wd_ref, nm_ref, win_ref, wgu_ref, bg_ref,
                  x1_ref, qk_ref, g_ref, v_ref, r_ref, qs_ref, kv_ref, act_ref):
    x = x_ref[...]
    h = _rms(x, n1_ref[...]).astype(BF16)
    x1 = x + 0.5 * _swiglu(h, wg_ref, wu_ref, wd_ref, act_ref)
    x1_ref[...] = x1
    h2 = _rms(x1, nm_ref[...]).astype(BF16)
    proj = _dot(h2, win_ref[...])
    qk_ref[:, 0:GLA_KEY_WIDTH] = proj[:, PROJ_Q_G:PROJ_K_G] * HEAD_SCALE
    qk_ref[:, GLA_KEY_WIDTH:] = proj[:, PROJ_K_G:PROJ_V_G]
    v_ref[...] = proj[:, PROJ_V_G:PROJ_R_G].astype(BF16)
    r_ref[...] = proj[:, PROJ_R_G:PROJ_Q_S]
    qs_ref[...] = (proj[:, PROJ_Q_S:PROJ_KV_S] * HEAD_SCALE).astype(BF16)
    kv_ref[...] = proj[:, PROJ_KV_S:PROJ_A]
    a = proj[:, PROJ_A:PROJ_WIDTH].astype(BF16)
    z = _dot(a, wgu_ref[...]) + bg_ref[...]
    g_ref[...] = jax.nn.log_sigmoid(z) * (1.0 / GLA_GATE_TAU)


def _resident(shape):
    return pl.BlockSpec(shape, lambda i: (0,) * len(shape), pipeline_mode=pl.Buffered(1))


def _rows(tm, n):
    return pl.BlockSpec((tm, n), lambda i: (i, 0))


def _stage_a(x, w):
    rows = x.shape[0]
    tm = min(ROW_TILE, rows)
    assert rows % tm == 0
    out_widths = ((D_MODEL, F32), (2 * GLA_KEY_WIDTH, F32), (GLA_KEY_WIDTH, F32), (GLA_WIDTH, BF16),
                  (GLA_WIDTH, F32), (SWA_WIDTH, BF16), (2 * SWA_KV_WIDTH, F32))
    return pl.pallas_call(
        _stage_a_body,
        grid=(rows // tm,),
        in_specs=[_rows(tm, D_MODEL), _resident((1, D_MODEL)),
                  _resident((D_MODEL, D_FF)), _resident((D_MODEL, D_FF)), _resident((D_FF, D_MODEL)),
                  _resident((1, D_MODEL)), _resident((D_MODEL, PROJ_WIDTH)),
                  _resident((LANES, GLA_KEY_WIDTH)), _resident((1, GLA_KEY_WIDTH))],
        out_specs=[_rows(tm, n) for n, _ in out_widths],
        out_shape=[jax.ShapeDtypeStruct((rows, n), dt) for n, dt in out_widths],
        scratch_shapes=[pltpu.VMEM((tm, D_FF), BF16)],
        compiler_params=pltpu.CompilerParams(dimension_semantics=("arbitrary",),
                                             vmem_limit_bytes=VMEM_LIMIT_BYTES),
        name="stage_a_ffn1_proj",
    )(x, w["n1"], w["wg1"], w["wu1"], w["wd1"], w["nm"], w["win"], w["wgu"], w["bg"])


def _stage_d_body(x1_ref, og_ref, os_ref, wo_ref, n2_ref, wg_ref, wu_ref, wd_ref, nf_ref, y_ref, act_ref):
    mixed = jnp.concatenate([og_ref[...], os_ref[...]], axis=1)
    x2 = x1_ref[...] + _dot(mixed, wo_ref[...])
    h = _rms(x2, n2_ref[...]).astype(BF16)
    x3 = x2 + 0.5 * _swiglu(h, wg_ref, wu_ref, wd_ref, act_ref)
    y_ref[...] = _rms(x3, nf_ref[...])


def _stage_d(x1, og, osw, w):
    rows = x1.shape[0]
    tm = min(ROW_TILE, rows)
    assert rows % tm == 0
    return pl.pallas_call(
        _stage_d_body,
        grid=(rows // tm,),
        in_specs=[_rows(tm, D_MODEL), _rows(tm, GLA_WIDTH), _rows(tm, SWA_WIDTH),
                  _resident((D_MODEL, D_MODEL)), _resident((1, D_MODEL)),
                  _resident((D_MODEL, D_FF)), _resident((D_MODEL, D_FF)), _resident((D_FF, D_MODEL)),
                  _resident((1, D_MODEL))],
        out_specs=_rows(tm, D_MODEL),
        out_shape=jax.ShapeDtypeStruct((rows, D_MODEL), F32),
        scratch_shapes=[pltpu.VMEM((tm, D_FF), BF16)],
        compiler_params=pltpu.CompilerParams(dimension_semantics=("arbitrary",),
                                             vmem_limit_bytes=VMEM_LIMIT_BYTES),
        name="stage_d_out_ffn2",
    )(x1, og, osw, w["wo"], w["n2"], w["wg2"], w["wu2"], w["wd2"], w["nf"])


def _split_bf16(x):
    hi = x.astype(BF16)
    lo = (x - hi.astype(F32)).astype(BF16)
    return hi, lo


def _gla_prompt_body(qk_ref, g_ref, v_ref, r_ref, gn_ref, og_ref, s_ref, o_scr, cum_scr, inter_scr):
    c_len = GLA_CHUNK

    @pl.when(pl.program_id(0) == 0)
    def _():
        s_ref[...] = jnp.zeros_like(s_ref)

    row = lax.broadcasted_iota(jnp.int32, (c_len, c_len), 0)
    col = lax.broadcasted_iota(jnp.int32, (c_len, c_len), 1)
    causal = row >= col
    ltri = jnp.where(causal, 1.0, 0.0).astype(BF16)
    causal4 = (lax.broadcasted_iota(jnp.int32, (GLA_HEADS * c_len, c_len), 0) % c_len
               >= lax.broadcasted_iota(jnp.int32, (GLA_HEADS * c_len, c_len), 1))
    lane_head = lax.broadcasted_iota(jnp.int32, (1, GLA_KEY_WIDTH), 1) // GLA_HEAD_K
    row_head = lax.broadcasted_iota(jnp.int32, (GLA_KEY_WIDTH, 1), 0) // GLA_HEAD_K
    eye = (lax.broadcasted_iota(jnp.int32, (GLA_KEY_WIDTH, GLA_KEY_WIDTH), 0)
           == lax.broadcasted_iota(jnp.int32, (GLA_KEY_WIDTH, GLA_KEY_WIDTH), 1))
    ind = jnp.where(lax.broadcasted_iota(jnp.int32, (GLA_KEY_WIDTH, LANES), 0) // GLA_HEAD_K
                    == lax.broadcasted_iota(jnp.int32, (GLA_KEY_WIDTH, LANES), 1), 1.0, 0.0).astype(BF16)
    zero_b = jnp.zeros((), BF16)

    for c in range(GLA_BLOCK // c_len):
        r0 = c * c_len
        rows = slice(r0, r0 + c_len)
        g_hi, g_lo = _split_bf16(g_ref[rows, :])
        cum = _dot(ltri, g_hi) + _dot(ltri, g_lo)
        last = cum[c_len - 1:c_len, :]
        q = qk_ref[rows, 0:GLA_KEY_WIDTH]
        k = qk_ref[rows, GLA_KEY_WIDTH:]
        vb = v_ref[rows, :]
        qe = (q * jnp.exp(cum)).astype(BF16)
        ke = (k * jnp.exp(jnp.minimum(-cum, DECAY_CLAMP))).astype(BF16)
        kl = (k * jnp.exp(last - cum)).astype(BF16)
        state = s_ref[...]
        sb = state.astype(BF16)
        s_bd = jnp.concatenate([jnp.where(row_head == h, sb, zero_b) for h in range(GLA_HEADS)], axis=1)
        o_inter = _dot(qe, s_bd)
        q_st = jnp.concatenate([jnp.where(lane_head == h, qe, zero_b) for h in range(GLA_HEADS)], axis=0)
        attn = jnp.where(causal4, _dot_tb(q_st, ke), 0.0).astype(BF16)
        o_intra = jnp.concatenate(
            [_dot(attn[h * c_len:(h + 1) * c_len], vb[:, h * GLA_HEAD_V:(h + 1) * GLA_HEAD_V])
             for h in range(GLA_HEADS)], axis=1)
        o_scr[rows, :] = o_inter + o_intra

        @pl.when(jnp.min(last) < -DECAY_CLAMP)
        def _():
            cum_scr[...] = cum
            inter_scr[...] = o_inter
            j_idx = lax.broadcasted_iota(jnp.int32, (c_len, 1), 0)

            def one_row(i, carry):
                ci = cum_scr[pl.ds(i, 1), :]
                qi = qk_ref[pl.ds(r0 + i, 1), 0:GLA_KEY_WIDTH]
                kk = qk_ref[rows, GLA_KEY_WIDTH:]
                dec = jnp.exp(jnp.minimum(ci - cum_scr[...], 0.0))
                a_cols = _dot(((qi * kk) * dec).astype(BF16), ind)
                outs = []
                for h in range(GLA_HEADS):
                    w_col = jnp.where(j_idx <= i, a_cols[:, h:h + 1], 0.0)
                    v_h = v_ref[rows, h * GLA_HEAD_V:(h + 1) * GLA_HEAD_V].astype(F32)
                    outs.append(jnp.sum(w_col * v_h, axis=0, keepdims=True))
                o_scr[pl.ds(r0 + i, 1), :] = inter_scr[pl.ds(i, 1), :] + jnp.concatenate(outs, axis=1)
                return carry

            lax.fori_loop(0, c_len, one_row, 0)

        upd = _dot_ta(kl, vb)
        upd = jnp.concatenate([upd[h * GLA_HEAD_K:(h + 1) * GLA_HEAD_K, h * GLA_HEAD_V:(h + 1) * GLA_HEAD_V]
                               for h in range(GLA_HEADS)], axis=0)
        last_col = jnp.sum(jnp.where(eye, last, 0.0), axis=1, keepdims=True)
        s_ref[...] = state * jnp.exp(last_col) + upd

    og_ref[...] = _head_norm_gate(o_scr[...], r_ref[...], gn_ref[...])


def _gla_prompt(qk, g, v, r, gn):
    t = qk.shape[0]
    assert t % GLA_BLOCK == 0
    blk = lambda n: pl.BlockSpec((GLA_BLOCK, n), lambda i: (i, 0))
    return pl.pallas_call(
        _gla_prompt_body,
        grid=(t // GLA_BLOCK,),
        in_specs=[blk(2 * GLA_KEY_WIDTH), blk(GLA_KEY_WIDTH), blk(GLA_WIDTH), blk(GLA_WIDTH),
                  pl.BlockSpec((1, GLA_WIDTH), lambda i: (0, 0))],
        out_specs=[blk(GLA_WIDTH), pl.BlockSpec((GLA_KEY_WIDTH, GLA_HEAD_V), lambda i: (0, 0))],
        out_shape=[jax.ShapeDtypeStruct((t, GLA_WIDTH), BF16),
                   jax.ShapeDtypeStruct((GLA_KEY_WIDTH, GLA_HEAD_V), F32)],
        scratch_shapes=[pltpu.VMEM((GLA_BLOCK, GLA_WIDTH), F32),
                        pltpu.VMEM((GLA_CHUNK, GLA_KEY_WIDTH), F32),
                        pltpu.VMEM((GLA_CHUNK, GLA_WIDTH), F32)],
        compiler_params=pltpu.CompilerParams(dimension_semantics=("arbitrary",)),
        name="gla_prompt",
    )(qk, g, v, r, gn)


def _dup_halves(x):
    lo = lax.broadcasted_iota(jnp.int32, (1, LANES), 1) < SWA_HEAD_DIM
    sw = pltpu.roll(x, SWA_HEAD_DIM, 1)
    return jnp.where(lo, x, sw).astype(BF16), jnp.where(lo, sw, x).astype(BF16)


def _swa_prompt_body(sink_ref, q_ref, kvc_ref, kvp_ref, nrm_ref, o_ref):
    n = pl.program_id(0)
    w = WINDOW
    kv = jnp.concatenate([kvp_ref[...], kvc_ref[...]], axis=0)
    kk = _dup_halves(kv[:, 0:SWA_KV_WIDTH])
    vv = _dup_halves(kv[:, SWA_KV_WIDTH:])
    q = q_ref[...]
    lo = lax.broadcasted_iota(jnp.int32, (1, LANES), 1) < SWA_HEAD_DIM
    qi = lax.broadcasted_iota(jnp.int32, (w, 2 * w), 0) + w
    kj = lax.broadcasted_iota(jnp.int32, (w, 2 * w), 1)
    first_valid = jnp.where(n > 0, 0, w)
    mask = (kj <= qi) & (qi - kj < w) & (kj >= first_valid)
    zero_b = jnp.zeros((), BF16)
    tiles = []
    for grp in range(SWA_KV_HEADS):
        stacked = []
        for j in range(SWA_GROUP):
            hq = grp * SWA_GROUP + j
            tile = q[:, (hq // 2) * LANES:(hq // 2 + 1) * LANES]
            stacked.append(jnp.where(lo if hq % 2 == 0 else ~lo, tile, zero_b))
        s = _dot_tb(jnp.concatenate(stacked, axis=0), kk[grp])
        probs, dens = [], []
        for j in range(SWA_GROUP):
            sink = sink_ref[grp * SWA_GROUP + j]
            sj = jnp.where(mask, s[j * w:(j + 1) * w], -jnp.inf)
            m = jnp.maximum(jnp.max(sj, axis=1, keepdims=True), sink)
            p = jnp.exp(sj - m)
            dens.append(jnp.sum(p, axis=1, keepdims=True) + jnp.exp(sink - m))
            probs.append(p.astype(BF16))
        o = _dot(jnp.concatenate(probs, axis=0), vv[grp])
        res = [o[j * w:(j + 1) * w] / dens[j] for j in range(SWA_GROUP)]
        tiles.append(jnp.where(lo, res[0], res[1]))
        tiles.append(jnp.where(lo, res[2], res[3]))
    o_ref[...] = _rms(jnp.concatenate(tiles, axis=1), nrm_ref[...]).astype(BF16)


def _swa_prompt(sinks, qs, kv, nrm):
    t = qs.shape[0]
    assert t % WINDOW == 0
    return pl.pallas_call(
        _swa_prompt_body,
        grid=(t // WINDOW,),
        in_specs=[pl.BlockSpec(memory_space=pltpu.SMEM),
                  pl.BlockSpec((WINDOW, SWA_WIDTH), lambda i: (i, 0)),
                  pl.BlockSpec((WINDOW, 2 * SWA_KV_WIDTH), lambda i: (i, 0)),
                  pl.BlockSpec((WINDOW, 2 * SWA_KV_WIDTH), lambda i: (jnp.maximum(i - 1, 0), 0)),
                  pl.BlockSpec((1, SWA_WIDTH), lambda i: (0, 0))],
        out_specs=pl.BlockSpec((WINDOW, SWA_WIDTH), lambda i: (i, 0)),
        out_shape=jax.ShapeDtypeStruct((t, SWA_WIDTH), BF16),
        compiler_params=pltpu.CompilerParams(dimension_semantics=("arbitrary",)),
        name="swa_prompt",
    )(sinks, qs, kv, kv, nrm)


def _gla_sample_body(qk_ref, g_ref, v_ref, r_ref, gn_ref, s_ref, og_ref, so_ref, o_scr):
    rows = SAMPLE_ROWS
    g = g_ref[...]
    q = qk_ref[:, 0:GLA_KEY_WIDTH]
    k = qk_ref[:, GLA_KEY_WIDTH:]
    vb = v_ref[...]
    vf = vb.astype(F32)
    tok = lax.broadcasted_iota(jnp.int32, (rows, 1), 0) % DEC_SEQ

    cum = g
    for d in range(1, DEC_SEQ):
        cum = cum + jnp.where(tok >= d, pltpu.roll(g, d, 0), 0.0)
    tot = jnp.where(tok == DEC_SEQ - 1, cum, 0.0)
    for d in range(1, DEC_SEQ):
        tot = tot + jnp.where(tok == DEC_SEQ - 1 - d, pltpu.roll(cum, rows - d, 0), 0.0)

    qe = q * jnp.exp(cum)
    kl = k * jnp.exp(tot - cum)
    decay_t = jnp.exp(tot).T

    ind = jnp.where(lax.broadcasted_iota(jnp.int32, (GLA_KEY_WIDTH, LANES), 0) // GLA_HEAD_K
                    == lax.broadcasted_iota(jnp.int32, (GLA_KEY_WIDTH, LANES), 1), 1.0, 0.0).astype(BF16)
    expand = jnp.where(lax.broadcasted_iota(jnp.int32, (LANES, GLA_WIDTH), 0)
                       == lax.broadcasted_iota(jnp.int32, (LANES, GLA_WIDTH), 1) // GLA_HEAD_V,
                       1.0, 0.0).astype(BF16)

    o_intra = jnp.zeros((rows, GLA_WIDTH), F32)
    for d in range(DEC_SEQ):
        k_d = k if d == 0 else pltpu.roll(k, d, 0)
        c_d = cum if d == 0 else pltpu.roll(cum, d, 0)
        v_d = vf if d == 0 else pltpu.roll(vf, d, 0)
        pair = jnp.where(tok >= d, q * k_d * jnp.exp(jnp.minimum(cum - c_d, 0.0)), 0.0)
        a = _dot(pair.astype(BF16), ind)
        o_intra = o_intra + _dot(a.astype(BF16), expand) * v_d

    lane_head = lax.broadcasted_iota(jnp.int32, (1, GLA_KEY_WIDTH), 1) // GLA_HEAD_K
    row8 = lax.broadcasted_iota(jnp.int32, (SUBLANES, 1), 0)
    row32 = lax.broadcasted_iota(jnp.int32, (GLA_HEADS * SUBLANES, 1), 0)
    for pair_idx in range(SAMPLE_SEQS // 2):
        r8 = slice(pair_idx * SUBLANES, (pair_idx + 1) * SUBLANES)
        q8 = qe[r8, :]
        lhs = jnp.concatenate([jnp.where(lane_head == h, q8, 0.0) for h in range(GLA_HEADS)],
                              axis=0).astype(BF16)
        kl8 = kl[r8, :]
        v8 = vb[r8, :]
        res = []
        for s in range(2):
            b = 2 * pair_idx + s
            state = s_ref[b]
            res.append(_dot(lhs, state.astype(BF16)))
            kl_b = jnp.where(row8 // DEC_SEQ == s, kl8, 0.0).astype(BF16)
            upd = _dot_ta(kl_b, v8)
            upd = jnp.concatenate(
                [upd[h * GLA_HEAD_K:(h + 1) * GLA_HEAD_K, h * GLA_HEAD_V:(h + 1) * GLA_HEAD_V]
                 for h in range(GLA_HEADS)], axis=0)
            so_ref[b] = state * decay_t[:, DEC_SEQ * b:DEC_SEQ * b + 1] + upd
        sel = jnp.where(row32 % SUBLANES < DEC_SEQ, res[0], res[1])
        o_scr[r8, :] = jnp.concatenate([sel[h * SUBLANES:(h + 1) * SUBLANES] for h in range(GLA_HEADS)], axis=1)

    og_ref[...] = _head_norm_gate(o_scr[...] + o_intra, r_ref[...], gn_ref[...])


def _gla_sample(qk, g, v, r, gn, state):
    rows = qk.shape[0]
    nseq = rows // DEC_SEQ
    assert nseq % SAMPLE_SEQS == 0
    blk = lambda n: pl.BlockSpec((SAMPLE_ROWS, n), lambda i: (i, 0))
    sblk = pl.BlockSpec((SAMPLE_SEQS, GLA_KEY_WIDTH, GLA_HEAD_V), lambda i: (i, 0, 0))
    return pl.pallas_call(
        _gla_sample_body,
        grid=(nseq // SAMPLE_SEQS,),
        in_specs=[blk(2 * GLA_KEY_WIDTH), blk(GLA_KEY_WIDTH), blk(GLA_WIDTH), blk(GLA_WIDTH),
                  pl.BlockSpec((1, GLA_WIDTH), lambda i: (0, 0)), sblk],
        out_specs=[blk(GLA_WIDTH), sblk],
        out_shape=[jax.ShapeDtypeStruct((rows, GLA_WIDTH), BF16),
                   jax.ShapeDtypeStruct((nseq, GLA_KEY_WIDTH, GLA_HEAD_V), F32)],
        scratch_shapes=[pltpu.VMEM((SAMPLE_ROWS, GLA_WIDTH), F32)],
        compiler_params=pltpu.CompilerParams(dimension_semantics=("arbitrary",)),
        name="gla_sample",
    )(qk, g, v, r, gn, state)


def _swa_sample_body(sink_ref, q_ref, kvn_ref, kc_ref, vc_ref, nrm_ref, o_ref, kco_ref, vco_ref,
                     lhs_scr, sc_scr, pc_scr, oc_scr):
    rows = SAMPLE_ROWS
    q = q_ref[...].astype(F32)
    lo = lax.broadcasted_iota(jnp.int32, (1, LANES), 1) < SWA_HEAD_DIM
    for hq in range(SWA_Q_HEADS):
        grp = hq // SWA_GROUP
        tile = q[:, (hq // 2) * LANES:(hq // 2 + 1) * LANES]
        src = tile if hq % 2 == grp else pltpu.roll(tile, SWA_HEAD_DIM, 1)
        lhs_scr[hq * rows:(hq + 1) * rows, :] = jnp.where(lo if grp == 0 else ~lo, src, 0.0)

    kvn = kvn_ref[...]
    k_new = kvn[:, 0:SWA_KV_WIDTH]
    v_new = kvn[:, SWA_KV_WIDTH:]
    s_new = _dot_tb(lhs_scr[...].astype(BF16), k_new.astype(BF16))

    row64 = lax.broadcasted_iota(jnp.int32, (SWA_Q_HEADS * SUBLANES, 1), 0)
    first_of_pair = row64 % SUBLANES < DEC_SEQ

    def gather_pair(ref, pair_idx):
        return jnp.concatenate(
            [ref[hq * rows + pair_idx * SUBLANES:hq * rows + (pair_idx + 1) * SUBLANES, :]
             for hq in range(SWA_Q_HEADS)], axis=0).astype(BF16)

    def scatter_pair(ref, pair_idx, val):
        for hq in range(SWA_Q_HEADS):
            ref[hq * rows + pair_idx * SUBLANES:hq * rows + (pair_idx + 1) * SUBLANES, :] = (
                val[hq * SUBLANES:(hq + 1) * SUBLANES])

    for pair_idx in range(SAMPLE_SEQS // 2):
        l64 = gather_pair(lhs_scr, pair_idx)
        sa = _dot_tb(l64, kc_ref[2 * pair_idx].astype(BF16))
        sb = _dot_tb(l64, kc_ref[2 * pair_idx + 1].astype(BF16))
        scatter_pair(sc_scr, pair_idx, jnp.where(first_of_pair, sa, sb))

    rr = lax.broadcasted_iota(jnp.int32, (rows, rows), 0)
    cc = lax.broadcasted_iota(jnp.int32, (rows, rows), 1)
    tok = rr % DEC_SEQ
    mask_cache = cc > tok
    mask_new = (cc // DEC_SEQ == rr // DEC_SEQ) & (cc % DEC_SEQ <= tok)
    p_new, dens = [], []
    for hq in range(SWA_Q_HEADS):
        sl = slice(hq * rows, (hq + 1) * rows)
        sink = sink_ref[hq]
        s_c = jnp.where(mask_cache, sc_scr[sl, :], -jnp.inf)
        s_n = jnp.where(mask_new, s_new[sl, :], -jnp.inf)
        m = jnp.maximum(jnp.maximum(jnp.max(s_c, axis=1, keepdims=True),
                                    jnp.max(s_n, axis=1, keepdims=True)), sink)
        p_c = jnp.exp(s_c - m)
        p_n = jnp.exp(s_n - m)
        dens.append(jnp.sum(p_c, axis=1, keepdims=True) + jnp.sum(p_n, axis=1, keepdims=True)
                    + jnp.exp(sink - m))
        pc_scr[sl, :] = p_c
        p_new.append(p_n.astype(BF16))
    o_new = _dot(jnp.concatenate(p_new, axis=0), v_new.astype(BF16))

    for pair_idx in range(SAMPLE_SEQS // 2):
        p64 = gather_pair(pc_scr, pair_idx)
        oa = _dot(p64, vc_ref[2 * pair_idx].astype(BF16))
        ob = _dot(p64, vc_ref[2 * pair_idx + 1].astype(BF16))
        scatter_pair(oc_scr, pair_idx, jnp.where(first_of_pair, oa, ob))

    tiles = []
    for i in range(SWA_Q_HEADS // 2):
        halves = []
        for hq in (2 * i, 2 * i + 1):
            sl = slice(hq * rows, (hq + 1) * rows)
            oh = (oc_scr[sl, :] + o_new[sl, :]) / dens[hq]
            halves.append(oh if hq % 2 == hq // SWA_GROUP else pltpu.roll(oh, SWA_HEAD_DIM, 1))
        tiles.append(jnp.where(lo, halves[0], halves[1]))
    o_ref[...] = _rms(jnp.concatenate(tiles, axis=1), nrm_ref[...]).astype(BF16)

    for b in range(SAMPLE_SEQS):
        kco_ref[b, 0:WINDOW - DEC_SEQ, :] = kc_ref[b, DEC_SEQ:WINDOW, :]
        vco_ref[b, 0:WINDOW - DEC_SEQ, :] = vc_ref[b, DEC_SEQ:WINDOW, :]
        kco_ref[b, WINDOW - DEC_SEQ:WINDOW, :] = k_new[DEC_SEQ * b:DEC_SEQ * (b + 1), :]
        vco_ref[b, WINDOW - DEC_SEQ:WINDOW, :] = v_new[DEC_SEQ * b:DEC_SEQ * (b + 1), :]


def _swa_sample(sinks, qs, kvn, kc, vc, nrm):
    rows = qs.shape[0]
    nseq = rows // DEC_SEQ
    assert nseq % SAMPLE_SEQS == 0 and kc.shape[1] == WINDOW
    cblk = pl.BlockSpec((SAMPLE_SEQS, WINDOW, SWA_KV_WIDTH), lambda i: (i, 0, 0))
    big = pltpu.VMEM((SWA_Q_HEADS * SAMPLE_ROWS, LANES), F32)
    return pl.pallas_call(
        _swa_sample_body,
        grid=(nseq // SAMPLE_SEQS,),
        in_specs=[pl.BlockSpec(memory_space=pltpu.SMEM),
                  pl.BlockSpec((SAMPLE_ROWS, SWA_WIDTH), lambda i: (i, 0)),
                  pl.BlockSpec((SAMPLE_ROWS, 2 * SWA_KV_WIDTH), lambda i: (i, 0)),
                  cblk, cblk,
                  pl.BlockSpec((1, SWA_WIDTH), lambda i: (0, 0))],
        out_specs=[pl.BlockSpec((SAMPLE_ROWS, SWA_WIDTH), lambda i: (i, 0)), cblk, cblk],
        out_shape=[jax.ShapeDtypeStruct((rows, SWA_WIDTH), BF16),
                   jax.ShapeDtypeStruct(kc.shape, F32), jax.ShapeDtypeStruct(vc.shape, F32)],
        scratch_shapes=[big, big, big, big],
        compiler_params=pltpu.CompilerParams(dimension_semantics=("arbitrary",)),
        name="swa_sample",
    )(sinks, qs, kvn, kc, vc, nrm)


def _prepare_weights(ffn1_norm, ffn1_w_gate, ffn1_w_up, ffn1_w_down, mix_norm, w_in, w_gate_up, b_gate,
                     gla_head_norm, swa_out_norm, swa_sinks, w_out,
                     ffn2_norm, ffn2_w_gate, ffn2_w_up, ffn2_w_down, final_norm, layer):
    a0 = 2 * GLA_KEY_WIDTH + 2 * GLA_WIDTH
    a1 = a0 + GLA_GATE_RANK
    win = w_in[layer]
    win = jnp.concatenate([win[:, :a0], win[:, a1:], win[:, a0:a1],
                           jnp.zeros((D_MODEL, LANES - GLA_GATE_RANK), win.dtype)], axis=1)
    wgu = jnp.concatenate([w_gate_up[layer],
                           jnp.zeros((LANES - GLA_GATE_RANK, GLA_KEY_WIDTH), w_gate_up.dtype)], axis=0)
    row = lambda a: a.reshape(1, -1).astype(F32)
    return dict(
        n1=row(ffn1_norm[layer]), wg1=ffn1_w_gate[layer].astype(BF16), wu1=ffn1_w_up[layer].astype(BF16),
        wd1=ffn1_w_down[layer].astype(BF16), nm=row(mix_norm[layer]), win=win.astype(BF16),
        wgu=wgu.astype(BF16), bg=row(b_gate[layer]), gn=row(gla_head_norm[layer]),
        sn=row(swa_out_norm[layer]), sinks=swa_sinks[layer].astype(F32), wo=w_out[layer].astype(BF16),
        n2=row(ffn2_norm[layer]), wg2=ffn2_w_gate[layer].astype(BF16), wu2=ffn2_w_up[layer].astype(BF16),
        wd2=ffn2_w_down[layer].astype(BF16), nf=row(final_norm))


def kernel(x_prompt, x_sample, state_gla, cache_swa_k, cache_swa_v, ffn1_norm, ffn1_w_gate, ffn1_w_up,
           ffn1_w_down, mix_norm, w_in, w_gate_up, b_gate, gla_head_norm, swa_out_norm, swa_sinks, w_out,
           ffn2_norm, ffn2_w_gate, ffn2_w_up, ffn2_w_down, final_norm):
    depth = state_gla.shape[0]
    assert depth == 1 and x_prompt.shape[0] == 1 and x_sample.shape[1] == DEC_SEQ
    seq = x_prompt.shape[1]
    nseq = x_sample.shape[0]
    w = _prepare_weights(ffn1_norm, ffn1_w_gate, ffn1_w_up, ffn1_w_down, mix_norm, w_in, w_gate_up, b_gate,
                         gla_head_norm, swa_out_norm, swa_sinks, w_out,
                         ffn2_norm, ffn2_w_gate, ffn2_w_up, ffn2_w_down, final_norm, 0)

    x1, qk, g, v, r, qs, kv = _stage_a(x_prompt.reshape(seq, D_MODEL), w)
    og, state_p = _gla_prompt(qk, g, v, r, w["gn"])
    osw = _swa_prompt(w["sinks"], qs, kv, w["sn"])
    y_prompt = _stage_d(x1, og, osw, w).reshape(1, seq, D_MODEL)
    cw = min(WINDOW, seq)
    k_cache_p = kv[seq - cw:, 0:SWA_KV_WIDTH].reshape(1, 1, cw, SWA_KV_HEADS, SWA_HEAD_DIM)
    v_cache_p = kv[seq - cw:, SWA_KV_WIDTH:].reshape(1, 1, cw, SWA_KV_HEADS, SWA_HEAD_DIM)
    state_p = state_p.reshape(1, 1, GLA_HEADS, GLA_HEAD_K, GLA_HEAD_V)

    x1, qk, g, v, r, qs, kv = _stage_a(x_sample.reshape(nseq * DEC_SEQ, D_MODEL), w)
    og, state_s = _gla_sample(qk, g, v, r, w["gn"],
                              state_gla[0].reshape(nseq, GLA_KEY_WIDTH, GLA_HEAD_V))
    cache_w = cache_swa_k.shape[2]
    osw, k_cache_s, v_cache_s = _swa_sample(
        w["sinks"], qs, kv, cache_swa_k[0].reshape(nseq, cache_w, SWA_KV_WIDTH),
        cache_swa_v[0].reshape(nseq, cache_w, SWA_KV_WIDTH), w["sn"])
    y_sample = _stage_d(x1, og, osw, w).reshape(nseq, DEC_SEQ, D_MODEL)
    state_s = state_s.reshape(1, nseq, GLA_HEADS, GLA_HEAD_K, GLA_HEAD_V)
    k_cache_s = k_cache_s.reshape(1, nseq, cache_w, SWA_KV_HEADS, SWA_HEAD_DIM)
    v_cache_s = v_cache_s.reshape(1, nseq, cache_w, SWA_KV_HEADS, SWA_HEAD_DIM)

    return (y_prompt, y_sample, state_p, k_cache_p, v_cache_p, state_s, k_cache_s, v_cache_s)
```

```python
import functools

import jax
import jax.numpy as jnp
from jax import lax
from jax.experimental import pallas as pl
from jax.experimental.pallas import tpu as pltpu

F32 = jnp.float32
BF16 = jnp.bfloat16

D_MODEL = 1024
D_FF = 2816
GLA_HEADS = 4
GLA_HEAD_K = 64
GLA_HEAD_V = 128
GLA_KEY_WIDTH = GLA_HEADS * GLA_HEAD_K
GLA_WIDTH = GLA_HEADS * GLA_HEAD_V
GLA_GATE_RANK = 16
GLA_GATE_TAU = 16.0
SWA_HEAD_DIM = 64
SWA_Q_HEADS = 8
SWA_KV_HEADS = 2
SWA_GROUP = SWA_Q_HEADS // SWA_KV_HEADS
SWA_WIDTH = SWA_Q_HEADS * SWA_HEAD_DIM
SWA_KV_WIDTH = SWA_KV_HEADS * SWA_HEAD_DIM
WINDOW = 128
DEC_SEQ = 4
NORM_EPS = 1e-6
HEAD_SCALE = 0.125

LANES = 128
SUBLANES = 8
VMEM_LIMIT_BYTES = 56 * 1024 * 1024

ROW_TILE = 512
FF_CHUNK = 256
DOWN_CHUNK = 256
GLA_CHUNK = 128
SAMPLE_SEQS = 32
SAMPLE_ROWS = SAMPLE_SEQS * DEC_SEQ
DECAY_CLAMP = 60.0

PROJ_Q_G = 0
PROJ_K_G = 256
PROJ_V_G = 512
PROJ_R_G = 1024
PROJ_Q_S = 1536
PROJ_KV_S = 2048
PROJ_A = 2304
PROJ_WIDTH = 2432


def _dot(a, b):
    return jnp.dot(a, b, preferred_element_type=F32)


def _dot_tb(a, b):
    return lax.dot_general(a, b, (((1,), (1,)), ((), ())), preferred_element_type=F32)


def _dot_ta(a, b):
    return lax.dot_general(a, b, (((0,), (0,)), ((), ())), preferred_element_type=F32)


def _rms(x, g):
    return x * lax.rsqrt(jnp.mean(x * x, axis=-1, keepdims=True) + NORM_EPS) * g


def _swiglu(h, wg_ref, wu_ref, wd_ref, act_ref, side_jobs=()):
    side_jobs = list(side_jobs)
    for c0 in range(0, D_FF, FF_CHUNK):
        g = _dot(h, wg_ref[:, c0:c0 + FF_CHUNK])
        u = _dot(h, wu_ref[:, c0:c0 + FF_CHUNK])
        act_ref[:, c0:c0 + FF_CHUNK] = (g * jax.nn.sigmoid(g) * u).astype(BF16)
        if side_jobs:
            side_jobs.pop(0)()
    outs = []
    for n0 in range(0, D_MODEL, DOWN_CHUNK):
        outs.append(_dot(act_ref[...], wd_ref[:, n0:n0 + DOWN_CHUNK]))
        if side_jobs:
            side_jobs.pop(0)()
    assert not side_jobs
    return jnp.concatenate(outs, axis=1)


def _head_norm_gate(o, r, gn):
    parts = []
    for h in range(GLA_HEADS):
        sl = slice(h * GLA_HEAD_V, (h + 1) * GLA_HEAD_V)
        parts.append(_rms(o[:, sl], gn[:, sl]))
    return (jnp.concatenate(parts, axis=1) * (r * jax.nn.sigmoid(r))).astype(BF16)


def _out_ffn(x1_ref, og, osw, wo_ref, n2_ref, wg_ref, wu_ref, wd_ref, nf_ref, y_ref, act_ref, side_jobs=()):
    mixed = jnp.concatenate([og, osw], axis=1)
    x2 = x1_ref[...] + _dot(mixed, wo_ref[...])
    h = _rms(x2, n2_ref[...]).astype(BF16)
    x3 = x2 + 0.5 * _swiglu(h, wg_ref, wu_ref, wd_ref, act_ref, side_jobs)
    y_ref[...] = _rms(x3, nf_ref[...])


def _stage_a_body(x_ref, n1_ref, wg_ref, wu_ref, wd_ref, nm_ref, win_ref, wgu_ref, bg_ref,
                  x1_ref, qk_ref, g_ref, v_ref, r_ref, qs_ref, kv_ref, act_ref):
    x = x_ref[...]
    h = _rms(x, n1_ref[...]).astype(BF16)
    x1 = x + 0.5 * _swiglu(h, wg_ref, wu_ref, wd_ref, act_ref)
    x1_ref[...] = x1
    h2 = _rms(x1, nm_ref[...]).astype(BF16)
    proj = _dot(h2, win_ref[...])
    qk_ref[:, 0:GLA_KEY_WIDTH] = proj[:, PROJ_Q_G:PROJ_K_G] * HEAD_SCALE
    qk_ref[:, GLA_KEY_WIDTH:] = proj[:, PROJ_K_G:PROJ_V_G]
    v_ref[...] = proj[:, PROJ_V_G:PROJ_R_G].astype(BF16)
    r_ref[...] = proj[:, PROJ_R_G:PROJ_Q_S]
    qs_ref[...] = (proj[:, PROJ_Q_S:PROJ_KV_S] * HEAD_SCALE).astype(BF16)
    kv_ref[...] = proj[:, PROJ_KV_S:PROJ_A]
    a = proj[:, PROJ_A:PROJ_WIDTH].astype(BF16)
    z = _dot(a, wgu_ref[...]) + bg_ref[...]
    g_ref[...] = jax.nn.log_sigmoid(z) * (1.0 / GLA_GATE_TAU)


def _resident(shape):
    return pl.BlockSpec(shape, lambda i: (0,) * len(shape), pipeline_mode=pl.Buffered(1))


def _rows(tm, n):
    return pl.BlockSpec((tm, n), lambda i: (i, 0))


def _stage_a(x, w):
    rows = x.shape[0]
    tm = min(ROW_TILE, rows)
    assert rows % tm == 0
    out_widths = ((D_MODEL, F32), (2 * GLA_KEY_WIDTH, F32), (GLA_KEY_WIDTH, F32), (GLA_WIDTH, BF16),
                  (GLA_WIDTH, F32), (SWA_WIDTH, BF16), (2 * SWA_KV_WIDTH, F32))
    return pl.pallas_call(
        _stage_a_body,
        grid=(rows // tm,),
        in_specs=[_rows(tm, D_MODEL), _resident((1, D_MODEL)),
                  _resident((D_MODEL, D_FF)), _resident((D_MODEL, D_FF)), _resident((D_FF, D_MODEL)),
                  _resident((1, D_MODEL)), _resident((D_MODEL, PROJ_WIDTH)),
                  _resident((LANES, GLA_KEY_WIDTH)), _resident((1, GLA_KEY_WIDTH))],
        out_specs=[_rows(tm, n) for n, _ in out_widths],
        out_shape=[jax.ShapeDtypeStruct((rows, n), dt) for n, dt in out_widths],
        scratch_shapes=[pltpu.VMEM((tm, D_FF), BF16)],
        compiler_params=pltpu.CompilerParams(dimension_semantics=("arbitrary",),
                                             vmem_limit_bytes=VMEM_LIMIT_BYTES),
        name="stage_a_ffn1_proj",
    )(x, w["n1"], w["wg1"], w["wu1"], w["wd1"], w["nm"], w["win"], w["wgu"], w["bg"])


def _stage_d_body(x1_ref, og_ref, os_ref, wo_ref, n2_ref, wg_ref, wu_ref, wd_ref, nf_ref, y_ref, act_ref):
    _out_ffn(x1_ref, og_ref[...], os_ref[...], wo_ref, n2_ref, wg_ref, wu_ref, wd_ref, nf_ref, y_ref, act_ref)


def _stage_d(x1, og, osw, w):
    rows = x1.shape[0]
    tm = min(ROW_TILE, rows)
    assert rows % tm == 0
    return pl.pallas_call(
        _stage_d_body,
        grid=(rows // tm,),
        in_specs=[_rows(tm, D_MODEL), _rows(tm, GLA_WIDTH), _rows(tm, SWA_WIDTH),
                  _resident((D_MODEL, D_MODEL)), _resident((1, D_MODEL)),
                  _resident((D_MODEL, D_FF)), _resident((D_MODEL, D_FF)), _resident((D_FF, D_MODEL)),
                  _resident((1, D_MODEL))],
        out_specs=_rows(tm, D_MODEL),
        out_shape=jax.ShapeDtypeStruct((rows, D_MODEL), F32),
        scratch_shapes=[pltpu.VMEM((tm, D_FF), BF16)],
        compiler_params=pltpu.CompilerParams(dimension_semantics=("arbitrary",),
                                             vmem_limit_bytes=VMEM_LIMIT_BYTES),
        name="stage_d_out_ffn2",
    )(x1, og, osw, w["wo"], w["n2"], w["wg2"], w["wu2"], w["wd2"], w["nf"])


def _split_bf16(x):
    hi = x.astype(BF16)
    lo = (x - hi.astype(F32)).astype(BF16)
    return hi, lo


def _gla_jobs(qk_ref, g_ref, v_ref, r_ref, gn_ref, s_ref, og_scr, cum_scr, inter_scr, flag_ref):
    c_len = GLA_CHUNK
    n_chunks = ROW_TILE // c_len
    causal = (lax.broadcasted_iota(jnp.int32, (c_len, c_len), 0)
              >= lax.broadcasted_iota(jnp.int32, (c_len, c_len), 1))
    ltri = jnp.where(causal, 1.0, 0.0).astype(BF16)
    causal_cat = (lax.broadcasted_iota(jnp.int32, (c_len, GLA_HEADS * c_len), 0)
                  >= lax.broadcasted_iota(jnp.int32, (c_len, GLA_HEADS * c_len), 1) % c_len)
    lane_head = lax.broadcasted_iota(jnp.int32, (1, GLA_KEY_WIDTH), 1) // GLA_HEAD_K
    row_head = lax.broadcasted_iota(jnp.int32, (GLA_KEY_WIDTH, 1), 0) // GLA_HEAD_K
    eye = (lax.broadcasted_iota(jnp.int32, (GLA_KEY_WIDTH, GLA_KEY_WIDTH), 0)
           == lax.broadcasted_iota(jnp.int32, (GLA_KEY_WIDTH, GLA_KEY_WIDTH), 1))
    zero_b = jnp.zeros((), BF16)
    zero_v = jnp.zeros((c_len, GLA_HEAD_V), BF16)
    worst = []

    def chunk(c):
        rows = slice(c * c_len, (c + 1) * c_len)
        g_hi, g_lo = _split_bf16(g_ref[rows, :])
        cum = _dot(ltri, g_hi) + _dot(ltri, g_lo)
        cum_scr[rows, :] = cum
        last = cum[c_len - 1:c_len, :]
        q = qk_ref[rows, 0:GLA_KEY_WIDTH]
        k = qk_ref[rows, GLA_KEY_WIDTH:]
        vb = v_ref[rows, :]
        qe = (q * jnp.exp(cum)).astype(BF16)
        ke = (k * jnp.exp(jnp.minimum(-cum, DECAY_CLAMP))).astype(BF16)
        kl = (k * jnp.exp(last - cum)).astype(BF16)
        state = s_ref[...]
        sb = state.astype(BF16)
        s_bd = jnp.concatenate([jnp.where(row_head == h, sb, zero_b) for h in range(GLA_HEADS)], axis=1)
        o_inter = _dot(qe, s_bd)
        inter_scr[rows, :] = o_inter
        ke_bd = jnp.concatenate([jnp.where(lane_head == h, ke, zero_b) for h in range(GLA_HEADS)], axis=0)
        attn = jnp.where(causal_cat, _dot_tb(qe, ke_bd), 0.0).astype(BF16)
        o_pairs, upds = [], []
        for p in range(GLA_HEADS // 2):
            v_a = vb[:, (2 * p) * GLA_HEAD_V:(2 * p + 1) * GLA_HEAD_V]
            v_b = vb[:, (2 * p + 1) * GLA_HEAD_V:(2 * p + 2) * GLA_HEAD_V]
            v_bd = jnp.concatenate([jnp.concatenate([v_a, zero_v], axis=1),
                                    jnp.concatenate([zero_v, v_b], axis=1)], axis=0)
            o_pairs.append(_dot(attn[:, 2 * p * c_len:(2 * p + 2) * c_len], v_bd))
            u = _dot_ta(kl[:, p * LANES:(p + 1) * LANES], vb[:, 2 * p * GLA_HEAD_V:(2 * p + 2) * GLA_HEAD_V])
            upds.append(u[0:GLA_HEAD_K, 0:GLA_HEAD_V])
            upds.append(u[GLA_HEAD_K:, GLA_HEAD_V:])
        og_scr[rows, :] = _head_norm_gate(o_inter + jnp.concatenate(o_pairs, axis=1), r_ref[rows, :],
                                          gn_ref[...])
        last_col = jnp.sum(jnp.where(eye, last, 0.0), axis=1, keepdims=True)
        s_ref[...] = state * jnp.exp(last_col) + jnp.concatenate(upds, axis=0)
        worst.append(jnp.min(last, axis=1, keepdims=True))
        if c == n_chunks - 1:
            tile_min = functools.reduce(jnp.minimum, worst)
            flag_ref[0] = jnp.where(tile_min[0, 0] < -DECAY_CLAMP, 1, 0)

    return [functools.partial(chunk, c) for c in range(n_chunks)]


def _gla_tile_exact(qk_ref, v_ref, r_ref, gn_ref, og_scr, o_scr, cum_scr, inter_scr):
    c_len = GLA_CHUNK
    ind = jnp.where(lax.broadcasted_iota(jnp.int32, (GLA_KEY_WIDTH, LANES), 0) // GLA_HEAD_K
                    == lax.broadcasted_iota(jnp.int32, (GLA_KEY_WIDTH, LANES), 1), 1.0, 0.0).astype(BF16)
    j_idx = lax.broadcasted_iota(jnp.int32, (c_len, 1), 0)
    for c in range(ROW_TILE // c_len):
        r0 = c * c_len
        rows = slice(r0, r0 + c_len)

        def one_row(i, carry, r0=r0, rows=rows):
            ci = cum_scr[pl.ds(r0 + i, 1), :]
            qi = qk_ref[pl.ds(r0 + i, 1), 0:GLA_KEY_WIDTH]
            kk = qk_ref[rows, GLA_KEY_WIDTH:]
            dec = jnp.exp(jnp.minimum(ci - cum_scr[rows, :], 0.0))
            a_cols = _dot(((qi * kk) * dec).astype(BF16), ind)
            outs = []
            for h in range(GLA_HEADS):
                w_col = jnp.where(j_idx <= i, a_cols[:, h:h + 1], 0.0)
                v_h = v_ref[rows, h * GLA_HEAD_V:(h + 1) * GLA_HEAD_V].astype(F32)
                outs.append(jnp.sum(w_col * v_h, axis=0, keepdims=True))
            o_scr[pl.ds(r0 + i, 1), :] = inter_scr[pl.ds(r0 + i, 1), :] + jnp.concatenate(outs, axis=1)
            return carry

        lax.fori_loop(0, c_len, one_row, 0)
    og_scr[...] = _head_norm_gate(o_scr[...], r_ref[...], gn_ref[...])


def _dup_halves(x):
    lo = lax.broadcasted_iota(jnp.int32, (1, LANES), 1) < SWA_HEAD_DIM
    sw = pltpu.roll(x, SWA_HEAD_DIM, 1)
    return jnp.where(lo, x, sw).astype(BF16), jnp.where(lo, sw, x).astype(BF16)


def _swa_jobs(sink_ref, qs_ref, kv_ref, kvprev_scr, sn_ref, os_scr, no_prev_bias):
    w = WINDOW
    n_blocks = ROW_TILE // w
    lo = lax.broadcasted_iota(jnp.int32, (1, LANES), 1) < SWA_HEAD_DIM
    tri = (lax.broadcasted_iota(jnp.int32, (w, w), 1) <= lax.broadcasted_iota(jnp.int32, (w, w), 0))
    zero_b = jnp.zeros((), BF16)

    def block(b):
        rows = slice(b * w, (b + 1) * w)
        kv_prev = kvprev_scr[...] if b == 0 else kv_ref[(b - 1) * w:b * w, :]
        kv = jnp.concatenate([kv_prev, kv_ref[rows, :]], axis=0)
        kk = _dup_halves(kv[:, 0:SWA_KV_WIDTH])
        vv = _dup_halves(kv[:, SWA_KV_WIDTH:])
        q = qs_ref[rows, :]
        tiles = []
        for grp in range(SWA_KV_HEADS):
            stacked = []
            for j in range(SWA_GROUP):
                hq = grp * SWA_GROUP + j
                tile = q[:, (hq // 2) * LANES:(hq // 2 + 1) * LANES]
                stacked.append(jnp.where(lo if hq % 2 == 0 else ~lo, tile, zero_b))
            s = _dot_tb(jnp.concatenate(stacked, axis=0), kk[grp])
            probs, dens = [], []
            for j in range(SWA_GROUP):
                sink = sink_ref[grp * SWA_GROUP + j]
                s_prev = s[j * w:(j + 1) * w, 0:w]
                if b == 0:
                    s_prev = s_prev + no_prev_bias
                sf = jnp.where(tri, s[j * w:(j + 1) * w, w:], s_prev)
                m = jnp.maximum(jnp.max(sf, axis=1, keepdims=True), sink)
                p = jnp.exp(sf - m)
                dens.append(jnp.sum(p, axis=1, keepdims=True) + jnp.exp(sink - m))
                pb = p.astype(BF16)
                probs.append(jnp.concatenate([jnp.where(tri, zero_b, pb), jnp.where(tri, pb, zero_b)], axis=1))
            o = _dot(jnp.concatenate(probs, axis=0), vv[grp])
            res = [o[j * w:(j + 1) * w] / dens[j] for j in range(SWA_GROUP)]
            tiles.append(jnp.where(lo, res[0], res[1]))
            tiles.append(jnp.where(lo, res[2], res[3]))
        os_scr[rows, :] = _rms(jnp.concatenate(tiles, axis=1), sn_ref[...]).astype(BF16)
        if b == n_blocks - 1:
            kvprev_scr[...] = kv_ref[rows, :]

    return [functools.partial(block, b) for b in range(n_blocks)]


def _mix_out_body(sink_ref, qk_ref, g_ref, v_ref, r_ref, qs_ref, kv_ref, x1_ref, gn_ref, sn_ref,
                  wo_ref, n2_ref, wg_ref, wu_ref, wd_ref, nf_ref, y_ref, s_ref,
                  og_scr, os_scr, kvprev_scr, o_scr, cum_scr, inter_scr, act_scr, flag_ref):
    i = pl.program_id(0)
    last_step = pl.num_programs(0) - 1

    def stage_d(side_jobs=()):
        _out_ffn(x1_ref, og_scr[...], os_scr[...], wo_ref, n2_ref, wg_ref, wu_ref, wd_ref, nf_ref,
                 y_ref, act_scr, side_jobs)

    def mixer_jobs():
        gla = _gla_jobs(qk_ref, g_ref, v_ref, r_ref, gn_ref, s_ref, og_scr, cum_scr, inter_scr, flag_ref)
        swa = _swa_jobs(sink_ref, qs_ref, kv_ref, kvprev_scr, sn_ref, os_scr,
                        jnp.where(i > 0, 0.0, -jnp.inf))
        return [job for pair in zip(gla, swa) for job in pair]

    @pl.when(i == 0)
    def _():
        s_ref[...] = jnp.zeros_like(s_ref)
        kvprev_scr[...] = jnp.zeros_like(kvprev_scr)
        for job in mixer_jobs():
            job()

    @pl.when(jnp.logical_and(i > 0, i < last_step))
    def _():
        stage_d(mixer_jobs())

    @pl.when(i == last_step)
    def _():
        stage_d()
        flag_ref[0] = 0

    @pl.when(flag_ref[0] != 0)
    def _():
        _gla_tile_exact(qk_ref, v_ref, r_ref, gn_ref, og_scr, o_scr, cum_scr, inter_scr)


def _mix_out(qk, g, v, r, qs, kv, x1, w):
    t = qk.shape[0]
    assert t % ROW_TILE == 0
    nt = t // ROW_TILE
    cur = lambda n: pl.BlockSpec((ROW_TILE, n), lambda i: (jnp.minimum(i, nt - 1), 0))
    prev = lambda n: pl.BlockSpec((ROW_TILE, n), lambda i: (jnp.maximum(i - 1, 0), 0))
    return pl.pallas_call(
        _mix_out_body,
        grid=(nt + 1,),
        in_specs=[pl.BlockSpec(memory_space=pltpu.SMEM),
                  cur(2 * GLA_KEY_WIDTH), cur(GLA_KEY_WIDTH), cur(GLA_WIDTH), cur(GLA_WIDTH),
                  cur(SWA_WIDTH), cur(2 * SWA_KV_WIDTH), prev(D_MODEL),
                  _resident((1, GLA_WIDTH)), _resident((1, SWA_WIDTH)),
                  _resident((D_MODEL, D_MODEL)), _resident((1, D_MODEL)),
                  _resident((D_MODEL, D_FF)), _resident((D_MODEL, D_FF)), _resident((D_FF, D_MODEL)),
                  _resident((1, D_MODEL))],
        out_specs=[prev(D_MODEL), pl.BlockSpec((GLA_KEY_WIDTH, GLA_HEAD_V), lambda i: (0, 0))],
        out_shape=[jax.ShapeDtypeStruct((t, D_MODEL), F32),
                   jax.ShapeDtypeStruct((GLA_KEY_WIDTH, GLA_HEAD_V), F32)],
        scratch_shapes=[pltpu.VMEM((ROW_TILE, GLA_WIDTH), BF16), pltpu.VMEM((ROW_TILE, SWA_WIDTH), BF16),
                        pltpu.VMEM((WINDOW, 2 * SWA_KV_WIDTH), F32),
                        pltpu.VMEM((ROW_TILE, GLA_WIDTH), F32), pltpu.VMEM((ROW_TILE, GLA_KEY_WIDTH), F32),
                        pltpu.VMEM((ROW_TILE, GLA_WIDTH), F32), pltpu.VMEM((ROW_TILE, D_FF), BF16),
                        pltpu.SMEM((1,), jnp.int32)],
        compiler_params=pltpu.CompilerParams(dimension_semantics=("arbitrary",),
                                             vmem_limit_bytes=VMEM_LIMIT_BYTES),
        name="prompt_mixers_stage_d",
    )(w["sinks"], qk, g, v, r, qs, kv, x1, w["gn"], w["sn"], w["wo"], w["n2"], w["wg2"], w["wu2"], w["wd2"],
      w["nf"])


def _gla_sample_body(qk_ref, g_ref, v_ref, r_ref, gn_ref, s_ref, og_ref, so_ref, o_scr):
    rows = SAMPLE_ROWS
    g = g_ref[...]
    q = qk_ref[:, 0:GLA_KEY_WIDTH]
    k = qk_ref[:, GLA_KEY_WIDTH:]
    vb = v_ref[...]
    vf = vb.astype(F32)
    tok = lax.broadcasted_iota(jnp.int32, (rows, 1), 0) % DEC_SEQ

    cum = g
    for d in range(1, DEC_SEQ):
        cum = cum + jnp.where(tok >= d, pltpu.roll(g, d, 0), 0.0)
    tot = jnp.where(tok == DEC_SEQ - 1, cum, 0.0)
    for d in range(1, DEC_SEQ):
        tot = tot + jnp.where(tok == DEC_SEQ - 1 - d, pltpu.roll(cum, rows - d, 0), 0.0)

    qe = q * jnp.exp(cum)
    kl = k * jnp.exp(tot - cum)
    decay_t = jnp.exp(tot).T

    ind = jnp.where(lax.broadcasted_iota(jnp.int32, (GLA_KEY_WIDTH, LANES), 0) // GLA_HEAD_K
                    == lax.broadcasted_iota(jnp.int32, (GLA_KEY_WIDTH, LANES), 1), 1.0, 0.0).astype(BF16)
    expand = jnp.where(lax.broadcasted_iota(jnp.int32, (LANES, GLA_WIDTH), 0)
                       == lax.broadcasted_iota(jnp.int32, (LANES, GLA_WIDTH), 1) // GLA_HEAD_V,
                       1.0, 0.0).astype(BF16)

    o_intra = jnp.zeros((rows, GLA_WIDTH), F32)
    for d in range(DEC_SEQ):
        k_d = k if d == 0 else pltpu.roll(k, d, 0)
        c_d = cum if d == 0 else pltpu.roll(cum, d, 0)
        v_d = vf if d == 0 else pltpu.roll(vf, d, 0)
        pair = jnp.where(tok >= d, q * k_d * jnp.exp(jnp.minimum(cum - c_d, 0.0)), 0.0)
        a = _dot(pair.astype(BF16), ind)
        o_intra = o_intra + _dot(a.astype(BF16), expand) * v_d

    lane_head = lax.broadcasted_iota(jnp.int32, (1, GLA_KEY_WIDTH), 1) // GLA_HEAD_K
    row8 = lax.broadcasted_iota(jnp.int32, (SUBLANES, 1), 0)
    row32 = lax.broadcasted_iota(jnp.int32, (GLA_HEADS * SUBLANES, 1), 0)
    for pair_idx in range(SAMPLE_SEQS // 2):
        r8 = slice(pair_idx * SUBLANES, (pair_idx + 1) * SUBLANES)
        q8 = qe[r8, :]
        lhs = jnp.concatenate([jnp.where(lane_head == h, q8, 0.0) for h in range(GLA_HEADS)],
                              axis=0).astype(BF16)
        kl8 = kl[r8, :]
        v8 = vb[r8, :]
        res = []
        for s in range(2):
            b = 2 * pair_idx + s
            state = s_ref[b]
            res.append(_dot(lhs, state.astype(BF16)))
            kl_b = jnp.where(row8 // DEC_SEQ == s, kl8, 0.0).astype(BF16)
            upd = _dot_ta(kl_b, v8)
            upd = jnp.concatenate(
                [upd[h * GLA_HEAD_K:(h + 1) * GLA_HEAD_K, h * GLA_HEAD_V:(h + 1) * GLA_HEAD_V]
                 for h in range(GLA_HEADS)], axis=0)
            so_ref[b] = state * decay_t[:, DEC_SEQ * b:DEC_SEQ * b + 1] + upd
        sel = jnp.where(row32 % SUBLANES < DEC_SEQ, res[0], res[1])
        o_scr[r8, :] = jnp.concatenate([sel[h * SUBLANES:(h + 1) * SUBLANES] for h in range(GLA_HEADS)], axis=1)

    og_ref[...] = _head_norm_gate(o_scr[...] + o_intra, r_ref[...], gn_ref[...])


def _gla_sample(qk, g, v, r, gn, state):
    rows = qk.shape[0]
    nseq = rows // DEC_SEQ
    assert nseq % SAMPLE_SEQS == 0
    blk = lambda n: pl.BlockSpec((SAMPLE_ROWS, n), lambda i: (i, 0))
    sblk = pl.BlockSpec((SAMPLE_SEQS, GLA_KEY_WIDTH, GLA_HEAD_V), lambda i: (i, 0, 0))
    return pl.pallas_call(
        _gla_sample_body,
        grid=(nseq // SAMPLE_SEQS,),
        in_specs=[blk(2 * GLA_KEY_WIDTH), blk(GLA_KEY_WIDTH), blk(GLA_WIDTH), blk(GLA_WIDTH),
                  pl.BlockSpec((1, GLA_WIDTH), lambda i: (0, 0)), sblk],
        out_specs=[blk(GLA_WIDTH), sblk],
        out_shape=[jax.ShapeDtypeStruct((rows, GLA_WIDTH), BF16),
                   jax.ShapeDtypeStruct((nseq, GLA_KEY_WIDTH, GLA_HEAD_V), F32)],
        scratch_shapes=[pltpu.VMEM((SAMPLE_ROWS, GLA_WIDTH), F32)],
        compiler_params=pltpu.CompilerParams(dimension_semantics=("arbitrary",)),
        name="gla_sample",
    )(qk, g, v, r, gn, state)


def _swa_sample_body(sink_ref, q_ref, kvn_ref, kc_ref, vc_ref, nrm_ref, o_ref, kco_ref, vco_ref,
                     lhs_scr, sc_scr, pc_scr, oc_scr):
    rows = SAMPLE_ROWS
    q = q_ref[...].astype(F32)
    lo = lax.broadcasted_iota(jnp.int32, (1, LANES), 1) < SWA_HEAD_DIM
    for hq in range(SWA_Q_HEADS):
        grp = hq // SWA_GROUP
        tile = q[:, (hq // 2) * LANES:(hq // 2 + 1) * LANES]
        src = tile if hq % 2 == grp else pltpu.roll(tile, SWA_HEAD_DIM, 1)
        lhs_scr[hq * rows:(hq + 1) * rows, :] = jnp.where(lo if grp == 0 else ~lo, src, 0.0)

    kvn = kvn_ref[...]
    k_new = kvn[:, 0:SWA_KV_WIDTH]
    v_new = kvn[:, SWA_KV_WIDTH:]
    s_new = _dot_tb(lhs_scr[...].astype(BF16), k_new.astype(BF16))

    row64 = lax.broadcasted_iota(jnp.int32, (SWA_Q_HEADS * SUBLANES, 1), 0)
    first_of_pair = row64 % SUBLANES < DEC_SEQ

    def gather_pair(ref, pair_idx):
        return jnp.concatenate(
            [ref[hq * rows + pair_idx * SUBLANES:hq * rows + (pair_idx + 1) * SUBLANES, :]
             for hq in range(SWA_Q_HEADS)], axis=0).astype(BF16)

    def scatter_pair(ref, pair_idx, val):
        for hq in range(SWA_Q_HEADS):
            ref[hq * rows + pair_idx * SUBLANES:hq * rows + (pair_idx + 1) * SUBLANES, :] = (
                val[hq * SUBLANES:(hq + 1) * SUBLANES])

    for pair_idx in range(SAMPLE_SEQS // 2):
        l64 = gather_pair(lhs_scr, pair_idx)
        sa = _dot_tb(l64, kc_ref[2 * pair_idx].astype(BF16))
        sb = _dot_tb(l64, kc_ref[2 * pair_idx + 1].astype(BF16))
        scatter_pair(sc_scr, pair_idx, jnp.where(first_of_pair, sa, sb))

    rr = lax.broadcasted_iota(jnp.int32, (rows, rows), 0)
    cc = lax.broadcasted_iota(jnp.int32, (rows, rows), 1)
    tok = rr % DEC_SEQ
    mask_cache = cc > tok
    mask_new = (cc // DEC_SEQ == rr // DEC_SEQ) & (cc % DEC_SEQ <= tok)
    p_new, dens = [], []
    for hq in range(SWA_Q_HEADS):
        sl = slice(hq * rows, (hq + 1) * rows)
        sink = sink_ref[hq]
        s_c = jnp.where(mask_cache, sc_scr[sl, :], -jnp.inf)
        s_n = jnp.where(mask_new, s_new[sl, :], -jnp.inf)
        m = jnp.maximum(jnp.maximum(jnp.max(s_c, axis=1, keepdims=True),
                                    jnp.max(s_n, axis=1, keepdims=True)), sink)
        p_c = jnp.exp(s_c - m)
        p_n = jnp.exp(s_n - m)
        dens.append(jnp.sum(p_c, axis=1, keepdims=True) + jnp.sum(p_n, axis=1, keepdims=True)
                    + jnp.exp(sink - m))
        pc_scr[sl, :] = p_c
        p_new.append(p_n.astype(BF16))
    o_new = _dot(jnp.concatenate(p_new, axis=0), v_new.astype(BF16))

    for pair_idx in range(SAMPLE_SEQS // 2):
        p64 = gather_pair(pc_scr, pair_idx)
        oa = _dot(p64, vc_ref[2 * pair_idx].astype(BF16))
        ob = _dot(p64, vc_ref[2 * pair_idx + 1].astype(BF16))
        scatter_pair(oc_scr, pair_idx, jnp.where(first_of_pair, oa, ob))

    tiles = []
    for i in range(SWA_Q_HEADS // 2):
        halves = []
        for hq in (2 * i, 2 * i + 1):
            sl = slice(hq * rows, (hq + 1) * rows)
            oh = (oc_scr[sl, :] + o_new[sl, :]) / dens[hq]
            halves.append(oh if hq % 2 == hq // SWA_GROUP else pltpu.roll(oh, SWA_HEAD_DIM, 1))
        tiles.append(jnp.where(lo, halves[0], halves[1]))
    o_ref[...] = _rms(jnp.concatenate(tiles, axis=1), nrm_ref[...]).astype(BF16)

    for b in range(SAMPLE_SEQS):
        kco_ref[b, 0:WINDOW - DEC_SEQ, :] = kc_ref[b, DEC_SEQ:WINDOW, :]
        vco_ref[b, 0:WINDOW - DEC_SEQ, :] = vc_ref[b, DEC_SEQ:WINDOW, :]
        kco_ref[b, WINDOW - DEC_SEQ:WINDOW, :] = k_new[DEC_SEQ * b:DEC_SEQ * (b + 1), :]
        vco_ref[b, WINDOW - DEC_SEQ:WINDOW, :] = v_new[DEC_SEQ * b:DEC_SEQ * (b + 1), :]


def _swa_sample(sinks, qs, kvn, kc, vc, nrm):
    rows = qs.shape[0]
    nseq = rows // DEC_SEQ
    assert nseq % SAMPLE_SEQS == 0 and kc.shape[1] == WINDOW
    cblk = pl.BlockSpec((SAMPLE_SEQS, WINDOW, SWA_KV_WIDTH), lambda i: (i, 0, 0))
    big = pltpu.VMEM((SWA_Q_HEADS * SAMPLE_ROWS, LANES), F32)
    return pl.pallas_call(
        _swa_sample_body,
        grid=(nseq // SAMPLE_SEQS,),
        in_specs=[pl.BlockSpec(memory_space=pltpu.SMEM),
                  pl.BlockSpec((SAMPLE_ROWS, SWA_WIDTH), lambda i: (i, 0)),
                  pl.BlockSpec((SAMPLE_ROWS, 2 * SWA_KV_WIDTH), lambda i: (i, 0)),
                  cblk, cblk,
                  pl.BlockSpec((1, SWA_WIDTH), lambda i: (0, 0))],
        out_specs=[pl.BlockSpec((SAMPLE_ROWS, SWA_WIDTH), lambda i: (i, 0)), cblk, cblk],
        out_shape=[jax.ShapeDtypeStruct((rows, SWA_WIDTH), BF16),
                   jax.ShapeDtypeStruct(kc.shape, F32), jax.ShapeDtypeStruct(vc.shape, F32)],
        scratch_shapes=[big, big, big, big],
        compiler_params=pltpu.CompilerParams(dimension_semantics=("arbitrary",)),
        name="swa_sample",
    )(sinks, qs, kvn, kc, vc, nrm)


def _prepare_weights(ffn1_norm, ffn1_w_gate, ffn1_w_up, ffn1_w_down, mix_norm, w_in, w_gate_up, b_gate,
                     gla_head_norm, swa_out_norm, swa_sinks, w_out,
                     ffn2_norm, ffn2_w_gate, ffn2_w_up, ffn2_w_down, final_norm, layer):
    a0 = 2 * GLA_KEY_WIDTH + 2 * GLA_WIDTH
    a1 = a0 + GLA_GATE_RANK
    win = w_in[layer]
    win = jnp.concatenate([win[:, :a0], win[:, a1:], win[:, a0:a1],
                           jnp.zeros((D_MODEL, LANES - GLA_GATE_RANK), win.dtype)], axis=1)
    wgu = jnp.concatenate([w_gate_up[layer],
                           jnp.zeros((LANES - GLA_GATE_RANK, GLA_KEY_WIDTH), w_gate_up.dtype)], axis=0)
    row = lambda a: a.reshape(1, -1).astype(F32)
    return dict(
        n1=row(ffn1_norm[layer]), wg1=ffn1_w_gate[layer].astype(BF16), wu1=ffn1_w_up[layer].astype(BF16),
        wd1=ffn1_w_down[layer].astype(BF16), nm=row(mix_norm[layer]), win=win.astype(BF16),
        wgu=wgu.astype(BF16), bg=row(b_gate[layer]), gn=row(gla_head_norm[layer]),
        sn=row(swa_out_norm[layer]), sinks=swa_sinks[layer].astype(F32), wo=w_out[layer].astype(BF16),
        n2=row(ffn2_norm[layer]), wg2=ffn2_w_gate[layer].astype(BF16), wu2=ffn2_w_up[layer].astype(BF16),
        wd2=ffn2_w_down[layer].astype(BF16), nf=row(final_norm))


def kernel(x_prompt, x_sample, state_gla, cache_swa_k, cache_swa_v, ffn1_norm, ffn1_w_gate, ffn1_w_up,
           ffn1_w_down, mix_norm, w_in, w_gate_up, b_gate, gla_head_norm, swa_out_norm, swa_sinks, w_out,
           ffn2_norm, ffn2_w_gate, ffn2_w_up, ffn2_w_down, final_norm):
    depth = state_gla.shape[0]
    assert depth == 1 and x_prompt.shape[0] == 1 and x_sample.shape[1] == DEC_SEQ
    seq = x_prompt.shape[1]
    nseq = x_sample.shape[0]
    w = _prepare_weights(ffn1_norm, ffn1_w_gate, ffn1_w_up, ffn1_w_down, mix_norm, w_in, w_gate_up, b_gate,
                         gla_head_norm, swa_out_norm, swa_sinks, w_out,
                         ffn2_norm, ffn2_w_gate, ffn2_w_up, ffn2_w_down, final_norm, 0)

    x1, qk, g, v, r, qs, kv = _stage_a(x_prompt.reshape(seq, D_MODEL), w)
    y_prompt, state_p = _mix_out(qk, g, v, r, qs, kv, x1, w)
    y_prompt = y_prompt.reshape(1, seq, D_MODEL)
    cw = min(WINDOW, seq)
    k_cache_p = kv[seq - cw:, 0:SWA_KV_WIDTH].reshape(1, 1, cw, SWA_KV_HEADS, SWA_HEAD_DIM)
    v_cache_p = kv[seq - cw:, SWA_KV_WIDTH:].reshape(1, 1, cw, SWA_KV_HEADS, SWA_HEAD_DIM)
    state_p = state_p.reshape(1, 1, GLA_HEADS, GLA_HEAD_K, GLA_HEAD_V)

    x1, qk, g, v, r, qs, kv = _stage_a(x_sample.reshape(nseq * DEC_SEQ, D_MODEL), w)
    og, state_s = _gla_sample(qk, g, v, r, w["gn"],
                              state_gla[0].reshape(nseq, GLA_KEY_WIDTH, GLA_HEAD_V))
    cache_w = cache_swa_k.shape[2]
    osw, k_cache_s, v_cache_s = _swa_sample(
        w["sinks"], qs, kv, cache_swa_k[0].reshape(nseq, cache_w, SWA_KV_WIDTH),
        cache_swa_v[0].reshape(nseq, cache_w, SWA_KV_WIDTH), w["sn"])
    y_sample = _stage_d(x1, og, osw, w).reshape(nseq, DEC_SEQ, D_MODEL)
    state_s = state_s.reshape(1, nseq, GLA_HEADS, GLA_HEAD_K, GLA_HEAD_V)
    k_cache_s = k_cache_s.reshape(1, nseq, cache_w, SWA_KV_HEADS, SWA_HEAD_DIM)
    v_cache_s = v_cache_s.reshape(1, nseq, cache_w, SWA_KV_HEADS, SWA_HEAD_DIM)

    return (y_prompt, y_sample, state_p, k_cache_p, v_cache_p, state_s, k_cache_s, v_cache_s)
```

```python
import functools
import itertools

import jax
import jax.numpy as jnp
from jax import lax
from jax.experimental import pallas as pl
from jax.experimental.pallas import tpu as pltpu

F32 = jnp.float32
BF16 = jnp.bfloat16

D_MODEL = 1024
D_FF = 2816
GLA_HEADS = 4
GLA_HEAD_K = 64
GLA_HEAD_V = 128
GLA_KEY_WIDTH = GLA_HEADS * GLA_HEAD_K
GLA_WIDTH = GLA_HEADS * GLA_HEAD_V
GLA_GATE_RANK = 16
GLA_GATE_TAU = 16.0
SWA_HEAD_DIM = 64
SWA_Q_HEADS = 8
SWA_KV_HEADS = 2
SWA_GROUP = SWA_Q_HEADS // SWA_KV_HEADS
SWA_WIDTH = SWA_Q_HEADS * SWA_HEAD_DIM
SWA_KV_WIDTH = SWA_KV_HEADS * SWA_HEAD_DIM
WINDOW = 128
DEC_SEQ = 4
NORM_EPS = 1e-6
HEAD_SCALE = 0.125

LANES = 128
SUBLANES = 8
VMEM_LIMIT_BYTES = 56 * 1024 * 1024

ROW_TILE = 512
FF_CHUNK = 256
DOWN_CHUNK = 256
GLA_CHUNK = 128
SAMPLE_SEQS = 32
SAMPLE_ROWS = SAMPLE_SEQS * DEC_SEQ
MIXER_STAGES = 3 * (ROW_TILE // GLA_CHUNK) + 3 * (ROW_TILE // WINDOW)
SIDE_STAGES_AT_NORM = 2
DECAY_CLAMP = 60.0

PROJ_Q_G = 0
PROJ_K_G = 256
PROJ_V_G = 512
PROJ_R_G = 1024
PROJ_Q_S = 1536
PROJ_KV_S = 2048
PROJ_A = 2304
PROJ_WIDTH = 2432


def _dot(a, b):
    return jnp.dot(a, b, preferred_element_type=F32)


def _dot_tb(a, b):
    return lax.dot_general(a, b, (((1,), (1,)), ((), ())), preferred_element_type=F32)


def _dot_ta(a, b):
    return lax.dot_general(a, b, (((0,), (0,)), ((), ())), preferred_element_type=F32)


def _rms(x, g):
    return x * lax.rsqrt(jnp.mean(x * x, axis=-1, keepdims=True) + NORM_EPS) * g


def _swiglu(h, wg_ref, wu_ref, wd_ref, act_ref, side_stages=iter(()), n_side_stages=0):
    n_dots = 2 * (D_FF // FF_CHUNK) + D_MODEL // DOWN_CHUNK
    done = [0, 0]

    def after_dot():
        done[0] += 1
        while done[1] * n_dots < done[0] * n_side_stages:
            next(side_stages, None)
            done[1] += 1

    for c0 in range(0, D_FF, FF_CHUNK):
        g = _dot(h, wg_ref[:, c0:c0 + FF_CHUNK])
        after_dot()
        u = _dot(h, wu_ref[:, c0:c0 + FF_CHUNK])
        after_dot()
        act_ref[:, c0:c0 + FF_CHUNK] = (g * jax.nn.sigmoid(g) * u).astype(BF16)
    outs = []
    for n0 in range(0, D_MODEL, DOWN_CHUNK):
        outs.append(_dot(act_ref[...], wd_ref[:, n0:n0 + DOWN_CHUNK]))
        after_dot()
    return jnp.concatenate(outs, axis=1)


def _head_norm_gate(o, r, gn):
    parts = []
    for h in range(GLA_HEADS):
        sl = slice(h * GLA_HEAD_V, (h + 1) * GLA_HEAD_V)
        parts.append(_rms(o[:, sl], gn[:, sl]))
    return (jnp.concatenate(parts, axis=1) * (r * jax.nn.sigmoid(r))).astype(BF16)


def _out_ffn(x1_ref, og, osw, wo_ref, n2_ref, wg_ref, wu_ref, wd_ref, nf_ref, y_ref, act_ref, side_stages=None):
    n_inside = 0 if side_stages is None else MIXER_STAGES - 2 * SIDE_STAGES_AT_NORM
    side_stages = iter(()) if side_stages is None else side_stages
    mixed = jnp.concatenate([og, osw], axis=1)
    x2 = x1_ref[...] + _dot(mixed, wo_ref[...])
    for _ in range(SIDE_STAGES_AT_NORM):
        next(side_stages, None)
    h = _rms(x2, n2_ref[...]).astype(BF16)
    x3 = x2 + 0.5 * _swiglu(h, wg_ref, wu_ref, wd_ref, act_ref, side_stages, n_inside)
    for _ in side_stages:
        pass
    y_ref[...] = _rms(x3, nf_ref[...])


_HBM = pl.BlockSpec(memory_space=pl.ANY)
WIDE_CHUNK_ROWS = 128
NARROW_CHUNK_ROWS = 256


def _load_weights_bf16(pairs):
    jobs = []
    for src, dst in pairs:
        rows, cols = src.shape
        assert cols in (D_FF, D_MODEL) and dst.shape == src.shape
        kind, step = (0, WIDE_CHUNK_ROWS) if cols == D_FF else (1, NARROW_CHUNK_ROWS)
        assert rows % step == 0
        jobs += [(src, dst, r0, step, kind) for r0 in range(0, rows, step)]

    def body(wide, narrow, sems):
        staging = (wide, narrow)
        used = [0, 0]
        copies = []
        for src, _, r0, nr, kind in jobs:
            slot = used[kind] % 2
            used[kind] += 1
            copies.append((pltpu.make_async_copy(src.at[pl.ds(r0, nr), :], staging[kind].at[slot],
                                                 sems.at[kind, slot]), slot))
        copies[0][0].start()
        for j, (_, dst, r0, nr, kind) in enumerate(jobs):
            if j + 1 < len(jobs):
                copies[j + 1][0].start()
            copy, slot = copies[j]
            copy.wait()
            dst[r0:r0 + nr, :] = staging[kind][slot].astype(BF16)

    pl.run_scoped(body,
                  pltpu.VMEM((2, WIDE_CHUNK_ROWS, D_FF), F32),
                  pltpu.VMEM((2, NARROW_CHUNK_ROWS, D_MODEL), F32),
                  pltpu.SemaphoreType.DMA((2, 2)))


def _stage_a_body(xp_ref, xs_ref, n1_ref, wg_hbm, wu_hbm, wd_hbm, nm_ref, win_ref, wgu_ref, bg_ref,
                  x1_ref, qk_ref, g_ref, v_ref, r_ref, qs_ref, kv_ref, act_ref, wg_ref, wu_ref, wd_ref):
    i = pl.program_id(0)

    @pl.when(i == 0)
    def _():
        _load_weights_bf16([(wg_hbm, wg_ref), (wu_hbm, wu_ref), (wd_hbm, wd_ref)])

    x = jnp.where(i < pl.num_programs(0) - 1, xp_ref[...], xs_ref[...])
    h = _rms(x, n1_ref[...]).astype(BF16)
    x1 = x + 0.5 * _swiglu(h, wg_ref, wu_ref, wd_ref, act_ref)
    x1_ref[...] = x1
    h2 = _rms(x1, nm_ref[...]).astype(BF16)
    proj = _dot(h2, win_ref[...])
    qk_ref[:, 0:GLA_KEY_WIDTH] = proj[:, PROJ_Q_G:PROJ_K_G] * HEAD_SCALE
    qk_ref[:, GLA_KEY_WIDTH:] = proj[:, PROJ_K_G:PROJ_V_G]
    v_ref[...] = proj[:, PROJ_V_G:PROJ_R_G].astype(BF16)
    r_ref[...] = proj[:, PROJ_R_G:PROJ_Q_S]
    qs_ref[...] = (proj[:, PROJ_Q_S:PROJ_KV_S] * HEAD_SCALE).astype(BF16)
    kv_ref[...] = proj[:, PROJ_KV_S:PROJ_A]
    a = proj[:, PROJ_A:PROJ_WIDTH].astype(BF16)
    z = _dot(a, wgu_ref[...]) + bg_ref[...]
    g_ref[...] = jax.nn.log_sigmoid(z) * (1.0 / GLA_GATE_TAU)


def _resident(shape):
    return pl.BlockSpec(shape, lambda i: (0,) * len(shape), pipeline_mode=pl.Buffered(1))


def _rows(tm, n):
    return pl.BlockSpec((tm, n), lambda i: (i, 0))


def _stage_a(xp, xs, w):
    t = xp.shape[0]
    assert t % ROW_TILE == 0 and xs.shape[0] == ROW_TILE
    nt = t // ROW_TILE
    out_widths = ((D_MODEL, F32), (2 * GLA_KEY_WIDTH, F32), (GLA_KEY_WIDTH, F32), (GLA_WIDTH, BF16),
                  (GLA_WIDTH, F32), (SWA_WIDTH, BF16), (2 * SWA_KV_WIDTH, F32))
    return pl.pallas_call(
        _stage_a_body,
        grid=(nt + 1,),
        in_specs=[pl.BlockSpec((ROW_TILE, D_MODEL), lambda i: (jnp.minimum(i, nt - 1), 0)),
                  _resident((ROW_TILE, D_MODEL)), _resident((1, D_MODEL)),
                  _HBM, _HBM, _HBM,
                  _resident((1, D_MODEL)), _resident((D_MODEL, PROJ_WIDTH)),
                  _resident((LANES, GLA_KEY_WIDTH)), _resident((1, GLA_KEY_WIDTH))],
        out_specs=[_rows(ROW_TILE, n) for n, _ in out_widths],
        out_shape=[jax.ShapeDtypeStruct((t + ROW_TILE, n), dt) for n, dt in out_widths],
        scratch_shapes=[pltpu.VMEM((ROW_TILE, D_FF), BF16), pltpu.VMEM((D_MODEL, D_FF), BF16),
                        pltpu.VMEM((D_MODEL, D_FF), BF16), pltpu.VMEM((D_FF, D_MODEL), BF16)],
        compiler_params=pltpu.CompilerParams(dimension_semantics=("arbitrary",),
                                             vmem_limit_bytes=VMEM_LIMIT_BYTES),
        name="stage_a_ffn1_proj",
    )(xp, xs, w["n1"], w["wg1"], w["wu1"], w["wd1"], w["nm"], w["win"], w["wgu"], w["bg"])


def _split_bf16(x):
    hi = x.astype(BF16)
    lo = (x - hi.astype(F32)).astype(BF16)
    return hi, lo


def _alternate(a, b):
    pending = [iter(a), iter(b)]
    while pending:
        for it in list(pending):
            try:
                next(it)
            except StopIteration:
                pending.remove(it)
                continue
            yield


def _gla_stages(qk_ref, g_ref, v_ref, r_ref, gn_ref, s_ref, og_scr, cum_scr, inter_scr, flag_ref):
    c_len = GLA_CHUNK
    n_chunks = ROW_TILE // c_len
    causal = (lax.broadcasted_iota(jnp.int32, (c_len, c_len), 0)
              >= lax.broadcasted_iota(jnp.int32, (c_len, c_len), 1))
    ltri = jnp.where(causal, 1.0, 0.0).astype(BF16)
    causal_cat = (lax.broadcasted_iota(jnp.int32, (c_len, GLA_HEADS * c_len), 0)
                  >= lax.broadcasted_iota(jnp.int32, (c_len, GLA_HEADS * c_len), 1) % c_len)
    lane_head = lax.broadcasted_iota(jnp.int32, (1, GLA_KEY_WIDTH), 1) // GLA_HEAD_K
    row_head = lax.broadcasted_iota(jnp.int32, (GLA_KEY_WIDTH, 1), 0) // GLA_HEAD_K
    eye = (lax.broadcasted_iota(jnp.int32, (GLA_KEY_WIDTH, GLA_KEY_WIDTH), 0)
           == lax.broadcasted_iota(jnp.int32, (GLA_KEY_WIDTH, GLA_KEY_WIDTH), 1))
    zero_b = jnp.zeros((), BF16)
    zero_v = jnp.zeros((c_len, GLA_HEAD_V), BF16)
    worst = []

    def chunk(c):
        rows = slice(c * c_len, (c + 1) * c_len)
        g_hi, g_lo = _split_bf16(g_ref[rows, :])
        cum = _dot(ltri, g_hi) + _dot(ltri, g_lo)
        yield
        cum_scr[rows, :] = cum
        last = cum[c_len - 1:c_len, :]
        q = qk_ref[rows, 0:GLA_KEY_WIDTH]
        k = qk_ref[rows, GLA_KEY_WIDTH:]
        vb = v_ref[rows, :]
        qe = (q * jnp.exp(cum)).astype(BF16)
        ke = (k * jnp.exp(jnp.minimum(-cum, DECAY_CLAMP))).astype(BF16)
        kl = (k * jnp.exp(last - cum)).astype(BF16)
        state = s_ref[...]
        sb = state.astype(BF16)
        s_bd = jnp.concatenate([jnp.where(row_head == h, sb, zero_b) for h in range(GLA_HEADS)], axis=1)
        o_inter = _dot(qe, s_bd)
        inter_scr[rows, :] = o_inter
        ke_bd = jnp.concatenate([jnp.where(lane_head == h, ke, zero_b) for h in range(GLA_HEADS)], axis=0)
        attn = _dot_tb(qe, ke_bd)
        upds = []
        for p in range(GLA_HEADS // 2):
            u = _dot_ta(kl[:, p * LANES:(p + 1) * LANES], vb[:, 2 * p * GLA_HEAD_V:(2 * p + 2) * GLA_HEAD_V])
            upds.append(u[0:GLA_HEAD_K, 0:GLA_HEAD_V])
            upds.append(u[GLA_HEAD_K:, GLA_HEAD_V:])
        yield
        attn = jnp.where(causal_cat, attn, 0.0).astype(BF16)
        o_pairs = []
        for p in range(GLA_HEADS // 2):
            v_a = vb[:, (2 * p) * GLA_HEAD_V:(2 * p + 1) * GLA_HEAD_V]
            v_b = vb[:, (2 * p + 1) * GLA_HEAD_V:(2 * p + 2) * GLA_HEAD_V]
            v_bd = jnp.concatenate([jnp.concatenate([v_a, zero_v], axis=1),
                                    jnp.concatenate([zero_v, v_b], axis=1)], axis=0)
            o_pairs.append(_dot(attn[:, 2 * p * c_len:(2 * p + 2) * c_len], v_bd))
        og_scr[rows, :] = _head_norm_gate(o_inter + jnp.concatenate(o_pairs, axis=1), r_ref[rows, :],
                                          gn_ref[...])
        last_col = jnp.sum(jnp.where(eye, last, 0.0), axis=1, keepdims=True)
        s_ref[...] = state * jnp.exp(last_col) + jnp.concatenate(upds, axis=0)
        worst.append(jnp.min(last, axis=1, keepdims=True))
        if c == n_chunks - 1:
            tile_min = functools.reduce(jnp.minimum, worst)
            flag_ref[0] = jnp.where(tile_min[0, 0] < -DECAY_CLAMP, 1, 0)
        yield

    return itertools.chain.from_iterable(chunk(c) for c in range(n_chunks))


def _gla_tile_exact(qk_ref, v_ref, r_ref, gn_ref, og_scr, o_scr, cum_scr, inter_scr):
    c_len = GLA_CHUNK
    ind = jnp.where(lax.broadcasted_iota(jnp.int32, (GLA_KEY_WIDTH, LANES), 0) // GLA_HEAD_K
                    == lax.broadcasted_iota(jnp.int32, (GLA_KEY_WIDTH, LANES), 1), 1.0, 0.0).astype(BF16)
    j_idx = lax.broadcasted_iota(jnp.int32, (c_len, 1), 0)
    for c in range(ROW_TILE // c_len):
        r0 = c * c_len
        rows = slice(r0, r0 + c_len)

        def one_row(i, carry, r0=r0, rows=rows):
            ci = cum_scr[pl.ds(r0 + i, 1), :]
            qi = qk_ref[pl.ds(r0 + i, 1), 0:GLA_KEY_WIDTH]
            kk = qk_ref[rows, GLA_KEY_WIDTH:]
            dec = jnp.exp(jnp.minimum(ci - cum_scr[rows, :], 0.0))
            a_cols = _dot(((qi * kk) * dec).astype(BF16), ind)
            outs = []
            for h in range(GLA_HEADS):
                w_col = jnp.where(j_idx <= i, a_cols[:, h:h + 1], 0.0)
                v_h = v_ref[rows, h * GLA_HEAD_V:(h + 1) * GLA_HEAD_V].astype(F32)
                outs.append(jnp.sum(w_col * v_h, axis=0, keepdims=True))
            o_scr[pl.ds(r0 + i, 1), :] = inter_scr[pl.ds(r0 + i, 1), :] + jnp.concatenate(outs, axis=1)
            return carry

        lax.fori_loop(0, c_len, one_row, 0)
    og_scr[...] = _head_norm_gate(o_scr[...], r_ref[...], gn_ref[...])


def _dup_halves(x):
    lo = lax.broadcasted_iota(jnp.int32, (1, LANES), 1) < SWA_HEAD_DIM
    sw = pltpu.roll(x, SWA_HEAD_DIM, 1)
    return jnp.where(lo, x, sw).astype(BF16), jnp.where(lo, sw, x).astype(BF16)


def _swa_stages(sink_ref, qs_ref, kv_ref, kvprev_scr, sn_ref, os_scr, no_prev_bias):
    w = WINDOW
    n_blocks = ROW_TILE // w
    lo = lax.broadcasted_iota(jnp.int32, (1, LANES), 1) < SWA_HEAD_DIM
    tri = (lax.broadcasted_iota(jnp.int32, (w, w), 1) <= lax.broadcasted_iota(jnp.int32, (w, w), 0))
    zero_b = jnp.zeros((), BF16)

    def block(b):
        rows = slice(b * w, (b + 1) * w)
        kv_prev = kvprev_scr[...] if b == 0 else kv_ref[(b - 1) * w:b * w, :]
        kv = jnp.concatenate([kv_prev, kv_ref[rows, :]], axis=0)
        kk = _dup_halves(kv[:, 0:SWA_KV_WIDTH])
        vv = _dup_halves(kv[:, SWA_KV_WIDTH:])
        q = qs_ref[rows, :]
        scores = []
        for grp in range(SWA_KV_HEADS):
            stacked = []
            for j in range(SWA_GROUP):
                hq = grp * SWA_GROUP + j
                tile = q[:, (hq // 2) * LANES:(hq // 2 + 1) * LANES]
                stacked.append(jnp.where(lo if hq % 2 == 0 else ~lo, tile, zero_b))
            scores.append(_dot_tb(jnp.concatenate(stacked, axis=0), kk[grp]))
        yield
        outs, dens = [], []
        for grp in range(SWA_KV_HEADS):
            s = scores[grp]
            probs = []
            for j in range(SWA_GROUP):
                sink = sink_ref[grp * SWA_GROUP + j]
                s_prev = s[j * w:(j + 1) * w, 0:w]
                if b == 0:
                    s_prev = s_prev + no_prev_bias
                sf = jnp.where(tri, s[j * w:(j + 1) * w, w:], s_prev)
                m = jnp.maximum(jnp.max(sf, axis=1, keepdims=True), sink)
                p = jnp.exp(sf - m)
                dens.append(jnp.sum(p, axis=1, keepdims=True) + jnp.exp(sink - m))
                pb = p.astype(BF16)
                probs.append(jnp.concatenate([jnp.where(tri, zero_b, pb), jnp.where(tri, pb, zero_b)], axis=1))
            outs.append(_dot(jnp.concatenate(probs, axis=0), vv[grp]))
        yield
        tiles = []
        for grp in range(SWA_KV_HEADS):
            res = [outs[grp][j * w:(j + 1) * w] / dens[grp * SWA_GROUP + j] for j in range(SWA_GROUP)]
            tiles.append(jnp.where(lo, res[0], res[1]))
            tiles.append(jnp.where(lo, res[2], res[3]))
        os_scr[rows, :] = _rms(jnp.concatenate(tiles, axis=1), sn_ref[...]).astype(BF16)
        if b == n_blocks - 1:
            kvprev_scr[...] = kv_ref[rows, :]
        yield

    return itertools.chain.from_iterable(block(b) for b in range(n_blocks))


def _mix_out_body(sink_ref, qk_ref, g_ref, v_ref, r_ref, qs_ref, kv_ref, x1_ref, ogs_ref, oss_ref,
                  gn_ref, sn_ref, wo_hbm, n2_ref, wg_hbm, wu_hbm, wd_hbm, nf_ref, yp_ref, ys_ref, s_ref,
                  og_scr, os_scr, kvprev_scr, o_scr, cum_scr, inter_scr, act_scr,
                  wo_ref, wg_ref, wu_ref, wd_ref, flag_ref):
    i = pl.program_id(0)
    n_prompt = pl.num_programs(0) - 2

    def stage_d(og, osw, y_ref, side_stages=None):
        _out_ffn(x1_ref, og, osw, wo_ref, n2_ref, wg_ref, wu_ref, wd_ref, nf_ref, y_ref, act_scr,
                 side_stages)

    def mixer_stages():
        gla = _gla_stages(qk_ref, g_ref, v_ref, r_ref, gn_ref, s_ref, og_scr, cum_scr, inter_scr, flag_ref)
        swa = _swa_stages(sink_ref, qs_ref, kv_ref, kvprev_scr, sn_ref, os_scr,
                          jnp.where(i > 0, 0.0, -jnp.inf))
        return _alternate(gla, swa)

    @pl.when(i == 0)
    def _():
        _load_weights_bf16([(wo_hbm, wo_ref), (wg_hbm, wg_ref), (wu_hbm, wu_ref), (wd_hbm, wd_ref)])
        s_ref[...] = jnp.zeros_like(s_ref)
        kvprev_scr[...] = jnp.zeros_like(kvprev_scr)
        for _ in mixer_stages():
            pass

    @pl.when(jnp.logical_and(i > 0, i < n_prompt))
    def _():
        stage_d(og_scr[...], os_scr[...], yp_ref, mixer_stages())

    @pl.when(i == n_prompt)
    def _():
        stage_d(og_scr[...], os_scr[...], yp_ref)
        flag_ref[0] = 0

    @pl.when(i == n_prompt + 1)
    def _():
        stage_d(ogs_ref[...], oss_ref[...], ys_ref)

    @pl.when(flag_ref[0] != 0)
    def _():
        _gla_tile_exact(qk_ref, v_ref, r_ref, gn_ref, og_scr, o_scr, cum_scr, inter_scr)


def _mix_out(qk, g, v, r, qs, kv, x1, og_s, os_s, w):
    t = qk.shape[0] - ROW_TILE
    assert t % ROW_TILE == 0 and og_s.shape[0] == ROW_TILE
    nt = t // ROW_TILE
    cur = lambda n: pl.BlockSpec((ROW_TILE, n), lambda i: (jnp.minimum(i, nt - 1), 0))
    return pl.pallas_call(
        _mix_out_body,
        grid=(nt + 2,),
        in_specs=[pl.BlockSpec(memory_space=pltpu.SMEM),
                  cur(2 * GLA_KEY_WIDTH), cur(GLA_KEY_WIDTH), cur(GLA_WIDTH), cur(GLA_WIDTH),
                  cur(SWA_WIDTH), cur(2 * SWA_KV_WIDTH),
                  pl.BlockSpec((ROW_TILE, D_MODEL), lambda i: (jnp.maximum(i - 1, 0), 0)),
                  _resident((ROW_TILE, GLA_WIDTH)), _resident((ROW_TILE, SWA_WIDTH)),
                  _resident((1, GLA_WIDTH)), _resident((1, SWA_WIDTH)),
                  _HBM, _resident((1, D_MODEL)), _HBM, _HBM, _HBM, _resident((1, D_MODEL))],
        out_specs=[pl.BlockSpec((ROW_TILE, D_MODEL), lambda i: (jnp.clip(i - 1, 0, nt - 1), 0)),
                   pl.BlockSpec((ROW_TILE, D_MODEL), lambda i: (0, 0)),
                   pl.BlockSpec((GLA_KEY_WIDTH, GLA_HEAD_V), lambda i: (0, 0))],
        out_shape=[jax.ShapeDtypeStruct((t, D_MODEL), F32),
                   jax.ShapeDtypeStruct((ROW_TILE, D_MODEL), F32),
                   jax.ShapeDtypeStruct((GLA_KEY_WIDTH, GLA_HEAD_V), F32)],
        scratch_shapes=[pltpu.VMEM((ROW_TILE, GLA_WIDTH), BF16), pltpu.VMEM((ROW_TILE, SWA_WIDTH), BF16),
                        pltpu.VMEM((WINDOW, 2 * SWA_KV_WIDTH), F32),
                        pltpu.VMEM((ROW_TILE, GLA_WIDTH), F32), pltpu.VMEM((ROW_TILE, GLA_KEY_WIDTH), F32),
                        pltpu.VMEM((ROW_TILE, GLA_WIDTH), F32), pltpu.VMEM((ROW_TILE, D_FF), BF16),
                        pltpu.VMEM((D_MODEL, D_MODEL), BF16), pltpu.VMEM((D_MODEL, D_FF), BF16),
                        pltpu.VMEM((D_MODEL, D_FF), BF16), pltpu.VMEM((D_FF, D_MODEL), BF16),
                        pltpu.SMEM((1,), jnp.int32)],
        compiler_params=pltpu.CompilerParams(dimension_semantics=("arbitrary",),
                                             vmem_limit_bytes=VMEM_LIMIT_BYTES),
        name="prompt_mixers_stage_d",
    )(w["sinks"], qk, g, v, r, qs, kv, x1, og_s, os_s, w["gn"], w["sn"], w["wo"], w["n2"], w["wg2"],
      w["wu2"], w["wd2"], w["nf"])


def _gla_sample_body(qk_ref, g_ref, v_ref, r_ref, gn_ref, s_ref, og_ref, so_ref, o_scr):
    rows = SAMPLE_ROWS
    g = g_ref[...]
    q = qk_ref[:, 0:GLA_KEY_WIDTH]
    k = qk_ref[:, GLA_KEY_WIDTH:]
    vb = v_ref[...]
    vf = vb.astype(F32)
    tok = lax.broadcasted_iota(jnp.int32, (rows, 1), 0) % DEC_SEQ

    cum = g
    for d in range(1, DEC_SEQ):
        cum = cum + jnp.where(tok >= d, pltpu.roll(g, d, 0), 0.0)
    tot = jnp.where(tok == DEC_SEQ - 1, cum, 0.0)
    for d in range(1, DEC_SEQ):
        tot = tot + jnp.where(tok == DEC_SEQ - 1 - d, pltpu.roll(cum, rows - d, 0), 0.0)

    qe = q * jnp.exp(cum)
    kl = k * jnp.exp(tot - cum)
    decay_t = jnp.exp(tot).T

    ind = jnp.where(lax.broadcasted_iota(jnp.int32, (GLA_KEY_WIDTH, LANES), 0) // GLA_HEAD_K
                    == lax.broadcasted_iota(jnp.int32, (GLA_KEY_WIDTH, LANES), 1), 1.0, 0.0).astype(BF16)
    expand = jnp.where(lax.broadcasted_iota(jnp.int32, (LANES, GLA_WIDTH), 0)
                       == lax.broadcasted_iota(jnp.int32, (LANES, GLA_WIDTH), 1) // GLA_HEAD_V,
                       1.0, 0.0).astype(BF16)

    o_intra = jnp.zeros((rows, GLA_WIDTH), F32)
    for d in range(DEC_SEQ):
        k_d = k if d == 0 else pltpu.roll(k, d, 0)
        c_d = cum if d == 0 else pltpu.roll(cum, d, 0)
        v_d = vf if d == 0 else pltpu.roll(vf, d, 0)
        pair = jnp.where(tok >= d, q * k_d * jnp.exp(jnp.minimum(cum - c_d, 0.0)), 0.0)
        a = _dot(pair.astype(BF16), ind)
        o_intra = o_intra + _dot(a.astype(BF16), expand) * v_d

    lane_head = lax.broadcasted_iota(jnp.int32, (1, GLA_KEY_WIDTH), 1) // GLA_HEAD_K
    row8 = lax.broadcasted_iota(jnp.int32, (SUBLANES, 1), 0)
    row32 = lax.broadcasted_iota(jnp.int32, (GLA_HEADS * SUBLANES, 1), 0)
    for pair_idx in range(SAMPLE_SEQS // 2):
        r8 = slice(pair_idx * SUBLANES, (pair_idx + 1) * SUBLANES)
        q8 = qe[r8, :]
        lhs = jnp.concatenate([jnp.where(lane_head == h, q8, 0.0) for h in range(GLA_HEADS)],
                              axis=0).astype(BF16)
        kl8 = kl[r8, :]
        v8 = vb[r8, :]
        res = []
        for s in range(2):
            b = 2 * pair_idx + s
            state = s_ref[b]
            res.append(_dot(lhs, state.astype(BF16)))
            kl_b = jnp.where(row8 // DEC_SEQ == s, kl8, 0.0).astype(BF16)
            upd = _dot_ta(kl_b, v8)
            upd = jnp.concatenate(
                [upd[h * GLA_HEAD_K:(h + 1) * GLA_HEAD_K, h * GLA_HEAD_V:(h + 1) * GLA_HEAD_V]
                 for h in range(GLA_HEADS)], axis=0)
            so_ref[b] = state * decay_t[:, DEC_SEQ * b:DEC_SEQ * b + 1] + upd
        sel = jnp.where(row32 % SUBLANES < DEC_SEQ, res[0], res[1])
        o_scr[r8, :] = jnp.concatenate([sel[h * SUBLANES:(h + 1) * SUBLANES] for h in range(GLA_HEADS)], axis=1)

    og_ref[...] = _head_norm_gate(o_scr[...] + o_intra, r_ref[...], gn_ref[...])


def _gla_sample(qk, g, v, r, gn, state, row0):
    nseq = state.shape[0]
    rows = nseq * DEC_SEQ
    assert nseq % SAMPLE_SEQS == 0 and row0 % SAMPLE_ROWS == 0
    blk0 = row0 // SAMPLE_ROWS
    src = lambda n: pl.BlockSpec((SAMPLE_ROWS, n), lambda i: (i + blk0, 0))
    blk = lambda n: pl.BlockSpec((SAMPLE_ROWS, n), lambda i: (i, 0))
    sblk = pl.BlockSpec((SAMPLE_SEQS, GLA_KEY_WIDTH, GLA_HEAD_V), lambda i: (i, 0, 0))
    return pl.pallas_call(
        _gla_sample_body,
        grid=(nseq // SAMPLE_SEQS,),
        in_specs=[src(2 * GLA_KEY_WIDTH), src(GLA_KEY_WIDTH), src(GLA_WIDTH), src(GLA_WIDTH),
                  pl.BlockSpec((1, GLA_WIDTH), lambda i: (0, 0)), sblk],
        out_specs=[blk(GLA_WIDTH), sblk],
        out_shape=[jax.ShapeDtypeStruct((rows, GLA_WIDTH), BF16),
                   jax.ShapeDtypeStruct((nseq, GLA_KEY_WIDTH, GLA_HEAD_V), F32)],
        scratch_shapes=[pltpu.VMEM((SAMPLE_ROWS, GLA_WIDTH), F32)],
        compiler_params=pltpu.CompilerParams(dimension_semantics=("arbitrary",)),
        name="gla_sample",
    )(qk, g, v, r, gn, state)


def _swa_sample_body(sink_ref, q_ref, kvn_ref, kc_ref, vc_ref, nrm_ref, o_ref, kco_ref, vco_ref,
                     lhs_scr, sc_scr, pc_scr, oc_scr):
    rows = SAMPLE_ROWS
    q = q_ref[...].astype(F32)
    lo = lax.broadcasted_iota(jnp.int32, (1, LANES), 1) < SWA_HEAD_DIM
    for hq in range(SWA_Q_HEADS):
        grp = hq // SWA_GROUP
        tile = q[:, (hq // 2) * LANES:(hq // 2 + 1) * LANES]
        src = tile if hq % 2 == grp else pltpu.roll(tile, SWA_HEAD_DIM, 1)
        lhs_scr[hq * rows:(hq + 1) * rows, :] = jnp.where(lo if grp == 0 else ~lo, src, 0.0)

    kvn = kvn_ref[...]
    k_new = kvn[:, 0:SWA_KV_WIDTH]
    v_new = kvn[:, SWA_KV_WIDTH:]
    s_new = _dot_tb(lhs_scr[...].astype(BF16), k_new.astype(BF16))

    row64 = lax.broadcasted_iota(jnp.int32, (SWA_Q_HEADS * SUBLANES, 1), 0)
    first_of_pair = row64 % SUBLANES < DEC_SEQ

    def gather_pair(ref, pair_idx):
        return jnp.concatenate(
            [ref[hq * rows + pair_idx * SUBLANES:hq * rows + (pair_idx + 1) * SUBLANES, :]
             for hq in range(SWA_Q_HEADS)], axis=0).astype(BF16)

    def scatter_pair(ref, pair_idx, val):
        for hq in range(SWA_Q_HEADS):
            ref[hq * rows + pair_idx * SUBLANES:hq * rows + (pair_idx + 1) * SUBLANES, :] = (
                val[hq * SUBLANES:(hq + 1) * SUBLANES])

    for pair_idx in range(SAMPLE_SEQS // 2):
        l64 = gather_pair(lhs_scr, pair_idx)
        sa = _dot_tb(l64, kc_ref[2 * pair_idx].astype(BF16))
        sb = _dot_tb(l64, kc_ref[2 * pair_idx + 1].astype(BF16))
        scatter_pair(sc_scr, pair_idx, jnp.where(first_of_pair, sa, sb))

    rr = lax.broadcasted_iota(jnp.int32, (rows, rows), 0)
    cc = lax.broadcasted_iota(jnp.int32, (rows, rows), 1)
    tok = rr % DEC_SEQ
    mask_cache = cc > tok
    mask_new = (cc // DEC_SEQ == rr // DEC_SEQ) & (cc % DEC_SEQ <= tok)
    p_new, dens = [], []
    for hq in range(SWA_Q_HEADS):
        sl = slice(hq * rows, (hq + 1) * rows)
        sink = sink_ref[hq]
        s_c = jnp.where(mask_cache, sc_scr[sl, :], -jnp.inf)
        s_n = jnp.where(mask_new, s_new[sl, :], -jnp.inf)
        m = jnp.maximum(jnp.maximum(jnp.max(s_c, axis=1, keepdims=True),
                                    jnp.max(s_n, axis=1, keepdims=True)), sink)
        p_c = jnp.exp(s_c - m)
        p_n = jnp.exp(s_n - m)
        dens.append(jnp.sum(p_c, axis=1, keepdims=True) + jnp.sum(p_n, axis=1, keepdims=True)
                    + jnp.exp(sink - m))
        pc_scr[sl, :] = p_c
        p_new.append(p_n.astype(BF16))
    o_new = _dot(jnp.concatenate(p_new, axis=0), v_new.astype(BF16))

    for pair_idx in range(SAMPLE_SEQS // 2):
        p64 = gather_pair(pc_scr, pair_idx)
        oa = _dot(p64, vc_ref[2 * pair_idx].astype(BF16))
        ob = _dot(p64, vc_ref[2 * pair_idx + 1].astype(BF16))
        scatter_pair(oc_scr, pair_idx, jnp.where(first_of_pair, oa, ob))

    tiles = []
    for i in range(SWA_Q_HEADS // 2):
        halves = []
        for hq in (2 * i, 2 * i + 1):
            sl = slice(hq * rows, (hq + 1) * rows)
            oh = (oc_scr[sl, :] + o_new[sl, :]) / dens[hq]
            halves.append(oh if hq % 2 == hq // SWA_GROUP else pltpu.roll(oh, SWA_HEAD_DIM, 1))
        tiles.append(jnp.where(lo, halves[0], halves[1]))
    o_ref[...] = _rms(jnp.concatenate(tiles, axis=1), nrm_ref[...]).astype(BF16)

    for b in range(SAMPLE_SEQS):
        kco_ref[b, 0:WINDOW - DEC_SEQ, :] = kc_ref[b, DEC_SEQ:WINDOW, :]
        vco_ref[b, 0:WINDOW - DEC_SEQ, :] = vc_ref[b, DEC_SEQ:WINDOW, :]
        kco_ref[b, WINDOW - DEC_SEQ:WINDOW, :] = k_new[DEC_SEQ * b:DEC_SEQ * (b + 1), :]
        vco_ref[b, WINDOW - DEC_SEQ:WINDOW, :] = v_new[DEC_SEQ * b:DEC_SEQ * (b + 1), :]


def _swa_sample(sinks, qs, kvn, kc, vc, nrm, row0):
    nseq = kc.shape[0]
    rows = nseq * DEC_SEQ
    assert nseq % SAMPLE_SEQS == 0 and kc.shape[1] == WINDOW and row0 % SAMPLE_ROWS == 0
    blk0 = row0 // SAMPLE_ROWS
    cblk = pl.BlockSpec((SAMPLE_SEQS, WINDOW, SWA_KV_WIDTH), lambda i: (i, 0, 0))
    big = pltpu.VMEM((SWA_Q_HEADS * SAMPLE_ROWS, LANES), F32)
    return pl.pallas_call(
        _swa_sample_body,
        grid=(nseq // SAMPLE_SEQS,),
        in_specs=[pl.BlockSpec(memory_space=pltpu.SMEM),
                  pl.BlockSpec((SAMPLE_ROWS, SWA_WIDTH), lambda i: (i + blk0, 0)),
                  pl.BlockSpec((SAMPLE_ROWS, 2 * SWA_KV_WIDTH), lambda i: (i + blk0, 0)),
                  cblk, cblk,
                  pl.BlockSpec((1, SWA_WIDTH), lambda i: (0, 0))],
        out_specs=[pl.BlockSpec((SAMPLE_ROWS, SWA_WIDTH), lambda i: (i, 0)), cblk, cblk],
        out_shape=[jax.ShapeDtypeStruct((rows, SWA_WIDTH), BF16),
                   jax.ShapeDtypeStruct(kc.shape, F32), jax.ShapeDtypeStruct(vc.shape, F32)],
        scratch_shapes=[big, big, big, big],
        compiler_params=pltpu.CompilerParams(dimension_semantics=("arbitrary",)),
        name="swa_sample",
    )(sinks, qs, kvn, kc, vc, nrm)


def _prepare_weights(ffn1_norm, ffn1_w_gate, ffn1_w_up, ffn1_w_down, mix_norm, w_in, w_gate_up, b_gate,
                     gla_head_norm, swa_out_norm, swa_sinks, w_out,
                     ffn2_norm, ffn2_w_gate, ffn2_w_up, ffn2_w_down, final_norm, layer):
    a0 = 2 * GLA_KEY_WIDTH + 2 * GLA_WIDTH
    a1 = a0 + GLA_GATE_RANK
    win = w_in[layer]
    win = jnp.concatenate([win[:, :a0], win[:, a1:], win[:, a0:a1],
                           jnp.zeros((D_MODEL, LANES - GLA_GATE_RANK), win.dtype)], axis=1)
    wgu = jnp.concatenate([w_gate_up[layer],
                           jnp.zeros((LANES - GLA_GATE_RANK, GLA_KEY_WIDTH), w_gate_up.dtype)], axis=0)
    row = lambda a: a.reshape(1, -1).astype(F32)
    return dict(
        n1=row(ffn1_norm[layer]), wg1=ffn1_w_gate[layer], wu1=ffn1_w_up[layer], wd1=ffn1_w_down[layer],
        nm=row(mix_norm[layer]), win=win.astype(BF16), wgu=wgu.astype(BF16), bg=row(b_gate[layer]),
        gn=row(gla_head_norm[layer]), sn=row(swa_out_norm[layer]), sinks=swa_sinks[layer].astype(F32),
        wo=w_out[layer], n2=row(ffn2_norm[layer]), wg2=ffn2_w_gate[layer], wu2=ffn2_w_up[layer],
        wd2=ffn2_w_down[layer], nf=row(final_norm))


def kernel(x_prompt, x_sample, state_gla, cache_swa_k, cache_swa_v, ffn1_norm, ffn1_w_gate, ffn1_w_up,
           ffn1_w_down, mix_norm, w_in, w_gate_up, b_gate, gla_head_norm, swa_out_norm, swa_sinks, w_out,
           ffn2_norm, ffn2_w_gate, ffn2_w_up, ffn2_w_down, final_norm):
    depth = state_gla.shape[0]
    assert depth == 1 and x_prompt.shape[0] == 1 and x_sample.shape[1] == DEC_SEQ
    seq = x_prompt.shape[1]
    nseq = x_sample.shape[0]
    w = _prepare_weights(ffn1_norm, ffn1_w_gate, ffn1_w_up, ffn1_w_down, mix_norm, w_in, w_gate_up, b_gate,
                         gla_head_norm, swa_out_norm, swa_sinks, w_out,
                         ffn2_norm, ffn2_w_gate, ffn2_w_up, ffn2_w_down, final_norm, 0)

    x1, qk, g, v, r, qs, kv = _stage_a(x_prompt.reshape(seq, D_MODEL),
                                       x_sample.reshape(nseq * DEC_SEQ, D_MODEL), w)

    og_s, state_s = _gla_sample(qk, g, v, r, w["gn"],
                                state_gla[0].reshape(nseq, GLA_KEY_WIDTH, GLA_HEAD_V), seq)
    cache_w = cache_swa_k.shape[2]
    os_s, k_cache_s, v_cache_s = _swa_sample(
        w["sinks"], qs, kv, cache_swa_k[0].reshape(nseq, cache_w, SWA_KV_WIDTH),
        cache_swa_v[0].reshape(nseq, cache_w, SWA_KV_WIDTH), w["sn"], seq)

    y_prompt, y_sample, state_p = _mix_out(qk, g, v, r, qs, kv, x1, og_s, os_s, w)
    y_prompt = y_prompt.reshape(1, seq, D_MODEL)
    y_sample = y_sample.reshape(nseq, DEC_SEQ, D_MODEL)
    cw = min(WINDOW, seq)
    k_cache_p = kv[seq - cw:seq, 0:SWA_KV_WIDTH].reshape(1, 1, cw, SWA_KV_HEADS, SWA_HEAD_DIM)
    v_cache_p = kv[seq - cw:seq, SWA_KV_WIDTH:].reshape(1, 1, cw, SWA_KV_HEADS, SWA_HEAD_DIM)
    state_p = state_p.reshape(1, 1, GLA_HEADS, GLA_HEAD_K, GLA_HEAD_V)
    state_s = state_s.reshape(1, nseq, GLA_HEADS, GLA_HEAD_K, GLA_HEAD_V)
    k_cache_s = k_cache_s.reshape(1, nseq, cache_w, SWA_KV_HEADS, SWA_HEAD_DIM)
    v_cache_s = v_cache_s.reshape(1, nseq, cache_w, SWA_KV_HEADS, SWA_HEAD_DIM)

    return (y_prompt, y_sample, state_p, k_cache_p, v_cache_p, state_s, k_cache_s, v_cache_s)
```

```python
import functools
import itertools

import jax
import jax.numpy as jnp
from jax import lax
from jax.experimental import pallas as pl
from jax.experimental.pallas import tpu as pltpu

F32 = jnp.float32
BF16 = jnp.bfloat16

D_MODEL = 1024
D_FF = 2816
GLA_HEADS = 4
GLA_HEAD_K = 64
GLA_HEAD_V = 128
GLA_KEY_WIDTH = GLA_HEADS * GLA_HEAD_K
GLA_WIDTH = GLA_HEADS * GLA_HEAD_V
GLA_GATE_RANK = 16
GLA_GATE_TAU = 16.0
SWA_HEAD_DIM = 64
SWA_Q_HEADS = 8
SWA_KV_HEADS = 2
SWA_GROUP = SWA_Q_HEADS // SWA_KV_HEADS
SWA_WIDTH = SWA_Q_HEADS * SWA_HEAD_DIM
SWA_KV_WIDTH = SWA_KV_HEADS * SWA_HEAD_DIM
WINDOW = 128
DEC_SEQ = 4
NORM_EPS = 1e-6
HEAD_SCALE = 0.125

LANES = 128
SUBLANES = 8
VMEM_LIMIT_BYTES = 56 * 1024 * 1024

ROW_TILE = 512
FF_CHUNK = 256
DOWN_CHUNK = 256
GLA_CHUNK = 128
SAMPLE_SEQS = 32
SAMPLE_ROWS = SAMPLE_SEQS * DEC_SEQ
MIXER_STAGES = 3 * (ROW_TILE // GLA_CHUNK) + 3 * (ROW_TILE // WINDOW)
SIDE_STAGES_AT_NORM = 0
DECAY_CLAMP = 60.0

PROJ_Q_G = 0
PROJ_K_G = 256
PROJ_V_G = 512
PROJ_R_G = 1024
PROJ_Q_S = 1536
PROJ_KV_S = 2048
PROJ_A = 2304
PROJ_WIDTH = 2432


def _dot(a, b):
    return jnp.dot(a, b, preferred_element_type=F32)


def _dot_tb(a, b):
    return lax.dot_general(a, b, (((1,), (1,)), ((), ())), preferred_element_type=F32)


def _dot_ta(a, b):
    return lax.dot_general(a, b, (((0,), (0,)), ((), ())), preferred_element_type=F32)


def _rms(x, g):
    return x * lax.rsqrt(jnp.mean(x * x, axis=-1, keepdims=True) + NORM_EPS) * g


def _swiglu(h, wg_ref, wu_ref, wd_ref, act_ref, side_stages=iter(()), n_side_stages=0):
    n_dots = 2 * (D_FF // FF_CHUNK) + D_MODEL // DOWN_CHUNK
    done = [0, 0]

    def after_dot():
        done[0] += 1
        while done[1] * n_dots < done[0] * n_side_stages:
            next(side_stages, None)
            done[1] += 1

    for c0 in range(0, D_FF, FF_CHUNK):
        g = _dot(h, wg_ref[:, c0:c0 + FF_CHUNK])
        after_dot()
        u = _dot(h, wu_ref[:, c0:c0 + FF_CHUNK])
        after_dot()
        act_ref[:, c0:c0 + FF_CHUNK] = (g * jax.nn.sigmoid(g) * u).astype(BF16)
    outs = []
    for n0 in range(0, D_MODEL, DOWN_CHUNK):
        outs.append(_dot(act_ref[...], wd_ref[:, n0:n0 + DOWN_CHUNK]))
        after_dot()
    return jnp.concatenate(outs, axis=1)


def _head_norm_gate(o, r, gn):
    parts = []
    for h in range(GLA_HEADS):
        sl = slice(h * GLA_HEAD_V, (h + 1) * GLA_HEAD_V)
        parts.append(_rms(o[:, sl], gn[:, sl]))
    return (jnp.concatenate(parts, axis=1) * (r * jax.nn.sigmoid(r))).astype(BF16)


def _out_ffn(x1_ref, og, osw, wo_ref, n2_ref, wg_ref, wu_ref, wd_ref, nf_ref, y_ref, act_ref, side_stages=None):
    n_inside = 0 if side_stages is None else MIXER_STAGES - 2 * SIDE_STAGES_AT_NORM
    side_stages = iter(()) if side_stages is None else side_stages
    mixed = jnp.concatenate([og, osw], axis=1)
    x2 = x1_ref[...] + _dot(mixed, wo_ref[...])
    for _ in range(SIDE_STAGES_AT_NORM):
        next(side_stages, None)
    h = _rms(x2, n2_ref[...]).astype(BF16)
    x3 = x2 + 0.5 * _swiglu(h, wg_ref, wu_ref, wd_ref, act_ref, side_stages, n_inside)
    for _ in side_stages:
        pass
    y_ref[...] = _rms(x3, nf_ref[...])


_HBM = pl.BlockSpec(memory_space=pl.ANY)
WIDE_CHUNK_ROWS = 128
NARROW_CHUNK_ROWS = 256


def _load_weights_bf16(pairs):
    jobs = []
    for src, dst in pairs:
        rows, cols = src.shape
        assert cols in (D_FF, D_MODEL) and dst.shape == src.shape
        kind, step = (0, WIDE_CHUNK_ROWS) if cols == D_FF else (1, NARROW_CHUNK_ROWS)
        assert rows % step == 0
        jobs += [(src, dst, r0, step, kind) for r0 in range(0, rows, step)]

    def body(wide, narrow, sems):
        staging = (wide, narrow)
        used = [0, 0]
        copies = []
        for src, _, r0, nr, kind in jobs:
            slot = used[kind] % 2
            used[kind] += 1
            copies.append((pltpu.make_async_copy(src.at[pl.ds(r0, nr), :], staging[kind].at[slot],
                                                 sems.at[kind, slot]), slot))
        copies[0][0].start()
        for j, (_, dst, r0, nr, kind) in enumerate(jobs):
            if j + 1 < len(jobs):
                copies[j + 1][0].start()
            copy, slot = copies[j]
            copy.wait()
            dst[r0:r0 + nr, :] = staging[kind][slot].astype(BF16)

    pl.run_scoped(body,
                  pltpu.VMEM((2, WIDE_CHUNK_ROWS, D_FF), F32),
                  pltpu.VMEM((2, NARROW_CHUNK_ROWS, D_MODEL), F32),
                  pltpu.SemaphoreType.DMA((2, 2)))


def _stage_a_body(xp_ref, xs_ref, n1_ref, wg_hbm, wu_hbm, wd_hbm, nm_ref, win_ref, wgu_ref, bg_ref,
                  x1_ref, qk_ref, g_ref, v_ref, r_ref, qs_ref, kv_ref, act_ref, wg_ref, wu_ref, wd_ref):
    i = pl.program_id(0)

    @pl.when(i == 0)
    def _():
        _load_weights_bf16([(wg_hbm, wg_ref), (wu_hbm, wu_ref), (wd_hbm, wd_ref)])

    x = jnp.where(i < pl.num_programs(0) - 1, xp_ref[...], xs_ref[...])
    h = _rms(x, n1_ref[...]).astype(BF16)
    x1 = x + 0.5 * _swiglu(h, wg_ref, wu_ref, wd_ref, act_ref)
    x1_ref[...] = x1
    h2 = _rms(x1, nm_ref[...]).astype(BF16)
    proj = _dot(h2, win_ref[...])
    qk_ref[:, 0:GLA_KEY_WIDTH] = proj[:, PROJ_Q_G:PROJ_K_G] * HEAD_SCALE
    qk_ref[:, GLA_KEY_WIDTH:] = proj[:, PROJ_K_G:PROJ_V_G]
    v_ref[...] = proj[:, PROJ_V_G:PROJ_R_G].astype(BF16)
    r_ref[...] = proj[:, PROJ_R_G:PROJ_Q_S]
    qs_ref[...] = (proj[:, PROJ_Q_S:PROJ_KV_S] * HEAD_SCALE).astype(BF16)
    kv_ref[...] = proj[:, PROJ_KV_S:PROJ_A]
    a = proj[:, PROJ_A:PROJ_WIDTH].astype(BF16)
    z = _dot(a, wgu_ref[...]) + bg_ref[...]
    g_ref[...] = jax.nn.log_sigmoid(z) * (1.0 / GLA_GATE_TAU)


def _resident(shape):
    return pl.BlockSpec(shape, lambda i: (0,) * len(shape), pipeline_mode=pl.Buffered(1))


def _rows(tm, n):
    return pl.BlockSpec((tm, n), lambda i: (i, 0))


def _stage_a(xp, xs, w):
    t = xp.shape[0]
    assert t % ROW_TILE == 0 and xs.shape[0] == ROW_TILE
    nt = t // ROW_TILE
    out_widths = ((D_MODEL, F32), (2 * GLA_KEY_WIDTH, F32), (GLA_KEY_WIDTH, F32), (GLA_WIDTH, BF16),
                  (GLA_WIDTH, F32), (SWA_WIDTH, BF16), (2 * SWA_KV_WIDTH, F32))
    return pl.pallas_call(
        _stage_a_body,
        grid=(nt + 1,),
        in_specs=[pl.BlockSpec((ROW_TILE, D_MODEL), lambda i: (jnp.minimum(i, nt - 1), 0)),
                  _resident((ROW_TILE, D_MODEL)), _resident((1, D_MODEL)),
                  _HBM, _HBM, _HBM,
                  _resident((1, D_MODEL)), _resident((D_MODEL, PROJ_WIDTH)),
                  _resident((LANES, GLA_KEY_WIDTH)), _resident((1, GLA_KEY_WIDTH))],
        out_specs=[_rows(ROW_TILE, n) for n, _ in out_widths],
        out_shape=[jax.ShapeDtypeStruct((t + ROW_TILE, n), dt) for n, dt in out_widths],
        scratch_shapes=[pltpu.VMEM((ROW_TILE, D_FF), BF16), pltpu.VMEM((D_MODEL, D_FF), BF16),
                        pltpu.VMEM((D_MODEL, D_FF), BF16), pltpu.VMEM((D_FF, D_MODEL), BF16)],
        compiler_params=pltpu.CompilerParams(dimension_semantics=("arbitrary",),
                                             vmem_limit_bytes=VMEM_LIMIT_BYTES),
        name="stage_a_ffn1_proj",
    )(xp, xs, w["n1"], w["wg1"], w["wu1"], w["wd1"], w["nm"], w["win"], w["wgu"], w["bg"])


def _split_bf16(x):
    hi = x.astype(BF16)
    lo = (x - hi.astype(F32)).astype(BF16)
    return hi, lo


def _alternate(a, b):
    pending = [iter(a), iter(b)]
    while pending:
        for it in list(pending):
            try:
                next(it)
            except StopIteration:
                pending.remove(it)
                continue
            yield


def _gla_stages(qk_ref, g_ref, v_ref, r_ref, gn_ref, s_ref, og_scr, cum_scr, inter_scr, flag_ref):
    c_len = GLA_CHUNK
    n_chunks = ROW_TILE // c_len
    causal = (lax.broadcasted_iota(jnp.int32, (c_len, c_len), 0)
              >= lax.broadcasted_iota(jnp.int32, (c_len, c_len), 1))
    ltri = jnp.where(causal, 1.0, 0.0).astype(BF16)
    causal_cat = (lax.broadcasted_iota(jnp.int32, (c_len, GLA_HEADS * c_len), 0)
                  >= lax.broadcasted_iota(jnp.int32, (c_len, GLA_HEADS * c_len), 1) % c_len)
    lane_head = lax.broadcasted_iota(jnp.int32, (1, GLA_KEY_WIDTH), 1) // GLA_HEAD_K
    row_head = lax.broadcasted_iota(jnp.int32, (GLA_KEY_WIDTH, 1), 0) // GLA_HEAD_K
    eye = (lax.broadcasted_iota(jnp.int32, (GLA_KEY_WIDTH, GLA_KEY_WIDTH), 0)
           == lax.broadcasted_iota(jnp.int32, (GLA_KEY_WIDTH, GLA_KEY_WIDTH), 1))
    zero_b = jnp.zeros((), BF16)
    zero_v = jnp.zeros((c_len, GLA_HEAD_V), BF16)
    worst = []

    def chunk(c):
        rows = slice(c * c_len, (c + 1) * c_len)
        g_hi, g_lo = _split_bf16(g_ref[rows, :])
        cum = _dot(ltri, g_hi) + _dot(ltri, g_lo)
        yield
        cum_scr[rows, :] = cum
        last = cum[c_len - 1:c_len, :]
        q = qk_ref[rows, 0:GLA_KEY_WIDTH]
        k = qk_ref[rows, GLA_KEY_WIDTH:]
        vb = v_ref[rows, :]
        qe = (q * jnp.exp(cum)).astype(BF16)
        ke = (k * jnp.exp(jnp.minimum(-cum, DECAY_CLAMP))).astype(BF16)
        kl = (k * jnp.exp(last - cum)).astype(BF16)
        state = s_ref[...]
        sb = state.astype(BF16)
        s_bd = jnp.concatenate([jnp.where(row_head == h, sb, zero_b) for h in range(GLA_HEADS)], axis=1)
        o_inter = _dot(qe, s_bd)
        inter_scr[rows, :] = o_inter
        ke_bd = jnp.concatenate([jnp.where(lane_head == h, ke, zero_b) for h in range(GLA_HEADS)], axis=0)
        attn = _dot_tb(qe, ke_bd)
        upds = []
        for p in range(GLA_HEADS // 2):
            u = _dot_ta(kl[:, p * LANES:(p + 1) * LANES], vb[:, 2 * p * GLA_HEAD_V:(2 * p + 2) * GLA_HEAD_V])
            upds.append(u[0:GLA_HEAD_K, 0:GLA_HEAD_V])
            upds.append(u[GLA_HEAD_K:, GLA_HEAD_V:])
        yield
        attn = jnp.where(causal_cat, attn, 0.0).astype(BF16)
        o_pairs = []
        for p in range(GLA_HEADS // 2):
            v_a = vb[:, (2 * p) * GLA_HEAD_V:(2 * p + 1) * GLA_HEAD_V]
            v_b = vb[:, (2 * p + 1) * GLA_HEAD_V:(2 * p + 2) * GLA_HEAD_V]
            v_bd = jnp.concatenate([jnp.concatenate([v_a, zero_v], axis=1),
                                    jnp.concatenate([zero_v, v_b], axis=1)], axis=0)
            o_pairs.append(_dot(attn[:, 2 * p * c_len:(2 * p + 2) * c_len], v_bd))
        og_scr[rows, :] = _head_norm_gate(o_inter + jnp.concatenate(o_pairs, axis=1), r_ref[rows, :],
                                          gn_ref[...])
        last_col = jnp.sum(jnp.where(eye, last, 0.0), axis=1, keepdims=True)
        s_ref[...] = state * jnp.exp(last_col) + jnp.concatenate(upds, axis=0)
        worst.append(jnp.min(last, axis=1, keepdims=True))
        if c == n_chunks - 1:
            tile_min = functools.reduce(jnp.minimum, worst)
            flag_ref[0] = jnp.where(tile_min[0, 0] < -DECAY_CLAMP, 1, 0)
        yield

    return itertools.chain.from_iterable(chunk(c) for c in range(n_chunks))


def _gla_tile_exact(qk_ref, v_ref, r_ref, gn_ref, og_scr, o_scr, cum_scr, inter_scr):
    c_len = GLA_CHUNK
    ind = jnp.where(lax.broadcasted_iota(jnp.int32, (GLA_KEY_WIDTH, LANES), 0) // GLA_HEAD_K
                    == lax.broadcasted_iota(jnp.int32, (GLA_KEY_WIDTH, LANES), 1), 1.0, 0.0).astype(BF16)
    j_idx = lax.broadcasted_iota(jnp.int32, (c_len, 1), 0)
    for c in range(ROW_TILE // c_len):
        r0 = c * c_len
        rows = slice(r0, r0 + c_len)

        def one_row(i, carry, r0=r0, rows=rows):
            ci = cum_scr[pl.ds(r0 + i, 1), :]
            qi = qk_ref[pl.ds(r0 + i, 1), 0:GLA_KEY_WIDTH]
            kk = qk_ref[rows, GLA_KEY_WIDTH:]
            dec = jnp.exp(jnp.minimum(ci - cum_scr[rows, :], 0.0))
            a_cols = _dot(((qi * kk) * dec).astype(BF16), ind)
            outs = []
            for h in range(GLA_HEADS):
                w_col = jnp.where(j_idx <= i, a_cols[:, h:h + 1], 0.0)
                v_h = v_ref[rows, h * GLA_HEAD_V:(h + 1) * GLA_HEAD_V].astype(F32)
                outs.append(jnp.sum(w_col * v_h, axis=0, keepdims=True))
            o_scr[pl.ds(r0 + i, 1), :] = inter_scr[pl.ds(r0 + i, 1), :] + jnp.concatenate(outs, axis=1)
            return carry

        lax.fori_loop(0, c_len, one_row, 0)
    og_scr[...] = _head_norm_gate(o_scr[...], r_ref[...], gn_ref[...])


def _dup_halves(x):
    lo = lax.broadcasted_iota(jnp.int32, (1, LANES), 1) < SWA_HEAD_DIM
    sw = pltpu.roll(x, SWA_HEAD_DIM, 1)
    return jnp.where(lo, x, sw).astype(BF16), jnp.where(lo, sw, x).astype(BF16)


def _swa_stages(sink_ref, qs_ref, kv_ref, kvprev_scr, sn_ref, os_scr, no_prev_bias):
    w = WINDOW
    n_blocks = ROW_TILE // w
    lo = lax.broadcasted_iota(jnp.int32, (1, LANES), 1) < SWA_HEAD_DIM
    tri = (lax.broadcasted_iota(jnp.int32, (w, w), 1) <= lax.broadcasted_iota(jnp.int32, (w, w), 0))
    zero_b = jnp.zeros((), BF16)

    def block(b):
        rows = slice(b * w, (b + 1) * w)
        kv_prev = kvprev_scr[...] if b == 0 else kv_ref[(b - 1) * w:b * w, :]
        kv = jnp.concatenate([kv_prev, kv_ref[rows, :]], axis=0)
        kk = _dup_halves(kv[:, 0:SWA_KV_WIDTH])
        vv = _dup_halves(kv[:, SWA_KV_WIDTH:])
        q = qs_ref[rows, :]
        scores = []
        for grp in range(SWA_KV_HEADS):
            stacked = []
            for j in range(SWA_GROUP):
                hq = grp * SWA_GROUP + j
                tile = q[:, (hq // 2) * LANES:(hq // 2 + 1) * LANES]
                stacked.append(jnp.where(lo if hq % 2 == 0 else ~lo, tile, zero_b))
            scores.append(_dot_tb(jnp.concatenate(stacked, axis=0), kk[grp]))
        yield
        outs, dens = [], []
        for grp in range(SWA_KV_HEADS):
            s = scores[grp]
            probs = []
            for j in range(SWA_GROUP):
                sink = sink_ref[grp * SWA_GROUP + j]
                s_prev = s[j * w:(j + 1) * w, 0:w]
                if b == 0:
                    s_prev = s_prev + no_prev_bias
                sf = jnp.where(tri, s[j * w:(j + 1) * w, w:], s_prev)
                m = jnp.maximum(jnp.max(sf, axis=1, keepdims=True), sink)
                p = jnp.exp(sf - m)
                dens.append(jnp.sum(p, axis=1, keepdims=True) + jnp.exp(sink - m))
                pb = p.astype(BF16)
                probs.append(jnp.concatenate([jnp.where(tri, zero_b, pb), jnp.where(tri, pb, zero_b)], axis=1))
            outs.append(_dot(jnp.concatenate(probs, axis=0), vv[grp]))
        yield
        tiles = []
        for grp in range(SWA_KV_HEADS):
            res = [outs[grp][j * w:(j + 1) * w] / dens[grp * SWA_GROUP + j] for j in range(SWA_GROUP)]
            tiles.append(jnp.where(lo, res[0], res[1]))
            tiles.append(jnp.where(lo, res[2], res[3]))
        os_scr[rows, :] = _rms(jnp.concatenate(tiles, axis=1), sn_ref[...]).astype(BF16)
        if b == n_blocks - 1:
            kvprev_scr[...] = kv_ref[rows, :]
        yield

    return itertools.chain.from_iterable(block(b) for b in range(n_blocks))


def _mix_out_body(sink_ref, qk_ref, g_ref, v_ref, r_ref, qs_ref, kv_ref, x1_ref, ogs_ref, oss_ref,
                  gn_ref, sn_ref, wo_hbm, n2_ref, wg_hbm, wu_hbm, wd_hbm, nf_ref, yp_ref, ys_ref, s_ref,
                  og_scr, os_scr, kvprev_scr, o_scr, cum_scr, inter_scr, act_scr,
                  wo_ref, wg_ref, wu_ref, wd_ref, flag_ref):
    i = pl.program_id(0)
    n_prompt = pl.num_programs(0) - 2

    def stage_d(og, osw, y_ref, side_stages=None):
        _out_ffn(x1_ref, og, osw, wo_ref, n2_ref, wg_ref, wu_ref, wd_ref, nf_ref, y_ref, act_scr,
                 side_stages)

    def mixer_stages():
        gla = _gla_stages(qk_ref, g_ref, v_ref, r_ref, gn_ref, s_ref, og_scr, cum_scr, inter_scr, flag_ref)
        swa = _swa_stages(sink_ref, qs_ref, kv_ref, kvprev_scr, sn_ref, os_scr,
                          jnp.where(i > 0, 0.0, -jnp.inf))
        return _alternate(gla, swa)

    @pl.when(i == 0)
    def _():
        _load_weights_bf16([(wo_hbm, wo_ref), (wg_hbm, wg_ref), (wu_hbm, wu_ref), (wd_hbm, wd_ref)])
        s_ref[...] = jnp.zeros_like(s_ref)
        kvprev_scr[...] = jnp.zeros_like(kvprev_scr)
        for _ in mixer_stages():
            pass

    @pl.when(jnp.logical_and(i > 0, i < n_prompt))
    def _():
        stage_d(og_scr[...], os_scr[...], yp_ref, mixer_stages())

    @pl.when(i == n_prompt)
    def _():
        stage_d(og_scr[...], os_scr[...], yp_ref)
        flag_ref[0] = 0

    @pl.when(i == n_prompt + 1)
    def _():
        stage_d(ogs_ref[...], oss_ref[...], ys_ref)

    @pl.when(flag_ref[0] != 0)
    def _():
        _gla_tile_exact(qk_ref, v_ref, r_ref, gn_ref, og_scr, o_scr, cum_scr, inter_scr)


def _mix_out(qk, g, v, r, qs, kv, x1, og_s, os_s, w):
    t = qk.shape[0] - ROW_TILE
    assert t % ROW_TILE == 0 and og_s.shape[0] == ROW_TILE
    nt = t // ROW_TILE
    cur = lambda n: pl.BlockSpec((ROW_TILE, n), lambda i: (jnp.minimum(i, nt - 1), 0))
    return pl.pallas_call(
        _mix_out_body,
        grid=(nt + 2,),
        in_specs=[pl.BlockSpec(memory_space=pltpu.SMEM),
                  cur(2 * GLA_KEY_WIDTH), cur(GLA_KEY_WIDTH), cur(GLA_WIDTH), cur(GLA_WIDTH),
                  cur(SWA_WIDTH), cur(2 * SWA_KV_WIDTH),
                  pl.BlockSpec((ROW_TILE, D_MODEL), lambda i: (jnp.maximum(i - 1, 0), 0)),
                  _resident((ROW_TILE, GLA_WIDTH)), _resident((ROW_TILE, SWA_WIDTH)),
                  _resident((1, GLA_WIDTH)), _resident((1, SWA_WIDTH)),
                  _HBM, _resident((1, D_MODEL)), _HBM, _HBM, _HBM, _resident((1, D_MODEL))],
        out_specs=[pl.BlockSpec((ROW_TILE, D_MODEL), lambda i: (jnp.clip(i - 1, 0, nt - 1), 0)),
                   pl.BlockSpec((ROW_TILE, D_MODEL), lambda i: (0, 0)),
                   pl.BlockSpec((GLA_KEY_WIDTH, GLA_HEAD_V), lambda i: (0, 0))],
        out_shape=[jax.ShapeDtypeStruct((t, D_MODEL), F32),
                   jax.ShapeDtypeStruct((ROW_TILE, D_MODEL), F32),
                   jax.ShapeDtypeStruct((GLA_KEY_WIDTH, GLA_HEAD_V), F32)],
        scratch_shapes=[pltpu.VMEM((ROW_TILE, GLA_WIDTH), BF16), pltpu.VMEM((ROW_TILE, SWA_WIDTH), BF16),
                        pltpu.VMEM((WINDOW, 2 * SWA_KV_WIDTH), F32),
                        pltpu.VMEM((ROW_TILE, GLA_WIDTH), F32), pltpu.VMEM((ROW_TILE, GLA_KEY_WIDTH), F32),
                        pltpu.VMEM((ROW_TILE, GLA_WIDTH), F32), pltpu.VMEM((ROW_TILE, D_FF), BF16),
                        pltpu.VMEM((D_MODEL, D_MODEL), BF16), pltpu.VMEM((D_MODEL, D_FF), BF16),
                        pltpu.VMEM((D_MODEL, D_FF), BF16), pltpu.VMEM((D_FF, D_MODEL), BF16),
                        pltpu.SMEM((1,), jnp.int32)],
        compiler_params=pltpu.CompilerParams(dimension_semantics=("arbitrary",),
                                             vmem_limit_bytes=VMEM_LIMIT_BYTES),
        name="prompt_mixers_stage_d",
    )(w["sinks"], qk, g, v, r, qs, kv, x1, og_s, os_s, w["gn"], w["sn"], w["wo"], w["n2"], w["wg2"],
      w["wu2"], w["wd2"], w["nf"])


def _gla_sample_body(qk_ref, g_ref, v_ref, r_ref, gn_ref, s_ref, og_ref, so_ref, o_scr):
    rows = SAMPLE_ROWS
    g = g_ref[...]
    q = qk_ref[:, 0:GLA_KEY_WIDTH]
    k = qk_ref[:, GLA_KEY_WIDTH:]
    vb = v_ref[...]
    vf = vb.astype(F32)
    tok = lax.broadcasted_iota(jnp.int32, (rows, 1), 0) % DEC_SEQ

    cum = g
    for d in range(1, DEC_SEQ):
        cum = cum + jnp.where(tok >= d, pltpu.roll(g, d, 0), 0.0)
    tot = jnp.where(tok == DEC_SEQ - 1, cum, 0.0)
    for d in range(1, DEC_SEQ):
        tot = tot + jnp.where(tok == DEC_SEQ - 1 - d, pltpu.roll(cum, rows - d, 0), 0.0)

    qe = q * jnp.exp(cum)
    kl = k * jnp.exp(tot - cum)
    decay_t = jnp.exp(tot).T

    ind = jnp.where(lax.broadcasted_iota(jnp.int32, (GLA_KEY_WIDTH, LANES), 0) // GLA_HEAD_K
                    == lax.broadcasted_iota(jnp.int32, (GLA_KEY_WIDTH, LANES), 1), 1.0, 0.0).astype(BF16)
    expand = jnp.where(lax.broadcasted_iota(jnp.int32, (LANES, GLA_WIDTH), 0)
                       == lax.broadcasted_iota(jnp.int32, (LANES, GLA_WIDTH), 1) // GLA_HEAD_V,
                       1.0, 0.0).astype(BF16)

    o_intra = jnp.zeros((rows, GLA_WIDTH), F32)
    for d in range(DEC_SEQ):
        k_d = k if d == 0 else pltpu.roll(k, d, 0)
        c_d = cum if d == 0 else pltpu.roll(cum, d, 0)
        v_d = vf if d == 0 else pltpu.roll(vf, d, 0)
        pair = jnp.where(tok >= d, q * k_d * jnp.exp(jnp.minimum(cum - c_d, 0.0)), 0.0)
        a = _dot(pair.astype(BF16), ind)
        o_intra = o_intra + _dot(a.astype(BF16), expand) * v_d

    lane_head = lax.broadcasted_iota(jnp.int32, (1, GLA_KEY_WIDTH), 1) // GLA_HEAD_K
    row8 = lax.broadcasted_iota(jnp.int32, (SUBLANES, 1), 0)
    row32 = lax.broadcasted_iota(jnp.int32, (GLA_HEADS * SUBLANES, 1), 0)
    for pair_idx in range(SAMPLE_SEQS // 2):
        r8 = slice(pair_idx * SUBLANES, (pair_idx + 1) * SUBLANES)
        q8 = qe[r8, :]
        lhs = jnp.concatenate([jnp.where(lane_head == h, q8, 0.0) for h in range(GLA_HEADS)],
                              axis=0).astype(BF16)
        kl8 = kl[r8, :]
        v8 = vb[r8, :]
        res = []
        for s in range(2):
            b = 2 * pair_idx + s
            state = s_ref[b]
            res.append(_dot(lhs, state.astype(BF16)))
            kl_b = jnp.where(row8 // DEC_SEQ == s, kl8, 0.0).astype(BF16)
            upd = _dot_ta(kl_b, v8)
            upd = jnp.concatenate(
                [upd[h * GLA_HEAD_K:(h + 1) * GLA_HEAD_K, h * GLA_HEAD_V:(h + 1) * GLA_HEAD_V]
                 for h in range(GLA_HEADS)], axis=0)
            so_ref[b] = state * decay_t[:, DEC_SEQ * b:DEC_SEQ * b + 1] + upd
        sel = jnp.where(row32 % SUBLANES < DEC_SEQ, res[0], res[1])
        o_scr[r8, :] = jnp.concatenate([sel[h * SUBLANES:(h + 1) * SUBLANES] for h in range(GLA_HEADS)], axis=1)

    og_ref[...] = _head_norm_gate(o_scr[...] + o_intra, r_ref[...], gn_ref[...])


def _gla_sample(qk, g, v, r, gn, state, row0):
    nseq = state.shape[0]
    rows = nseq * DEC_SEQ
    assert nseq % SAMPLE_SEQS == 0 and row0 % SAMPLE_ROWS == 0
    blk0 = row0 // SAMPLE_ROWS
    src = lambda n: pl.BlockSpec((SAMPLE_ROWS, n), lambda i: (i + blk0, 0))
    blk = lambda n: pl.BlockSpec((SAMPLE_ROWS, n), lambda i: (i, 0))
    sblk = pl.BlockSpec((SAMPLE_SEQS, GLA_KEY_WIDTH, GLA_HEAD_V), lambda i: (i, 0, 0))
    return pl.pallas_call(
        _gla_sample_body,
        grid=(nseq // SAMPLE_SEQS,),
        in_specs=[src(2 * GLA_KEY_WIDTH), src(GLA_KEY_WIDTH), src(GLA_WIDTH), src(GLA_WIDTH),
                  pl.BlockSpec((1, GLA_WIDTH), lambda i: (0, 0)), sblk],
        out_specs=[blk(GLA_WIDTH), sblk],
        out_shape=[jax.ShapeDtypeStruct((rows, GLA_WIDTH), BF16),
                   jax.ShapeDtypeStruct((nseq, GLA_KEY_WIDTH, GLA_HEAD_V), F32)],
        scratch_shapes=[pltpu.VMEM((SAMPLE_ROWS, GLA_WIDTH), F32)],
        compiler_params=pltpu.CompilerParams(dimension_semantics=("arbitrary",)),
        name="gla_sample",
    )(qk, g, v, r, gn, state)


def _swa_sample_body(sink_ref, q_ref, kvn_ref, kc_ref, vc_ref, nrm_ref, o_ref, kco_ref, vco_ref,
                     lhs_scr, sc_scr, pc_scr, oc_scr):
    rows = SAMPLE_ROWS
    q = q_ref[...].astype(F32)
    lo = lax.broadcasted_iota(jnp.int32, (1, LANES), 1) < SWA_HEAD_DIM
    for hq in range(SWA_Q_HEADS):
        grp = hq // SWA_GROUP
        tile = q[:, (hq // 2) * LANES:(hq // 2 + 1) * LANES]
        src = tile if hq % 2 == grp else pltpu.roll(tile, SWA_HEAD_DIM, 1)
        lhs_scr[hq * rows:(hq + 1) * rows, :] = jnp.where(lo if grp == 0 else ~lo, src, 0.0)

    kvn = kvn_ref[...]
    k_new = kvn[:, 0:SWA_KV_WIDTH]
    v_new = kvn[:, SWA_KV_WIDTH:]
    s_new = _dot_tb(lhs_scr[...].astype(BF16), k_new.astype(BF16))

    row64 = lax.broadcasted_iota(jnp.int32, (SWA_Q_HEADS * SUBLANES, 1), 0)
    first_of_pair = row64 % SUBLANES < DEC_SEQ

    def gather_pair(ref, pair_idx):
        return jnp.concatenate(
            [ref[hq * rows + pair_idx * SUBLANES:hq * rows + (pair_idx + 1) * SUBLANES, :]
             for hq in range(SWA_Q_HEADS)], axis=0).astype(BF16)

    def scatter_pair(ref, pair_idx, val):
        for hq in range(SWA_Q_HEADS):
            ref[hq * rows + pair_idx * SUBLANES:hq * rows + (pair_idx + 1) * SUBLANES, :] = (
                val[hq * SUBLANES:(hq + 1) * SUBLANES])

    for pair_idx in range(SAMPLE_SEQS // 2):
        l64 = gather_pair(lhs_scr, pair_idx)
        sa = _dot_tb(l64, kc_ref[2 * pair_idx].astype(BF16))
        sb = _dot_tb(l64, kc_ref[2 * pair_idx + 1].astype(BF16))
        scatter_pair(sc_scr, pair_idx, jnp.where(first_of_pair, sa, sb))

    rr = lax.broadcasted_iota(jnp.int32, (rows, rows), 0)
    cc = lax.broadcasted_iota(jnp.int32, (rows, rows), 1)
    tok = rr % DEC_SEQ
    mask_cache = cc > tok
    mask_new = (cc // DEC_SEQ == rr // DEC_SEQ) & (cc % DEC_SEQ <= tok)
    p_new, dens = [], []
    for hq in range(SWA_Q_HEADS):
        sl = slice(hq * rows, (hq + 1) * rows)
        sink = sink_ref[hq]
        s_c = jnp.where(mask_cache, sc_scr[sl, :], -jnp.inf)
        s_n = jnp.where(mask_new, s_new[sl, :], -jnp.inf)
        m = jnp.maximum(jnp.maximum(jnp.max(s_c, axis=1, keepdims=True),
                                    jnp.max(s_n, axis=1, keepdims=True)), sink)
        p_c = jnp.exp(s_c - m)
        p_n = jnp.exp(s_n - m)
        dens.append(jnp.sum(p_c, axis=1, keepdims=True) + jnp.sum(p_n, axis=1, keepdims=True)
                    + jnp.exp(sink - m))
        pc_scr[sl, :] = p_c
        p_new.append(p_n.astype(BF16))
    o_new = _dot(jnp.concatenate(p_new, axis=0), v_new.astype(BF16))

    for pair_idx in range(SAMPLE_SEQS // 2):
        p64 = gather_pair(pc_scr, pair_idx)
        oa = _dot(p64, vc_ref[2 * pair_idx].astype(BF16))
        ob = _dot(p64, vc_ref[2 * pair_idx + 1].astype(BF16))
        scatter_pair(oc_scr, pair_idx, jnp.where(first_of_pair, oa, ob))

    tiles = []
    for i in range(SWA_Q_HEADS // 2):
        halves = []
        for hq in (2 * i, 2 * i + 1):
            sl = slice(hq * rows, (hq + 1) * rows)
            oh = (oc_scr[sl, :] + o_new[sl, :]) / dens[hq]
            halves.append(oh if hq % 2 == hq // SWA_GROUP else pltpu.roll(oh, SWA_HEAD_DIM, 1))
        tiles.append(jnp.where(lo, halves[0], halves[1]))
    o_ref[...] = _rms(jnp.concatenate(tiles, axis=1), nrm_ref[...]).astype(BF16)

    for b in range(SAMPLE_SEQS):
        kco_ref[b, 0:WINDOW - DEC_SEQ, :] = kc_ref[b, DEC_SEQ:WINDOW, :]
        vco_ref[b, 0:WINDOW - DEC_SEQ, :] = vc_ref[b, DEC_SEQ:WINDOW, :]
        kco_ref[b, WINDOW - DEC_SEQ:WINDOW, :] = k_new[DEC_SEQ * b:DEC_SEQ * (b + 1), :]
        vco_ref[b, WINDOW - DEC_SEQ:WINDOW, :] = v_new[DEC_SEQ * b:DEC_SEQ * (b + 1), :]


def _swa_sample(sinks, qs, kvn, kc, vc, nrm, row0):
    nseq = kc.shape[0]
    rows = nseq * DEC_SEQ
    assert nseq % SAMPLE_SEQS == 0 and kc.shape[1] == WINDOW and row0 % SAMPLE_ROWS == 0
    blk0 = row0 // SAMPLE_ROWS
    cblk = pl.BlockSpec((SAMPLE_SEQS, WINDOW, SWA_KV_WIDTH), lambda i: (i, 0, 0))
    big = pltpu.VMEM((SWA_Q_HEADS * SAMPLE_ROWS, LANES), F32)
    return pl.pallas_call(
        _swa_sample_body,
        grid=(nseq // SAMPLE_SEQS,),
        in_specs=[pl.BlockSpec(memory_space=pltpu.SMEM),
                  pl.BlockSpec((SAMPLE_ROWS, SWA_WIDTH), lambda i: (i + blk0, 0)),
                  pl.BlockSpec((SAMPLE_ROWS, 2 * SWA_KV_WIDTH), lambda i: (i + blk0, 0)),
                  cblk, cblk,
                  pl.BlockSpec((1, SWA_WIDTH), lambda i: (0, 0))],
        out_specs=[pl.BlockSpec((SAMPLE_ROWS, SWA_WIDTH), lambda i: (i, 0)), cblk, cblk],
        out_shape=[jax.ShapeDtypeStruct((rows, SWA_WIDTH), BF16),
                   jax.ShapeDtypeStruct(kc.shape, F32), jax.ShapeDtypeStruct(vc.shape, F32)],
        scratch_shapes=[big, big, big, big],
        compiler_params=pltpu.CompilerParams(dimension_semantics=("arbitrary",)),
        name="swa_sample",
    )(sinks, qs, kvn, kc, vc, nrm)


def _prepare_weights(ffn1_norm, ffn1_w_gate, ffn1_w_up, ffn1_w_down, mix_norm, w_in, w_gate_up, b_gate,
                     gla_head_norm, swa_out_norm, swa_sinks, w_out,
                     ffn2_norm, ffn2_w_gate, ffn2_w_up, ffn2_w_down, final_norm, layer):
    a0 = 2 * GLA_KEY_WIDTH + 2 * GLA_WIDTH
    a1 = a0 + GLA_GATE_RANK
    win = w_in[layer]
    win = jnp.concatenate([win[:, :a0], win[:, a1:], win[:, a0:a1],
                           jnp.zeros((D_MODEL, LANES - GLA_GATE_RANK), win.dtype)], axis=1)
    wgu = jnp.concatenate([w_gate_up[layer],
                           jnp.zeros((LANES - GLA_GATE_RANK, GLA_KEY_WIDTH), w_gate_up.dtype)], axis=0)
    row = lambda a: a.reshape(1, -1).astype(F32)
    return dict(
        n1=row(ffn1_norm[layer]), wg1=ffn1_w_gate[layer], wu1=ffn1_w_up[layer], wd1=ffn1_w_down[layer],
        nm=row(mix_norm[layer]), win=win.astype(BF16), wgu=wgu.astype(BF16), bg=row(b_gate[layer]),
        gn=row(gla_head_norm[layer]), sn=row(swa_out_norm[layer]), sinks=swa_sinks[layer].astype(F32),
        wo=w_out[layer], n2=row(ffn2_norm[layer]), wg2=ffn2_w_gate[layer], wu2=ffn2_w_up[layer],
        wd2=ffn2_w_down[layer], nf=row(final_norm))


def kernel(x_prompt, x_sample, state_gla, cache_swa_k, cache_swa_v, ffn1_norm, ffn1_w_gate, ffn1_w_up,
           ffn1_w_down, mix_norm, w_in, w_gate_up, b_gate, gla_head_norm, swa_out_norm, swa_sinks, w_out,
           ffn2_norm, ffn2_w_gate, ffn2_w_up, ffn2_w_down, final_norm):
    depth = state_gla.shape[0]
    assert depth == 1 and x_prompt.shape[0] == 1 and x_sample.shape[1] == DEC_SEQ
    seq = x_prompt.shape[1]
    nseq = x_sample.shape[0]
    w = _prepare_weights(ffn1_norm, ffn1_w_gate, ffn1_w_up, ffn1_w_down, mix_norm, w_in, w_gate_up, b_gate,
                         gla_head_norm, swa_out_norm, swa_sinks, w_out,
                         ffn2_norm, ffn2_w_gate, ffn2_w_up, ffn2_w_down, final_norm, 0)

    x1, qk, g, v, r, qs, kv = _stage_a(x_prompt.reshape(seq, D_MODEL),
                                       x_sample.reshape(nseq * DEC_SEQ, D_MODEL), w)

    og_s, state_s = _gla_sample(qk, g, v, r, w["gn"],
                                state_gla[0].reshape(nseq, GLA_KEY_WIDTH, GLA_HEAD_V), seq)
    cache_w = cache_swa_k.shape[2]
    os_s, k_cache_s, v_cache_s = _swa_sample(
        w["sinks"], qs, kv, cache_swa_k[0].reshape(nseq, cache_w, SWA_KV_WIDTH),
        cache_swa_v[0].reshape(nseq, cache_w, SWA_KV_WIDTH), w["sn"], seq)

    y_prompt, y_sample, state_p = _mix_out(qk, g, v, r, qs, kv, x1, og_s, os_s, w)
    y_prompt = y_prompt.reshape(1, seq, D_MODEL)
    y_sample = y_sample.reshape(nseq, DEC_SEQ, D_MODEL)
    cw = min(WINDOW, seq)
    k_cache_p = kv[seq - cw:seq, 0:SWA_KV_WIDTH].reshape(1, 1, cw, SWA_KV_HEADS, SWA_HEAD_DIM)
    v_cache_p = kv[seq - cw:seq, SWA_KV_WIDTH:].reshape(1, 1, cw, SWA_KV_HEADS, SWA_HEAD_DIM)
    state_p = state_p.reshape(1, 1, GLA_HEADS, GLA_HEAD_K, GLA_HEAD_V)
    state_s = state_s.reshape(1, nseq, GLA_HEADS, GLA_HEAD_K, GLA_HEAD_V)
    k_cache_s = k_cache_s.reshape(1, nseq, cache_w, SWA_KV_HEADS, SWA_HEAD_DIM)
    v_cache_s = v_cache_s.reshape(1, nseq, cache_w, SWA_KV_HEADS, SWA_HEAD_DIM)

    return (y_prompt, y_sample, state_p, k_cache_p, v_cache_p, state_s, k_cache_s, v_cache_s)
```

```python
import functools
import itertools

import jax
import jax.numpy as jnp
from jax import lax
from jax.experimental import pallas as pl
from jax.experimental.pallas import tpu as pltpu

F32 = jnp.float32
BF16 = jnp.bfloat16

D_MODEL = 1024
D_FF = 2816
GLA_HEADS = 4
GLA_HEAD_K = 64
GLA_HEAD_V = 128
GLA_KEY_WIDTH = GLA_HEADS * GLA_HEAD_K
GLA_WIDTH = GLA_HEADS * GLA_HEAD_V
GLA_GATE_RANK = 16
GLA_GATE_TAU = 16.0
SWA_HEAD_DIM = 64
SWA_Q_HEADS = 8
SWA_KV_HEADS = 2
SWA_GROUP = SWA_Q_HEADS // SWA_KV_HEADS
SWA_WIDTH = SWA_Q_HEADS * SWA_HEAD_DIM
SWA_KV_WIDTH = SWA_KV_HEADS * SWA_HEAD_DIM
WINDOW = 128
DEC_SEQ = 4
NORM_EPS = 1e-6
HEAD_SCALE = 0.125

LANES = 128
SUBLANES = 8
VMEM_LIMIT_BYTES = 56 * 1024 * 1024

ROW_TILE = 512
FF_CHUNK = 256
DOWN_CHUNK = 256
GLA_CHUNK = 128
SAMPLE_SEQS = 32
SAMPLE_ROWS = SAMPLE_SEQS * DEC_SEQ
MIXER_STAGES = 3 * (ROW_TILE // GLA_CHUNK) + 3 * (ROW_TILE // WINDOW)
SIDE_STAGES_AT_NORM = 2
DECAY_CLAMP = 60.0

PROJ_Q_G = 0
PROJ_K_G = 256
PROJ_V_G = 512
PROJ_R_G = 1024
PROJ_Q_S = 1536
PROJ_KV_S = 2048
PROJ_A = 2304
PROJ_WIDTH = 2432


def _dot(a, b):
    return jnp.dot(a, b, preferred_element_type=F32)


def _dot_tb(a, b):
    return lax.dot_general(a, b, (((1,), (1,)), ((), ())), preferred_element_type=F32)


def _dot_ta(a, b):
    return lax.dot_general(a, b, (((0,), (0,)), ((), ())), preferred_element_type=F32)


def _rms(x, g):
    return x * lax.rsqrt(jnp.mean(x * x, axis=-1, keepdims=True) + NORM_EPS) * g


def _swiglu(h, wg_ref, wu_ref, wd_ref, act_ref, side_stages=iter(()), n_side_stages=0):
    n_dots = 2 * (D_FF // FF_CHUNK) + D_MODEL // DOWN_CHUNK
    done = [0, 0]

    def after_dot():
        done[0] += 1
        while done[1] * n_dots < done[0] * n_side_stages:
            next(side_stages, None)
            done[1] += 1

    for c0 in range(0, D_FF, FF_CHUNK):
        g = _dot(h, wg_ref[:, c0:c0 + FF_CHUNK])
        after_dot()
        u = _dot(h, wu_ref[:, c0:c0 + FF_CHUNK])
        after_dot()
        act_ref[:, c0:c0 + FF_CHUNK] = (g * jax.nn.sigmoid(g) * u).astype(BF16)
    outs = []
    for n0 in range(0, D_MODEL, DOWN_CHUNK):
        outs.append(_dot(act_ref[...], wd_ref[:, n0:n0 + DOWN_CHUNK]))
        after_dot()
    return jnp.concatenate(outs, axis=1)


def _head_norm_gate(o, r, gn):
    parts = []
    for h in range(GLA_HEADS):
        sl = slice(h * GLA_HEAD_V, (h + 1) * GLA_HEAD_V)
        parts.append(_rms(o[:, sl], gn[:, sl]))
    return (jnp.concatenate(parts, axis=1) * (r * jax.nn.sigmoid(r))).astype(BF16)


def _out_ffn(x1_ref, og, osw, wo_ref, n2_ref, wg_ref, wu_ref, wd_ref, nf_ref, y_ref, act_ref, side_stages=None):
    n_inside = 0 if side_stages is None else MIXER_STAGES - 2 * SIDE_STAGES_AT_NORM
    side_stages = iter(()) if side_stages is None else side_stages
    mixed = jnp.concatenate([og, osw], axis=1)
    x2 = x1_ref[...] + _dot(mixed, wo_ref[...])
    for _ in range(SIDE_STAGES_AT_NORM):
        next(side_stages, None)
    h = _rms(x2, n2_ref[...]).astype(BF16)
    x3 = x2 + 0.5 * _swiglu(h, wg_ref, wu_ref, wd_ref, act_ref, side_stages, n_inside)
    for _ in side_stages:
        pass
    y_ref[...] = _rms(x3, nf_ref[...])


_HBM = pl.BlockSpec(memory_space=pl.ANY)
WIDE_CHUNK_ROWS = 128
NARROW_CHUNK_ROWS = 256
LOAD_SLOTS = 4


def _load_weights_bf16(pairs):
    jobs = []
    for src, dst in pairs:
        rows, cols = src.shape
        assert cols in (D_FF, D_MODEL) and dst.shape == src.shape
        kind, step = (0, WIDE_CHUNK_ROWS) if cols == D_FF else (1, NARROW_CHUNK_ROWS)
        assert rows % step == 0
        jobs += [(src, dst, r0, step, kind) for r0 in range(0, rows, step)]

    def body(wide, narrow, sems):
        staging = (wide, narrow)
        used = [0, 0]
        copies = []
        for src, _, r0, nr, kind in jobs:
            slot = used[kind] % LOAD_SLOTS
            used[kind] += 1
            copies.append((pltpu.make_async_copy(src.at[pl.ds(r0, nr), :], staging[kind].at[slot],
                                                 sems.at[kind, slot]), slot))
        for copy, _ in copies[:LOAD_SLOTS - 1]:
            copy.start()
        for j, (_, dst, r0, nr, kind) in enumerate(jobs):
            ahead = j + LOAD_SLOTS - 1
            if ahead < len(jobs):
                copies[ahead][0].start()
            copy, slot = copies[j]
            copy.wait()
            dst[r0:r0 + nr, :] = staging[kind][slot].astype(BF16)

    pl.run_scoped(body,
                  pltpu.VMEM((LOAD_SLOTS, WIDE_CHUNK_ROWS, D_FF), F32),
                  pltpu.VMEM((LOAD_SLOTS, NARROW_CHUNK_ROWS, D_MODEL), F32),
                  pltpu.SemaphoreType.DMA((2, LOAD_SLOTS)))


def _stage_a_body(xp_ref, xs_ref, n1_ref, wg_hbm, wu_hbm, wd_hbm, nm_ref, win_ref, wgu_ref, bg_ref,
                  x1_ref, qk_ref, g_ref, v_ref, r_ref, qs_ref, kv_ref, act_ref, wg_ref, wu_ref, wd_ref):
    i = pl.program_id(0)

    @pl.when(i == 0)
    def _():
        _load_weights_bf16([(wg_hbm, wg_ref), (wu_hbm, wu_ref), (wd_hbm, wd_ref)])

    x = jnp.where(i < pl.num_programs(0) - 1, xp_ref[...], xs_ref[...])
    h = _rms(x, n1_ref[...]).astype(BF16)
    x1 = x + 0.5 * _swiglu(h, wg_ref, wu_ref, wd_ref, act_ref)
    x1_ref[...] = x1
    h2 = _rms(x1, nm_ref[...]).astype(BF16)
    proj = _dot(h2, win_ref[...])
    qk_ref[:, 0:GLA_KEY_WIDTH] = proj[:, PROJ_Q_G:PROJ_K_G] * HEAD_SCALE
    qk_ref[:, GLA_KEY_WIDTH:] = proj[:, PROJ_K_G:PROJ_V_G]
    v_ref[...] = proj[:, PROJ_V_G:PROJ_R_G].astype(BF16)
    r_ref[...] = proj[:, PROJ_R_G:PROJ_Q_S]
    qs_ref[...] = (proj[:, PROJ_Q_S:PROJ_KV_S] * HEAD_SCALE).astype(BF16)
    kv_ref[...] = proj[:, PROJ_KV_S:PROJ_A]
    a = proj[:, PROJ_A:PROJ_WIDTH].astype(BF16)
    z = _dot(a, wgu_ref[...]) + bg_ref[...]
    g_ref[...] = jax.nn.log_sigmoid(z) * (1.0 / GLA_GATE_TAU)


def _resident(shape):
    return pl.BlockSpec(shape, lambda i: (0,) * len(shape), pipeline_mode=pl.Buffered(1))


def _rows(tm, n):
    return pl.BlockSpec((tm, n), lambda i: (i, 0))


def _stage_a(xp, xs, w):
    t = xp.shape[0]
    assert t % ROW_TILE == 0 and xs.shape[0] == ROW_TILE
    nt = t // ROW_TILE
    out_widths = ((D_MODEL, F32), (2 * GLA_KEY_WIDTH, F32), (GLA_KEY_WIDTH, F32), (GLA_WIDTH, BF16),
                  (GLA_WIDTH, F32), (SWA_WIDTH, BF16), (2 * SWA_KV_WIDTH, F32))
    return pl.pallas_call(
        _stage_a_body,
        grid=(nt + 1,),
        in_specs=[pl.BlockSpec((ROW_TILE, D_MODEL), lambda i: (jnp.minimum(i, nt - 1), 0)),
                  _resident((ROW_TILE, D_MODEL)), _resident((1, D_MODEL)),
                  _HBM, _HBM, _HBM,
                  _resident((1, D_MODEL)), _resident((D_MODEL, PROJ_WIDTH)),
                  _resident((LANES, GLA_KEY_WIDTH)), _resident((1, GLA_KEY_WIDTH))],
        out_specs=[_rows(ROW_TILE, n) for n, _ in out_widths],
        out_shape=[jax.ShapeDtypeStruct((t + ROW_TILE, n), dt) for n, dt in out_widths],
        scratch_shapes=[pltpu.VMEM((ROW_TILE, D_FF), BF16), pltpu.VMEM((D_MODEL, D_FF), BF16),
                        pltpu.VMEM((D_MODEL, D_FF), BF16), pltpu.VMEM((D_FF, D_MODEL), BF16)],
        compiler_params=pltpu.CompilerParams(dimension_semantics=("arbitrary",),
                                             vmem_limit_bytes=VMEM_LIMIT_BYTES),
        name="stage_a_ffn1_proj",
    )(xp, xs, w["n1"], w["wg1"], w["wu1"], w["wd1"], w["nm"], w["win"], w["wgu"], w["bg"])


def _split_bf16(x):
    hi = x.astype(BF16)
    lo = (x - hi.astype(F32)).astype(BF16)
    return hi, lo


def _alternate(a, b):
    pending = [iter(a), iter(b)]
    while pending:
        for it in list(pending):
            try:
                next(it)
            except StopIteration:
                pending.remove(it)
                continue
            yield


def _gla_stages(qk_ref, g_ref, v_ref, r_ref, gn_ref, s_ref, og_scr, cum_scr, inter_scr, flag_ref):
    c_len = GLA_CHUNK
    n_chunks = ROW_TILE // c_len
    causal = (lax.broadcasted_iota(jnp.int32, (c_len, c_len), 0)
              >= lax.broadcasted_iota(jnp.int32, (c_len, c_len), 1))
    ltri = jnp.where(causal, 1.0, 0.0).astype(BF16)
    causal_cat = (lax.broadcasted_iota(jnp.int32, (c_len, GLA_HEADS * c_len), 0)
                  >= lax.broadcasted_iota(jnp.int32, (c_len, GLA_HEADS * c_len), 1) % c_len)
    lane_head = lax.broadcasted_iota(jnp.int32, (1, GLA_KEY_WIDTH), 1) // GLA_HEAD_K
    row_head = lax.broadcasted_iota(jnp.int32, (GLA_KEY_WIDTH, 1), 0) // GLA_HEAD_K
    eye = (lax.broadcasted_iota(jnp.int32, (GLA_KEY_WIDTH, GLA_KEY_WIDTH), 0)
           == lax.broadcasted_iota(jnp.int32, (GLA_KEY_WIDTH, GLA_KEY_WIDTH), 1))
    zero_b = jnp.zeros((), BF16)
    zero_v = jnp.zeros((c_len, GLA_HEAD_V), BF16)
    worst = []

    def chunk(c):
        rows = slice(c * c_len, (c + 1) * c_len)
        g_hi, g_lo = _split_bf16(g_ref[rows, :])
        cum = _dot(ltri, g_hi) + _dot(ltri, g_lo)
        yield
        cum_scr[rows, :] = cum
        last = cum[c_len - 1:c_len, :]
        q = qk_ref[rows, 0:GLA_KEY_WIDTH]
        k = qk_ref[rows, GLA_KEY_WIDTH:]
        vb = v_ref[rows, :]
        qe = (q * jnp.exp(cum)).astype(BF16)
        ke = (k * jnp.exp(jnp.minimum(-cum, DECAY_CLAMP))).astype(BF16)
        kl = (k * jnp.exp(last - cum)).astype(BF16)
        state = s_ref[...]
        sb = state.astype(BF16)
        s_bd = jnp.concatenate([jnp.where(row_head == h, sb, zero_b) for h in range(GLA_HEADS)], axis=1)
        o_inter = _dot(qe, s_bd)
        inter_scr[rows, :] = o_inter
        ke_bd = jnp.concatenate([jnp.where(lane_head == h, ke, zero_b) for h in range(GLA_HEADS)], axis=0)
        attn = _dot_tb(qe, ke_bd)
        upds = []
        for p in range(GLA_HEADS // 2):
            u = _dot_ta(kl[:, p * LANES:(p + 1) * LANES], vb[:, 2 * p * GLA_HEAD_V:(2 * p + 2) * GLA_HEAD_V])
            upds.append(u[0:GLA_HEAD_K, 0:GLA_HEAD_V])
            upds.append(u[GLA_HEAD_K:, GLA_HEAD_V:])
        yield
        attn = jnp.where(causal_cat, attn, 0.0).astype(BF16)
        o_pairs = []
        for p in range(GLA_HEADS // 2):
            v_a = vb[:, (2 * p) * GLA_HEAD_V:(2 * p + 1) * GLA_HEAD_V]
            v_b = vb[:, (2 * p + 1) * GLA_HEAD_V:(2 * p + 2) * GLA_HEAD_V]
            v_bd = jnp.concatenate([jnp.concatenate([v_a, zero_v], axis=1),
                                    jnp.concatenate([zero_v, v_b], axis=1)], axis=0)
            o_pairs.append(_dot(attn[:, 2 * p * c_len:(2 * p + 2) * c_len], v_bd))
        og_scr[rows, :] = _head_norm_gate(o_inter + jnp.concatenate(o_pairs, axis=1), r_ref[rows, :],
                                          gn_ref[...])
        last_col = jnp.sum(jnp.where(eye, last, 0.0), axis=1, keepdims=True)
        s_ref[...] = state * jnp.exp(last_col) + jnp.concatenate(upds, axis=0)
        worst.append(jnp.min(last, axis=1, keepdims=True))
        if c == n_chunks - 1:
            tile_min = functools.reduce(jnp.minimum, worst)
            flag_ref[0] = jnp.where(tile_min[0, 0] < -DECAY_CLAMP, 1, 0)
        yield

    return itertools.chain.from_iterable(chunk(c) for c in range(n_chunks))


def _gla_tile_exact(qk_ref, v_ref, r_ref, gn_ref, og_scr, o_scr, cum_scr, inter_scr):
    c_len = GLA_CHUNK
    ind = jnp.where(lax.broadcasted_iota(jnp.int32, (GLA_KEY_WIDTH, LANES), 0) // GLA_HEAD_K
                    == lax.broadcasted_iota(jnp.int32, (GLA_KEY_WIDTH, LANES), 1), 1.0, 0.0).astype(BF16)
    j_idx = lax.broadcasted_iota(jnp.int32, (c_len, 1), 0)
    for c in range(ROW_TILE // c_len):
        r0 = c * c_len
        rows = slice(r0, r0 + c_len)

        def one_row(i, carry, r0=r0, rows=rows):
            ci = cum_scr[pl.ds(r0 + i, 1), :]
            qi = qk_ref[pl.ds(r0 + i, 1), 0:GLA_KEY_WIDTH]
            kk = qk_ref[rows, GLA_KEY_WIDTH:]
            dec = jnp.exp(jnp.minimum(ci - cum_scr[rows, :], 0.0))
            a_cols = _dot(((qi * kk) * dec).astype(BF16), ind)
            outs = []
            for h in range(GLA_HEADS):
                w_col = jnp.where(j_idx <= i, a_cols[:, h:h + 1], 0.0)
                v_h = v_ref[rows, h * GLA_HEAD_V:(h + 1) * GLA_HEAD_V].astype(F32)
                outs.append(jnp.sum(w_col * v_h, axis=0, keepdims=True))
            o_scr[pl.ds(r0 + i, 1), :] = inter_scr[pl.ds(r0 + i, 1), :] + jnp.concatenate(outs, axis=1)
            return carry

        lax.fori_loop(0, c_len, one_row, 0)
    og_scr[...] = _head_norm_gate(o_scr[...], r_ref[...], gn_ref[...])


def _dup_halves(x):
    lo = lax.broadcasted_iota(jnp.int32, (1, LANES), 1) < SWA_HEAD_DIM
    sw = pltpu.roll(x, SWA_HEAD_DIM, 1)
    return jnp.where(lo, x, sw).astype(BF16), jnp.where(lo, sw, x).astype(BF16)


def _swa_stages(sink_ref, qs_ref, kv_ref, kvprev_scr, sn_ref, os_scr, no_prev_bias):
    w = WINDOW
    n_blocks = ROW_TILE // w
    lo = lax.broadcasted_iota(jnp.int32, (1, LANES), 1) < SWA_HEAD_DIM
    tri = (lax.broadcasted_iota(jnp.int32, (w, w), 1) <= lax.broadcasted_iota(jnp.int32, (w, w), 0))
    zero_b = jnp.zeros((), BF16)

    def block(b):
        rows = slice(b * w, (b + 1) * w)
        kv_prev = kvprev_scr[...] if b == 0 else kv_ref[(b - 1) * w:b * w, :]
        kv = jnp.concatenate([kv_prev, kv_ref[rows, :]], axis=0)
        kk = _dup_halves(kv[:, 0:SWA_KV_WIDTH])
        vv = _dup_halves(kv[:, SWA_KV_WIDTH:])
        q = qs_ref[rows, :]
        scores = []
        for grp in range(SWA_KV_HEADS):
            stacked = []
            for j in range(SWA_GROUP):
                hq = grp * SWA_GROUP + j
                tile = q[:, (hq // 2) * LANES:(hq // 2 + 1) * LANES]
                stacked.append(jnp.where(lo if hq % 2 == 0 else ~lo, tile, zero_b))
            scores.append(_dot_tb(jnp.concatenate(stacked, axis=0), kk[grp]))
        yield
        outs, dens = [], []
        for grp in range(SWA_KV_HEADS):
            s = scores[grp]
            probs = []
            for j in range(SWA_GROUP):
                sink = sink_ref[grp * SWA_GROUP + j]
                s_prev = s[j * w:(j + 1) * w, 0:w]
                if b == 0:
                    s_prev = s_prev + no_prev_bias
                sf = jnp.where(tri, s[j * w:(j + 1) * w, w:], s_prev)
                m = jnp.maximum(jnp.max(sf, axis=1, keepdims=True), sink)
                p = jnp.exp(sf - m)
                dens.append(jnp.sum(p, axis=1, keepdims=True) + jnp.exp(sink - m))
                pb = p.astype(BF16)
                probs.append(jnp.concatenate([jnp.where(tri, zero_b, pb), jnp.where(tri, pb, zero_b)], axis=1))
            outs.append(_dot(jnp.concatenate(probs, axis=0), vv[grp]))
        yield
        tiles = []
        for grp in range(SWA_KV_HEADS):
            res = [outs[grp][j * w:(j + 1) * w] / dens[grp * SWA_GROUP + j] for j in range(SWA_GROUP)]
            tiles.append(jnp.where(lo, res[0], res[1]))
            tiles.append(jnp.where(lo, res[2], res[3]))
        os_scr[rows, :] = _rms(jnp.concatenate(tiles, axis=1), sn_ref[...]).astype(BF16)
        if b == n_blocks - 1:
            kvprev_scr[...] = kv_ref[rows, :]
        yield

    return itertools.chain.from_iterable(block(b) for b in range(n_blocks))


def _mix_out_body(sink_ref, qk_ref, g_ref, v_ref, r_ref, qs_ref, kv_ref, x1_ref, ogs_ref, oss_ref,
                  gn_ref, sn_ref, wo_ref, n2_ref, wg_ref, wu_ref, wd_ref, nf_ref, yp_ref, ys_ref, s_ref,
                  og_scr, os_scr, kvprev_scr, o_scr, cum_scr, inter_scr, act_scr, flag_ref):
    i = pl.program_id(0)
    n_prompt = pl.num_programs(0) - 2

    def stage_d(og, osw, y_ref, side_stages=None):
        _out_ffn(x1_ref, og, osw, wo_ref, n2_ref, wg_ref, wu_ref, wd_ref, nf_ref, y_ref, act_scr,
                 side_stages)

    def mixer_stages():
        gla = _gla_stages(qk_ref, g_ref, v_ref, r_ref, gn_ref, s_ref, og_scr, cum_scr, inter_scr, flag_ref)
        swa = _swa_stages(sink_ref, qs_ref, kv_ref, kvprev_scr, sn_ref, os_scr,
                          jnp.where(i > 0, 0.0, -jnp.inf))
        return _alternate(gla, swa)

    @pl.when(i == 0)
    def _():
        s_ref[...] = jnp.zeros_like(s_ref)
        kvprev_scr[...] = jnp.zeros_like(kvprev_scr)
        for _ in mixer_stages():
            pass

    @pl.when(jnp.logical_and(i > 0, i < n_prompt))
    def _():
        stage_d(og_scr[...], os_scr[...], yp_ref, mixer_stages())

    @pl.when(i == n_prompt)
    def _():
        stage_d(og_scr[...], os_scr[...], yp_ref)
        flag_ref[0] = 0

    @pl.when(i == n_prompt + 1)
    def _():
        stage_d(ogs_ref[...], oss_ref[...], ys_ref)

    @pl.when(flag_ref[0] != 0)
    def _():
        _gla_tile_exact(qk_ref, v_ref, r_ref, gn_ref, og_scr, o_scr, cum_scr, inter_scr)


def _mix_out(qk, g, v, r, qs, kv, x1, og_s, os_s, w):
    t = qk.shape[0] - ROW_TILE
    assert t % ROW_TILE == 0 and og_s.shape[0] == ROW_TILE
    nt = t // ROW_TILE
    cur = lambda n: pl.BlockSpec((ROW_TILE, n), lambda i: (jnp.minimum(i, nt - 1), 0))
    return pl.pallas_call(
        _mix_out_body,
        grid=(nt + 2,),
        in_specs=[pl.BlockSpec(memory_space=pltpu.SMEM),
                  cur(2 * GLA_KEY_WIDTH), cur(GLA_KEY_WIDTH), cur(GLA_WIDTH), cur(GLA_WIDTH),
                  cur(SWA_WIDTH), cur(2 * SWA_KV_WIDTH),
                  pl.BlockSpec((ROW_TILE, D_MODEL), lambda i: (jnp.maximum(i - 1, 0), 0)),
                  _resident((ROW_TILE, GLA_WIDTH)), _resident((ROW_TILE, SWA_WIDTH)),
                  _resident((1, GLA_WIDTH)), _resident((1, SWA_WIDTH)),
                  _resident((D_MODEL, D_MODEL)), _resident((1, D_MODEL)),
                  _resident((D_MODEL, D_FF)), _resident((D_MODEL, D_FF)), _resident((D_FF, D_MODEL)),
                  _resident((1, D_MODEL))],
        out_specs=[pl.BlockSpec((ROW_TILE, D_MODEL), lambda i: (jnp.clip(i - 1, 0, nt - 1), 0)),
                   pl.BlockSpec((ROW_TILE, D_MODEL), lambda i: (0, 0)),
                   pl.BlockSpec((GLA_KEY_WIDTH, GLA_HEAD_V), lambda i: (0, 0))],
        out_shape=[jax.ShapeDtypeStruct((t, D_MODEL), F32),
                   jax.ShapeDtypeStruct((ROW_TILE, D_MODEL), F32),
                   jax.ShapeDtypeStruct((GLA_KEY_WIDTH, GLA_HEAD_V), F32)],
        scratch_shapes=[pltpu.VMEM((ROW_TILE, GLA_WIDTH), BF16), pltpu.VMEM((ROW_TILE, SWA_WIDTH), BF16),
                        pltpu.VMEM((WINDOW, 2 * SWA_KV_WIDTH), F32),
                        pltpu.VMEM((ROW_TILE, GLA_WIDTH), F32), pltpu.VMEM((ROW_TILE, GLA_KEY_WIDTH), F32),
                        pltpu.VMEM((ROW_TILE, GLA_WIDTH), F32), pltpu.VMEM((ROW_TILE, D_FF), BF16),
                        pltpu.SMEM((1,), jnp.int32)],
        compiler_params=pltpu.CompilerParams(dimension_semantics=("arbitrary",),
                                             vmem_limit_bytes=VMEM_LIMIT_BYTES),
        name="prompt_mixers_stage_d",
    )(w["sinks"], qk, g, v, r, qs, kv, x1, og_s, os_s, w["gn"], w["sn"], w["wo"], w["n2"], w["wg2"],
      w["wu2"], w["wd2"], w["nf"])


def _gla_sample_body(qk_ref, g_ref, v_ref, r_ref, gn_ref, s_ref, og_ref, so_ref, o_scr):
    rows = SAMPLE_ROWS
    g = g_ref[...]
    q = qk_ref[:, 0:GLA_KEY_WIDTH]
    k = qk_ref[:, GLA_KEY_WIDTH:]
    vb = v_ref[...]
    vf = vb.astype(F32)
    tok = lax.broadcasted_iota(jnp.int32, (rows, 1), 0) % DEC_SEQ

    cum = g
    for d in range(1, DEC_SEQ):
        cum = cum + jnp.where(tok >= d, pltpu.roll(g, d, 0), 0.0)
    tot = jnp.where(tok == DEC_SEQ - 1, cum, 0.0)
    for d in range(1, DEC_SEQ):
        tot = tot + jnp.where(tok == DEC_SEQ - 1 - d, pltpu.roll(cum, rows - d, 0), 0.0)

    qe = q * jnp.exp(cum)
    kl = k * jnp.exp(tot - cum)
    decay_t = jnp.exp(tot).T

    ind = jnp.where(lax.broadcasted_iota(jnp.int32, (GLA_KEY_WIDTH, LANES), 0) // GLA_HEAD_K
                    == lax.broadcasted_iota(jnp.int32, (GLA_KEY_WIDTH, LANES), 1), 1.0, 0.0).astype(BF16)
    expand = jnp.where(lax.broadcasted_iota(jnp.int32, (LANES, GLA_WIDTH), 0)
                       == lax.broadcasted_iota(jnp.int32, (LANES, GLA_WIDTH), 1) // GLA_HEAD_V,
                       1.0, 0.0).astype(BF16)

    o_intra = jnp.zeros((rows, GLA_WIDTH), F32)
    for d in range(DEC_SEQ):
        k_d = k if d == 0 else pltpu.roll(k, d, 0)
        c_d = cum if d == 0 else pltpu.roll(cum, d, 0)
        v_d = vf if d == 0 else pltpu.roll(vf, d, 0)
        pair = jnp.where(tok >= d, q * k_d * jnp.exp(jnp.minimum(cum - c_d, 0.0)), 0.0)
        a = _dot(pair.astype(BF16), ind)
        o_intra = o_intra + _dot(a.astype(BF16), expand) * v_d

    lane_head = lax.broadcasted_iota(jnp.int32, (1, GLA_KEY_WIDTH), 1) // GLA_HEAD_K
    row8 = lax.broadcasted_iota(jnp.int32, (SUBLANES, 1), 0)
    row32 = lax.broadcasted_iota(jnp.int32, (GLA_HEADS * SUBLANES, 1), 0)
    for pair_idx in range(SAMPLE_SEQS // 2):
        r8 = slice(pair_idx * SUBLANES, (pair_idx + 1) * SUBLANES)
        q8 = qe[r8, :]
        lhs = jnp.concatenate([jnp.where(lane_head == h, q8, 0.0) for h in range(GLA_HEADS)],
                              axis=0).astype(BF16)
        kl8 = kl[r8, :]
        v8 = vb[r8, :]
        res = []
        for s in range(2):
            b = 2 * pair_idx + s
            state = s_ref[b]
            res.append(_dot(lhs, state.astype(BF16)))
            kl_b = jnp.where(row8 // DEC_SEQ == s, kl8, 0.0).astype(BF16)
            upd = _dot_ta(kl_b, v8)
            upd = jnp.concatenate(
                [upd[h * GLA_HEAD_K:(h + 1) * GLA_HEAD_K, h * GLA_HEAD_V:(h + 1) * GLA_HEAD_V]
                 for h in range(GLA_HEADS)], axis=0)
            so_ref[b] = state * decay_t[:, DEC_SEQ * b:DEC_SEQ * b + 1] + upd
        sel = jnp.where(row32 % SUBLANES < DEC_SEQ, res[0], res[1])
        o_scr[r8, :] = jnp.concatenate([sel[h * SUBLANES:(h + 1) * SUBLANES] for h in range(GLA_HEADS)], axis=1)

    og_ref[...] = _head_norm_gate(o_scr[...] + o_intra, r_ref[...], gn_ref[...])


def _gla_sample(qk, g, v, r, gn, state, row0):
    nseq = state.shape[0]
    rows = nseq * DEC_SEQ
    assert nseq % SAMPLE_SEQS == 0 and row0 % SAMPLE_ROWS == 0
    blk0 = row0 // SAMPLE_ROWS
    src = lambda n: pl.BlockSpec((SAMPLE_ROWS, n), lambda i: (i + blk0, 0))
    blk = lambda n: pl.BlockSpec((SAMPLE_ROWS, n), lambda i: (i, 0))
    sblk = pl.BlockSpec((SAMPLE_SEQS, GLA_KEY_WIDTH, GLA_HEAD_V), lambda i: (i, 0, 0))
    return pl.pallas_call(
        _gla_sample_body,
        grid=(nseq // SAMPLE_SEQS,),
        in_specs=[src(2 * GLA_KEY_WIDTH), src(GLA_KEY_WIDTH), src(GLA_WIDTH), src(GLA_WIDTH),
                  pl.BlockSpec((1, GLA_WIDTH), lambda i: (0, 0)), sblk],
        out_specs=[blk(GLA_WIDTH), sblk],
        out_shape=[jax.ShapeDtypeStruct((rows, GLA_WIDTH), BF16),
                   jax.ShapeDtypeStruct((nseq, GLA_KEY_WIDTH, GLA_HEAD_V), F32)],
        scratch_shapes=[pltpu.VMEM((SAMPLE_ROWS, GLA_WIDTH), F32)],
        compiler_params=pltpu.CompilerParams(dimension_semantics=("arbitrary",)),
        name="gla_sample",
    )(qk, g, v, r, gn, state)


def _swa_sample_body(sink_ref, q_ref, kvn_ref, kc_ref, vc_ref, nrm_ref, o_ref, kco_ref, vco_ref,
                     lhs_scr, sc_scr, pc_scr, oc_scr):
    rows = SAMPLE_ROWS
    q = q_ref[...].astype(F32)
    lo = lax.broadcasted_iota(jnp.int32, (1, LANES), 1) < SWA_HEAD_DIM
    for hq in range(SWA_Q_HEADS):
        grp = hq // SWA_GROUP
        tile = q[:, (hq // 2) * LANES:(hq // 2 + 1) * LANES]
        src = tile if hq % 2 == grp else pltpu.roll(tile, SWA_HEAD_DIM, 1)
        lhs_scr[hq * rows:(hq + 1) * rows, :] = jnp.where(lo if grp == 0 else ~lo, src, 0.0)

    kvn = kvn_ref[...]
    k_new = kvn[:, 0:SWA_KV_WIDTH]
    v_new = kvn[:, SWA_KV_WIDTH:]
    s_new = _dot_tb(lhs_scr[...].astype(BF16), k_new.astype(BF16))

    row64 = lax.broadcasted_iota(jnp.int32, (SWA_Q_HEADS * SUBLANES, 1), 0)
    first_of_pair = row64 % SUBLANES < DEC_SEQ

    def gather_pair(ref, pair_idx):
        return jnp.concatenate(
            [ref[hq * rows + pair_idx * SUBLANES:hq * rows + (pair_idx + 1) * SUBLANES, :]
             for hq in range(SWA_Q_HEADS)], axis=0).astype(BF16)

    def scatter_pair(ref, pair_idx, val):
        for hq in range(SWA_Q_HEADS):
            ref[hq * rows + pair_idx * SUBLANES:hq * rows + (pair_idx + 1) * SUBLANES, :] = (
                val[hq * SUBLANES:(hq + 1) * SUBLANES])

    for pair_idx in range(SAMPLE_SEQS // 2):
        l64 = gather_pair(lhs_scr, pair_idx)
        sa = _dot(l64, kc_ref[2 * pair_idx].astype(BF16))
        sb = _dot(l64, kc_ref[2 * pair_idx + 1].astype(BF16))
        scatter_pair(sc_scr, pair_idx, jnp.where(first_of_pair, sa, sb))

    rr = lax.broadcasted_iota(jnp.int32, (rows, rows), 0)
    cc = lax.broadcasted_iota(jnp.int32, (rows, rows), 1)
    tok = rr % DEC_SEQ
    mask_cache = cc > tok
    mask_new = (cc // DEC_SEQ == rr // DEC_SEQ) & (cc % DEC_SEQ <= tok)
    p_new, dens = [], []
    for hq in range(SWA_Q_HEADS):
        sl = slice(hq * rows, (hq + 1) * rows)
        sink = sink_ref[hq]
        s_c = jnp.where(mask_cache, sc_scr[sl, :], -jnp.inf)
        s_n = jnp.where(mask_new, s_new[sl, :], -jnp.inf)
        m = jnp.maximum(jnp.maximum(jnp.max(s_c, axis=1, keepdims=True),
                                    jnp.max(s_n, axis=1, keepdims=True)), sink)
        p_c = jnp.exp(s_c - m)
        p_n = jnp.exp(s_n - m)
        dens.append(jnp.sum(p_c, axis=1, keepdims=True) + jnp.sum(p_n, axis=1, keepdims=True)
                    + jnp.exp(sink - m))
        pc_scr[sl, :] = p_c
        p_new.append(p_n.astype(BF16))
    o_new = _dot(jnp.concatenate(p_new, axis=0), v_new.astype(BF16))

    for pair_idx in range(SAMPLE_SEQS // 2):
        p64 = gather_pair(pc_scr, pair_idx)
        oa = _dot_tb(p64, vc_ref[2 * pair_idx].astype(BF16))
        ob = _dot_tb(p64, vc_ref[2 * pair_idx + 1].astype(BF16))
        scatter_pair(oc_scr, pair_idx, jnp.where(first_of_pair, oa, ob))

    tiles = []
    for i in range(SWA_Q_HEADS // 2):
        halves = []
        for hq in (2 * i, 2 * i + 1):
            sl = slice(hq * rows, (hq + 1) * rows)
            oh = (oc_scr[sl, :] + o_new[sl, :]) / dens[hq]
            halves.append(oh if hq % 2 == hq // SWA_GROUP else pltpu.roll(oh, SWA_HEAD_DIM, 1))
        tiles.append(jnp.where(lo, halves[0], halves[1]))
    o_ref[...] = _rms(jnp.concatenate(tiles, axis=1), nrm_ref[...]).astype(BF16)

    keep = lax.broadcasted_iota(jnp.int32, (1, WINDOW), 1) < WINDOW - DEC_SEQ
    k_new_t = k_new.T
    v_new_t = v_new.T
    for b in range(SAMPLE_SEQS):
        to_tail = (WINDOW - DEC_SEQ - DEC_SEQ * b) % rows
        kco_ref[b] = jnp.where(keep, pltpu.roll(kc_ref[b], WINDOW - DEC_SEQ, 1),
                               pltpu.roll(k_new_t, to_tail, 1))
        vco_ref[b] = jnp.where(keep, pltpu.roll(vc_ref[b], WINDOW - DEC_SEQ, 1),
                               pltpu.roll(v_new_t, to_tail, 1))


def _swa_sample(sinks, qs, kvn, kc, vc, nrm, row0):
    nseq = kc.shape[0]
    rows = nseq * DEC_SEQ
    assert nseq % SAMPLE_SEQS == 0 and row0 % SAMPLE_ROWS == 0
    assert kc.shape[1:] == (SWA_KV_WIDTH, WINDOW) and WINDOW == SAMPLE_ROWS
    blk0 = row0 // SAMPLE_ROWS
    cblk = pl.BlockSpec((SAMPLE_SEQS, SWA_KV_WIDTH, WINDOW), lambda i: (i, 0, 0))
    big = pltpu.VMEM((SWA_Q_HEADS * SAMPLE_ROWS, LANES), F32)
    return pl.pallas_call(
        _swa_sample_body,
        grid=(nseq // SAMPLE_SEQS,),
        in_specs=[pl.BlockSpec(memory_space=pltpu.SMEM),
                  pl.BlockSpec((SAMPLE_ROWS, SWA_WIDTH), lambda i: (i + blk0, 0)),
                  pl.BlockSpec((SAMPLE_ROWS, 2 * SWA_KV_WIDTH), lambda i: (i + blk0, 0)),
                  cblk, cblk,
                  pl.BlockSpec((1, SWA_WIDTH), lambda i: (0, 0))],
        out_specs=[pl.BlockSpec((SAMPLE_ROWS, SWA_WIDTH), lambda i: (i, 0)), cblk, cblk],
        out_shape=[jax.ShapeDtypeStruct((rows, SWA_WIDTH), BF16),
                   jax.ShapeDtypeStruct(kc.shape, F32), jax.ShapeDtypeStruct(vc.shape, F32)],
        scratch_shapes=[big, big, big, big],
        compiler_params=pltpu.CompilerParams(dimension_semantics=("arbitrary",)),
        name="swa_sample",
    )(sinks, qs, kvn, kc, vc, nrm)


def _prepare_weights(ffn1_norm, ffn1_w_gate, ffn1_w_up, ffn1_w_down, mix_norm, w_in, w_gate_up, b_gate,
                     gla_head_norm, swa_out_norm, swa_sinks, w_out,
                     ffn2_norm, ffn2_w_gate, ffn2_w_up, ffn2_w_down, final_norm, layer):
    a0 = 2 * GLA_KEY_WIDTH + 2 * GLA_WIDTH
    a1 = a0 + GLA_GATE_RANK
    win = w_in[layer]
    win = jnp.concatenate([win[:, :a0], win[:, a1:], win[:, a0:a1],
                           jnp.zeros((D_MODEL, LANES - GLA_GATE_RANK), win.dtype)], axis=1)
    wgu = jnp.concatenate([w_gate_up[layer],
                           jnp.zeros((LANES - GLA_GATE_RANK, GLA_KEY_WIDTH), w_gate_up.dtype)], axis=0)
    row = lambda a: a.reshape(1, -1).astype(F32)
    return dict(
        n1=row(ffn1_norm[layer]), wg1=ffn1_w_gate[layer], wu1=ffn1_w_up[layer], wd1=ffn1_w_down[layer],
        nm=row(mix_norm[layer]), win=win.astype(BF16), wgu=wgu.astype(BF16), bg=row(b_gate[layer]),
        gn=row(gla_head_norm[layer]), sn=row(swa_out_norm[layer]), sinks=swa_sinks[layer].astype(F32),
        wo=w_out[layer].astype(BF16), n2=row(ffn2_norm[layer]), wg2=ffn2_w_gate[layer].astype(BF16),
        wu2=ffn2_w_up[layer].astype(BF16), wd2=ffn2_w_down[layer].astype(BF16), nf=row(final_norm))


def kernel(x_prompt, x_sample, state_gla, cache_swa_k, cache_swa_v, ffn1_norm, ffn1_w_gate, ffn1_w_up,
           ffn1_w_down, mix_norm, w_in, w_gate_up, b_gate, gla_head_norm, swa_out_norm, swa_sinks, w_out,
           ffn2_norm, ffn2_w_gate, ffn2_w_up, ffn2_w_down, final_norm):
    depth = state_gla.shape[0]
    assert depth == 1 and x_prompt.shape[0] == 1 and x_sample.shape[1] == DEC_SEQ
    seq = x_prompt.shape[1]
    nseq = x_sample.shape[0]
    w = _prepare_weights(ffn1_norm, ffn1_w_gate, ffn1_w_up, ffn1_w_down, mix_norm, w_in, w_gate_up, b_gate,
                         gla_head_norm, swa_out_norm, swa_sinks, w_out,
                         ffn2_norm, ffn2_w_gate, ffn2_w_up, ffn2_w_down, final_norm, 0)

    x1, qk, g, v, r, qs, kv = _stage_a(x_prompt.reshape(seq, D_MODEL),
                                       x_sample.reshape(nseq * DEC_SEQ, D_MODEL), w)

    og_s, state_s = _gla_sample(qk, g, v, r, w["gn"],
                                state_gla[0].reshape(nseq, GLA_KEY_WIDTH, GLA_HEAD_V), seq)
    cache_w = cache_swa_k.shape[2]
    to_feature_major = lambda c: jnp.transpose(c[0], (0, 2, 3, 1)).reshape(nseq, SWA_KV_WIDTH, cache_w)
    from_feature_major = lambda c: jnp.transpose(
        c.reshape(nseq, SWA_KV_HEADS, SWA_HEAD_DIM, cache_w), (0, 3, 1, 2))[None]
    os_s, k_cache_s, v_cache_s = _swa_sample(
        w["sinks"], qs, kv, to_feature_major(cache_swa_k), to_feature_major(cache_swa_v), w["sn"], seq)
    k_cache_s = from_feature_major(k_cache_s)
    v_cache_s = from_feature_major(v_cache_s)

    y_prompt, y_sample, state_p = _mix_out(qk, g, v, r, qs, kv, x1, og_s, os_s, w)
    y_prompt = y_prompt.reshape(1, seq, D_MODEL)
    y_sample = y_sample.reshape(nseq, DEC_SEQ, D_MODEL)
    cw = min(WINDOW, seq)
    k_cache_p = kv[seq - cw:seq, 0:SWA_KV_WIDTH].reshape(1, 1, cw, SWA_KV_HEADS, SWA_HEAD_DIM)
    v_cache_p = kv[seq - cw:seq, SWA_KV_WIDTH:].reshape(1, 1, cw, SWA_KV_HEADS, SWA_HEAD_DIM)
    state_p = state_p.reshape(1, 1, GLA_HEADS, GLA_HEAD_K, GLA_HEAD_V)
    state_s = state_s.reshape(1, nseq, GLA_HEADS, GLA_HEAD_K, GLA_HEAD_V)

    return (y_prompt, y_sample, state_p, k_cache_p, v_cache_p, state_s, k_cache_s, v_cache_s)
```

```python
import functools
import itertools

import jax
import jax.numpy as jnp
from jax import lax
from jax.experimental import pallas as pl
from jax.experimental.pallas import tpu as pltpu

F32 = jnp.float32
BF16 = jnp.bfloat16

D_MODEL = 1024
D_FF = 2816
GLA_HEADS = 4
GLA_HEAD_K = 64
GLA_HEAD_V = 128
GLA_KEY_WIDTH = GLA_HEADS * GLA_HEAD_K
GLA_WIDTH = GLA_HEADS * GLA_HEAD_V
GLA_GATE_RANK = 16
GLA_GATE_TAU = 16.0
SWA_HEAD_DIM = 64
SWA_Q_HEADS = 8
SWA_KV_HEADS = 2
SWA_GROUP = SWA_Q_HEADS // SWA_KV_HEADS
SWA_WIDTH = SWA_Q_HEADS * SWA_HEAD_DIM
SWA_KV_WIDTH = SWA_KV_HEADS * SWA_HEAD_DIM
WINDOW = 128
DEC_SEQ = 4
NORM_EPS = 1e-6
HEAD_SCALE = 0.125

LANES = 128
SUBLANES = 8
VMEM_LIMIT_BYTES = 56 * 1024 * 1024

ROW_TILE = 512
FF_CHUNK = 256
DOWN_CHUNK = 256
GLA_CHUNK = 128
SAMPLE_SEQS = 32
SAMPLE_ROWS = SAMPLE_SEQS * DEC_SEQ
MIXER_STAGES = 3 * (ROW_TILE // GLA_CHUNK) + 3 * (ROW_TILE // WINDOW)
SIDE_STAGES_AT_NORM = 2
DECAY_CLAMP = 60.0

PROJ_Q_G = 0
PROJ_K_G = 256
PROJ_V_G = 512
PROJ_R_G = 1024
PROJ_Q_S = 1536
PROJ_KV_S = 2048
PROJ_A = 2304
PROJ_WIDTH = 2432


def _dot(a, b):
    return jnp.dot(a, b, preferred_element_type=F32)


def _dot_tb(a, b):
    return lax.dot_general(a, b, (((1,), (1,)), ((), ())), preferred_element_type=F32)


def _dot_ta(a, b):
    return lax.dot_general(a, b, (((0,), (0,)), ((), ())), preferred_element_type=F32)


def _rms(x, g):
    return x * lax.rsqrt(jnp.mean(x * x, axis=-1, keepdims=True) + NORM_EPS) * g


def _swiglu(h, wg_ref, wu_ref, wd_ref, act_ref, side_stages=iter(()), n_side_stages=0):
    n_dots = 2 * (D_FF // FF_CHUNK) + D_MODEL // DOWN_CHUNK
    done = [0, 0]

    def after_dot():
        done[0] += 1
        while done[1] * n_dots < done[0] * n_side_stages:
            next(side_stages, None)
            done[1] += 1

    for c0 in range(0, D_FF, FF_CHUNK):
        g = _dot(h, wg_ref[:, c0:c0 + FF_CHUNK])
        after_dot()
        u = _dot(h, wu_ref[:, c0:c0 + FF_CHUNK])
        after_dot()
        act_ref[:, c0:c0 + FF_CHUNK] = (g * jax.nn.sigmoid(g) * u).astype(BF16)
    outs = []
    for n0 in range(0, D_MODEL, DOWN_CHUNK):
        outs.append(_dot(act_ref[...], wd_ref[:, n0:n0 + DOWN_CHUNK]))
        after_dot()
    return jnp.concatenate(outs, axis=1)


def _head_norm_gate(o, r, gn):
    parts = []
    for h in range(GLA_HEADS):
        sl = slice(h * GLA_HEAD_V, (h + 1) * GLA_HEAD_V)
        parts.append(_rms(o[:, sl], gn[:, sl]))
    return (jnp.concatenate(parts, axis=1) * (r * jax.nn.sigmoid(r))).astype(BF16)


def _out_ffn(x1_ref, og, osw, wo_ref, n2_ref, wg_ref, wu_ref, wd_ref, nf_ref, y_ref, act_ref, side_stages=None):
    n_inside = 0 if side_stages is None else MIXER_STAGES - 2 * SIDE_STAGES_AT_NORM
    side_stages = iter(()) if side_stages is None else side_stages
    mixed = jnp.concatenate([og, osw], axis=1)
    x2 = x1_ref[...] + _dot(mixed, wo_ref[...])
    for _ in range(SIDE_STAGES_AT_NORM):
        next(side_stages, None)
    h = _rms(x2, n2_ref[...]).astype(BF16)
    x3 = x2 + 0.5 * _swiglu(h, wg_ref, wu_ref, wd_ref, act_ref, side_stages, n_inside)
    for _ in side_stages:
        pass
    y_ref[...] = _rms(x3, nf_ref[...])


_HBM = pl.BlockSpec(memory_space=pl.ANY)
WIDE_CHUNK_ROWS = 128
NARROW_CHUNK_ROWS = 256
LOAD_SLOTS = 3


def _load_weights_bf16(pairs, staging, sems):
    jobs = []
    for src, dst in pairs:
        rows, cols = src.shape
        assert cols in (D_FF, D_MODEL) and dst.shape == src.shape
        kind, step = (0, WIDE_CHUNK_ROWS) if cols == D_FF else (1, NARROW_CHUNK_ROWS)
        assert rows % step == 0
        jobs += [(src, dst, r0, step, kind) for r0 in range(0, rows, step)]

    used = [0, 0]
    copies = []
    for src, _, r0, nr, kind in jobs:
        slot = used[kind] % LOAD_SLOTS
        used[kind] += 1
        copies.append((pltpu.make_async_copy(src.at[pl.ds(r0, nr), :], staging[kind].at[slot],
                                             sems.at[kind, slot]), slot))
    for copy, _ in copies[:LOAD_SLOTS - 1]:
        copy.start()
    for j, (_, dst, r0, nr, kind) in enumerate(jobs):
        ahead = j + LOAD_SLOTS - 1
        if ahead < len(jobs):
            copies[ahead][0].start()
        copy, slot = copies[j]
        copy.wait()
        dst[r0:r0 + nr, :] = staging[kind][slot].astype(BF16)


def _staging_scratch():
    return [pltpu.VMEM((LOAD_SLOTS, WIDE_CHUNK_ROWS, D_FF), F32),
            pltpu.VMEM((LOAD_SLOTS, NARROW_CHUNK_ROWS, D_MODEL), F32),
            pltpu.SemaphoreType.DMA((2, LOAD_SLOTS))]


def _chunk_plan(matrices):
    plan, first = [], 0
    for m, mat in enumerate(matrices):
        rows, cols = mat.shape
        assert cols in (D_FF, D_MODEL)
        kind, nr = (0, WIDE_CHUNK_ROWS) if cols == D_FF else (1, NARROW_CHUNK_ROWS)
        assert rows % nr == 0
        plan.append((m, first, rows // nr, nr, kind))
        first += rows // nr
    return plan, first


def _background_cast(step, srcs, dsts, staging, out_staging, sem_in, sem_out):
    plan, _ = _chunk_plan(srcs)
    slot = step % 2

    def rows_of(m, chunk):
        _, first, _, nr, _ = plan[m]
        return pl.ds(pl.multiple_of((chunk - first) * nr, nr), nr)

    def copy_in(m, chunk, slot_):
        kind = plan[m][4]
        return pltpu.make_async_copy(srcs[m].at[rows_of(m, chunk), :], staging[kind].at[slot_],
                                     sem_in.at[kind, slot_])

    def copy_out(m, chunk, slot_):
        kind = plan[m][4]
        return pltpu.make_async_copy(out_staging[kind].at[slot_], dsts[m].at[rows_of(m, chunk), :],
                                     sem_out.at[kind, slot_])

    def for_chunk(chunk, fn):
        for m, first, n, _, _ in plan:
            pl.when(jnp.logical_and(chunk >= first, chunk < first + n))(functools.partial(fn, m))

    def before():
        @pl.when(step == 0)
        def _():
            copy_in(0, 0, 0).start()
        for_chunk(step, lambda m: copy_in(m, step, slot).wait())
        for_chunk(step - 2, lambda m: copy_out(m, step - 2, slot).wait())
        for_chunk(step + 1, lambda m: copy_in(m, step + 1, 1 - slot).start())

    def cast():
        for kind in range(2):
            out_staging[kind][slot] = staging[kind][slot].astype(BF16)

    def after():
        for_chunk(step, lambda m: copy_out(m, step, slot).start())

    return before, cast, after


def _stage_a_body(xp_ref, xs_ref, n1_ref, wg_hbm, wu_hbm, wd_hbm, nm_ref, win_ref, wgu_ref, bg_ref,
                  wo_hbm, wg2_hbm, wu2_hbm, wd2_hbm,
                  x1_ref, qk_ref, g_ref, v_ref, r_ref, qs_ref, kv_ref, wo_out, wg2_out, wu2_out, wd2_out,
                  act_ref, wg_ref, wu_ref, wd_ref, wide_stage, narrow_stage, load_sems,
                  wide_out, narrow_out, sem_in, sem_out):
    i = pl.program_id(0)

    @pl.when(i == 0)
    def _():
        _load_weights_bf16([(wg_hbm, wg_ref), (wu_hbm, wu_ref), (wd_hbm, wd_ref)],
                           (wide_stage, narrow_stage), load_sems)

    bg_before, bg_cast, bg_after = _background_cast(
        i, (wo_hbm, wg2_hbm, wu2_hbm, wd2_hbm), (wo_out, wg2_out, wu2_out, wd2_out),
        (wide_stage, narrow_stage), (wide_out, narrow_out), sem_in, sem_out)
    bg_before()

    x = jnp.where(i < pl.num_programs(0) - 1, xp_ref[...], xs_ref[...])
    h = _rms(x, n1_ref[...]).astype(BF16)
    bg_cast()
    x1 = x + 0.5 * _swiglu(h, wg_ref, wu_ref, wd_ref, act_ref)
    x1_ref[...] = x1
    h2 = _rms(x1, nm_ref[...]).astype(BF16)
    proj = _dot(h2, win_ref[...])
    qk_ref[:, 0:GLA_KEY_WIDTH] = proj[:, PROJ_Q_G:PROJ_K_G] * HEAD_SCALE
    qk_ref[:, GLA_KEY_WIDTH:] = proj[:, PROJ_K_G:PROJ_V_G]
    v_ref[...] = proj[:, PROJ_V_G:PROJ_R_G].astype(BF16)
    r_ref[...] = proj[:, PROJ_R_G:PROJ_Q_S]
    qs_ref[...] = (proj[:, PROJ_Q_S:PROJ_KV_S] * HEAD_SCALE).astype(BF16)
    kv_ref[...] = proj[:, PROJ_KV_S:PROJ_A]
    a = proj[:, PROJ_A:PROJ_WIDTH].astype(BF16)
    z = _dot(a, wgu_ref[...]) + bg_ref[...]
    g_ref[...] = jax.nn.log_sigmoid(z) * (1.0 / GLA_GATE_TAU)
    bg_after()


def _resident(shape):
    return pl.BlockSpec(shape, lambda i: (0,) * len(shape), pipeline_mode=pl.Buffered(1))


def _rows(tm, n):
    return pl.BlockSpec((tm, n), lambda i: (i, 0))


def _stage_a(xp, xs, w):
    t = xp.shape[0]
    assert t % ROW_TILE == 0 and xs.shape[0] == ROW_TILE
    nt = t // ROW_TILE
    out_widths = ((D_MODEL, F32), (2 * GLA_KEY_WIDTH, F32), (GLA_KEY_WIDTH, F32), (GLA_WIDTH, BF16),
                  (GLA_WIDTH, F32), (SWA_WIDTH, BF16), (2 * SWA_KV_WIDTH, F32))
    to_cast = (w["wo"], w["wg2"], w["wu2"], w["wd2"])
    assert _chunk_plan(to_cast)[1] + 2 <= nt + 1
    outs = pl.pallas_call(
        _stage_a_body,
        grid=(nt + 1,),
        in_specs=[pl.BlockSpec((ROW_TILE, D_MODEL), lambda i: (jnp.minimum(i, nt - 1), 0)),
                  _resident((ROW_TILE, D_MODEL)), _resident((1, D_MODEL)),
                  _HBM, _HBM, _HBM,
                  _resident((1, D_MODEL)), _resident((D_MODEL, PROJ_WIDTH)),
                  _resident((LANES, GLA_KEY_WIDTH)), _resident((1, GLA_KEY_WIDTH)),
                  _HBM, _HBM, _HBM, _HBM],
        out_specs=[_rows(ROW_TILE, n) for n, _ in out_widths] + [_HBM] * len(to_cast),
        out_shape=([jax.ShapeDtypeStruct((t + ROW_TILE, n), dt) for n, dt in out_widths]
                   + [jax.ShapeDtypeStruct(m.shape, BF16) for m in to_cast]),
        scratch_shapes=([pltpu.VMEM((ROW_TILE, D_FF), BF16), pltpu.VMEM((D_MODEL, D_FF), BF16),
                         pltpu.VMEM((D_MODEL, D_FF), BF16), pltpu.VMEM((D_FF, D_MODEL), BF16)]
                        + _staging_scratch()
                        + [pltpu.VMEM((2, WIDE_CHUNK_ROWS, D_FF), BF16),
                           pltpu.VMEM((2, NARROW_CHUNK_ROWS, D_MODEL), BF16),
                           pltpu.SemaphoreType.DMA((2, 2)), pltpu.SemaphoreType.DMA((2, 2))]),
        compiler_params=pltpu.CompilerParams(dimension_semantics=("arbitrary",),
                                             vmem_limit_bytes=VMEM_LIMIT_BYTES),
        name="stage_a_ffn1_proj",
    )(xp, xs, w["n1"], w["wg1"], w["wu1"], w["wd1"], w["nm"], w["win"], w["wgu"], w["bg"], *to_cast)
    return outs[:len(out_widths)], outs[len(out_widths):]


def _split_bf16(x):
    hi = x.astype(BF16)
    lo = (x - hi.astype(F32)).astype(BF16)
    return hi, lo


def _alternate(a, b):
    pending = [iter(a), iter(b)]
    while pending:
        for it in list(pending):
            try:
                next(it)
            except StopIteration:
                pending.remove(it)
                continue
            yield


def _gla_stages(qk_ref, g_ref, v_ref, r_ref, gn_ref, s_ref, og_scr, cum_scr, inter_scr, flag_ref):
    c_len = GLA_CHUNK
    n_chunks = ROW_TILE // c_len
    causal = (lax.broadcasted_iota(jnp.int32, (c_len, c_len), 0)
              >= lax.broadcasted_iota(jnp.int32, (c_len, c_len), 1))
    ltri = jnp.where(causal, 1.0, 0.0).astype(BF16)
    causal_cat = (lax.broadcasted_iota(jnp.int32, (c_len, GLA_HEADS * c_len), 0)
                  >= lax.broadcasted_iota(jnp.int32, (c_len, GLA_HEADS * c_len), 1) % c_len)
    lane_head = lax.broadcasted_iota(jnp.int32, (1, GLA_KEY_WIDTH), 1) // GLA_HEAD_K
    row_head = lax.broadcasted_iota(jnp.int32, (GLA_KEY_WIDTH, 1), 0) // GLA_HEAD_K
    eye = (lax.broadcasted_iota(jnp.int32, (GLA_KEY_WIDTH, GLA_KEY_WIDTH), 0)
           == lax.broadcasted_iota(jnp.int32, (GLA_KEY_WIDTH, GLA_KEY_WIDTH), 1))
    zero_b = jnp.zeros((), BF16)
    zero_v = jnp.zeros((c_len, GLA_HEAD_V), BF16)
    worst = []

    def chunk(c):
        rows = slice(c * c_len, (c + 1) * c_len)
        g_hi, g_lo = _split_bf16(g_ref[rows, :])
        cum = _dot(ltri, g_hi) + _dot(ltri, g_lo)
        yield
        cum_scr[rows, :] = cum
        last = cum[c_len - 1:c_len, :]
        q = qk_ref[rows, 0:GLA_KEY_WIDTH]
        k = qk_ref[rows, GLA_KEY_WIDTH:]
        vb = v_ref[rows, :]
        qe = (q * jnp.exp(cum)).astype(BF16)
        ke = (k * jnp.exp(jnp.minimum(-cum, DECAY_CLAMP))).astype(BF16)
        kl = (k * jnp.exp(last - cum)).astype(BF16)
        state = s_ref[...]
        sb = state.astype(BF16)
        s_bd = jnp.concatenate([jnp.where(row_head == h, sb, zero_b) for h in range(GLA_HEADS)], axis=1)
        o_inter = _dot(qe, s_bd)
        inter_scr[rows, :] = o_inter
        ke_bd = jnp.concatenate([jnp.where(lane_head == h, ke, zero_b) for h in range(GLA_HEADS)], axis=0)
        attn = _dot_tb(qe, ke_bd)
        upds = []
        for p in range(GLA_HEADS // 2):
            u = _dot_ta(kl[:, p * LANES:(p + 1) * LANES], vb[:, 2 * p * GLA_HEAD_V:(2 * p + 2) * GLA_HEAD_V])
            upds.append(u[0:GLA_HEAD_K, 0:GLA_HEAD_V])
            upds.append(u[GLA_HEAD_K:, GLA_HEAD_V:])
        yield
        attn = jnp.where(causal_cat, attn, 0.0).astype(BF16)
        o_pairs = []
        for p in range(GLA_HEADS // 2):
            v_a = vb[:, (2 * p) * GLA_HEAD_V:(2 * p + 1) * GLA_HEAD_V]
            v_b = vb[:, (2 * p + 1) * GLA_HEAD_V:(2 * p + 2) * GLA_HEAD_V]
            v_bd = jnp.concatenate([jnp.concatenate([v_a, zero_v], axis=1),
                                    jnp.concatenate([zero_v, v_b], axis=1)], axis=0)
            o_pairs.append(_dot(attn[:, 2 * p * c_len:(2 * p + 2) * c_len], v_bd))
        og_scr[rows, :] = _head_norm_gate(o_inter + jnp.concatenate(o_pairs, axis=1), r_ref[rows, :],
                                          gn_ref[...])
        last_col = jnp.sum(jnp.where(eye, last, 0.0), axis=1, keepdims=True)
        s_ref[...] = state * jnp.exp(last_col) + jnp.concatenate(upds, axis=0)
        worst.append(jnp.min(last, axis=1, keepdims=True))
        if c == n_chunks - 1:
            tile_min = functools.reduce(jnp.minimum, worst)
            flag_ref[0] = jnp.where(tile_min[0, 0] < -DECAY_CLAMP, 1, 0)
        yield

    return itertools.chain.from_iterable(chunk(c) for c in range(n_chunks))


def _gla_tile_exact(qk_ref, v_ref, r_ref, gn_ref, og_scr, o_scr, cum_scr, inter_scr):
    c_len = GLA_CHUNK
    ind = jnp.where(lax.broadcasted_iota(jnp.int32, (GLA_KEY_WIDTH, LANES), 0) // GLA_HEAD_K
                    == lax.broadcasted_iota(jnp.int32, (GLA_KEY_WIDTH, LANES), 1), 1.0, 0.0).astype(BF16)
    j_idx = lax.broadcasted_iota(jnp.int32, (c_len, 1), 0)
    for c in range(ROW_TILE // c_len):
        r0 = c * c_len
        rows = slice(r0, r0 + c_len)

        def one_row(i, carry, r0=r0, rows=rows):
            ci = cum_scr[pl.ds(r0 + i, 1), :]
            qi = qk_ref[pl.ds(r0 + i, 1), 0:GLA_KEY_WIDTH]
            kk = qk_ref[rows, GLA_KEY_WIDTH:]
            dec = jnp.exp(jnp.minimum(ci - cum_scr[rows, :], 0.0))
            a_cols = _dot(((qi * kk) * dec).astype(BF16), ind)
            outs = []
            for h in range(GLA_HEADS):
                w_col = jnp.where(j_idx <= i, a_cols[:, h:h + 1], 0.0)
                v_h = v_ref[rows, h * GLA_HEAD_V:(h + 1) * GLA_HEAD_V].astype(F32)
                outs.append(jnp.sum(w_col * v_h, axis=0, keepdims=True))
            o_scr[pl.ds(r0 + i, 1), :] = inter_scr[pl.ds(r0 + i, 1), :] + jnp.concatenate(outs, axis=1)
            return carry

        lax.fori_loop(0, c_len, one_row, 0)
    og_scr[...] = _head_norm_gate(o_scr[...], r_ref[...], gn_ref[...])


def _dup_halves(x):
    lo = lax.broadcasted_iota(jnp.int32, (1, LANES), 1) < SWA_HEAD_DIM
    sw = pltpu.roll(x, SWA_HEAD_DIM, 1)
    return jnp.where(lo, x, sw).astype(BF16), jnp.where(lo, sw, x).astype(BF16)


def _swa_stages(sink_ref, qs_ref, kv_ref, kvprev_scr, sn_ref, os_scr, no_prev_bias):
    w = WINDOW
    n_blocks = ROW_TILE // w
    lo = lax.broadcasted_iota(jnp.int32, (1, LANES), 1) < SWA_HEAD_DIM
    tri = (lax.broadcasted_iota(jnp.int32, (w, w), 1) <= lax.broadcasted_iota(jnp.int32, (w, w), 0))
    zero_b = jnp.zeros((), BF16)

    def block(b):
        rows = slice(b * w, (b + 1) * w)
        kv_prev = kvprev_scr[...] if b == 0 else kv_ref[(b - 1) * w:b * w, :]
        kv = jnp.concatenate([kv_prev, kv_ref[rows, :]], axis=0)
        kk = _dup_halves(kv[:, 0:SWA_KV_WIDTH])
        vv = _dup_halves(kv[:, SWA_KV_WIDTH:])
        q = qs_ref[rows, :]
        scores = []
        for grp in range(SWA_KV_HEADS):
            stacked = []
            for j in range(SWA_GROUP):
                hq = grp * SWA_GROUP + j
                tile = q[:, (hq // 2) * LANES:(hq // 2 + 1) * LANES]
                stacked.append(jnp.where(lo if hq % 2 == 0 else ~lo, tile, zero_b))
            scores.append(_dot_tb(jnp.concatenate(stacked, axis=0), kk[grp]))
        yield
        outs, dens = [], []
        for grp in range(SWA_KV_HEADS):
            s = scores[grp]
            probs = []
            for j in range(SWA_GROUP):
                sink = sink_ref[grp * SWA_GROUP + j]
                s_prev = s[j * w:(j + 1) * w, 0:w]
                if b == 0:
                    s_prev = s_prev + no_prev_bias
                sf = jnp.where(tri, s[j * w:(j + 1) * w, w:], s_prev)
                m = jnp.maximum(jnp.max(sf, axis=1, keepdims=True), sink)
                p = jnp.exp(sf - m)
                dens.append(jnp.sum(p, axis=1, keepdims=True) + jnp.exp(sink - m))
                pb = p.astype(BF16)
                probs.append(jnp.concatenate([jnp.where(tri, zero_b, pb), jnp.where(tri, pb, zero_b)], axis=1))
            outs.append(_dot(jnp.concatenate(probs, axis=0), vv[grp]))
        yield
        tiles = []
        for grp in range(SWA_KV_HEADS):
            res = [outs[grp][j * w:(j + 1) * w] / dens[grp * SWA_GROUP + j] for j in range(SWA_GROUP)]
            tiles.append(jnp.where(lo, res[0], res[1]))
            tiles.append(jnp.where(lo, res[2], res[3]))
        os_scr[rows, :] = _rms(jnp.concatenate(tiles, axis=1), sn_ref[...]).astype(BF16)
        if b == n_blocks - 1:
            kvprev_scr[...] = kv_ref[rows, :]
        yield

    return itertools.chain.from_iterable(block(b) for b in range(n_blocks))


def _mix_out_body(sink_ref, qk_ref, g_ref, v_ref, r_ref, qs_ref, kv_ref, x1_ref, ogs_ref, oss_ref,
                  gn_ref, sn_ref, wo_ref, n2_ref, wg_ref, wu_ref, wd_ref, nf_ref, yp_ref, ys_ref, s_ref,
                  og_scr, os_scr, kvprev_scr, o_scr, cum_scr, inter_scr, act_scr, flag_ref):
    i = pl.program_id(0)
    n_prompt = pl.num_programs(0) - 2

    def stage_d(og, osw, y_ref, side_stages=None):
        _out_ffn(x1_ref, og, osw, wo_ref, n2_ref, wg_ref, wu_ref, wd_ref, nf_ref, y_ref, act_scr,
                 side_stages)

    def mixer_stages():
        gla = _gla_stages(qk_ref, g_ref, v_ref, r_ref, gn_ref, s_ref, og_scr, cum_scr, inter_scr, flag_ref)
        swa = _swa_stages(sink_ref, qs_ref, kv_ref, kvprev_scr, sn_ref, os_scr,
                          jnp.where(i > 0, 0.0, -jnp.inf))
        return _alternate(gla, swa)

    @pl.when(i == 0)
    def _():
        s_ref[...] = jnp.zeros_like(s_ref)
        kvprev_scr[...] = jnp.zeros_like(kvprev_scr)
        for _ in mixer_stages():
            pass

    @pl.when(jnp.logical_and(i > 0, i < n_prompt))
    def _():
        stage_d(og_scr[...], os_scr[...], yp_ref, mixer_stages())

    @pl.when(i == n_prompt)
    def _():
        stage_d(og_scr[...], os_scr[...], yp_ref)
        flag_ref[0] = 0

    @pl.when(i == n_prompt + 1)
    def _():
        stage_d(ogs_ref[...], oss_ref[...], ys_ref)

    @pl.when(flag_ref[0] != 0)
    def _():
        _gla_tile_exact(qk_ref, v_ref, r_ref, gn_ref, og_scr, o_scr, cum_scr, inter_scr)


def _mix_out(qk, g, v, r, qs, kv, x1, og_s, os_s, w):
    t = qk.shape[0] - ROW_TILE
    assert t % ROW_TILE == 0 and og_s.shape[0] == ROW_TILE
    nt = t // ROW_TILE
    cur = lambda n: pl.BlockSpec((ROW_TILE, n), lambda i: (jnp.minimum(i, nt - 1), 0))
    return pl.pallas_call(
        _mix_out_body,
        grid=(nt + 2,),
        in_specs=[pl.BlockSpec(memory_space=pltpu.SMEM),
                  cur(2 * GLA_KEY_WIDTH), cur(GLA_KEY_WIDTH), cur(GLA_WIDTH), cur(GLA_WIDTH),
                  cur(SWA_WIDTH), cur(2 * SWA_KV_WIDTH),
                  pl.BlockSpec((ROW_TILE, D_MODEL), lambda i: (jnp.maximum(i - 1, 0), 0)),
                  _resident((ROW_TILE, GLA_WIDTH)), _resident((ROW_TILE, SWA_WIDTH)),
                  _resident((1, GLA_WIDTH)), _resident((1, SWA_WIDTH)),
                  _resident((D_MODEL, D_MODEL)), _resident((1, D_MODEL)),
                  _resident((D_MODEL, D_FF)), _resident((D_MODEL, D_FF)), _resident((D_FF, D_MODEL)),
                  _resident((1, D_MODEL))],
        out_specs=[pl.BlockSpec((ROW_TILE, D_MODEL), lambda i: (jnp.clip(i - 1, 0, nt - 1), 0)),
                   pl.BlockSpec((ROW_TILE, D_MODEL), lambda i: (0, 0)),
                   pl.BlockSpec((GLA_KEY_WIDTH, GLA_HEAD_V), lambda i: (0, 0))],
        out_shape=[jax.ShapeDtypeStruct((t, D_MODEL), F32),
                   jax.ShapeDtypeStruct((ROW_TILE, D_MODEL), F32),
                   jax.ShapeDtypeStruct((GLA_KEY_WIDTH, GLA_HEAD_V), F32)],
        scratch_shapes=[pltpu.VMEM((ROW_TILE, GLA_WIDTH), BF16), pltpu.VMEM((ROW_TILE, SWA_WIDTH), BF16),
                        pltpu.VMEM((WINDOW, 2 * SWA_KV_WIDTH), F32),
                        pltpu.VMEM((ROW_TILE, GLA_WIDTH), F32), pltpu.VMEM((ROW_TILE, GLA_KEY_WIDTH), F32),
                        pltpu.VMEM((ROW_TILE, GLA_WIDTH), F32), pltpu.VMEM((ROW_TILE, D_FF), BF16),
                        pltpu.SMEM((1,), jnp.int32)],
        compiler_params=pltpu.CompilerParams(dimension_semantics=("arbitrary",),
                                             vmem_limit_bytes=VMEM_LIMIT_BYTES),
        name="prompt_mixers_stage_d",
    )(w["sinks"], qk, g, v, r, qs, kv, x1, og_s, os_s, w["gn"], w["sn"], w["wo"], w["n2"], w["wg2"],
      w["wu2"], w["wd2"], w["nf"])


def _gla_sample_body(qk_ref, g_ref, v_ref, r_ref, gn_ref, s_ref, og_ref, so_ref, o_scr):
    rows = SAMPLE_ROWS
    g = g_ref[...]
    q = qk_ref[:, 0:GLA_KEY_WIDTH]
    k = qk_ref[:, GLA_KEY_WIDTH:]
    vb = v_ref[...]
    vf = vb.astype(F32)
    tok = lax.broadcasted_iota(jnp.int32, (rows, 1), 0) % DEC_SEQ

    cum = g
    for d in range(1, DEC_SEQ):
        cum = cum + jnp.where(tok >= d, pltpu.roll(g, d, 0), 0.0)
    tot = jnp.where(tok == DEC_SEQ - 1, cum, 0.0)
    for d in range(1, DEC_SEQ):
        tot = tot + jnp.where(tok == DEC_SEQ - 1 - d, pltpu.roll(cum, rows - d, 0), 0.0)

    qe = q * jnp.exp(cum)
    kl = k * jnp.exp(tot - cum)
    decay_t = jnp.exp(tot).T

    ind = jnp.where(lax.broadcasted_iota(jnp.int32, (GLA_KEY_WIDTH, LANES), 0) // GLA_HEAD_K
                    == lax.broadcasted_iota(jnp.int32, (GLA_KEY_WIDTH, LANES), 1), 1.0, 0.0).astype(BF16)
    expand = jnp.where(lax.broadcasted_iota(jnp.int32, (LANES, GLA_WIDTH), 0)
                       == lax.broadcasted_iota(jnp.int32, (LANES, GLA_WIDTH), 1) // GLA_HEAD_V,
                       1.0, 0.0).astype(BF16)

    o_intra = jnp.zeros((rows, GLA_WIDTH), F32)
    for d in range(DEC_SEQ):
        k_d = k if d == 0 else pltpu.roll(k, d, 0)
        c_d = cum if d == 0 else pltpu.roll(cum, d, 0)
        v_d = vf if d == 0 else pltpu.roll(vf, d, 0)
        pair = jnp.where(tok >= d, q * k_d * jnp.exp(jnp.minimum(cum - c_d, 0.0)), 0.0)
        a = _dot(pair.astype(BF16), ind)
        o_intra = o_intra + _dot(a.astype(BF16), expand) * v_d

    lane_head = lax.broadcasted_iota(jnp.int32, (1, GLA_KEY_WIDTH), 1) // GLA_HEAD_K
    row8 = lax.broadcasted_iota(jnp.int32, (SUBLANES, 1), 0)
    row32 = lax.broadcasted_iota(jnp.int32, (GLA_HEADS * SUBLANES, 1), 0)
    for pair_idx in range(SAMPLE_SEQS // 2):
        r8 = slice(pair_idx * SUBLANES, (pair_idx + 1) * SUBLANES)
        q8 = qe[r8, :]
        lhs = jnp.concatenate([jnp.where(lane_head == h, q8, 0.0) for h in range(GLA_HEADS)],
                              axis=0).astype(BF16)
        kl8 = kl[r8, :]
        v8 = vb[r8, :]
        res = []
        for s in range(2):
            b = 2 * pair_idx + s
            state = s_ref[b]
            res.append(_dot(lhs, state.astype(BF16)))
            kl_b = jnp.where(row8 // DEC_SEQ == s, kl8, 0.0).astype(BF16)
            upd = _dot_ta(kl_b, v8)
            upd = jnp.concatenate(
                [upd[h * GLA_HEAD_K:(h + 1) * GLA_HEAD_K, h * GLA_HEAD_V:(h + 1) * GLA_HEAD_V]
                 for h in range(GLA_HEADS)], axis=0)
            so_ref[b] = state * decay_t[:, DEC_SEQ * b:DEC_SEQ * b + 1] + upd
        sel = jnp.where(row32 % SUBLANES < DEC_SEQ, res[0], res[1])
        o_scr[r8, :] = jnp.concatenate([sel[h * SUBLANES:(h + 1) * SUBLANES] for h in range(GLA_HEADS)], axis=1)

    og_ref[...] = _head_norm_gate(o_scr[...] + o_intra, r_ref[...], gn_ref[...])


def _gla_sample(qk, g, v, r, gn, state, row0):
    nseq = state.shape[0]
    rows = nseq * DEC_SEQ
    assert nseq % SAMPLE_SEQS == 0 and row0 % SAMPLE_ROWS == 0
    blk0 = row0 // SAMPLE_ROWS
    src = lambda n: pl.BlockSpec((SAMPLE_ROWS, n), lambda i: (i + blk0, 0))
    blk = lambda n: pl.BlockSpec((SAMPLE_ROWS, n), lambda i: (i, 0))
    sblk = pl.BlockSpec((SAMPLE_SEQS, GLA_KEY_WIDTH, GLA_HEAD_V), lambda i: (i, 0, 0))
    return pl.pallas_call(
        _gla_sample_body,
        grid=(nseq // SAMPLE_SEQS,),
        in_specs=[src(2 * GLA_KEY_WIDTH), src(GLA_KEY_WIDTH), src(GLA_WIDTH), src(GLA_WIDTH),
                  pl.BlockSpec((1, GLA_WIDTH), lambda i: (0, 0)), sblk],
        out_specs=[blk(GLA_WIDTH), sblk],
        out_shape=[jax.ShapeDtypeStruct((rows, GLA_WIDTH), BF16),
                   jax.ShapeDtypeStruct((nseq, GLA_KEY_WIDTH, GLA_HEAD_V), F32)],
        scratch_shapes=[pltpu.VMEM((SAMPLE_ROWS, GLA_WIDTH), F32)],
        compiler_params=pltpu.CompilerParams(dimension_semantics=("arbitrary",)),
        name="gla_sample",
    )(qk, g, v, r, gn, state)


def _swa_sample_body(sink_ref, q_ref, kvn_ref, kc_ref, vc_ref, nrm_ref, o_ref, kco_ref, vco_ref,
                     lhs_scr, sc_scr, pc_scr, oc_scr):
    rows = SAMPLE_ROWS
    q = q_ref[...].astype(F32)
    lo = lax.broadcasted_iota(jnp.int32, (1, LANES), 1) < SWA_HEAD_DIM
    for hq in range(SWA_Q_HEADS):
        grp = hq // SWA_GROUP
        tile = q[:, (hq // 2) * LANES:(hq // 2 + 1) * LANES]
        src = tile if hq % 2 == grp else pltpu.roll(tile, SWA_HEAD_DIM, 1)
        lhs_scr[hq * rows:(hq + 1) * rows, :] = jnp.where(lo if grp == 0 else ~lo, src, 0.0)

    kvn = kvn_ref[...]
    k_new = kvn[:, 0:SWA_KV_WIDTH]
    v_new = kvn[:, SWA_KV_WIDTH:]
    s_new = _dot_tb(lhs_scr[...].astype(BF16), k_new.astype(BF16))

    row64 = lax.broadcasted_iota(jnp.int32, (SWA_Q_HEADS * SUBLANES, 1), 0)
    first_of_pair = row64 % SUBLANES < DEC_SEQ

    def gather_pair(ref, pair_idx):
        return jnp.concatenate(
            [ref[hq * rows + pair_idx * SUBLANES:hq * rows + (pair_idx + 1) * SUBLANES, :]
             for hq in range(SWA_Q_HEADS)], axis=0).astype(BF16)

    def scatter_pair(ref, pair_idx, val):
        for hq in range(SWA_Q_HEADS):
            ref[hq * rows + pair_idx * SUBLANES:hq * rows + (pair_idx + 1) * SUBLANES, :] = (
                val[hq * SUBLANES:(hq + 1) * SUBLANES])

    for pair_idx in range(SAMPLE_SEQS // 2):
        l64 = gather_pair(lhs_scr, pair_idx)
        sa = _dot(l64, kc_ref[2 * pair_idx].astype(BF16))
        sb = _dot(l64, kc_ref[2 * pair_idx + 1].astype(BF16))
        scatter_pair(sc_scr, pair_idx, jnp.where(first_of_pair, sa, sb))

    rr = lax.broadcasted_iota(jnp.int32, (rows, rows), 0)
    cc = lax.broadcasted_iota(jnp.int32, (rows, rows), 1)
    tok = rr % DEC_SEQ
    mask_cache = cc > tok
    mask_new = (cc // DEC_SEQ == rr // DEC_SEQ) & (cc % DEC_SEQ <= tok)
    p_new, dens = [], []
    for hq in range(SWA_Q_HEADS):
        sl = slice(hq * rows, (hq + 1) * rows)
        sink = sink_ref[hq]
        s_c = jnp.where(mask_cache, sc_scr[sl, :], -jnp.inf)
        s_n = jnp.where(mask_new, s_new[sl, :], -jnp.inf)
        m = jnp.maximum(jnp.maximum(jnp.max(s_c, axis=1, keepdims=True),
                                    jnp.max(s_n, axis=1, keepdims=True)), sink)
        p_c = jnp.exp(s_c - m)
        p_n = jnp.exp(s_n - m)
        dens.append(jnp.sum(p_c, axis=1, keepdims=True) + jnp.sum(p_n, axis=1, keepdims=True)
                    + jnp.exp(sink - m))
        pc_scr[sl, :] = p_c
        p_new.append(p_n.astype(BF16))
    o_new = _dot(jnp.concatenate(p_new, axis=0), v_new.astype(BF16))

    for pair_idx in range(SAMPLE_SEQS // 2):
        p64 = gather_pair(pc_scr, pair_idx)
        oa = _dot_tb(p64, vc_ref[2 * pair_idx].astype(BF16))
        ob = _dot_tb(p64, vc_ref[2 * pair_idx + 1].astype(BF16))
        scatter_pair(oc_scr, pair_idx, jnp.where(first_of_pair, oa, ob))

    tiles = []
    for i in range(SWA_Q_HEADS // 2):
        halves = []
        for hq in (2 * i, 2 * i + 1):
            sl = slice(hq * rows, (hq + 1) * rows)
            oh = (oc_scr[sl, :] + o_new[sl, :]) / dens[hq]
            halves.append(oh if hq % 2 == hq // SWA_GROUP else pltpu.roll(oh, SWA_HEAD_DIM, 1))
        tiles.append(jnp.where(lo, halves[0], halves[1]))
    o_ref[...] = _rms(jnp.concatenate(tiles, axis=1), nrm_ref[...]).astype(BF16)

    keep = lax.broadcasted_iota(jnp.int32, (1, WINDOW), 1) < WINDOW - DEC_SEQ
    k_new_t = k_new.T
    v_new_t = v_new.T
    for b in range(SAMPLE_SEQS):
        to_tail = (WINDOW - DEC_SEQ - DEC_SEQ * b) % rows
        kco_ref[b] = jnp.where(keep, pltpu.roll(kc_ref[b], WINDOW - DEC_SEQ, 1),
                               pltpu.roll(k_new_t, to_tail, 1))
        vco_ref[b] = jnp.where(keep, pltpu.roll(vc_ref[b], WINDOW - DEC_SEQ, 1),
                               pltpu.roll(v_new_t, to_tail, 1))


def _swa_sample(sinks, qs, kvn, kc, vc, nrm, row0):
    nseq = kc.shape[0]
    rows = nseq * DEC_SEQ
    assert nseq % SAMPLE_SEQS == 0 and row0 % SAMPLE_ROWS == 0
    assert kc.shape[1:] == (SWA_KV_WIDTH, WINDOW) and WINDOW == SAMPLE_ROWS
    blk0 = row0 // SAMPLE_ROWS
    cblk = pl.BlockSpec((SAMPLE_SEQS, SWA_KV_WIDTH, WINDOW), lambda i: (i, 0, 0))
    big = pltpu.VMEM((SWA_Q_HEADS * SAMPLE_ROWS, LANES), F32)
    return pl.pallas_call(
        _swa_sample_body,
        grid=(nseq // SAMPLE_SEQS,),
        in_specs=[pl.BlockSpec(memory_space=pltpu.SMEM),
                  pl.BlockSpec((SAMPLE_ROWS, SWA_WIDTH), lambda i: (i + blk0, 0)),
                  pl.BlockSpec((SAMPLE_ROWS, 2 * SWA_KV_WIDTH), lambda i: (i + blk0, 0)),
                  cblk, cblk,
                  pl.BlockSpec((1, SWA_WIDTH), lambda i: (0, 0))],
        out_specs=[pl.BlockSpec((SAMPLE_ROWS, SWA_WIDTH), lambda i: (i, 0)), cblk, cblk],
        out_shape=[jax.ShapeDtypeStruct((rows, SWA_WIDTH), BF16),
                   jax.ShapeDtypeStruct(kc.shape, F32), jax.ShapeDtypeStruct(vc.shape, F32)],
        scratch_shapes=[big, big, big, big],
        compiler_params=pltpu.CompilerParams(dimension_semantics=("arbitrary",)),
        name="swa_sample",
    )(sinks, qs, kvn, kc, vc, nrm)


def _prepare_weights(ffn1_norm, ffn1_w_gate, ffn1_w_up, ffn1_w_down, mix_norm, w_in, w_gate_up, b_gate,
                     gla_head_norm, swa_out_norm, swa_sinks, w_out,
                     ffn2_norm, ffn2_w_gate, ffn2_w_up, ffn2_w_down, final_norm, layer):
    a0 = 2 * GLA_KEY_WIDTH + 2 * GLA_WIDTH
    a1 = a0 + GLA_GATE_RANK
    win = w_in[layer]
    win = jnp.concatenate([win[:, :a0], win[:, a1:], win[:, a0:a1],
                           jnp.zeros((D_MODEL, LANES - GLA_GATE_RANK), win.dtype)], axis=1)
    wgu = jnp.concatenate([w_gate_up[layer],
                           jnp.zeros((LANES - GLA_GATE_RANK, GLA_KEY_WIDTH), w_gate_up.dtype)], axis=0)
    row = lambda a: a.reshape(1, -1).astype(F32)
    return dict(
        n1=row(ffn1_norm[layer]), wg1=ffn1_w_gate[layer], wu1=ffn1_w_up[layer], wd1=ffn1_w_down[layer],
        nm=row(mix_norm[layer]), win=win.astype(BF16), wgu=wgu.astype(BF16), bg=row(b_gate[layer]),
        gn=row(gla_head_norm[layer]), sn=row(swa_out_norm[layer]), sinks=swa_sinks[layer].astype(F32),
        wo=w_out[layer], n2=row(ffn2_norm[layer]), wg2=ffn2_w_gate[layer], wu2=ffn2_w_up[layer],
        wd2=ffn2_w_down[layer], nf=row(final_norm))


def kernel(x_prompt, x_sample, state_gla, cache_swa_k, cache_swa_v, ffn1_norm, ffn1_w_gate, ffn1_w_up,
           ffn1_w_down, mix_norm, w_in, w_gate_up, b_gate, gla_head_norm, swa_out_norm, swa_sinks, w_out,
           ffn2_norm, ffn2_w_gate, ffn2_w_up, ffn2_w_down, final_norm):
    depth = state_gla.shape[0]
    assert depth == 1 and x_prompt.shape[0] == 1 and x_sample.shape[1] == DEC_SEQ
    seq = x_prompt.shape[1]
    nseq = x_sample.shape[0]
    w = _prepare_weights(ffn1_norm, ffn1_w_gate, ffn1_w_up, ffn1_w_down, mix_norm, w_in, w_gate_up, b_gate,
                         gla_head_norm, swa_out_norm, swa_sinks, w_out,
                         ffn2_norm, ffn2_w_gate, ffn2_w_up, ffn2_w_down, final_norm, 0)

    (x1, qk, g, v, r, qs, kv), (wo, wg2, wu2, wd2) = _stage_a(
        x_prompt.reshape(seq, D_MODEL), x_sample.reshape(nseq * DEC_SEQ, D_MODEL), w)
    w = dict(w, wo=wo, wg2=wg2, wu2=wu2, wd2=wd2)

    og_s, state_s = _gla_sample(qk, g, v, r, w["gn"],
                                state_gla[0].reshape(nseq, GLA_KEY_WIDTH, GLA_HEAD_V), seq)
    cache_w = cache_swa_k.shape[2]
    to_feature_major = lambda c: jnp.transpose(c[0], (0, 2, 3, 1)).reshape(nseq, SWA_KV_WIDTH, cache_w)
    from_feature_major = lambda c: jnp.transpose(
        c.reshape(nseq, SWA_KV_HEADS, SWA_HEAD_DIM, cache_w), (0, 3, 1, 2))[None]
    os_s, k_cache_s, v_cache_s = _swa_sample(
        w["sinks"], qs, kv, to_feature_major(cache_swa_k), to_feature_major(cache_swa_v), w["sn"], seq)
    k_cache_s = from_feature_major(k_cache_s)
    v_cache_s = from_feature_major(v_cache_s)

    y_prompt, y_sample, state_p = _mix_out(qk, g, v, r, qs, kv, x1, og_s, os_s, w)
    y_prompt = y_prompt.reshape(1, seq, D_MODEL)
    y_sample = y_sample.reshape(nseq, DEC_SEQ, D_MODEL)
    cw = min(WINDOW, seq)
    k_cache_p = kv[seq - cw:seq, 0:SWA_KV_WIDTH].reshape(1, 1, cw, SWA_KV_HEADS, SWA_HEAD_DIM)
    v_cache_p = kv[seq - cw:seq, SWA_KV_WIDTH:].reshape(1, 1, cw, SWA_KV_HEADS, SWA_HEAD_DIM)
    state_p = state_p.reshape(1, 1, GLA_HEADS, GLA_HEAD_K, GLA_HEAD_V)
    state_s = state_s.reshape(1, nseq, GLA_HEADS, GLA_HEAD_K, GLA_HEAD_V)

    return (y_prompt, y_sample, state_p, k_cache_p, v_cache_p, state_s, k_cache_s, v_cache_s)
```

```python
import functools
import itertools

import jax
import jax.numpy as jnp
from jax import lax
from jax.experimental import pallas as pl
from jax.experimental.pallas import tpu as pltpu

F32 = jnp.float32
BF16 = jnp.bfloat16

D_MODEL = 1024
D_FF = 2816
GLA_HEADS = 4
GLA_HEAD_K = 64
GLA_HEAD_V = 128
GLA_KEY_WIDTH = GLA_HEADS * GLA_HEAD_K
GLA_WIDTH = GLA_HEADS * GLA_HEAD_V
GLA_GATE_RANK = 16
GLA_GATE_TAU = 16.0
SWA_HEAD_DIM = 64
SWA_Q_HEADS = 8
SWA_KV_HEADS = 2
SWA_GROUP = SWA_Q_HEADS // SWA_KV_HEADS
SWA_WIDTH = SWA_Q_HEADS * SWA_HEAD_DIM
SWA_KV_WIDTH = SWA_KV_HEADS * SWA_HEAD_DIM
WINDOW = 128
DEC_SEQ = 4
NORM_EPS = 1e-6
HEAD_SCALE = 0.125

LANES = 128
SUBLANES = 8
VMEM_LIMIT_BYTES = 58 * 1024 * 1024

ROW_TILE = 512
FF_CHUNK = 256
DOWN_CHUNK = 256
GLA_CHUNK = 128
SAMPLE_SEQS = 32
SAMPLE_ROWS = SAMPLE_SEQS * DEC_SEQ
MIXER_STAGES = 3 * (ROW_TILE // GLA_CHUNK) + 3 * (ROW_TILE // WINDOW)
SIDE_STAGES_AT_NORM = 2
DECAY_CLAMP = 60.0

PROJ_Q_G = 0
PROJ_K_G = 256
PROJ_V_G = 512
PROJ_R_G = 1024
PROJ_Q_S = 1536
PROJ_KV_S = 2048
PROJ_A = 2304
PROJ_WIDTH = 2432
D_IN = 2320


def _dot(a, b):
    return jnp.dot(a, b, preferred_element_type=F32)


def _dot_tb(a, b):
    return lax.dot_general(a, b, (((1,), (1,)), ((), ())), preferred_element_type=F32)


def _dot_ta(a, b):
    return lax.dot_general(a, b, (((0,), (0,)), ((), ())), preferred_element_type=F32)


def _rms(x, g):
    return x * lax.rsqrt(jnp.mean(x * x, axis=-1, keepdims=True) + NORM_EPS) * g


def _swiglu(h, wg_ref, wu_ref, wd_ref, act_ref, side_stages=iter(()), n_side_stages=0):
    n_dots = 2 * (D_FF // FF_CHUNK) + D_MODEL // DOWN_CHUNK
    done = [0, 0]

    def after_dot():
        done[0] += 1
        while done[1] * n_dots < done[0] * n_side_stages:
            next(side_stages, None)
            done[1] += 1

    for c0 in range(0, D_FF, FF_CHUNK):
        g = _dot(h, wg_ref[:, c0:c0 + FF_CHUNK])
        after_dot()
        u = _dot(h, wu_ref[:, c0:c0 + FF_CHUNK])
        after_dot()
        act_ref[:, c0:c0 + FF_CHUNK] = (g * jax.nn.sigmoid(g) * u).astype(BF16)
    outs = []
    for n0 in range(0, D_MODEL, DOWN_CHUNK):
        outs.append(_dot(act_ref[...], wd_ref[:, n0:n0 + DOWN_CHUNK]))
        after_dot()
    return jnp.concatenate(outs, axis=1)


def _head_norm_gate(o, r, gn):
    parts = []
    for h in range(GLA_HEADS):
        sl = slice(h * GLA_HEAD_V, (h + 1) * GLA_HEAD_V)
        parts.append(_rms(o[:, sl], gn[:, sl]))
    return (jnp.concatenate(parts, axis=1) * (r * jax.nn.sigmoid(r))).astype(BF16)


def _out_ffn(x1_ref, og, osw, wo_ref, n2_ref, wg_ref, wu_ref, wd_ref, nf_ref, y_ref, act_ref, side_stages=None):
    n_inside = 0 if side_stages is None else MIXER_STAGES - 2 * SIDE_STAGES_AT_NORM
    side_stages = iter(()) if side_stages is None else side_stages
    mixed = jnp.concatenate([og, osw], axis=1)
    x2 = x1_ref[...] + _dot(mixed, wo_ref[...])
    for _ in range(SIDE_STAGES_AT_NORM):
        next(side_stages, None)
    h = _rms(x2, n2_ref[...]).astype(BF16)
    x3 = x2 + 0.5 * _swiglu(h, wg_ref, wu_ref, wd_ref, act_ref, side_stages, n_inside)
    for _ in side_stages:
        pass
    y_ref[...] = _rms(x3, nf_ref[...])


_HBM = pl.BlockSpec(memory_space=pl.ANY)
WIDE_CHUNK_ROWS = 128
NARROW_CHUNK_ROWS = 256
LOAD_SLOTS = 3


def _load_weights_bf16(pairs, staging, sems):
    jobs = []
    for src, dst in pairs:
        rows, cols = src.shape
        assert cols in (D_FF, D_MODEL) and dst.shape == src.shape
        kind, step = (0, WIDE_CHUNK_ROWS) if cols == D_FF else (1, NARROW_CHUNK_ROWS)
        assert rows % step == 0
        jobs += [(src, dst, r0, step, kind) for r0 in range(0, rows, step)]

    used = [0, 0]
    copies = []
    for src, _, r0, nr, kind in jobs:
        slot = used[kind] % LOAD_SLOTS
        used[kind] += 1
        copies.append((pltpu.make_async_copy(src.at[pl.ds(r0, nr), :], staging[kind].at[slot],
                                             sems.at[kind, slot]), slot))
    for copy, _ in copies[:LOAD_SLOTS - 1]:
        copy.start()
    for j, (_, dst, r0, nr, kind) in enumerate(jobs):
        ahead = j + LOAD_SLOTS - 1
        if ahead < len(jobs):
            copies[ahead][0].start()
        copy, slot = copies[j]
        copy.wait()
        dst[r0:r0 + nr, :] = staging[kind][slot].astype(BF16)


def _load_w_in(win_hbm, win_ref):
    a0, a1 = PROJ_Q_S, PROJ_Q_S + GLA_GATE_RANK
    n_rows = WIDE_CHUNK_ROWS
    n_chunks = D_MODEL // n_rows

    def body(stage, sems):
        copies = [pltpu.make_async_copy(win_hbm.at[pl.ds(j * n_rows, n_rows), :], stage.at[j % 2],
                                        sems.at[j % 2]) for j in range(n_chunks)]
        copies[0].start()
        for j in range(n_chunks):
            if j + 1 < n_chunks:
                copies[j + 1].start()
            copies[j].wait()
            rows = slice(j * n_rows, (j + 1) * n_rows)
            chunk = stage[j % 2]
            win_ref[rows, 0:a0] = chunk[:, 0:a0].astype(BF16)
            win_ref[rows, a0:PROJ_A] = chunk[:, a1:D_IN].astype(BF16)
            gate = jnp.concatenate([chunk[:, a0:a1], jnp.zeros((n_rows, LANES - GLA_GATE_RANK), F32)], axis=1)
            win_ref[rows, PROJ_A:PROJ_WIDTH] = gate.astype(BF16)

    pl.run_scoped(body, pltpu.VMEM((2, n_rows, D_IN), F32), pltpu.SemaphoreType.DMA((2,)))


def _staging_scratch():
    return [pltpu.VMEM((LOAD_SLOTS, WIDE_CHUNK_ROWS, D_FF), F32),
            pltpu.VMEM((LOAD_SLOTS, NARROW_CHUNK_ROWS, D_MODEL), F32),
            pltpu.SemaphoreType.DMA((2, LOAD_SLOTS))]


def _chunk_plan(matrices):
    plan, first = [], 0
    for m, mat in enumerate(matrices):
        rows, cols = mat.shape
        assert cols in (D_FF, D_MODEL)
        kind, nr = (0, WIDE_CHUNK_ROWS) if cols == D_FF else (1, NARROW_CHUNK_ROWS)
        assert rows % nr == 0
        plan.append((m, first, rows // nr, nr, kind))
        first += rows // nr
    return plan, first


def _background_cast(step, srcs, dsts, staging, out_staging, sem_in, sem_out):
    plan, _ = _chunk_plan(srcs)
    slot = step % 2

    def rows_of(m, chunk):
        _, first, _, nr, _ = plan[m]
        return pl.ds(pl.multiple_of((chunk - first) * nr, nr), nr)

    def copy_in(m, chunk, slot_):
        kind = plan[m][4]
        return pltpu.make_async_copy(srcs[m].at[rows_of(m, chunk), :], staging[kind].at[slot_],
                                     sem_in.at[kind, slot_])

    def copy_out(m, chunk, slot_):
        kind = plan[m][4]
        return pltpu.make_async_copy(out_staging[kind].at[slot_], dsts[m].at[rows_of(m, chunk), :],
                                     sem_out.at[kind, slot_])

    def for_chunk(chunk, fn):
        for m, first, n, _, _ in plan:
            pl.when(jnp.logical_and(chunk >= first, chunk < first + n))(functools.partial(fn, m))

    def before():
        @pl.when(step == 0)
        def _():
            copy_in(0, 0, 0).start()
        for_chunk(step, lambda m: copy_in(m, step, slot).wait())
        for_chunk(step - 2, lambda m: copy_out(m, step - 2, slot).wait())
        for_chunk(step + 1, lambda m: copy_in(m, step + 1, 1 - slot).start())

    def cast():
        for kind in range(2):
            out_staging[kind][slot] = staging[kind][slot].astype(BF16)

    def after():
        for_chunk(step, lambda m: copy_out(m, step, slot).start())

    return before, cast, after


def _stage_a_body(xp_ref, xs_ref, n1_ref, wg_hbm, wu_hbm, wd_hbm, nm_ref, win_hbm, wgu_ref, bg_ref,
                  wo_hbm, wg2_hbm, wu2_hbm, wd2_hbm,
                  x1_ref, qk_ref, g_ref, v_ref, r_ref, qs_ref, kv_ref, wo_out, wg2_out, wu2_out, wd2_out,
                  act_ref, wg_ref, wu_ref, wd_ref, win_ref, wide_stage, narrow_stage, load_sems,
                  wide_out, narrow_out, sem_in, sem_out):
    i = pl.program_id(0)

    @pl.when(i == 0)
    def _():
        _load_weights_bf16([(wg_hbm, wg_ref), (wu_hbm, wu_ref), (wd_hbm, wd_ref)],
                           (wide_stage, narrow_stage), load_sems)
        _load_w_in(win_hbm, win_ref)

    bg_before, bg_cast, bg_after = _background_cast(
        i, (wo_hbm, wg2_hbm, wu2_hbm, wd2_hbm), (wo_out, wg2_out, wu2_out, wd2_out),
        (wide_stage, narrow_stage), (wide_out, narrow_out), sem_in, sem_out)
    bg_before()

    x = jnp.where(i < pl.num_programs(0) - 1, xp_ref[...], xs_ref[...])
    h = _rms(x, n1_ref[...]).astype(BF16)
    bg_cast()
    x1 = x + 0.5 * _swiglu(h, wg_ref, wu_ref, wd_ref, act_ref)
    x1_ref[...] = x1
    h2 = _rms(x1, nm_ref[...]).astype(BF16)
    proj = _dot(h2, win_ref[...])
    qk_ref[:, 0:GLA_KEY_WIDTH] = proj[:, PROJ_Q_G:PROJ_K_G] * HEAD_SCALE
    qk_ref[:, GLA_KEY_WIDTH:] = proj[:, PROJ_K_G:PROJ_V_G]
    v_ref[...] = proj[:, PROJ_V_G:PROJ_R_G].astype(BF16)
    r_ref[...] = proj[:, PROJ_R_G:PROJ_Q_S]
    qs_ref[...] = (proj[:, PROJ_Q_S:PROJ_KV_S] * HEAD_SCALE).astype(BF16)
    kv_ref[...] = proj[:, PROJ_KV_S:PROJ_A]
    a = proj[:, PROJ_A:PROJ_WIDTH].astype(BF16)
    z = _dot(a, wgu_ref[...]) + bg_ref[...]
    g_ref[...] = jax.nn.log_sigmoid(z) * (1.0 / GLA_GATE_TAU)
    bg_after()


def _resident(shape):
    return pl.BlockSpec(shape, lambda i: (0,) * len(shape), pipeline_mode=pl.Buffered(1))


def _rows(tm, n):
    return pl.BlockSpec((tm, n), lambda i: (i, 0))


def _stage_a(xp, xs, w):
    t = xp.shape[0]
    assert t % ROW_TILE == 0 and xs.shape[0] == ROW_TILE
    nt = t // ROW_TILE
    out_widths = ((D_MODEL, F32), (2 * GLA_KEY_WIDTH, F32), (GLA_KEY_WIDTH, F32), (GLA_WIDTH, BF16),
                  (GLA_WIDTH, F32), (SWA_WIDTH, BF16), (2 * SWA_KV_WIDTH, F32))
    to_cast = (w["wo"], w["wg2"], w["wu2"], w["wd2"])
    assert _chunk_plan(to_cast)[1] + 2 <= nt + 1
    outs = pl.pallas_call(
        _stage_a_body,
        grid=(nt + 1,),
        in_specs=[pl.BlockSpec((ROW_TILE, D_MODEL), lambda i: (jnp.minimum(i, nt - 1), 0)),
                  _resident((ROW_TILE, D_MODEL)), _resident((1, D_MODEL)),
                  _HBM, _HBM, _HBM,
                  _resident((1, D_MODEL)), _HBM,
                  _resident((LANES, GLA_KEY_WIDTH)), _resident((1, GLA_KEY_WIDTH)),
                  _HBM, _HBM, _HBM, _HBM],
        out_specs=[_rows(ROW_TILE, n) for n, _ in out_widths] + [_HBM] * len(to_cast),
        out_shape=([jax.ShapeDtypeStruct((t + ROW_TILE, n), dt) for n, dt in out_widths]
                   + [jax.ShapeDtypeStruct(m.shape, BF16) for m in to_cast]),
        scratch_shapes=([pltpu.VMEM((ROW_TILE, D_FF), BF16), pltpu.VMEM((D_MODEL, D_FF), BF16),
                         pltpu.VMEM((D_MODEL, D_FF), BF16), pltpu.VMEM((D_FF, D_MODEL), BF16),
                         pltpu.VMEM((D_MODEL, PROJ_WIDTH), BF16)]
                        + _staging_scratch()
                        + [pltpu.VMEM((2, WIDE_CHUNK_ROWS, D_FF), BF16),
                           pltpu.VMEM((2, NARROW_CHUNK_ROWS, D_MODEL), BF16),
                           pltpu.SemaphoreType.DMA((2, 2)), pltpu.SemaphoreType.DMA((2, 2))]),
        compiler_params=pltpu.CompilerParams(dimension_semantics=("arbitrary",),
                                             vmem_limit_bytes=VMEM_LIMIT_BYTES),
        name="stage_a_ffn1_proj",
    )(xp, xs, w["n1"], w["wg1"], w["wu1"], w["wd1"], w["nm"], w["win"], w["wgu"], w["bg"], *to_cast)
    return outs[:len(out_widths)], outs[len(out_widths):]


def _split_bf16(x):
    hi = x.astype(BF16)
    lo = (x - hi.astype(F32)).astype(BF16)
    return hi, lo


def _alternate(a, b):
    pending = [iter(a), iter(b)]
    while pending:
        for it in list(pending):
            try:
                next(it)
            except StopIteration:
                pending.remove(it)
                continue
            yield


def _gla_stages(qk_ref, g_ref, v_ref, r_ref, gn_ref, s_ref, og_scr, cum_scr, inter_scr, flag_ref):
    c_len = GLA_CHUNK
    n_chunks = ROW_TILE // c_len
    causal = (lax.broadcasted_iota(jnp.int32, (c_len, c_len), 0)
              >= lax.broadcasted_iota(jnp.int32, (c_len, c_len), 1))
    ltri = jnp.where(causal, 1.0, 0.0).astype(BF16)
    causal_cat = (lax.broadcasted_iota(jnp.int32, (c_len, GLA_HEADS * c_len), 0)
                  >= lax.broadcasted_iota(jnp.int32, (c_len, GLA_HEADS * c_len), 1) % c_len)
    lane_head = lax.broadcasted_iota(jnp.int32, (1, GLA_KEY_WIDTH), 1) // GLA_HEAD_K
    row_head = lax.broadcasted_iota(jnp.int32, (GLA_KEY_WIDTH, 1), 0) // GLA_HEAD_K
    eye = (lax.broadcasted_iota(jnp.int32, (GLA_KEY_WIDTH, GLA_KEY_WIDTH), 0)
           == lax.broadcasted_iota(jnp.int32, (GLA_KEY_WIDTH, GLA_KEY_WIDTH), 1))
    zero_b = jnp.zeros((), BF16)
    zero_v = jnp.zeros((c_len, GLA_HEAD_V), BF16)
    worst = []

    def chunk(c):
        rows = slice(c * c_len, (c + 1) * c_len)
        g_hi, g_lo = _split_bf16(g_ref[rows, :])
        cum = _dot(ltri, g_hi) + _dot(ltri, g_lo)
        yield
        cum_scr[rows, :] = cum
        last = cum[c_len - 1:c_len, :]
        q = qk_ref[rows, 0:GLA_KEY_WIDTH]
        k = qk_ref[rows, GLA_KEY_WIDTH:]
        vb = v_ref[rows, :]
        qe = (q * jnp.exp(cum)).astype(BF16)
        ke = (k * jnp.exp(jnp.minimum(-cum, DECAY_CLAMP))).astype(BF16)
        kl = (k * jnp.exp(last - cum)).astype(BF16)
        state = s_ref[...]
        sb = state.astype(BF16)
        s_bd = jnp.concatenate([jnp.where(row_head == h, sb, zero_b) for h in range(GLA_HEADS)], axis=1)
        o_inter = _dot(qe, s_bd)
        inter_scr[rows, :] = o_inter
        ke_bd = jnp.concatenate([jnp.where(lane_head == h, ke, zero_b) for h in range(GLA_HEADS)], axis=0)
        attn = _dot_tb(qe, ke_bd)
        upds = []
        for p in range(GLA_HEADS // 2):
            u = _dot_ta(kl[:, p * LANES:(p + 1) * LANES], vb[:, 2 * p * GLA_HEAD_V:(2 * p + 2) * GLA_HEAD_V])
            upds.append(u[0:GLA_HEAD_K, 0:GLA_HEAD_V])
            upds.append(u[GLA_HEAD_K:, GLA_HEAD_V:])
        yield
        attn = jnp.where(causal_cat, attn, 0.0).astype(BF16)
        o_pairs = []
        for p in range(GLA_HEADS // 2):
            v_a = vb[:, (2 * p) * GLA_HEAD_V:(2 * p + 1) * GLA_HEAD_V]
            v_b = vb[:, (2 * p + 1) * GLA_HEAD_V:(2 * p + 2) * GLA_HEAD_V]
            v_bd = jnp.concatenate([jnp.concatenate([v_a, zero_v], axis=1),
                                    jnp.concatenate([zero_v, v_b], axis=1)], axis=0)
            o_pairs.append(_dot(attn[:, 2 * p * c_len:(2 * p + 2) * c_len], v_bd))
        og_scr[rows, :] = _head_norm_gate(o_inter + jnp.concatenate(o_pairs, axis=1), r_ref[rows, :],
                                          gn_ref[...])
        last_col = jnp.sum(jnp.where(eye, last, 0.0), axis=1, keepdims=True)
        s_ref[...] = state * jnp.exp(last_col) + jnp.concatenate(upds, axis=0)
        worst.append(jnp.min(last, axis=1, keepdims=True))
        if c == n_chunks - 1:
            tile_min = functools.reduce(jnp.minimum, worst)
            flag_ref[0] = jnp.where(tile_min[0, 0] < -DECAY_CLAMP, 1, 0)
        yield

    return itertools.chain.from_iterable(chunk(c) for c in range(n_chunks))


def _gla_tile_exact(qk_ref, v_ref, r_ref, gn_ref, og_scr, o_scr, cum_scr, inter_scr):
    c_len = GLA_CHUNK
    ind = jnp.where(lax.broadcasted_iota(jnp.int32, (GLA_KEY_WIDTH, LANES), 0) // GLA_HEAD_K
                    == lax.broadcasted_iota(jnp.int32, (GLA_KEY_WIDTH, LANES), 1), 1.0, 0.0).astype(BF16)
    j_idx = lax.broadcasted_iota(jnp.int32, (c_len, 1), 0)
    for c in range(ROW_TILE // c_len):
        r0 = c * c_len
        rows = slice(r0, r0 + c_len)

        def one_row(i, carry, r0=r0, rows=rows):
            ci = cum_scr[pl.ds(r0 + i, 1), :]
            qi = qk_ref[pl.ds(r0 + i, 1), 0:GLA_KEY_WIDTH]
            kk = qk_ref[rows, GLA_KEY_WIDTH:]
            dec = jnp.exp(jnp.minimum(ci - cum_scr[rows, :], 0.0))
            a_cols = _dot(((qi * kk) * dec).astype(BF16), ind)
            outs = []
            for h in range(GLA_HEADS):
                w_col = jnp.where(j_idx <= i, a_cols[:, h:h + 1], 0.0)
                v_h = v_ref[rows, h * GLA_HEAD_V:(h + 1) * GLA_HEAD_V].astype(F32)
                outs.append(jnp.sum(w_col * v_h, axis=0, keepdims=True))
            o_scr[pl.ds(r0 + i, 1), :] = inter_scr[pl.ds(r0 + i, 1), :] + jnp.concatenate(outs, axis=1)
            return carry

        lax.fori_loop(0, c_len, one_row, 0)
    og_scr[...] = _head_norm_gate(o_scr[...], r_ref[...], gn_ref[...])


def _dup_halves(x):
    lo = lax.broadcasted_iota(jnp.int32, (1, LANES), 1) < SWA_HEAD_DIM
    sw = pltpu.roll(x, SWA_HEAD_DIM, 1)
    return jnp.where(lo, x, sw).astype(BF16), jnp.where(lo, sw, x).astype(BF16)


def _swa_stages(sink_ref, qs_ref, kv_ref, kvprev_scr, sn_ref, os_scr, no_prev_bias):
    w = WINDOW
    n_blocks = ROW_TILE // w
    lo = lax.broadcasted_iota(jnp.int32, (1, LANES), 1) < SWA_HEAD_DIM
    tri = (lax.broadcasted_iota(jnp.int32, (w, w), 1) <= lax.broadcasted_iota(jnp.int32, (w, w), 0))
    zero_b = jnp.zeros((), BF16)

    def block(b):
        rows = slice(b * w, (b + 1) * w)
        kv_prev = kvprev_scr[...] if b == 0 else kv_ref[(b - 1) * w:b * w, :]
        kv = jnp.concatenate([kv_prev, kv_ref[rows, :]], axis=0)
        kk = _dup_halves(kv[:, 0:SWA_KV_WIDTH])
        vv = _dup_halves(kv[:, SWA_KV_WIDTH:])
        q = qs_ref[rows, :]
        scores = []
        for grp in range(SWA_KV_HEADS):
            stacked = []
            for j in range(SWA_GROUP):
                hq = grp * SWA_GROUP + j
                tile = q[:, (hq // 2) * LANES:(hq // 2 + 1) * LANES]
                stacked.append(jnp.where(lo if hq % 2 == 0 else ~lo, tile, zero_b))
            scores.append(_dot_tb(jnp.concatenate(stacked, axis=0), kk[grp]))
        yield
        outs, dens = [], []
        for grp in range(SWA_KV_HEADS):
            s = scores[grp]
            probs = []
            for j in range(SWA_GROUP):
                sink = sink_ref[grp * SWA_GROUP + j]
                s_prev = s[j * w:(j + 1) * w, 0:w]
                if b == 0:
                    s_prev = s_prev + no_prev_bias
                sf = jnp.where(tri, s[j * w:(j + 1) * w, w:], s_prev)
                m = jnp.maximum(jnp.max(sf, axis=1, keepdims=True), sink)
                p = jnp.exp(sf - m)
                dens.append(jnp.sum(p, axis=1, keepdims=True) + jnp.exp(sink - m))
                pb = p.astype(BF16)
                probs.append(jnp.concatenate([jnp.where(tri, zero_b, pb), jnp.where(tri, pb, zero_b)], axis=1))
            outs.append(_dot(jnp.concatenate(probs, axis=0), vv[grp]))
        yield
        tiles = []
        for grp in range(SWA_KV_HEADS):
            res = [outs[grp][j * w:(j + 1) * w] / dens[grp * SWA_GROUP + j] for j in range(SWA_GROUP)]
            tiles.append(jnp.where(lo, res[0], res[1]))
            tiles.append(jnp.where(lo, res[2], res[3]))
        os_scr[rows, :] = _rms(jnp.concatenate(tiles, axis=1), sn_ref[...]).astype(BF16)
        if b == n_blocks - 1:
            kvprev_scr[...] = kv_ref[rows, :]
        yield

    return itertools.chain.from_iterable(block(b) for b in range(n_blocks))


def _mix_out_body(sink_ref, qk_ref, g_ref, v_ref, r_ref, qs_ref, kv_ref, x1_ref, ogs_ref, oss_ref,
                  gn_ref, sn_ref, wo_ref, n2_ref, wg_ref, wu_ref, wd_ref, nf_ref, yp_ref, ys_ref, s_ref,
                  og_scr, os_scr, kvprev_scr, o_scr, cum_scr, inter_scr, act_scr, flag_ref):
    i = pl.program_id(0)
    n_prompt = pl.num_programs(0) - 2

    def stage_d(og, osw, y_ref, side_stages=None):
        _out_ffn(x1_ref, og, osw, wo_ref, n2_ref, wg_ref, wu_ref, wd_ref, nf_ref, y_ref, act_scr,
                 side_stages)

    def mixer_stages():
        gla = _gla_stages(qk_ref, g_ref, v_ref, r_ref, gn_ref, s_ref, og_scr, cum_scr, inter_scr, flag_ref)
        swa = _swa_stages(sink_ref, qs_ref, kv_ref, kvprev_scr, sn_ref, os_scr,
                          jnp.where(i > 0, 0.0, -jnp.inf))
        return _alternate(gla, swa)

    @pl.when(i == 0)
    def _():
        s_ref[...] = jnp.zeros_like(s_ref)
        kvprev_scr[...] = jnp.zeros_like(kvprev_scr)
        for _ in mixer_stages():
            pass

    @pl.when(jnp.logical_and(i > 0, i < n_prompt))
    def _():
        stage_d(og_scr[...], os_scr[...], yp_ref, mixer_stages())

    @pl.when(i == n_prompt)
    def _():
        stage_d(og_scr[...], os_scr[...], yp_ref)
        flag_ref[0] = 0

    @pl.when(i == n_prompt + 1)
    def _():
        stage_d(ogs_ref[...], oss_ref[...], ys_ref)

    @pl.when(flag_ref[0] != 0)
    def _():
        _gla_tile_exact(qk_ref, v_ref, r_ref, gn_ref, og_scr, o_scr, cum_scr, inter_scr)


def _mix_out(qk, g, v, r, qs, kv, x1, og_s, os_s, w):
    t = qk.shape[0] - ROW_TILE
    assert t % ROW_TILE == 0 and og_s.shape[0] == ROW_TILE
    nt = t // ROW_TILE
    cur = lambda n: pl.BlockSpec((ROW_TILE, n), lambda i: (jnp.minimum(i, nt - 1), 0))
    return pl.pallas_call(
        _mix_out_body,
        grid=(nt + 2,),
        in_specs=[pl.BlockSpec(memory_space=pltpu.SMEM),
                  cur(2 * GLA_KEY_WIDTH), cur(GLA_KEY_WIDTH), cur(GLA_WIDTH), cur(GLA_WIDTH),
                  cur(SWA_WIDTH), cur(2 * SWA_KV_WIDTH),
                  pl.BlockSpec((ROW_TILE, D_MODEL), lambda i: (jnp.maximum(i - 1, 0), 0)),
                  _resident((ROW_TILE, GLA_WIDTH)), _resident((ROW_TILE, SWA_WIDTH)),
                  _resident((1, GLA_WIDTH)), _resident((1, SWA_WIDTH)),
                  _resident((D_MODEL, D_MODEL)), _resident((1, D_MODEL)),
                  _resident((D_MODEL, D_FF)), _resident((D_MODEL, D_FF)), _resident((D_FF, D_MODEL)),
                  _resident((1, D_MODEL))],
        out_specs=[pl.BlockSpec((ROW_TILE, D_MODEL), lambda i: (jnp.clip(i - 1, 0, nt - 1), 0)),
                   pl.BlockSpec((ROW_TILE, D_MODEL), lambda i: (0, 0)),
                   pl.BlockSpec((GLA_KEY_WIDTH, GLA_HEAD_V), lambda i: (0, 0))],
        out_shape=[jax.ShapeDtypeStruct((t, D_MODEL), F32),
                   jax.ShapeDtypeStruct((ROW_TILE, D_MODEL), F32),
                   jax.ShapeDtypeStruct((GLA_KEY_WIDTH, GLA_HEAD_V), F32)],
        scratch_shapes=[pltpu.VMEM((ROW_TILE, GLA_WIDTH), BF16), pltpu.VMEM((ROW_TILE, SWA_WIDTH), BF16),
                        pltpu.VMEM((WINDOW, 2 * SWA_KV_WIDTH), F32),
                        pltpu.VMEM((ROW_TILE, GLA_WIDTH), F32), pltpu.VMEM((ROW_TILE, GLA_KEY_WIDTH), F32),
                        pltpu.VMEM((ROW_TILE, GLA_WIDTH), F32), pltpu.VMEM((ROW_TILE, D_FF), BF16),
                        pltpu.SMEM((1,), jnp.int32)],
        compiler_params=pltpu.CompilerParams(dimension_semantics=("arbitrary",),
                                             vmem_limit_bytes=VMEM_LIMIT_BYTES),
        name="prompt_mixers_stage_d",
    )(w["sinks"], qk, g, v, r, qs, kv, x1, og_s, os_s, w["gn"], w["sn"], w["wo"], w["n2"], w["wg2"],
      w["wu2"], w["wd2"], w["nf"])


def _gla_sample_body(qk_ref, g_ref, v_ref, r_ref, gn_ref, s_ref, og_ref, so_ref, o_scr):
    rows = SAMPLE_ROWS
    g = g_ref[...]
    q = qk_ref[:, 0:GLA_KEY_WIDTH]
    k = qk_ref[:, GLA_KEY_WIDTH:]
    vb = v_ref[...]
    vf = vb.astype(F32)
    tok = lax.broadcasted_iota(jnp.int32, (rows, 1), 0) % DEC_SEQ

    cum = g
    for d in range(1, DEC_SEQ):
        cum = cum + jnp.where(tok >= d, pltpu.roll(g, d, 0), 0.0)
    tot = jnp.where(tok == DEC_SEQ - 1, cum, 0.0)
    for d in range(1, DEC_SEQ):
        tot = tot + jnp.where(tok == DEC_SEQ - 1 - d, pltpu.roll(cum, rows - d, 0), 0.0)

    qe = q * jnp.exp(cum)
    kl = k * jnp.exp(tot - cum)
    decay_t = jnp.exp(tot).T

    ind = jnp.where(lax.broadcasted_iota(jnp.int32, (GLA_KEY_WIDTH, LANES), 0) // GLA_HEAD_K
                    == lax.broadcasted_iota(jnp.int32, (GLA_KEY_WIDTH, LANES), 1), 1.0, 0.0).astype(BF16)
    expand = jnp.where(lax.broadcasted_iota(jnp.int32, (LANES, GLA_WIDTH), 0)
                       == lax.broadcasted_iota(jnp.int32, (LANES, GLA_WIDTH), 1) // GLA_HEAD_V,
                       1.0, 0.0).astype(BF16)

    o_intra = jnp.zeros((rows, GLA_WIDTH), F32)
    for d in range(DEC_SEQ):
        k_d = k if d == 0 else pltpu.roll(k, d, 0)
        c_d = cum if d == 0 else pltpu.roll(cum, d, 0)
        v_d = vf if d == 0 else pltpu.roll(vf, d, 0)
        pair = jnp.where(tok >= d, q * k_d * jnp.exp(jnp.minimum(cum - c_d, 0.0)), 0.0)
        a = _dot(pair.astype(BF16), ind)
        o_intra = o_intra + _dot(a.astype(BF16), expand) * v_d

    lane_head = lax.broadcasted_iota(jnp.int32, (1, GLA_KEY_WIDTH), 1) // GLA_HEAD_K
    row8 = lax.broadcasted_iota(jnp.int32, (SUBLANES, 1), 0)
    row32 = lax.broadcasted_iota(jnp.int32, (GLA_HEADS * SUBLANES, 1), 0)
    for pair_idx in range(SAMPLE_SEQS // 2):
        r8 = slice(pair_idx * SUBLANES, (pair_idx + 1) * SUBLANES)
        q8 = qe[r8, :]
        lhs = jnp.concatenate([jnp.where(lane_head == h, q8, 0.0) for h in range(GLA_HEADS)],
                              axis=0).astype(BF16)
        kl8 = kl[r8, :]
        v8 = vb[r8, :]
        res = []
        for s in range(2):
            b = 2 * pair_idx + s
            state = s_ref[b]
            res.append(_dot(lhs, state.astype(BF16)))
            kl_b = jnp.where(row8 // DEC_SEQ == s, kl8, 0.0).astype(BF16)
            upd = _dot_ta(kl_b, v8)
            upd = jnp.concatenate(
                [upd[h * GLA_HEAD_K:(h + 1) * GLA_HEAD_K, h * GLA_HEAD_V:(h + 1) * GLA_HEAD_V]
                 for h in range(GLA_HEADS)], axis=0)
            so_ref[b] = state * decay_t[:, DEC_SEQ * b:DEC_SEQ * b + 1] + upd
        sel = jnp.where(row32 % SUBLANES < DEC_SEQ, res[0], res[1])
        o_scr[r8, :] = jnp.concatenate([sel[h * SUBLANES:(h + 1) * SUBLANES] for h in range(GLA_HEADS)], axis=1)

    og_ref[...] = _head_norm_gate(o_scr[...] + o_intra, r_ref[...], gn_ref[...])


def _gla_sample(qk, g, v, r, gn, state, row0):
    nseq = state.shape[0]
    rows = nseq * DEC_SEQ
    assert nseq % SAMPLE_SEQS == 0 and row0 % SAMPLE_ROWS == 0
    blk0 = row0 // SAMPLE_ROWS
    src = lambda n: pl.BlockSpec((SAMPLE_ROWS, n), lambda i: (i + blk0, 0))
    blk = lambda n: pl.BlockSpec((SAMPLE_ROWS, n), lambda i: (i, 0))
    sblk = pl.BlockSpec((SAMPLE_SEQS, GLA_KEY_WIDTH, GLA_HEAD_V), lambda i: (i, 0, 0))
    return pl.pallas_call(
        _gla_sample_body,
        grid=(nseq // SAMPLE_SEQS,),
        in_specs=[src(2 * GLA_KEY_WIDTH), src(GLA_KEY_WIDTH), src(GLA_WIDTH), src(GLA_WIDTH),
                  pl.BlockSpec((1, GLA_WIDTH), lambda i: (0, 0)), sblk],
        out_specs=[blk(GLA_WIDTH), sblk],
        out_shape=[jax.ShapeDtypeStruct((rows, GLA_WIDTH), BF16),
                   jax.ShapeDtypeStruct((nseq, GLA_KEY_WIDTH, GLA_HEAD_V), F32)],
        scratch_shapes=[pltpu.VMEM((SAMPLE_ROWS, GLA_WIDTH), F32)],
        compiler_params=pltpu.CompilerParams(dimension_semantics=("arbitrary",)),
        name="gla_sample",
    )(qk, g, v, r, gn, state)


def _swa_sample_body(sink_ref, q_ref, kvn_ref, kc_ref, vc_ref, nrm_ref, o_ref, kco_ref, vco_ref,
                     lhs_scr, sc_scr, pc_scr, oc_scr):
    rows = SAMPLE_ROWS
    q = q_ref[...].astype(F32)
    lo = lax.broadcasted_iota(jnp.int32, (1, LANES), 1) < SWA_HEAD_DIM
    for hq in range(SWA_Q_HEADS):
        grp = hq // SWA_GROUP
        tile = q[:, (hq // 2) * LANES:(hq // 2 + 1) * LANES]
        src = tile if hq % 2 == grp else pltpu.roll(tile, SWA_HEAD_DIM, 1)
        lhs_scr[hq * rows:(hq + 1) * rows, :] = jnp.where(lo if grp == 0 else ~lo, src, 0.0)

    kvn = kvn_ref[...]
    k_new = kvn[:, 0:SWA_KV_WIDTH]
    v_new = kvn[:, SWA_KV_WIDTH:]
    s_new = _dot_tb(lhs_scr[...].astype(BF16), k_new.astype(BF16))

    row64 = lax.broadcasted_iota(jnp.int32, (SWA_Q_HEADS * SUBLANES, 1), 0)
    first_of_pair = row64 % SUBLANES < DEC_SEQ

    def gather_pair(ref, pair_idx):
        return jnp.concatenate(
            [ref[hq * rows + pair_idx * SUBLANES:hq * rows + (pair_idx + 1) * SUBLANES, :]
             for hq in range(SWA_Q_HEADS)], axis=0).astype(BF16)

    def scatter_pair(ref, pair_idx, val):
        for hq in range(SWA_Q_HEADS):
            ref[hq * rows + pair_idx * SUBLANES:hq * rows + (pair_idx + 1) * SUBLANES, :] = (
                val[hq * SUBLANES:(hq + 1) * SUBLANES])

    for pair_idx in range(SAMPLE_SEQS // 2):
        l64 = gather_pair(lhs_scr, pair_idx)
        sa = _dot(l64, kc_ref[2 * pair_idx].astype(BF16))
        sb = _dot(l64, kc_ref[2 * pair_idx + 1].astype(BF16))
        scatter_pair(sc_scr, pair_idx, jnp.where(first_of_pair, sa, sb))

    rr = lax.broadcasted_iota(jnp.int32, (rows, rows), 0)
    cc = lax.broadcasted_iota(jnp.int32, (rows, rows), 1)
    tok = rr % DEC_SEQ
    mask_cache = cc > tok
    mask_new = (cc // DEC_SEQ == rr // DEC_SEQ) & (cc % DEC_SEQ <= tok)
    p_new, dens = [], []
    for hq in range(SWA_Q_HEADS):
        sl = slice(hq * rows, (hq + 1) * rows)
        sink = sink_ref[hq]
        s_c = jnp.where(mask_cache, sc_scr[sl, :], -jnp.inf)
        s_n = jnp.where(mask_new, s_new[sl, :], -jnp.inf)
        m = jnp.maximum(jnp.maximum(jnp.max(s_c, axis=1, keepdims=True),
                                    jnp.max(s_n, axis=1, keepdims=True)), sink)
        p_c = jnp.exp(s_c - m)
        p_n = jnp.exp(s_n - m)
        dens.append(jnp.sum(p_c, axis=1, keepdims=True) + jnp.sum(p_n, axis=1, keepdims=True)
                    + jnp.exp(sink - m))
        pc_scr[sl, :] = p_c
        p_new.append(p_n.astype(BF16))
    o_new = _dot(jnp.concatenate(p_new, axis=0), v_new.astype(BF16))

    for pair_idx in range(SAMPLE_SEQS // 2):
        p64 = gather_pair(pc_scr, pair_idx)
        oa = _dot_tb(p64, vc_ref[2 * pair_idx].astype(BF16))
        ob = _dot_tb(p64, vc_ref[2 * pair_idx + 1].astype(BF16))
        scatter_pair(oc_scr, pair_idx, jnp.where(first_of_pair, oa, ob))

    tiles = []
    for i in range(SWA_Q_HEADS // 2):
        halves = []
        for hq in (2 * i, 2 * i + 1):
            sl = slice(hq * rows, (hq + 1) * rows)
            oh = (oc_scr[sl, :] + o_new[sl, :]) / dens[hq]
            halves.append(oh if hq % 2 == hq // SWA_GROUP else pltpu.roll(oh, SWA_HEAD_DIM, 1))
        tiles.append(jnp.where(lo, halves[0], halves[1]))
    o_ref[...] = _rms(jnp.concatenate(tiles, axis=1), nrm_ref[...]).astype(BF16)

    keep = lax.broadcasted_iota(jnp.int32, (1, WINDOW), 1) < WINDOW - DEC_SEQ
    k_new_t = k_new.T
    v_new_t = v_new.T
    for b in range(SAMPLE_SEQS):
        to_tail = (WINDOW - DEC_SEQ - DEC_SEQ * b) % rows
        kco_ref[b] = jnp.where(keep, pltpu.roll(kc_ref[b], WINDOW - DEC_SEQ, 1),
                               pltpu.roll(k_new_t, to_tail, 1))
        vco_ref[b] = jnp.where(keep, pltpu.roll(vc_ref[b], WINDOW - DEC_SEQ, 1),
                               pltpu.roll(v_new_t, to_tail, 1))


def _swa_sample(sinks, qs, kvn, kc, vc, nrm, row0):
    nseq = kc.shape[0]
    rows = nseq * DEC_SEQ
    assert nseq % SAMPLE_SEQS == 0 and row0 % SAMPLE_ROWS == 0
    assert kc.shape[1:] == (SWA_KV_WIDTH, WINDOW) and WINDOW == SAMPLE_ROWS
    blk0 = row0 // SAMPLE_ROWS
    cblk = pl.BlockSpec((SAMPLE_SEQS, SWA_KV_WIDTH, WINDOW), lambda i: (i, 0, 0))
    big = pltpu.VMEM((SWA_Q_HEADS * SAMPLE_ROWS, LANES), F32)
    return pl.pallas_call(
        _swa_sample_body,
        grid=(nseq // SAMPLE_SEQS,),
        in_specs=[pl.BlockSpec(memory_space=pltpu.SMEM),
                  pl.BlockSpec((SAMPLE_ROWS, SWA_WIDTH), lambda i: (i + blk0, 0)),
                  pl.BlockSpec((SAMPLE_ROWS, 2 * SWA_KV_WIDTH), lambda i: (i + blk0, 0)),
                  cblk, cblk,
                  pl.BlockSpec((1, SWA_WIDTH), lambda i: (0, 0))],
        out_specs=[pl.BlockSpec((SAMPLE_ROWS, SWA_WIDTH), lambda i: (i, 0)), cblk, cblk],
        out_shape=[jax.ShapeDtypeStruct((rows, SWA_WIDTH), BF16),
                   jax.ShapeDtypeStruct(kc.shape, F32), jax.ShapeDtypeStruct(vc.shape, F32)],
        scratch_shapes=[big, big, big, big],
        compiler_params=pltpu.CompilerParams(dimension_semantics=("arbitrary",)),
        name="swa_sample",
    )(sinks, qs, kvn, kc, vc, nrm)


def _prepare_weights(ffn1_norm, ffn1_w_gate, ffn1_w_up, ffn1_w_down, mix_norm, w_in, w_gate_up, b_gate,
                     gla_head_norm, swa_out_norm, swa_sinks, w_out,
                     ffn2_norm, ffn2_w_gate, ffn2_w_up, ffn2_w_down, final_norm, layer):
    assert w_in.shape[1:] == (D_MODEL, D_IN) and PROJ_Q_S == 2 * GLA_KEY_WIDTH + 2 * GLA_WIDTH
    wgu = jnp.concatenate([w_gate_up[layer],
                           jnp.zeros((LANES - GLA_GATE_RANK, GLA_KEY_WIDTH), w_gate_up.dtype)], axis=0)
    row = lambda a: a.reshape(1, -1).astype(F32)
    return dict(
        n1=row(ffn1_norm[layer]), wg1=ffn1_w_gate[layer], wu1=ffn1_w_up[layer], wd1=ffn1_w_down[layer],
        nm=row(mix_norm[layer]), win=w_in[layer], wgu=wgu.astype(BF16), bg=row(b_gate[layer]),
        gn=row(gla_head_norm[layer]), sn=row(swa_out_norm[layer]), sinks=swa_sinks[layer].astype(F32),
        wo=w_out[layer], n2=row(ffn2_norm[layer]), wg2=ffn2_w_gate[layer], wu2=ffn2_w_up[layer],
        wd2=ffn2_w_down[layer], nf=row(final_norm))


def kernel(x_prompt, x_sample, state_gla, cache_swa_k, cache_swa_v, ffn1_norm, ffn1_w_gate, ffn1_w_up,
           ffn1_w_down, mix_norm, w_in, w_gate_up, b_gate, gla_head_norm, swa_out_norm, swa_sinks, w_out,
           ffn2_norm, ffn2_w_gate, ffn2_w_up, ffn2_w_down, final_norm):
    depth = state_gla.shape[0]
    assert depth == 1 and x_prompt.shape[0] == 1 and x_sample.shape[1] == DEC_SEQ
    seq = x_prompt.shape[1]
    nseq = x_sample.shape[0]
    w = _prepare_weights(ffn1_norm, ffn1_w_gate, ffn1_w_up, ffn1_w_down, mix_norm, w_in, w_gate_up, b_gate,
                         gla_head_norm, swa_out_norm, swa_sinks, w_out,
                         ffn2_norm, ffn2_w_gate, ffn2_w_up, ffn2_w_down, final_norm, 0)

    (x1, qk, g, v, r, qs, kv), (wo, wg2, wu2, wd2) = _stage_a(
        x_prompt.reshape(seq, D_MODEL), x_sample.reshape(nseq * DEC_SEQ, D_MODEL), w)
    w = dict(w, wo=wo, wg2=wg2, wu2=wu2, wd2=wd2)

    og_s, state_s = _gla_sample(qk, g, v, r, w["gn"],
                                state_gla[0].reshape(nseq, GLA_KEY_WIDTH, GLA_HEAD_V), seq)
    cache_w = cache_swa_k.shape[2]
    to_feature_major = lambda c: jnp.transpose(c[0], (0, 2, 3, 1)).reshape(nseq, SWA_KV_WIDTH, cache_w)
    from_feature_major = lambda c: jnp.transpose(
        c.reshape(nseq, SWA_KV_HEADS, SWA_HEAD_DIM, cache_w), (0, 3, 1, 2))[None]
    os_s, k_cache_s, v_cache_s = _swa_sample(
        w["sinks"], qs, kv, to_feature_major(cache_swa_k), to_feature_major(cache_swa_v), w["sn"], seq)
    k_cache_s = from_feature_major(k_cache_s)
    v_cache_s = from_feature_major(v_cache_s)

    y_prompt, y_sample, state_p = _mix_out(qk, g, v, r, qs, kv, x1, og_s, os_s, w)
    y_prompt = y_prompt.reshape(1, seq, D_MODEL)
    y_sample = y_sample.reshape(nseq, DEC_SEQ, D_MODEL)
    cw = min(WINDOW, seq)
    k_cache_p = kv[seq - cw:seq, 0:SWA_KV_WIDTH].reshape(1, 1, cw, SWA_KV_HEADS, SWA_HEAD_DIM)
    v_cache_p = kv[seq - cw:seq, SWA_KV_WIDTH:].reshape(1, 1, cw, SWA_KV_HEADS, SWA_HEAD_DIM)
    state_p = state_p.reshape(1, 1, GLA_HEADS, GLA_HEAD_K, GLA_HEAD_V)
    state_s = state_s.reshape(1, nseq, GLA_HEADS, GLA_HEAD_K, GLA_HEAD_V)

    return (y_prompt, y_sample, state_p, k_cache_p, v_cache_p, state_s, k_cache_s, v_cache_s)
```

```python
import functools
import itertools

import jax
import jax.numpy as jnp
from jax import lax
from jax.experimental import pallas as pl
from jax.experimental.pallas import tpu as pltpu

F32 = jnp.float32
BF16 = jnp.bfloat16

D_MODEL = 1024
D_FF = 2816
GLA_HEADS = 4
GLA_HEAD_K = 64
GLA_HEAD_V = 128
GLA_KEY_WIDTH = GLA_HEADS * GLA_HEAD_K
GLA_WIDTH = GLA_HEADS * GLA_HEAD_V
GLA_GATE_RANK = 16
GLA_GATE_TAU = 16.0
SWA_HEAD_DIM = 64
SWA_Q_HEADS = 8
SWA_KV_HEADS = 2
SWA_GROUP = SWA_Q_HEADS // SWA_KV_HEADS
SWA_WIDTH = SWA_Q_HEADS * SWA_HEAD_DIM
SWA_KV_WIDTH = SWA_KV_HEADS * SWA_HEAD_DIM
WINDOW = 128
DEC_SEQ = 4
NORM_EPS = 1e-6
HEAD_SCALE = 0.125

LANES = 128
SUBLANES = 8
VMEM_LIMIT_BYTES = 56 * 1024 * 1024

ROW_TILE = 512
FF_CHUNK = 256
DOWN_CHUNK = 256
GLA_CHUNK = 128
SAMPLE_SEQS = 32
SAMPLE_ROWS = SAMPLE_SEQS * DEC_SEQ
MIXER_STAGES = 3 * (ROW_TILE // GLA_CHUNK) + 3 * (ROW_TILE // WINDOW)
SIDE_STAGES_AT_NORM = 2
DECAY_CLAMP = 60.0

PROJ_Q_G = 0
PROJ_K_G = 256
PROJ_V_G = 512
PROJ_R_G = 1024
PROJ_Q_S = 1536
PROJ_KV_S = 2048
PROJ_A = 2304
PROJ_WIDTH = 2432
D_IN = 2320


def _dot(a, b):
    return jnp.dot(a, b, preferred_element_type=F32)


def _dot_tb(a, b):
    return lax.dot_general(a, b, (((1,), (1,)), ((), ())), preferred_element_type=F32)


def _dot_ta(a, b):
    return lax.dot_general(a, b, (((0,), (0,)), ((), ())), preferred_element_type=F32)


def _rms(x, g):
    return x * lax.rsqrt(jnp.mean(x * x, axis=-1, keepdims=True) + NORM_EPS) * g


def _swiglu(h, wg_ref, wu_ref, wd_ref, act_ref, side_stages=iter(()), n_side_stages=0):
    n_dots = 2 * (D_FF // FF_CHUNK) + D_MODEL // DOWN_CHUNK
    done = [0, 0]

    def after_dot():
        done[0] += 1
        while done[1] * n_dots < done[0] * n_side_stages:
            next(side_stages, None)
            done[1] += 1

    for c0 in range(0, D_FF, FF_CHUNK):
        g = _dot(h, wg_ref[:, c0:c0 + FF_CHUNK])
        after_dot()
        u = _dot(h, wu_ref[:, c0:c0 + FF_CHUNK])
        after_dot()
        act_ref[:, c0:c0 + FF_CHUNK] = (g * jax.nn.sigmoid(g) * u).astype(BF16)
    outs = []
    for n0 in range(0, D_MODEL, DOWN_CHUNK):
        outs.append(_dot(act_ref[...], wd_ref[:, n0:n0 + DOWN_CHUNK]))
        after_dot()
    return jnp.concatenate(outs, axis=1)


def _head_norm_gate(o, r, gn):
    parts = []
    for h in range(GLA_HEADS):
        sl = slice(h * GLA_HEAD_V, (h + 1) * GLA_HEAD_V)
        parts.append(_rms(o[:, sl], gn[:, sl]))
    return (jnp.concatenate(parts, axis=1) * (r * jax.nn.sigmoid(r))).astype(BF16)


def _out_ffn(x1_ref, og, osw, wo_ref, n2_ref, wg_ref, wu_ref, wd_ref, nf_ref, y_ref, act_ref, side_stages=None):
    n_inside = 0 if side_stages is None else MIXER_STAGES - 2 * SIDE_STAGES_AT_NORM
    side_stages = iter(()) if side_stages is None else side_stages
    mixed = jnp.concatenate([og, osw], axis=1)
    x2 = x1_ref[...] + _dot(mixed, wo_ref[...])
    for _ in range(SIDE_STAGES_AT_NORM):
        next(side_stages, None)
    h = _rms(x2, n2_ref[...]).astype(BF16)
    x3 = x2 + 0.5 * _swiglu(h, wg_ref, wu_ref, wd_ref, act_ref, side_stages, n_inside)
    for _ in side_stages:
        pass
    y_ref[...] = _rms(x3, nf_ref[...])


_HBM = pl.BlockSpec(memory_space=pl.ANY)
WIDE_CHUNK_ROWS = 128
NARROW_CHUNK_ROWS = 256
LOAD_SLOTS = 3


def _load_weights_bf16(loads, staging, sems):
    jobs = []
    for src, dst, segments in loads:
        cols = src.shape[1]
        assert cols in (D_FF, D_MODEL) and dst.shape[1] == cols
        kind, step = (0, WIDE_CHUNK_ROWS) if cols == D_FF else (1, NARROW_CHUNK_ROWS)
        for s0, d0, n in segments or [(0, 0, src.shape[0])]:
            jobs += [(src, dst, s0 + r, d0 + r, min(step, n - r), kind) for r in range(0, n, step)]

    used = [0, 0]
    copies = []
    for src, _, s0, _, nr, kind in jobs:
        slot = used[kind] % LOAD_SLOTS
        used[kind] += 1
        copies.append((pltpu.make_async_copy(src.at[pl.ds(s0, nr), :], staging[kind].at[slot, pl.ds(0, nr), :],
                                             sems.at[kind, slot]), slot))
    for copy, _ in copies[:LOAD_SLOTS - 1]:
        copy.start()
    for j, (_, dst, _, d0, nr, kind) in enumerate(jobs):
        ahead = j + LOAD_SLOTS - 1
        if ahead < len(jobs):
            copies[ahead][0].start()
        copy, slot = copies[j]
        copy.wait()
        dst[d0:d0 + nr, :] = staging[kind][slot, 0:nr, :].astype(BF16)


def _staging_scratch():
    return [pltpu.VMEM((LOAD_SLOTS, WIDE_CHUNK_ROWS, D_FF), F32),
            pltpu.VMEM((LOAD_SLOTS, NARROW_CHUNK_ROWS, D_MODEL), F32),
            pltpu.SemaphoreType.DMA((2, LOAD_SLOTS))]


def _chunk_plan(matrices):
    plan, first = [], 0
    for m, mat in enumerate(matrices):
        rows, cols = mat.shape
        assert cols in (D_FF, D_MODEL)
        kind, nr = (0, WIDE_CHUNK_ROWS) if cols == D_FF else (1, NARROW_CHUNK_ROWS)
        assert rows % nr == 0
        plan.append((m, first, rows // nr, nr, kind))
        first += rows // nr
    return plan, first


def _background_cast(step, srcs, dsts, staging, out_staging, sem_in, sem_out):
    plan, _ = _chunk_plan(srcs)
    slot = step % 2

    def rows_of(m, chunk):
        _, first, _, nr, _ = plan[m]
        return pl.ds(pl.multiple_of((chunk - first) * nr, nr), nr)

    def copy_in(m, chunk, slot_):
        kind = plan[m][4]
        return pltpu.make_async_copy(srcs[m].at[rows_of(m, chunk), :], staging[kind].at[slot_],
                                     sem_in.at[kind, slot_])

    def copy_out(m, chunk, slot_):
        kind = plan[m][4]
        return pltpu.make_async_copy(out_staging[kind].at[slot_], dsts[m].at[rows_of(m, chunk), :],
                                     sem_out.at[kind, slot_])

    def for_chunk(chunk, fn):
        for m, first, n, _, _ in plan:
            pl.when(jnp.logical_and(chunk >= first, chunk < first + n))(functools.partial(fn, m))

    def before():
        @pl.when(step == 0)
        def _():
            copy_in(0, 0, 0).start()
        for_chunk(step, lambda m: copy_in(m, step, slot).wait())
        for_chunk(step - 2, lambda m: copy_out(m, step - 2, slot).wait())
        for_chunk(step + 1, lambda m: copy_in(m, step + 1, 1 - slot).start())

    def cast():
        for kind in range(2):
            out_staging[kind][slot] = staging[kind][slot].astype(BF16)

    def after():
        for_chunk(step, lambda m: copy_out(m, step, slot).start())

    return before, cast, after


def _stage_a_body(xp_ref, xs_ref, n1_ref, wg_hbm, wu_hbm, wd_hbm, nm_ref, win_hbm, wgu_ref, bg_ref,
                  wo_hbm, wg2_hbm, wu2_hbm, wd2_hbm,
                  x1_ref, qk_ref, g_ref, v_ref, r_ref, qs_ref, kv_ref, wo_out, wg2_out, wu2_out, wd2_out,
                  act_ref, wg_ref, wu_ref, wd_ref, win_ref, wide_stage, narrow_stage, load_sems,
                  wide_out, narrow_out, sem_in, sem_out):
    i = pl.program_id(0)

    @pl.when(i == 0)
    def _():
        a0, a1 = PROJ_Q_S, PROJ_Q_S + GLA_GATE_RANK
        win_rows = [(0, 0, a0), (a1, a0, D_IN - a1), (a0, PROJ_A, GLA_GATE_RANK)]
        win_ref[PROJ_A + GLA_GATE_RANK:, :] = jnp.zeros((LANES - GLA_GATE_RANK, D_MODEL), BF16)
        _load_weights_bf16([(wg_hbm, wg_ref, None), (wu_hbm, wu_ref, None), (wd_hbm, wd_ref, None),
                            (win_hbm, win_ref, win_rows)], (wide_stage, narrow_stage), load_sems)

    bg_before, bg_cast, bg_after = _background_cast(
        i, (wo_hbm, wg2_hbm, wu2_hbm, wd2_hbm), (wo_out, wg2_out, wu2_out, wd2_out),
        (wide_stage, narrow_stage), (wide_out, narrow_out), sem_in, sem_out)
    bg_before()

    x = jnp.where(i < pl.num_programs(0) - 1, xp_ref[...], xs_ref[...])
    h = _rms(x, n1_ref[...]).astype(BF16)
    bg_cast()
    x1 = x + 0.5 * _swiglu(h, wg_ref, wu_ref, wd_ref, act_ref)
    x1_ref[...] = x1
    h2 = _rms(x1, nm_ref[...]).astype(BF16)
    proj = _dot_tb(h2, win_ref[...])
    qk_ref[:, 0:GLA_KEY_WIDTH] = proj[:, PROJ_Q_G:PROJ_K_G] * HEAD_SCALE
    qk_ref[:, GLA_KEY_WIDTH:] = proj[:, PROJ_K_G:PROJ_V_G]
    v_ref[...] = proj[:, PROJ_V_G:PROJ_R_G].astype(BF16)
    r_ref[...] = proj[:, PROJ_R_G:PROJ_Q_S]
    qs_ref[...] = (proj[:, PROJ_Q_S:PROJ_KV_S] * HEAD_SCALE).astype(BF16)
    kv_ref[...] = proj[:, PROJ_KV_S:PROJ_A]
    a = proj[:, PROJ_A:PROJ_WIDTH].astype(BF16)
    z = _dot(a, wgu_ref[...]) + bg_ref[...]
    g_ref[...] = jax.nn.log_sigmoid(z) * (1.0 / GLA_GATE_TAU)
    bg_after()


def _resident(shape):
    return pl.BlockSpec(shape, lambda i: (0,) * len(shape), pipeline_mode=pl.Buffered(1))


def _rows(tm, n):
    return pl.BlockSpec((tm, n), lambda i: (i, 0))


def _stage_a(xp, xs, w):
    t = xp.shape[0]
    assert t % ROW_TILE == 0 and xs.shape[0] == ROW_TILE
    nt = t // ROW_TILE
    out_widths = ((D_MODEL, F32), (2 * GLA_KEY_WIDTH, F32), (GLA_KEY_WIDTH, F32), (GLA_WIDTH, BF16),
                  (GLA_WIDTH, F32), (SWA_WIDTH, BF16), (2 * SWA_KV_WIDTH, F32))
    to_cast = (w["wo"], w["wg2"], w["wu2"], w["wd2"])
    assert _chunk_plan(to_cast)[1] + 2 <= nt + 1
    outs = pl.pallas_call(
        _stage_a_body,
        grid=(nt + 1,),
        in_specs=[pl.BlockSpec((ROW_TILE, D_MODEL), lambda i: (jnp.minimum(i, nt - 1), 0)),
                  _resident((ROW_TILE, D_MODEL)), _resident((1, D_MODEL)),
                  _HBM, _HBM, _HBM,
                  _resident((1, D_MODEL)), _HBM,
                  _resident((LANES, GLA_KEY_WIDTH)), _resident((1, GLA_KEY_WIDTH)),
                  _HBM, _HBM, _HBM, _HBM],
        out_specs=[_rows(ROW_TILE, n) for n, _ in out_widths] + [_HBM] * len(to_cast),
        out_shape=([jax.ShapeDtypeStruct((t + ROW_TILE, n), dt) for n, dt in out_widths]
                   + [jax.ShapeDtypeStruct(m.shape, BF16) for m in to_cast]),
        scratch_shapes=([pltpu.VMEM((ROW_TILE, D_FF), BF16), pltpu.VMEM((D_MODEL, D_FF), BF16),
                         pltpu.VMEM((D_MODEL, D_FF), BF16), pltpu.VMEM((D_FF, D_MODEL), BF16),
                         pltpu.VMEM((PROJ_WIDTH, D_MODEL), BF16)]
                        + _staging_scratch()
                        + [pltpu.VMEM((2, WIDE_CHUNK_ROWS, D_FF), BF16),
                           pltpu.VMEM((2, NARROW_CHUNK_ROWS, D_MODEL), BF16),
                           pltpu.SemaphoreType.DMA((2, 2)), pltpu.SemaphoreType.DMA((2, 2))]),
        compiler_params=pltpu.CompilerParams(dimension_semantics=("arbitrary",),
                                             vmem_limit_bytes=VMEM_LIMIT_BYTES),
        name="stage_a_ffn1_proj",
    )(xp, xs, w["n1"], w["wg1"], w["wu1"], w["wd1"], w["nm"], w["win"], w["wgu"], w["bg"], *to_cast)
    return outs[:len(out_widths)], outs[len(out_widths):]


def _split_bf16(x):
    hi = x.astype(BF16)
    lo = (x - hi.astype(F32)).astype(BF16)
    return hi, lo


def _alternate(a, b):
    pending = [iter(a), iter(b)]
    while pending:
        for it in list(pending):
            try:
                next(it)
            except StopIteration:
                pending.remove(it)
                continue
            yield


def _gla_stages(qk_ref, g_ref, v_ref, r_ref, gn_ref, s_ref, og_scr, cum_scr, inter_scr, flag_ref):
    c_len = GLA_CHUNK
    n_chunks = ROW_TILE // c_len
    causal = (lax.broadcasted_iota(jnp.int32, (c_len, c_len), 0)
              >= lax.broadcasted_iota(jnp.int32, (c_len, c_len), 1))
    ltri = jnp.where(causal, 1.0, 0.0).astype(BF16)
    causal_cat = (lax.broadcasted_iota(jnp.int32, (c_len, GLA_HEADS * c_len), 0)
                  >= lax.broadcasted_iota(jnp.int32, (c_len, GLA_HEADS * c_len), 1) % c_len)
    lane_head = lax.broadcasted_iota(jnp.int32, (1, GLA_KEY_WIDTH), 1) // GLA_HEAD_K
    row_head = lax.broadcasted_iota(jnp.int32, (GLA_KEY_WIDTH, 1), 0) // GLA_HEAD_K
    eye = (lax.broadcasted_iota(jnp.int32, (GLA_KEY_WIDTH, GLA_KEY_WIDTH), 0)
           == lax.broadcasted_iota(jnp.int32, (GLA_KEY_WIDTH, GLA_KEY_WIDTH), 1))
    zero_b = jnp.zeros((), BF16)
    zero_v = jnp.zeros((c_len, GLA_HEAD_V), BF16)
    worst = []

    def chunk(c):
        rows = slice(c * c_len, (c + 1) * c_len)
        g_hi, g_lo = _split_bf16(g_ref[rows, :])
        cum = _dot(ltri, g_hi) + _dot(ltri, g_lo)
        yield
        cum_scr[rows, :] = cum
        last = cum[c_len - 1:c_len, :]
        q = qk_ref[rows, 0:GLA_KEY_WIDTH]
        k = qk_ref[rows, GLA_KEY_WIDTH:]
        vb = v_ref[rows, :]
        qe = (q * jnp.exp(cum)).astype(BF16)
        ke = (k * jnp.exp(jnp.minimum(-cum, DECAY_CLAMP))).astype(BF16)
        kl = (k * jnp.exp(last - cum)).astype(BF16)
        state = s_ref[...]
        sb = state.astype(BF16)
        s_bd = jnp.concatenate([jnp.where(row_head == h, sb, zero_b) for h in range(GLA_HEADS)], axis=1)
        o_inter = _dot(qe, s_bd)
        inter_scr[rows, :] = o_inter
        ke_bd = jnp.concatenate([jnp.where(lane_head == h, ke, zero_b) for h in range(GLA_HEADS)], axis=0)
        attn = _dot_tb(qe, ke_bd)
        upds = []
        for p in range(GLA_HEADS // 2):
            u = _dot_ta(kl[:, p * LANES:(p + 1) * LANES], vb[:, 2 * p * GLA_HEAD_V:(2 * p + 2) * GLA_HEAD_V])
            upds.append(u[0:GLA_HEAD_K, 0:GLA_HEAD_V])
            upds.append(u[GLA_HEAD_K:, GLA_HEAD_V:])
        yield
        attn = jnp.where(causal_cat, attn, 0.0).astype(BF16)
        o_pairs = []
        for p in range(GLA_HEADS // 2):
            v_a = vb[:, (2 * p) * GLA_HEAD_V:(2 * p + 1) * GLA_HEAD_V]
            v_b = vb[:, (2 * p + 1) * GLA_HEAD_V:(2 * p + 2) * GLA_HEAD_V]
            v_bd = jnp.concatenate([jnp.concatenate([v_a, zero_v], axis=1),
                                    jnp.concatenate([zero_v, v_b], axis=1)], axis=0)
            o_pairs.append(_dot(attn[:, 2 * p * c_len:(2 * p + 2) * c_len], v_bd))
        og_scr[rows, :] = _head_norm_gate(o_inter + jnp.concatenate(o_pairs, axis=1), r_ref[rows, :],
                                          gn_ref[...])
        last_col = jnp.sum(jnp.where(eye, last, 0.0), axis=1, keepdims=True)
        s_ref[...] = state * jnp.exp(last_col) + jnp.concatenate(upds, axis=0)
        worst.append(jnp.min(last, axis=1, keepdims=True))
        if c == n_chunks - 1:
            tile_min = functools.reduce(jnp.minimum, worst)
            flag_ref[0] = jnp.where(tile_min[0, 0] < -DECAY_CLAMP, 1, 0)
        yield

    return itertools.chain.from_iterable(chunk(c) for c in range(n_chunks))


def _gla_tile_exact(qk_ref, v_ref, r_ref, gn_ref, og_scr, o_scr, cum_scr, inter_scr):
    c_len = GLA_CHUNK
    ind = jnp.where(lax.broadcasted_iota(jnp.int32, (GLA_KEY_WIDTH, LANES), 0) // GLA_HEAD_K
                    == lax.broadcasted_iota(jnp.int32, (GLA_KEY_WIDTH, LANES), 1), 1.0, 0.0).astype(BF16)
    j_idx = lax.broadcasted_iota(jnp.int32, (c_len, 1), 0)
    for c in range(ROW_TILE // c_len):
        r0 = c * c_len
        rows = slice(r0, r0 + c_len)

        def one_row(i, carry, r0=r0, rows=rows):
            ci = cum_scr[pl.ds(r0 + i, 1), :]
            qi = qk_ref[pl.ds(r0 + i, 1), 0:GLA_KEY_WIDTH]
            kk = qk_ref[rows, GLA_KEY_WIDTH:]
            dec = jnp.exp(jnp.minimum(ci - cum_scr[rows, :], 0.0))
            a_cols = _dot(((qi * kk) * dec).astype(BF16), ind)
            outs = []
            for h in range(GLA_HEADS):
                w_col = jnp.where(j_idx <= i, a_cols[:, h:h + 1], 0.0)
                v_h = v_ref[rows, h * GLA_HEAD_V:(h + 1) * GLA_HEAD_V].astype(F32)
                outs.append(jnp.sum(w_col * v_h, axis=0, keepdims=True))
            o_scr[pl.ds(r0 + i, 1), :] = inter_scr[pl.ds(r0 + i, 1), :] + jnp.concatenate(outs, axis=1)
            return carry

        lax.fori_loop(0, c_len, one_row, 0)
    og_scr[...] = _head_norm_gate(o_scr[...], r_ref[...], gn_ref[...])


def _dup_halves(x):
    lo = lax.broadcasted_iota(jnp.int32, (1, LANES), 1) < SWA_HEAD_DIM
    sw = pltpu.roll(x, SWA_HEAD_DIM, 1)
    return jnp.where(lo, x, sw).astype(BF16), jnp.where(lo, sw, x).astype(BF16)


def _swa_stages(sink_ref, qs_ref, kv_ref, kvprev_scr, sn_ref, os_scr, no_prev_bias):
    w = WINDOW
    n_blocks = ROW_TILE // w
    lo = lax.broadcasted_iota(jnp.int32, (1, LANES), 1) < SWA_HEAD_DIM
    tri = (lax.broadcasted_iota(jnp.int32, (w, w), 1) <= lax.broadcasted_iota(jnp.int32, (w, w), 0))
    zero_b = jnp.zeros((), BF16)

    def block(b):
        rows = slice(b * w, (b + 1) * w)
        kv_prev = kvprev_scr[...] if b == 0 else kv_ref[(b - 1) * w:b * w, :]
        kv = jnp.concatenate([kv_prev, kv_ref[rows, :]], axis=0)
        kk = _dup_halves(kv[:, 0:SWA_KV_WIDTH])
        vv = _dup_halves(kv[:, SWA_KV_WIDTH:])
        q = qs_ref[rows, :]
        scores = []
        for grp in range(SWA_KV_HEADS):
            stacked = []
            for j in range(SWA_GROUP):
                hq = grp * SWA_GROUP + j
                tile = q[:, (hq // 2) * LANES:(hq // 2 + 1) * LANES]
                stacked.append(jnp.where(lo if hq % 2 == 0 else ~lo, tile, zero_b))
            scores.append(_dot_tb(jnp.concatenate(stacked, axis=0), kk[grp]))
        yield
        outs, dens = [], []
        for grp in range(SWA_KV_HEADS):
            s = scores[grp]
            probs = []
            for j in range(SWA_GROUP):
                sink = sink_ref[grp * SWA_GROUP + j]
                s_prev = s[j * w:(j + 1) * w, 0:w]
                if b == 0:
                    s_prev = s_prev + no_prev_bias
                sf = jnp.where(tri, s[j * w:(j + 1) * w, w:], s_prev)
                m = jnp.maximum(jnp.max(sf, axis=1, keepdims=True), sink)
                p = jnp.exp(sf - m)
                dens.append(jnp.sum(p, axis=1, keepdims=True) + jnp.exp(sink - m))
                pb = p.astype(BF16)
                probs.append(jnp.concatenate([jnp.where(tri, zero_b, pb), jnp.where(tri, pb, zero_b)], axis=1))
            outs.append(_dot(jnp.concatenate(probs, axis=0), vv[grp]))
        yield
        tiles = []
        for grp in range(SWA_KV_HEADS):
            res = [outs[grp][j * w:(j + 1) * w] / dens[grp * SWA_GROUP + j] for j in range(SWA_GROUP)]
            tiles.append(jnp.where(lo, res[0], res[1]))
            tiles.append(jnp.where(lo, res[2], res[3]))
        os_scr[rows, :] = _rms(jnp.concatenate(tiles, axis=1), sn_ref[...]).astype(BF16)
        if b == n_blocks - 1:
            kvprev_scr[...] = kv_ref[rows, :]
        yield

    return itertools.chain.from_iterable(block(b) for b in range(n_blocks))


def _mix_out_body(sink_ref, qk_ref, g_ref, v_ref, r_ref, qs_ref, kv_ref, x1_ref, ogs_ref, oss_ref,
                  gn_ref, sn_ref, wo_ref, n2_ref, wg_ref, wu_ref, wd_ref, nf_ref, yp_ref, ys_ref, s_ref,
                  og_scr, os_scr, kvprev_scr, o_scr, cum_scr, inter_scr, act_scr, flag_ref):
    i = pl.program_id(0)
    n_prompt = pl.num_programs(0) - 2

    def stage_d(og, osw, y_ref, side_stages=None):
        _out_ffn(x1_ref, og, osw, wo_ref, n2_ref, wg_ref, wu_ref, wd_ref, nf_ref, y_ref, act_scr,
                 side_stages)

    def mixer_stages():
        gla = _gla_stages(qk_ref, g_ref, v_ref, r_ref, gn_ref, s_ref, og_scr, cum_scr, inter_scr, flag_ref)
        swa = _swa_stages(sink_ref, qs_ref, kv_ref, kvprev_scr, sn_ref, os_scr,
                          jnp.where(i > 0, 0.0, -jnp.inf))
        return _alternate(gla, swa)

    @pl.when(i == 0)
    def _():
        s_ref[...] = jnp.zeros_like(s_ref)
        kvprev_scr[...] = jnp.zeros_like(kvprev_scr)
        for _ in mixer_stages():
            pass

    @pl.when(jnp.logical_and(i > 0, i < n_prompt))
    def _():
        stage_d(og_scr[...], os_scr[...], yp_ref, mixer_stages())

    @pl.when(i == n_prompt)
    def _():
        stage_d(og_scr[...], os_scr[...], yp_ref)
        flag_ref[0] = 0

    @pl.when(i == n_prompt + 1)
    def _():
        stage_d(ogs_ref[...], oss_ref[...], ys_ref)

    @pl.when(flag_ref[0] != 0)
    def _():
        _gla_tile_exact(qk_ref, v_ref, r_ref, gn_ref, og_scr, o_scr, cum_scr, inter_scr)


def _mix_out(qk, g, v, r, qs, kv, x1, og_s, os_s, w):
    t = qk.shape[0] - ROW_TILE
    assert t % ROW_TILE == 0 and og_s.shape[0] == ROW_TILE
    nt = t // ROW_TILE
    cur = lambda n: pl.BlockSpec((ROW_TILE, n), lambda i: (jnp.minimum(i, nt - 1), 0))
    return pl.pallas_call(
        _mix_out_body,
        grid=(nt + 2,),
        in_specs=[pl.BlockSpec(memory_space=pltpu.SMEM),
                  cur(2 * GLA_KEY_WIDTH), cur(GLA_KEY_WIDTH), cur(GLA_WIDTH), cur(GLA_WIDTH),
                  cur(SWA_WIDTH), cur(2 * SWA_KV_WIDTH),
                  pl.BlockSpec((ROW_TILE, D_MODEL), lambda i: (jnp.maximum(i - 1, 0), 0)),
                  _resident((ROW_TILE, GLA_WIDTH)), _resident((ROW_TILE, SWA_WIDTH)),
                  _resident((1, GLA_WIDTH)), _resident((1, SWA_WIDTH)),
                  _resident((D_MODEL, D_MODEL)), _resident((1, D_MODEL)),
                  _resident((D_MODEL, D_FF)), _resident((D_MODEL, D_FF)), _resident((D_FF, D_MODEL)),
                  _resident((1, D_MODEL))],
        out_specs=[pl.BlockSpec((ROW_TILE, D_MODEL), lambda i: (jnp.clip(i - 1, 0, nt - 1), 0)),
                   pl.BlockSpec((ROW_TILE, D_MODEL), lambda i: (0, 0)),
                   pl.BlockSpec((GLA_KEY_WIDTH, GLA_HEAD_V), lambda i: (0, 0))],
        out_shape=[jax.ShapeDtypeStruct((t, D_MODEL), F32),
                   jax.ShapeDtypeStruct((ROW_TILE, D_MODEL), F32),
                   jax.ShapeDtypeStruct((GLA_KEY_WIDTH, GLA_HEAD_V), F32)],
        scratch_shapes=[pltpu.VMEM((ROW_TILE, GLA_WIDTH), BF16), pltpu.VMEM((ROW_TILE, SWA_WIDTH), BF16),
                        pltpu.VMEM((WINDOW, 2 * SWA_KV_WIDTH), F32),
                        pltpu.VMEM((ROW_TILE, GLA_WIDTH), F32), pltpu.VMEM((ROW_TILE, GLA_KEY_WIDTH), F32),
                        pltpu.VMEM((ROW_TILE, GLA_WIDTH), F32), pltpu.VMEM((ROW_TILE, D_FF), BF16),
                        pltpu.SMEM((1,), jnp.int32)],
        compiler_params=pltpu.CompilerParams(dimension_semantics=("arbitrary",),
                                             vmem_limit_bytes=VMEM_LIMIT_BYTES),
        name="prompt_mixers_stage_d",
    )(w["sinks"], qk, g, v, r, qs, kv, x1, og_s, os_s, w["gn"], w["sn"], w["wo"], w["n2"], w["wg2"],
      w["wu2"], w["wd2"], w["nf"])


def _gla_sample_body(qk_ref, g_ref, v_ref, r_ref, gn_ref, s_ref, og_ref, so_ref, o_scr):
    rows = SAMPLE_ROWS
    g = g_ref[...]
    q = qk_ref[:, 0:GLA_KEY_WIDTH]
    k = qk_ref[:, GLA_KEY_WIDTH:]
    vb = v_ref[...]
    vf = vb.astype(F32)
    tok = lax.broadcasted_iota(jnp.int32, (rows, 1), 0) % DEC_SEQ

    cum = g
    for d in range(1, DEC_SEQ):
        cum = cum + jnp.where(tok >= d, pltpu.roll(g, d, 0), 0.0)
    tot = jnp.where(tok == DEC_SEQ - 1, cum, 0.0)
    for d in range(1, DEC_SEQ):
        tot = tot + jnp.where(tok == DEC_SEQ - 1 - d, pltpu.roll(cum, rows - d, 0), 0.0)

    qe = q * jnp.exp(cum)
    kl = k * jnp.exp(tot - cum)
    decay_t = jnp.exp(tot).T

    ind = jnp.where(lax.broadcasted_iota(jnp.int32, (GLA_KEY_WIDTH, LANES), 0) // GLA_HEAD_K
                    == lax.broadcasted_iota(jnp.int32, (GLA_KEY_WIDTH, LANES), 1), 1.0, 0.0).astype(BF16)
    expand = jnp.where(lax.broadcasted_iota(jnp.int32, (LANES, GLA_WIDTH), 0)
                       == lax.broadcasted_iota(jnp.int32, (LANES, GLA_WIDTH), 1) // GLA_HEAD_V,
                       1.0, 0.0).astype(BF16)

    o_intra = jnp.zeros((rows, GLA_WIDTH), F32)
    for d in range(DEC_SEQ):
        k_d = k if d == 0 else pltpu.roll(k, d, 0)
        c_d = cum if d == 0 else pltpu.roll(cum, d, 0)
        v_d = vf if d == 0 else pltpu.roll(vf, d, 0)
        pair = jnp.where(tok >= d, q * k_d * jnp.exp(jnp.minimum(cum - c_d, 0.0)), 0.0)
        a = _dot(pair.astype(BF16), ind)
        o_intra = o_intra + _dot(a.astype(BF16), expand) * v_d

    lane_head = lax.broadcasted_iota(jnp.int32, (1, GLA_KEY_WIDTH), 1) // GLA_HEAD_K
    row8 = lax.broadcasted_iota(jnp.int32, (SUBLANES, 1), 0)
    row32 = lax.broadcasted_iota(jnp.int32, (GLA_HEADS * SUBLANES, 1), 0)
    for pair_idx in range(SAMPLE_SEQS // 2):
        r8 = slice(pair_idx * SUBLANES, (pair_idx + 1) * SUBLANES)
        q8 = qe[r8, :]
        lhs = jnp.concatenate([jnp.where(lane_head == h, q8, 0.0) for h in range(GLA_HEADS)],
                              axis=0).astype(BF16)
        kl8 = kl[r8, :]
        v8 = vb[r8, :]
        res = []
        for s in range(2):
            b = 2 * pair_idx + s
            state = s_ref[b]
            res.append(_dot(lhs, state.astype(BF16)))
            kl_b = jnp.where(row8 // DEC_SEQ == s, kl8, 0.0).astype(BF16)
            upd = _dot_ta(kl_b, v8)
            upd = jnp.concatenate(
                [upd[h * GLA_HEAD_K:(h + 1) * GLA_HEAD_K, h * GLA_HEAD_V:(h + 1) * GLA_HEAD_V]
                 for h in range(GLA_HEADS)], axis=0)
            so_ref[b] = state * decay_t[:, DEC_SEQ * b:DEC_SEQ * b + 1] + upd
        sel = jnp.where(row32 % SUBLANES < DEC_SEQ, res[0], res[1])
        o_scr[r8, :] = jnp.concatenate([sel[h * SUBLANES:(h + 1) * SUBLANES] for h in range(GLA_HEADS)], axis=1)

    og_ref[...] = _head_norm_gate(o_scr[...] + o_intra, r_ref[...], gn_ref[...])


def _gla_sample(qk, g, v, r, gn, state, row0):
    nseq = state.shape[0]
    rows = nseq * DEC_SEQ
    assert nseq % SAMPLE_SEQS == 0 and row0 % SAMPLE_ROWS == 0
    blk0 = row0 // SAMPLE_ROWS
    src = lambda n: pl.BlockSpec((SAMPLE_ROWS, n), lambda i: (i + blk0, 0))
    blk = lambda n: pl.BlockSpec((SAMPLE_ROWS, n), lambda i: (i, 0))
    sblk = pl.BlockSpec((SAMPLE_SEQS, GLA_KEY_WIDTH, GLA_HEAD_V), lambda i: (i, 0, 0))
    return pl.pallas_call(
        _gla_sample_body,
        grid=(nseq // SAMPLE_SEQS,),
        in_specs=[src(2 * GLA_KEY_WIDTH), src(GLA_KEY_WIDTH), src(GLA_WIDTH), src(GLA_WIDTH),
                  pl.BlockSpec((1, GLA_WIDTH), lambda i: (0, 0)), sblk],
        out_specs=[blk(GLA_WIDTH), sblk],
        out_shape=[jax.ShapeDtypeStruct((rows, GLA_WIDTH), BF16),
                   jax.ShapeDtypeStruct((nseq, GLA_KEY_WIDTH, GLA_HEAD_V), F32)],
        scratch_shapes=[pltpu.VMEM((SAMPLE_ROWS, GLA_WIDTH), F32)],
        compiler_params=pltpu.CompilerParams(dimension_semantics=("arbitrary",)),
        name="gla_sample",
    )(qk, g, v, r, gn, state)


def _swa_sample_body(sink_ref, q_ref, kvn_ref, kc_ref, vc_ref, nrm_ref, o_ref, kco_ref, vco_ref,
                     lhs_scr, sc_scr, pc_scr, oc_scr):
    rows = SAMPLE_ROWS
    q = q_ref[...].astype(F32)
    lo = lax.broadcasted_iota(jnp.int32, (1, LANES), 1) < SWA_HEAD_DIM
    for hq in range(SWA_Q_HEADS):
        grp = hq // SWA_GROUP
        tile = q[:, (hq // 2) * LANES:(hq // 2 + 1) * LANES]
        src = tile if hq % 2 == grp else pltpu.roll(tile, SWA_HEAD_DIM, 1)
        lhs_scr[hq * rows:(hq + 1) * rows, :] = jnp.where(lo if grp == 0 else ~lo, src, 0.0)

    kvn = kvn_ref[...]
    k_new = kvn[:, 0:SWA_KV_WIDTH]
    v_new = kvn[:, SWA_KV_WIDTH:]
    s_new = _dot_tb(lhs_scr[...].astype(BF16), k_new.astype(BF16))

    row64 = lax.broadcasted_iota(jnp.int32, (SWA_Q_HEADS * SUBLANES, 1), 0)
    first_of_pair = row64 % SUBLANES < DEC_SEQ

    def gather_pair(ref, pair_idx):
        return jnp.concatenate(
            [ref[hq * rows + pair_idx * SUBLANES:hq * rows + (pair_idx + 1) * SUBLANES, :]
             for hq in range(SWA_Q_HEADS)], axis=0).astype(BF16)

    def scatter_pair(ref, pair_idx, val):
        for hq in range(SWA_Q_HEADS):
            ref[hq * rows + pair_idx * SUBLANES:hq * rows + (pair_idx + 1) * SUBLANES, :] = (
                val[hq * SUBLANES:(hq + 1) * SUBLANES])

    for pair_idx in range(SAMPLE_SEQS // 2):
        l64 = gather_pair(lhs_scr, pair_idx)
        sa = _dot(l64, kc_ref[2 * pair_idx].astype(BF16))
        sb = _dot(l64, kc_ref[2 * pair_idx + 1].astype(BF16))
        scatter_pair(sc_scr, pair_idx, jnp.where(first_of_pair, sa, sb))

    rr = lax.broadcasted_iota(jnp.int32, (rows, rows), 0)
    cc = lax.broadcasted_iota(jnp.int32, (rows, rows), 1)
    tok = rr % DEC_SEQ
    mask_cache = cc > tok
    mask_new = (cc // DEC_SEQ == rr // DEC_SEQ) & (cc % DEC_SEQ <= tok)
    p_new, dens = [], []
    for hq in range(SWA_Q_HEADS):
        sl = slice(hq * rows, (hq + 1) * rows)
        sink = sink_ref[hq]
        s_c = jnp.where(mask_cache, sc_scr[sl, :], -jnp.inf)
        s_n = jnp.where(mask_new, s_new[sl, :], -jnp.inf)
        m = jnp.maximum(jnp.maximum(jnp.max(s_c, axis=1, keepdims=True),
                                    jnp.max(s_n, axis=1, keepdims=True)), sink)
        p_c = jnp.exp(s_c - m)
        p_n = jnp.exp(s_n - m)
        dens.append(jnp.sum(p_c, axis=1, keepdims=True) + jnp.sum(p_n, axis=1, keepdims=True)
                    + jnp.exp(sink - m))
        pc_scr[sl, :] = p_c
        p_new.append(p_n.astype(BF16))
    o_new = _dot(jnp.concatenate(p_new, axis=0), v_new.astype(BF16))

    for pair_idx in range(SAMPLE_SEQS // 2):
        p64 = gather_pair(pc_scr, pair_idx)
        oa = _dot_tb(p64, vc_ref[2 * pair_idx].astype(BF16))
        ob = _dot_tb(p64, vc_ref[2 * pair_idx + 1].astype(BF16))
        scatter_pair(oc_scr, pair_idx, jnp.where(first_of_pair, oa, ob))

    tiles = []
    for i in range(SWA_Q_HEADS // 2):
        halves = []
        for hq in (2 * i, 2 * i + 1):
            sl = slice(hq * rows, (hq + 1) * rows)
            oh = (oc_scr[sl, :] + o_new[sl, :]) / dens[hq]
            halves.append(oh if hq % 2 == hq // SWA_GROUP else pltpu.roll(oh, SWA_HEAD_DIM, 1))
        tiles.append(jnp.where(lo, halves[0], halves[1]))
    o_ref[...] = _rms(jnp.concatenate(tiles, axis=1), nrm_ref[...]).astype(BF16)

    keep = lax.broadcasted_iota(jnp.int32, (1, WINDOW), 1) < WINDOW - DEC_SEQ
    k_new_t = k_new.T
    v_new_t = v_new.T
    for b in range(SAMPLE_SEQS):
        to_tail = (WINDOW - DEC_SEQ - DEC_SEQ * b) % rows
        kco_ref[b] = jnp.where(keep, pltpu.roll(kc_ref[b], WINDOW - DEC_SEQ, 1),
                               pltpu.roll(k_new_t, to_tail, 1))
        vco_ref[b] = jnp.where(keep, pltpu.roll(vc_ref[b], WINDOW - DEC_SEQ, 1),
                               pltpu.roll(v_new_t, to_tail, 1))


def _swa_sample(sinks, qs, kvn, kc, vc, nrm, row0):
    nseq = kc.shape[0]
    rows = nseq * DEC_SEQ
    assert nseq % SAMPLE_SEQS == 0 and row0 % SAMPLE_ROWS == 0
    assert kc.shape[1:] == (SWA_KV_WIDTH, WINDOW) and WINDOW == SAMPLE_ROWS
    blk0 = row0 // SAMPLE_ROWS
    cblk = pl.BlockSpec((SAMPLE_SEQS, SWA_KV_WIDTH, WINDOW), lambda i: (i, 0, 0))
    big = pltpu.VMEM((SWA_Q_HEADS * SAMPLE_ROWS, LANES), F32)
    return pl.pallas_call(
        _swa_sample_body,
        grid=(nseq // SAMPLE_SEQS,),
        in_specs=[pl.BlockSpec(memory_space=pltpu.SMEM),
                  pl.BlockSpec((SAMPLE_ROWS, SWA_WIDTH), lambda i: (i + blk0, 0)),
                  pl.BlockSpec((SAMPLE_ROWS, 2 * SWA_KV_WIDTH), lambda i: (i + blk0, 0)),
                  cblk, cblk,
                  pl.BlockSpec((1, SWA_WIDTH), lambda i: (0, 0))],
        out_specs=[pl.BlockSpec((SAMPLE_ROWS, SWA_WIDTH), lambda i: (i, 0)), cblk, cblk],
        out_shape=[jax.ShapeDtypeStruct((rows, SWA_WIDTH), BF16),
                   jax.ShapeDtypeStruct(kc.shape, F32), jax.ShapeDtypeStruct(vc.shape, F32)],
        scratch_shapes=[big, big, big, big],
        compiler_params=pltpu.CompilerParams(dimension_semantics=("arbitrary",)),
        name="swa_sample",
    )(sinks, qs, kvn, kc, vc, nrm)


def _prepare_weights(ffn1_norm, ffn1_w_gate, ffn1_w_up, ffn1_w_down, mix_norm, w_in, w_gate_up, b_gate,
                     gla_head_norm, swa_out_norm, swa_sinks, w_out,
                     ffn2_norm, ffn2_w_gate, ffn2_w_up, ffn2_w_down, final_norm, layer):
    assert w_in.shape[1:] == (D_MODEL, D_IN) and PROJ_Q_S == 2 * GLA_KEY_WIDTH + 2 * GLA_WIDTH
    wgu = jnp.concatenate([w_gate_up[layer],
                           jnp.zeros((LANES - GLA_GATE_RANK, GLA_KEY_WIDTH), w_gate_up.dtype)], axis=0)
    row = lambda a: a.reshape(1, -1).astype(F32)
    return dict(
        n1=row(ffn1_norm[layer]), wg1=ffn1_w_gate[layer], wu1=ffn1_w_up[layer], wd1=ffn1_w_down[layer],
        nm=row(mix_norm[layer]), win=w_in[layer].T, wgu=wgu.astype(BF16), bg=row(b_gate[layer]),
        gn=row(gla_head_norm[layer]), sn=row(swa_out_norm[layer]), sinks=swa_sinks[layer].astype(F32),
        wo=w_out[layer], n2=row(ffn2_norm[layer]), wg2=ffn2_w_gate[layer], wu2=ffn2_w_up[layer],
        wd2=ffn2_w_down[layer], nf=row(final_norm))


def kernel(x_prompt, x_sample, state_gla, cache_swa_k, cache_swa_v, ffn1_norm, ffn1_w_gate, ffn1_w_up,
           ffn1_w_down, mix_norm, w_in, w_gate_up, b_gate, gla_head_norm, swa_out_norm, swa_sinks, w_out,
           ffn2_norm, ffn2_w_gate, ffn2_w_up, ffn2_w_down, final_norm):
    depth = state_gla.shape[0]
    assert depth == 1 and x_prompt.shape[0] == 1 and x_sample.shape[1] == DEC_SEQ
    seq = x_prompt.shape[1]
    nseq = x_sample.shape[0]
    w = _prepare_weights(ffn1_norm, ffn1_w_gate, ffn1_w_up, ffn1_w_down, mix_norm, w_in, w_gate_up, b_gate,
                         gla_head_norm, swa_out_norm, swa_sinks, w_out,
                         ffn2_norm, ffn2_w_gate, ffn2_w_up, ffn2_w_down, final_norm, 0)

    (x1, qk, g, v, r, qs, kv), (wo, wg2, wu2, wd2) = _stage_a(
        x_prompt.reshape(seq, D_MODEL), x_sample.reshape(nseq * DEC_SEQ, D_MODEL), w)
    w = dict(w, wo=wo, wg2=wg2, wu2=wu2, wd2=wd2)

    og_s, state_s = _gla_sample(qk, g, v, r, w["gn"],
                                state_gla[0].reshape(nseq, GLA_KEY_WIDTH, GLA_HEAD_V), seq)
    cache_w = cache_swa_k.shape[2]
    to_feature_major = lambda c: jnp.transpose(c[0], (0, 2, 3, 1)).reshape(nseq, SWA_KV_WIDTH, cache_w)
    from_feature_major = lambda c: jnp.transpose(
        c.reshape(nseq, SWA_KV_HEADS, SWA_HEAD_DIM, cache_w), (0, 3, 1, 2))[None]
    os_s, k_cache_s, v_cache_s = _swa_sample(
        w["sinks"], qs, kv, to_feature_major(cache_swa_k), to_feature_major(cache_swa_v), w["sn"], seq)
    k_cache_s = from_feature_major(k_cache_s)
    v_cache_s = from_feature_major(v_cache_s)

    y_prompt, y_sample, state_p = _mix_out(qk, g, v, r, qs, kv, x1, og_s, os_s, w)
    y_prompt = y_prompt.reshape(1, seq, D_MODEL)
    y_sample = y_sample.reshape(nseq, DEC_SEQ, D_MODEL)
    cw = min(WINDOW, seq)
    k_cache_p = kv[seq - cw:seq, 0:SWA_KV_WIDTH].reshape(1, 1, cw, SWA_KV_HEADS, SWA_HEAD_DIM)
    v_cache_p = kv[seq - cw:seq, SWA_KV_WIDTH:].reshape(1, 1, cw, SWA_KV_HEADS, SWA_HEAD_DIM)
    state_p = state_p.reshape(1, 1, GLA_HEADS, GLA_HEAD_K, GLA_HEAD_V)
    state_s = state_s.reshape(1, nseq, GLA_HEADS, GLA_HEAD_K, GLA_HEAD_V)

    return (y_prompt, y_sample, state_p, k_cache_p, v_cache_p, state_s, k_cache_s, v_cache_s)
```

```python
import functools
import itertools

import jax
import jax.numpy as jnp
from jax import lax
from jax.experimental import pallas as pl
from jax.experimental.pallas import tpu as pltpu

F32 = jnp.float32
BF16 = jnp.bfloat16

D_MODEL = 1024
D_FF = 2816
GLA_HEADS = 4
GLA_HEAD_K = 64
GLA_HEAD_V = 128
GLA_KEY_WIDTH = GLA_HEADS * GLA_HEAD_K
GLA_WIDTH = GLA_HEADS * GLA_HEAD_V
GLA_GATE_RANK = 16
GLA_GATE_TAU = 16.0
SWA_HEAD_DIM = 64
SWA_Q_HEADS = 8
SWA_KV_HEADS = 2
SWA_GROUP = SWA_Q_HEADS // SWA_KV_HEADS
SWA_WIDTH = SWA_Q_HEADS * SWA_HEAD_DIM
SWA_KV_WIDTH = SWA_KV_HEADS * SWA_HEAD_DIM
WINDOW = 128
DEC_SEQ = 4
NORM_EPS = 1e-6
HEAD_SCALE = 0.125

LANES = 128
SUBLANES = 8
VMEM_LIMIT_BYTES = 56 * 1024 * 1024

ROW_TILE = 512
FF_CHUNK = 256
DOWN_CHUNK = 256
GLA_CHUNK = 128
SAMPLE_SEQS = 32
SAMPLE_ROWS = SAMPLE_SEQS * DEC_SEQ
MIXER_STAGES = 3 * (ROW_TILE // GLA_CHUNK) + 3 * (ROW_TILE // WINDOW)
SIDE_STAGES_AT_NORM = 2
DECAY_CLAMP = 60.0

PROJ_Q_G = 0
PROJ_K_G = 256
PROJ_V_G = 512
PROJ_R_G = 1024
PROJ_Q_S = 1536
PROJ_KV_S = 2048
PROJ_A = 2304
PROJ_WIDTH = 2432
D_IN = 2320


def _dot(a, b):
    return jnp.dot(a, b, preferred_element_type=F32)


def _dot_tb(a, b):
    return lax.dot_general(a, b, (((1,), (1,)), ((), ())), preferred_element_type=F32)


def _dot_ta(a, b):
    return lax.dot_general(a, b, (((0,), (0,)), ((), ())), preferred_element_type=F32)


def _rms(x, g):
    return x * lax.rsqrt(jnp.mean(x * x, axis=-1, keepdims=True) + NORM_EPS) * g


def _swiglu(h, wg_ref, wu_ref, wd_ref, act_ref, side_stages=iter(()), n_side_stages=0):
    n_dots = 2 * (D_FF // FF_CHUNK) + D_MODEL // DOWN_CHUNK
    done = [0, 0]

    def after_dot():
        done[0] += 1
        while done[1] * n_dots < done[0] * n_side_stages:
            next(side_stages, None)
            done[1] += 1

    for c0 in range(0, D_FF, FF_CHUNK):
        g = _dot(h, wg_ref[:, c0:c0 + FF_CHUNK])
        after_dot()
        u = _dot(h, wu_ref[:, c0:c0 + FF_CHUNK])
        after_dot()
        act_ref[:, c0:c0 + FF_CHUNK] = (g * jax.nn.sigmoid(g) * u).astype(BF16)
    outs = []
    for n0 in range(0, D_MODEL, DOWN_CHUNK):
        outs.append(_dot(act_ref[...], wd_ref[:, n0:n0 + DOWN_CHUNK]))
        after_dot()
    return jnp.concatenate(outs, axis=1)


def _head_norm_gate(o, r, gn):
    parts = []
    for h in range(GLA_HEADS):
        sl = slice(h * GLA_HEAD_V, (h + 1) * GLA_HEAD_V)
        parts.append(_rms(o[:, sl], gn[:, sl]))
    return (jnp.concatenate(parts, axis=1) * (r * jax.nn.sigmoid(r))).astype(BF16)


def _out_ffn(x1_ref, og, osw, wo_ref, n2_ref, wg_ref, wu_ref, wd_ref, nf_ref, y_ref, act_ref, side_stages=None):
    n_inside = 0 if side_stages is None else MIXER_STAGES - 2 * SIDE_STAGES_AT_NORM
    side_stages = iter(()) if side_stages is None else side_stages
    mixed = jnp.concatenate([og, osw], axis=1)
    x2 = x1_ref[...] + _dot(mixed, wo_ref[...])
    for _ in range(SIDE_STAGES_AT_NORM):
        next(side_stages, None)
    h = _rms(x2, n2_ref[...]).astype(BF16)
    x3 = x2 + 0.5 * _swiglu(h, wg_ref, wu_ref, wd_ref, act_ref, side_stages, n_inside)
    for _ in side_stages:
        pass
    y_ref[...] = _rms(x3, nf_ref[...])


_HBM = pl.BlockSpec(memory_space=pl.ANY)
WIDE_CHUNK_ROWS = 128
NARROW_CHUNK_ROWS = 256
LOAD_SLOTS = 3


def _load_weights_bf16(loads, staging, sems):
    jobs = []
    for src, dst, segments in loads:
        cols = src.shape[1]
        assert cols in (D_FF, D_MODEL) and dst.shape[1] == cols
        kind, step = (0, WIDE_CHUNK_ROWS) if cols == D_FF else (1, NARROW_CHUNK_ROWS)
        for s0, d0, n in segments or [(0, 0, src.shape[0])]:
            jobs += [(src, dst, s0 + r, d0 + r, min(step, n - r), kind) for r in range(0, n, step)]

    used = [0, 0]
    copies = []
    for src, _, s0, _, nr, kind in jobs:
        slot = used[kind] % LOAD_SLOTS
        used[kind] += 1
        copies.append((pltpu.make_async_copy(src.at[pl.ds(s0, nr), :], staging[kind].at[slot, pl.ds(0, nr), :],
                                             sems.at[kind, slot]), slot))
    for copy, _ in copies[:LOAD_SLOTS - 1]:
        copy.start()
    for j, (_, dst, _, d0, nr, kind) in enumerate(jobs):
        ahead = j + LOAD_SLOTS - 1
        if ahead < len(jobs):
            copies[ahead][0].start()
        copy, slot = copies[j]
        copy.wait()
        dst[d0:d0 + nr, :] = staging[kind][slot, 0:nr, :].astype(BF16)


def _staging_scratch():
    return [pltpu.VMEM((LOAD_SLOTS, WIDE_CHUNK_ROWS, D_FF), F32),
            pltpu.VMEM((LOAD_SLOTS, NARROW_CHUNK_ROWS, D_MODEL), F32),
            pltpu.SemaphoreType.DMA((2, LOAD_SLOTS))]


def _chunk_plan(matrices):
    plan, first = [], 0
    for m, mat in enumerate(matrices):
        rows, cols = mat.shape
        assert cols in (D_FF, D_MODEL)
        kind, nr = (0, WIDE_CHUNK_ROWS) if cols == D_FF else (1, NARROW_CHUNK_ROWS)
        assert rows % nr == 0
        plan.append((m, first, rows // nr, nr, kind))
        first += rows // nr
    return plan, first


def _background_cast(step, srcs, dsts, staging, out_staging, sem_in, sem_out):
    plan, _ = _chunk_plan(srcs)
    slot = step % 2

    def rows_of(m, chunk):
        _, first, _, nr, _ = plan[m]
        return pl.ds(pl.multiple_of((chunk - first) * nr, nr), nr)

    def copy_in(m, chunk, slot_):
        kind = plan[m][4]
        return pltpu.make_async_copy(srcs[m].at[rows_of(m, chunk), :], staging[kind].at[slot_],
                                     sem_in.at[kind, slot_])

    def copy_out(m, chunk, slot_):
        kind = plan[m][4]
        return pltpu.make_async_copy(out_staging[kind].at[slot_], dsts[m].at[rows_of(m, chunk), :],
                                     sem_out.at[kind, slot_])

    def for_chunk(chunk, fn):
        for m, first, n, _, _ in plan:
            pl.when(jnp.logical_and(chunk >= first, chunk < first + n))(functools.partial(fn, m))

    def before():
        @pl.when(step == 0)
        def _():
            copy_in(0, 0, 0).start()
        for_chunk(step, lambda m: copy_in(m, step, slot).wait())
        for_chunk(step - 2, lambda m: copy_out(m, step - 2, slot).wait())
        for_chunk(step + 1, lambda m: copy_in(m, step + 1, 1 - slot).start())

    def cast():
        for kind in range(2):
            out_staging[kind][slot] = staging[kind][slot].astype(BF16)

    def after():
        for_chunk(step, lambda m: copy_out(m, step, slot).start())

    return before, cast, after


def _stage_a_body(xp_ref, xs_ref, n1_ref, wg_hbm, wu_hbm, wd_hbm, nm_ref, win_hbm, wgu_ref, bg_ref,
                  wo_hbm, wg2_hbm, wu2_hbm, wd2_hbm,
                  x1_ref, qk_ref, g_ref, v_ref, r_ref, qs_ref, kv_ref, wo_out, wg2_out, wu2_out, wd2_out,
                  act_ref, wg_ref, wu_ref, wd_ref, win_ref, wide_stage, narrow_stage, load_sems,
                  wide_out, narrow_out, sem_in, sem_out):
    i = pl.program_id(0)

    @pl.when(i == 0)
    def _():
        a0, a1 = PROJ_Q_S, PROJ_Q_S + GLA_GATE_RANK
        win_rows = [(0, 0, a0), (a1, a0, D_IN - a1), (a0, PROJ_A, GLA_GATE_RANK)]
        win_ref[PROJ_A + GLA_GATE_RANK:, :] = jnp.zeros((LANES - GLA_GATE_RANK, D_MODEL), BF16)
        _load_weights_bf16([(wg_hbm, wg_ref, None), (wu_hbm, wu_ref, None), (wd_hbm, wd_ref, None),
                            (win_hbm, win_ref, win_rows)], (wide_stage, narrow_stage), load_sems)

    bg_before, bg_cast, bg_after = _background_cast(
        i, (wo_hbm, wg2_hbm, wu2_hbm, wd2_hbm), (wo_out, wg2_out, wu2_out, wd2_out),
        (wide_stage, narrow_stage), (wide_out, narrow_out), sem_in, sem_out)
    bg_before()

    x = jnp.where(i < pl.num_programs(0) - 1, xp_ref[...], xs_ref[...])
    h = _rms(x, n1_ref[...]).astype(BF16)
    bg_cast()
    x1 = x + 0.5 * _swiglu(h, wg_ref, wu_ref, wd_ref, act_ref)
    x1_ref[...] = x1
    h2 = _rms(x1, nm_ref[...]).astype(BF16)
    proj = _dot_tb(h2, win_ref[...])
    qk_ref[:, 0:GLA_KEY_WIDTH] = proj[:, PROJ_Q_G:PROJ_K_G] * HEAD_SCALE
    qk_ref[:, GLA_KEY_WIDTH:] = proj[:, PROJ_K_G:PROJ_V_G]
    v_ref[...] = proj[:, PROJ_V_G:PROJ_R_G].astype(BF16)
    r_ref[...] = proj[:, PROJ_R_G:PROJ_Q_S]
    qs_ref[...] = (proj[:, PROJ_Q_S:PROJ_KV_S] * HEAD_SCALE).astype(BF16)
    kv_ref[...] = proj[:, PROJ_KV_S:PROJ_A]
    a = proj[:, PROJ_A:PROJ_WIDTH].astype(BF16)
    z = _dot(a, wgu_ref[...]) + bg_ref[...]
    g_ref[...] = jax.nn.log_sigmoid(z) * (1.0 / GLA_GATE_TAU)
    bg_after()


def _resident(shape):
    return pl.BlockSpec(shape, lambda i: (0,) * len(shape), pipeline_mode=pl.Buffered(1))


def _rows(tm, n):
    return pl.BlockSpec((tm, n), lambda i: (i, 0))


def _stage_a(xp, xs, w):
    t = xp.shape[0]
    assert t % ROW_TILE == 0 and xs.shape[0] == ROW_TILE
    nt = t // ROW_TILE
    out_widths = ((D_MODEL, F32), (2 * GLA_KEY_WIDTH, F32), (GLA_KEY_WIDTH, F32), (GLA_WIDTH, BF16),
                  (GLA_WIDTH, F32), (SWA_WIDTH, BF16), (2 * SWA_KV_WIDTH, F32))
    to_cast = (w["wo"], w["wg2"], w["wu2"], w["wd2"])
    assert _chunk_plan(to_cast)[1] + 2 <= nt + 1
    outs = pl.pallas_call(
        _stage_a_body,
        grid=(nt + 1,),
        in_specs=[pl.BlockSpec((ROW_TILE, D_MODEL), lambda i: (jnp.minimum(i, nt - 1), 0)),
                  _resident((ROW_TILE, D_MODEL)), _resident((1, D_MODEL)),
                  _HBM, _HBM, _HBM,
                  _resident((1, D_MODEL)), _HBM,
                  _resident((LANES, GLA_KEY_WIDTH)), _resident((1, GLA_KEY_WIDTH)),
                  _HBM, _HBM, _HBM, _HBM],
        out_specs=[_rows(ROW_TILE, n) for n, _ in out_widths] + [_HBM] * len(to_cast),
        out_shape=([jax.ShapeDtypeStruct((t + ROW_TILE, n), dt) for n, dt in out_widths]
                   + [jax.ShapeDtypeStruct(m.shape, BF16) for m in to_cast]),
        scratch_shapes=([pltpu.VMEM((ROW_TILE, D_FF), BF16), pltpu.VMEM((D_MODEL, D_FF), BF16),
                         pltpu.VMEM((D_MODEL, D_FF), BF16), pltpu.VMEM((D_FF, D_MODEL), BF16),
                         pltpu.VMEM((PROJ_WIDTH, D_MODEL), BF16)]
                        + _staging_scratch()
                        + [pltpu.VMEM((2, WIDE_CHUNK_ROWS, D_FF), BF16),
                           pltpu.VMEM((2, NARROW_CHUNK_ROWS, D_MODEL), BF16),
                           pltpu.SemaphoreType.DMA((2, 2)), pltpu.SemaphoreType.DMA((2, 2))]),
        compiler_params=pltpu.CompilerParams(dimension_semantics=("arbitrary",),
                                             vmem_limit_bytes=VMEM_LIMIT_BYTES),
        name="stage_a_ffn1_proj",
    )(xp, xs, w["n1"], w["wg1"], w["wu1"], w["wd1"], w["nm"], w["win"], w["wgu"], w["bg"], *to_cast)
    return outs[:len(out_widths)], outs[len(out_widths):]


def _split_bf16(x):
    hi = x.astype(BF16)
    lo = (x - hi.astype(F32)).astype(BF16)
    return hi, lo


def _alternate(a, b):
    pending = [iter(a), iter(b)]
    while pending:
        for it in list(pending):
            try:
                next(it)
            except StopIteration:
                pending.remove(it)
                continue
            yield


def _gla_stages(qk_ref, g_ref, v_ref, r_ref, gn_ref, s_ref, og_scr, cum_scr, inter_scr, flag_ref):
    c_len = GLA_CHUNK
    n_chunks = ROW_TILE // c_len
    causal = (lax.broadcasted_iota(jnp.int32, (c_len, c_len), 0)
              >= lax.broadcasted_iota(jnp.int32, (c_len, c_len), 1))
    ltri = jnp.where(causal, 1.0, 0.0).astype(BF16)
    causal_cat = (lax.broadcasted_iota(jnp.int32, (c_len, GLA_HEADS * c_len), 0)
                  >= lax.broadcasted_iota(jnp.int32, (c_len, GLA_HEADS * c_len), 1) % c_len)
    lane_head = lax.broadcasted_iota(jnp.int32, (1, GLA_KEY_WIDTH), 1) // GLA_HEAD_K
    row_head = lax.broadcasted_iota(jnp.int32, (GLA_KEY_WIDTH, 1), 0) // GLA_HEAD_K
    eye = (lax.broadcasted_iota(jnp.int32, (GLA_KEY_WIDTH, GLA_KEY_WIDTH), 0)
           == lax.broadcasted_iota(jnp.int32, (GLA_KEY_WIDTH, GLA_KEY_WIDTH), 1))
    zero_b = jnp.zeros((), BF16)
    zero_v = jnp.zeros((c_len, GLA_HEAD_V), BF16)
    worst = []

    def chunk(c):
        rows = slice(c * c_len, (c + 1) * c_len)
        g_hi, g_lo = _split_bf16(g_ref[rows, :])
        cum = _dot(ltri, g_hi) + _dot(ltri, g_lo)
        yield
        cum_scr[rows, :] = cum
        last = cum[c_len - 1:c_len, :]
        q = qk_ref[rows, 0:GLA_KEY_WIDTH]
        k = qk_ref[rows, GLA_KEY_WIDTH:]
        vb = v_ref[rows, :]
        qe = (q * jnp.exp(cum)).astype(BF16)
        ke = (k * jnp.exp(jnp.minimum(-cum, DECAY_CLAMP))).astype(BF16)
        kl = (k * jnp.exp(last - cum)).astype(BF16)
        state = s_ref[...]
        sb = state.astype(BF16)
        s_bd = jnp.concatenate([jnp.where(row_head == h, sb, zero_b) for h in range(GLA_HEADS)], axis=1)
        o_inter = _dot(qe, s_bd)
        inter_scr[rows, :] = o_inter
        ke_bd = jnp.concatenate([jnp.where(lane_head == h, ke, zero_b) for h in range(GLA_HEADS)], axis=0)
        attn = _dot_tb(qe, ke_bd)
        upds = []
        for p in range(GLA_HEADS // 2):
            u = _dot_ta(kl[:, p * LANES:(p + 1) * LANES], vb[:, 2 * p * GLA_HEAD_V:(2 * p + 2) * GLA_HEAD_V])
            upds.append(u[0:GLA_HEAD_K, 0:GLA_HEAD_V])
            upds.append(u[GLA_HEAD_K:, GLA_HEAD_V:])
        yield
        attn = jnp.where(causal_cat, attn, 0.0).astype(BF16)
        o_pairs = []
        for p in range(GLA_HEADS // 2):
            v_a = vb[:, (2 * p) * GLA_HEAD_V:(2 * p + 1) * GLA_HEAD_V]
            v_b = vb[:, (2 * p + 1) * GLA_HEAD_V:(2 * p + 2) * GLA_HEAD_V]
            v_bd = jnp.concatenate([jnp.concatenate([v_a, zero_v], axis=1),
                                    jnp.concatenate([zero_v, v_b], axis=1)], axis=0)
            o_pairs.append(_dot(attn[:, 2 * p * c_len:(2 * p + 2) * c_len], v_bd))
        og_scr[rows, :] = _head_norm_gate(o_inter + jnp.concatenate(o_pairs, axis=1), r_ref[rows, :],
                                          gn_ref[...])
        last_col = jnp.sum(jnp.where(eye, last, 0.0), axis=1, keepdims=True)
        s_ref[...] = state * jnp.exp(last_col) + jnp.concatenate(upds, axis=0)
        worst.append(jnp.min(last, axis=1, keepdims=True))
        if c == n_chunks - 1:
            tile_min = functools.reduce(jnp.minimum, worst)
            flag_ref[0] = jnp.where(tile_min[0, 0] < -DECAY_CLAMP, 1, 0)
        yield

    return itertools.chain.from_iterable(chunk(c) for c in range(n_chunks))


def _gla_tile_exact(qk_ref, v_ref, r_ref, gn_ref, og_scr, o_scr, cum_scr, inter_scr):
    c_len = GLA_CHUNK
    ind = jnp.where(lax.broadcasted_iota(jnp.int32, (GLA_KEY_WIDTH, LANES), 0) // GLA_HEAD_K
                    == lax.broadcasted_iota(jnp.int32, (GLA_KEY_WIDTH, LANES), 1), 1.0, 0.0).astype(BF16)
    j_idx = lax.broadcasted_iota(jnp.int32, (c_len, 1), 0)
    for c in range(ROW_TILE // c_len):
        r0 = c * c_len
        rows = slice(r0, r0 + c_len)

        def one_row(i, carry, r0=r0, rows=rows):
            ci = cum_scr[pl.ds(r0 + i, 1), :]
            qi = qk_ref[pl.ds(r0 + i, 1), 0:GLA_KEY_WIDTH]
            kk = qk_ref[rows, GLA_KEY_WIDTH:]
            dec = jnp.exp(jnp.minimum(ci - cum_scr[rows, :], 0.0))
            a_cols = _dot(((qi * kk) * dec).astype(BF16), ind)
            outs = []
            for h in range(GLA_HEADS):
                w_col = jnp.where(j_idx <= i, a_cols[:, h:h + 1], 0.0)
                v_h = v_ref[rows, h * GLA_HEAD_V:(h + 1) * GLA_HEAD_V].astype(F32)
                outs.append(jnp.sum(w_col * v_h, axis=0, keepdims=True))
            o_scr[pl.ds(r0 + i, 1), :] = inter_scr[pl.ds(r0 + i, 1), :] + jnp.concatenate(outs, axis=1)
            return carry

        lax.fori_loop(0, c_len, one_row, 0)
    og_scr[...] = _head_norm_gate(o_scr[...], r_ref[...], gn_ref[...])


def _dup_halves(x):
    lo = lax.broadcasted_iota(jnp.int32, (1, LANES), 1) < SWA_HEAD_DIM
    sw = pltpu.roll(x, SWA_HEAD_DIM, 1)
    return jnp.where(lo, x, sw).astype(BF16), jnp.where(lo, sw, x).astype(BF16)


def _swa_stages(sink_ref, qs_ref, kv_ref, kvprev_scr, sn_ref, os_scr, no_prev_bias):
    w = WINDOW
    n_blocks = ROW_TILE // w
    lo = lax.broadcasted_iota(jnp.int32, (1, LANES), 1) < SWA_HEAD_DIM
    tri = (lax.broadcasted_iota(jnp.int32, (w, w), 1) <= lax.broadcasted_iota(jnp.int32, (w, w), 0))
    zero_b = jnp.zeros((), BF16)

    def block(b):
        rows = slice(b * w, (b + 1) * w)
        kv_prev = kvprev_scr[...] if b == 0 else kv_ref[(b - 1) * w:b * w, :]
        kv = jnp.concatenate([kv_prev, kv_ref[rows, :]], axis=0)
        kk = _dup_halves(kv[:, 0:SWA_KV_WIDTH])
        vv = _dup_halves(kv[:, SWA_KV_WIDTH:])
        q = qs_ref[rows, :]
        scores = []
        for grp in range(SWA_KV_HEADS):
            stacked = []
            for j in range(SWA_GROUP):
                hq = grp * SWA_GROUP + j
                tile = q[:, (hq // 2) * LANES:(hq // 2 + 1) * LANES]
                stacked.append(jnp.where(lo if hq % 2 == 0 else ~lo, tile, zero_b))
            scores.append(_dot_tb(jnp.concatenate(stacked, axis=0), kk[grp]))
        yield
        outs, dens = [], []
        for grp in range(SWA_KV_HEADS):
            s = scores[grp]
            probs = []
            for j in range(SWA_GROUP):
                sink = sink_ref[grp * SWA_GROUP + j]
                s_prev = s[j * w:(j + 1) * w, 0:w]
                if b == 0:
                    s_prev = s_prev + no_prev_bias
                sf = jnp.where(tri, s[j * w:(j + 1) * w, w:], s_prev)
                m = jnp.maximum(jnp.max(sf, axis=1, keepdims=True), sink)
                p = jnp.exp(sf - m)
                dens.append(jnp.sum(p, axis=1, keepdims=True) + jnp.exp(sink - m))
                pb = p.astype(BF16)
                probs.append(jnp.concatenate([jnp.where(tri, zero_b, pb), jnp.where(tri, pb, zero_b)], axis=1))
            outs.append(_dot(jnp.concatenate(probs, axis=0), vv[grp]))
        yield
        tiles = []
        for grp in range(SWA_KV_HEADS):
            res = [outs[grp][j * w:(j + 1) * w] / dens[grp * SWA_GROUP + j] for j in range(SWA_GROUP)]
            tiles.append(jnp.where(lo, res[0], res[1]))
            tiles.append(jnp.where(lo, res[2], res[3]))
        os_scr[rows, :] = _rms(jnp.concatenate(tiles, axis=1), sn_ref[...]).astype(BF16)
        if b == n_blocks - 1:
            kvprev_scr[...] = kv_ref[rows, :]
        yield

    return itertools.chain.from_iterable(block(b) for b in range(n_blocks))


def _mix_out_body(sink_ref, qk_ref, g_ref, v_ref, r_ref, qs_ref, kv_ref, x1_ref, ogs_ref, oss_ref,
                  gn_ref, sn_ref, wo_ref, n2_ref, wg_ref, wu_ref, wd_ref, nf_ref, yp_ref, ys_ref, s_ref,
                  og_scr, os_scr, kvprev_scr, o_scr, cum_scr, inter_scr, act_scr, flag_ref):
    i = pl.program_id(0)
    n_prompt = pl.num_programs(0) - 1

    def stage_d(og, osw, y_ref, side_stages=None):
        _out_ffn(x1_ref, og, osw, wo_ref, n2_ref, wg_ref, wu_ref, wd_ref, nf_ref, y_ref, act_scr,
                 side_stages)

    def mixer_stages():
        gla = _gla_stages(qk_ref, g_ref, v_ref, r_ref, gn_ref, s_ref, og_scr, cum_scr, inter_scr, flag_ref)
        swa = _swa_stages(sink_ref, qs_ref, kv_ref, kvprev_scr, sn_ref, os_scr,
                          jnp.where(i > 0, 0.0, -jnp.inf))
        return _alternate(gla, swa)

    @pl.when(i == 0)
    def _():
        s_ref[...] = jnp.zeros_like(s_ref)
        kvprev_scr[...] = jnp.zeros_like(kvprev_scr)
        stage_d(ogs_ref[...], oss_ref[...], ys_ref, mixer_stages())

    @pl.when(jnp.logical_and(i > 0, i < n_prompt))
    def _():
        stage_d(og_scr[...], os_scr[...], yp_ref, mixer_stages())

    @pl.when(i == n_prompt)
    def _():
        stage_d(og_scr[...], os_scr[...], yp_ref)
        flag_ref[0] = 0

    @pl.when(flag_ref[0] != 0)
    def _():
        _gla_tile_exact(qk_ref, v_ref, r_ref, gn_ref, og_scr, o_scr, cum_scr, inter_scr)


def _mix_out(qk, g, v, r, qs, kv, x1, og_s, os_s, w):
    t = qk.shape[0] - ROW_TILE
    assert t % ROW_TILE == 0 and og_s.shape[0] == ROW_TILE
    nt = t // ROW_TILE
    cur = lambda n: pl.BlockSpec((ROW_TILE, n), lambda i: (jnp.minimum(i, nt - 1), 0))
    return pl.pallas_call(
        _mix_out_body,
        grid=(nt + 1,),
        in_specs=[pl.BlockSpec(memory_space=pltpu.SMEM),
                  cur(2 * GLA_KEY_WIDTH), cur(GLA_KEY_WIDTH), cur(GLA_WIDTH), cur(GLA_WIDTH),
                  cur(SWA_WIDTH), cur(2 * SWA_KV_WIDTH),
                  pl.BlockSpec((ROW_TILE, D_MODEL), lambda i: (jnp.where(i == 0, nt, i - 1), 0)),
                  _resident((ROW_TILE, GLA_WIDTH)), _resident((ROW_TILE, SWA_WIDTH)),
                  _resident((1, GLA_WIDTH)), _resident((1, SWA_WIDTH)),
                  _resident((D_MODEL, D_MODEL)), _resident((1, D_MODEL)),
                  _resident((D_MODEL, D_FF)), _resident((D_MODEL, D_FF)), _resident((D_FF, D_MODEL)),
                  _resident((1, D_MODEL))],
        out_specs=[pl.BlockSpec((ROW_TILE, D_MODEL), lambda i: (jnp.clip(i - 1, 0, nt - 1), 0)),
                   pl.BlockSpec((ROW_TILE, D_MODEL), lambda i: (0, 0)),
                   pl.BlockSpec((GLA_KEY_WIDTH, GLA_HEAD_V), lambda i: (0, 0))],
        out_shape=[jax.ShapeDtypeStruct((t, D_MODEL), F32),
                   jax.ShapeDtypeStruct((ROW_TILE, D_MODEL), F32),
                   jax.ShapeDtypeStruct((GLA_KEY_WIDTH, GLA_HEAD_V), F32)],
        scratch_shapes=[pltpu.VMEM((ROW_TILE, GLA_WIDTH), BF16), pltpu.VMEM((ROW_TILE, SWA_WIDTH), BF16),
                        pltpu.VMEM((WINDOW, 2 * SWA_KV_WIDTH), F32),
                        pltpu.VMEM((ROW_TILE, GLA_WIDTH), F32), pltpu.VMEM((ROW_TILE, GLA_KEY_WIDTH), F32),
                        pltpu.VMEM((ROW_TILE, GLA_WIDTH), F32), pltpu.VMEM((ROW_TILE, D_FF), BF16),
                        pltpu.SMEM((1,), jnp.int32)],
        compiler_params=pltpu.CompilerParams(dimension_semantics=("arbitrary",),
                                             vmem_limit_bytes=VMEM_LIMIT_BYTES),
        name="prompt_mixers_stage_d",
    )(w["sinks"], qk, g, v, r, qs, kv, x1, og_s, os_s, w["gn"], w["sn"], w["wo"], w["n2"], w["wg2"],
      w["wu2"], w["wd2"], w["nf"])


def _gla_sample_body(qk_ref, g_ref, v_ref, r_ref, gn_ref, s_ref, og_ref, so_ref, o_scr):
    rows = SAMPLE_ROWS
    g = g_ref[...]
    q = qk_ref[:, 0:GLA_KEY_WIDTH]
    k = qk_ref[:, GLA_KEY_WIDTH:]
    vb = v_ref[...]
    vf = vb.astype(F32)
    tok = lax.broadcasted_iota(jnp.int32, (rows, 1), 0) % DEC_SEQ

    cum = g
    for d in range(1, DEC_SEQ):
        cum = cum + jnp.where(tok >= d, pltpu.roll(g, d, 0), 0.0)
    tot = jnp.where(tok == DEC_SEQ - 1, cum, 0.0)
    for d in range(1, DEC_SEQ):
        tot = tot + jnp.where(tok == DEC_SEQ - 1 - d, pltpu.roll(cum, rows - d, 0), 0.0)

    qe = q * jnp.exp(cum)
    kl = k * jnp.exp(tot - cum)
    decay_t = jnp.exp(tot).T

    ind = jnp.where(lax.broadcasted_iota(jnp.int32, (GLA_KEY_WIDTH, LANES), 0) // GLA_HEAD_K
                    == lax.broadcasted_iota(jnp.int32, (GLA_KEY_WIDTH, LANES), 1), 1.0, 0.0).astype(BF16)
    expand = jnp.where(lax.broadcasted_iota(jnp.int32, (LANES, GLA_WIDTH), 0)
                       == lax.broadcasted_iota(jnp.int32, (LANES, GLA_WIDTH), 1) // GLA_HEAD_V,
                       1.0, 0.0).astype(BF16)

    o_intra = jnp.zeros((rows, GLA_WIDTH), F32)
    for d in range(DEC_SEQ):
        k_d = k if d == 0 else pltpu.roll(k, d, 0)
        c_d = cum if d == 0 else pltpu.roll(cum, d, 0)
        v_d = vf if d == 0 else pltpu.roll(vf, d, 0)
        pair = jnp.where(tok >= d, q * k_d * jnp.exp(jnp.minimum(cum - c_d, 0.0)), 0.0)
        a = _dot(pair.astype(BF16), ind)
        o_intra = o_intra + _dot(a.astype(BF16), expand) * v_d

    lane_head = lax.broadcasted_iota(jnp.int32, (1, GLA_KEY_WIDTH), 1) // GLA_HEAD_K
    row8 = lax.broadcasted_iota(jnp.int32, (SUBLANES, 1), 0)
    row32 = lax.broadcasted_iota(jnp.int32, (GLA_HEADS * SUBLANES, 1), 0)
    for pair_idx in range(SAMPLE_SEQS // 2):
        r8 = slice(pair_idx * SUBLANES, (pair_idx + 1) * SUBLANES)
        q8 = qe[r8, :]
        lhs = jnp.concatenate([jnp.where(lane_head == h, q8, 0.0) for h in range(GLA_HEADS)],
                              axis=0).astype(BF16)
        kl8 = kl[r8, :]
        v8 = vb[r8, :]
        res = []
        for s in range(2):
            b = 2 * pair_idx + s
            state = s_ref[b]
            res.append(_dot(lhs, state.astype(BF16)))
            kl_b = jnp.where(row8 // DEC_SEQ == s, kl8, 0.0).astype(BF16)
            upd = _dot_ta(kl_b, v8)
            upd = jnp.concatenate(
                [upd[h * GLA_HEAD_K:(h + 1) * GLA_HEAD_K, h * GLA_HEAD_V:(h + 1) * GLA_HEAD_V]
                 for h in range(GLA_HEADS)], axis=0)
            so_ref[b] = state * decay_t[:, DEC_SEQ * b:DEC_SEQ * b + 1] + upd
        sel = jnp.where(row32 % SUBLANES < DEC_SEQ, res[0], res[1])
        o_scr[r8, :] = jnp.concatenate([sel[h * SUBLANES:(h + 1) * SUBLANES] for h in range(GLA_HEADS)], axis=1)

    og_ref[...] = _head_norm_gate(o_scr[...] + o_intra, r_ref[...], gn_ref[...])


def _gla_sample(qk, g, v, r, gn, state, row0):
    nseq = state.shape[0]
    rows = nseq * DEC_SEQ
    assert nseq % SAMPLE_SEQS == 0 and row0 % SAMPLE_ROWS == 0
    blk0 = row0 // SAMPLE_ROWS
    src = lambda n: pl.BlockSpec((SAMPLE_ROWS, n), lambda i: (i + blk0, 0))
    blk = lambda n: pl.BlockSpec((SAMPLE_ROWS, n), lambda i: (i, 0))
    sblk = pl.BlockSpec((SAMPLE_SEQS, GLA_KEY_WIDTH, GLA_HEAD_V), lambda i: (i, 0, 0))
    return pl.pallas_call(
        _gla_sample_body,
        grid=(nseq // SAMPLE_SEQS,),
        in_specs=[src(2 * GLA_KEY_WIDTH), src(GLA_KEY_WIDTH), src(GLA_WIDTH), src(GLA_WIDTH),
                  pl.BlockSpec((1, GLA_WIDTH), lambda i: (0, 0)), sblk],
        out_specs=[blk(GLA_WIDTH), sblk],
        out_shape=[jax.ShapeDtypeStruct((rows, GLA_WIDTH), BF16),
                   jax.ShapeDtypeStruct((nseq, GLA_KEY_WIDTH, GLA_HEAD_V), F32)],
        scratch_shapes=[pltpu.VMEM((SAMPLE_ROWS, GLA_WIDTH), F32)],
        compiler_params=pltpu.CompilerParams(dimension_semantics=("arbitrary",)),
        name="gla_sample",
    )(qk, g, v, r, gn, state)


def _swa_sample_body(sink_ref, q_ref, kvn_ref, kc_ref, vc_ref, nrm_ref, o_ref, kco_ref, vco_ref,
                     lhs_scr, sc_scr, pc_scr, oc_scr):
    rows = SAMPLE_ROWS
    q = q_ref[...].astype(F32)
    lo = lax.broadcasted_iota(jnp.int32, (1, LANES), 1) < SWA_HEAD_DIM
    for hq in range(SWA_Q_HEADS):
        grp = hq // SWA_GROUP
        tile = q[:, (hq // 2) * LANES:(hq // 2 + 1) * LANES]
        src = tile if hq % 2 == grp else pltpu.roll(tile, SWA_HEAD_DIM, 1)
        lhs_scr[hq * rows:(hq + 1) * rows, :] = jnp.where(lo if grp == 0 else ~lo, src, 0.0)

    kvn = kvn_ref[...]
    k_new = kvn[:, 0:SWA_KV_WIDTH]
    v_new = kvn[:, SWA_KV_WIDTH:]
    s_new = _dot_tb(lhs_scr[...].astype(BF16), k_new.astype(BF16))

    row64 = lax.broadcasted_iota(jnp.int32, (SWA_Q_HEADS * SUBLANES, 1), 0)
    first_of_pair = row64 % SUBLANES < DEC_SEQ

    def gather_pair(ref, pair_idx):
        return jnp.concatenate(
            [ref[hq * rows + pair_idx * SUBLANES:hq * rows + (pair_idx + 1) * SUBLANES, :]
             for hq in range(SWA_Q_HEADS)], axis=0).astype(BF16)

    def scatter_pair(ref, pair_idx, val):
        for hq in range(SWA_Q_HEADS):
            ref[hq * rows + pair_idx * SUBLANES:hq * rows + (pair_idx + 1) * SUBLANES, :] = (
                val[hq * SUBLANES:(hq + 1) * SUBLANES])

    for pair_idx in range(SAMPLE_SEQS // 2):
        l64 = gather_pair(lhs_scr, pair_idx)
        sa = _dot(l64, kc_ref[2 * pair_idx].astype(BF16))
        sb = _dot(l64, kc_ref[2 * pair_idx + 1].astype(BF16))
        scatter_pair(sc_scr, pair_idx, jnp.where(first_of_pair, sa, sb))

    rr = lax.broadcasted_iota(jnp.int32, (rows, rows), 0)
    cc = lax.broadcasted_iota(jnp.int32, (rows, rows), 1)
    tok = rr % DEC_SEQ
    mask_cache = cc > tok
    mask_new = (cc // DEC_SEQ == rr // DEC_SEQ) & (cc % DEC_SEQ <= tok)
    p_new, dens = [], []
    for hq in range(SWA_Q_HEADS):
        sl = slice(hq * rows, (hq + 1) * rows)
        sink = sink_ref[hq]
        s_c = jnp.where(mask_cache, sc_scr[sl, :], -jnp.inf)
        s_n = jnp.where(mask_new, s_new[sl, :], -jnp.inf)
        m = jnp.maximum(jnp.maximum(jnp.max(s_c, axis=1, keepdims=True),
                                    jnp.max(s_n, axis=1, keepdims=True)), sink)
        p_c = jnp.exp(s_c - m)
        p_n = jnp.exp(s_n - m)
        dens.append(jnp.sum(p_c, axis=1, keepdims=True) + jnp.sum(p_n, axis=1, keepdims=True)
                    + jnp.exp(sink - m))
        pc_scr[sl, :] = p_c
        p_new.append(p_n.astype(BF16))
    o_new = _dot(jnp.concatenate(p_new, axis=0), v_new.astype(BF16))

    for pair_idx in range(SAMPLE_SEQS // 2):
        p64 = gather_pair(pc_scr, pair_idx)
        oa = _dot_tb(p64, vc_ref[2 * pair_idx].astype(BF16))
        ob = _dot_tb(p64, vc_ref[2 * pair_idx + 1].astype(BF16))
        scatter_pair(oc_scr, pair_idx, jnp.where(first_of_pair, oa, ob))

    tiles = []
    for i in range(SWA_Q_HEADS // 2):
        halves = []
        for hq in (2 * i, 2 * i + 1):
            sl = slice(hq * rows, (hq + 1) * rows)
            oh = (oc_scr[sl, :] + o_new[sl, :]) / dens[hq]
            halves.append(oh if hq % 2 == hq // SWA_GROUP else pltpu.roll(oh, SWA_HEAD_DIM, 1))
        tiles.append(jnp.where(lo, halves[0], halves[1]))
    o_ref[...] = _rms(jnp.concatenate(tiles, axis=1), nrm_ref[...]).astype(BF16)

    keep = lax.broadcasted_iota(jnp.int32, (1, WINDOW), 1) < WINDOW - DEC_SEQ
    k_new_t = k_new.T
    v_new_t = v_new.T
    for b in range(SAMPLE_SEQS):
        to_tail = (WINDOW - DEC_SEQ - DEC_SEQ * b) % rows
        kco_ref[b] = jnp.where(keep, pltpu.roll(kc_ref[b], WINDOW - DEC_SEQ, 1),
                               pltpu.roll(k_new_t, to_tail, 1))
        vco_ref[b] = jnp.where(keep, pltpu.roll(vc_ref[b], WINDOW - DEC_SEQ, 1),
                               pltpu.roll(v_new_t, to_tail, 1))


def _swa_sample(sinks, qs, kvn, kc, vc, nrm, row0):
    nseq = kc.shape[0]
    rows = nseq * DEC_SEQ
    assert nseq % SAMPLE_SEQS == 0 and row0 % SAMPLE_ROWS == 0
    assert kc.shape[1:] == (SWA_KV_WIDTH, WINDOW) and WINDOW == SAMPLE_ROWS
    blk0 = row0 // SAMPLE_ROWS
    cblk = pl.BlockSpec((SAMPLE_SEQS, SWA_KV_WIDTH, WINDOW), lambda i: (i, 0, 0))
    big = pltpu.VMEM((SWA_Q_HEADS * SAMPLE_ROWS, LANES), F32)
    return pl.pallas_call(
        _swa_sample_body,
        grid=(nseq // SAMPLE_SEQS,),
        in_specs=[pl.BlockSpec(memory_space=pltpu.SMEM),
                  pl.BlockSpec((SAMPLE_ROWS, SWA_WIDTH), lambda i: (i + blk0, 0)),
                  pl.BlockSpec((SAMPLE_ROWS, 2 * SWA_KV_WIDTH), lambda i: (i + blk0, 0)),
                  cblk, cblk,
                  pl.BlockSpec((1, SWA_WIDTH), lambda i: (0, 0))],
        out_specs=[pl.BlockSpec((SAMPLE_ROWS, SWA_WIDTH), lambda i: (i, 0)), cblk, cblk],
        out_shape=[jax.ShapeDtypeStruct((rows, SWA_WIDTH), BF16),
                   jax.ShapeDtypeStruct(kc.shape, F32), jax.ShapeDtypeStruct(vc.shape, F32)],
        scratch_shapes=[big, big, big, big],
        compiler_params=pltpu.CompilerParams(dimension_semantics=("arbitrary",)),
        name="swa_sample",
    )(sinks, qs, kvn, kc, vc, nrm)


def _prepare_weights(ffn1_norm, ffn1_w_gate, ffn1_w_up, ffn1_w_down, mix_norm, w_in, w_gate_up, b_gate,
                     gla_head_norm, swa_out_norm, swa_sinks, w_out,
                     ffn2_norm, ffn2_w_gate, ffn2_w_up, ffn2_w_down, final_norm, layer):
    assert w_in.shape[1:] == (D_MODEL, D_IN) and PROJ_Q_S == 2 * GLA_KEY_WIDTH + 2 * GLA_WIDTH
    wgu = jnp.concatenate([w_gate_up[layer],
                           jnp.zeros((LANES - GLA_GATE_RANK, GLA_KEY_WIDTH), w_gate_up.dtype)], axis=0)
    row = lambda a: a.reshape(1, -1).astype(F32)
    return dict(
        n1=row(ffn1_norm[layer]), wg1=ffn1_w_gate[layer], wu1=ffn1_w_up[layer], wd1=ffn1_w_down[layer],
        nm=row(mix_norm[layer]), win=w_in[layer].T, wgu=wgu.astype(BF16), bg=row(b_gate[layer]),
        gn=row(gla_head_norm[layer]), sn=row(swa_out_norm[layer]), sinks=swa_sinks[layer].astype(F32),
        wo=w_out[layer], n2=row(ffn2_norm[layer]), wg2=ffn2_w_gate[layer], wu2=ffn2_w_up[layer],
        wd2=ffn2_w_down[layer], nf=row(final_norm))


def kernel(x_prompt, x_sample, state_gla, cache_swa_k, cache_swa_v, ffn1_norm, ffn1_w_gate, ffn1_w_up,
           ffn1_w_down, mix_norm, w_in, w_gate_up, b_gate, gla_head_norm, swa_out_norm, swa_sinks, w_out,
           ffn2_norm, ffn2_w_gate, ffn2_w_up, ffn2_w_down, final_norm):
    depth = state_gla.shape[0]
    assert depth == 1 and x_prompt.shape[0] == 1 and x_sample.shape[1] == DEC_SEQ
    seq = x_prompt.shape[1]
    nseq = x_sample.shape[0]
    w = _prepare_weights(ffn1_norm, ffn1_w_gate, ffn1_w_up, ffn1_w_down, mix_norm, w_in, w_gate_up, b_gate,
                         gla_head_norm, swa_out_norm, swa_sinks, w_out,
                         ffn2_norm, ffn2_w_gate, ffn2_w_up, ffn2_w_down, final_norm, 0)

    (x1, qk, g, v, r, qs, kv), (wo, wg2, wu2, wd2) = _stage_a(
        x_prompt.reshape(seq, D_MODEL), x_sample.reshape(nseq * DEC_SEQ, D_MODEL), w)
    w = dict(w, wo=wo, wg2=wg2, wu2=wu2, wd2=wd2)

    og_s, state_s = _gla_sample(qk, g, v, r, w["gn"],
                                state_gla[0].reshape(nseq, GLA_KEY_WIDTH, GLA_HEAD_V), seq)
    cache_w = cache_swa_k.shape[2]
    to_feature_major = lambda c: jnp.transpose(c[0], (0, 2, 3, 1)).reshape(nseq, SWA_KV_WIDTH, cache_w)
    from_feature_major = lambda c: jnp.transpose(
        c.reshape(nseq, SWA_KV_HEADS, SWA_HEAD_DIM, cache_w), (0, 3, 1, 2))[None]
    os_s, k_cache_s, v_cache_s = _swa_sample(
        w["sinks"], qs, kv, to_feature_major(cache_swa_k), to_feature_major(cache_swa_v), w["sn"], seq)
    k_cache_s = from_feature_major(k_cache_s)
    v_cache_s = from_feature_major(v_cache_s)

    y_prompt, y_sample, state_p = _mix_out(qk, g, v, r, qs, kv, x1, og_s, os_s, w)
    y_prompt = y_prompt.reshape(1, seq, D_MODEL)
    y_sample = y_sample.reshape(nseq, DEC_SEQ, D_MODEL)
    cw = min(WINDOW, seq)
    k_cache_p = kv[seq - cw:seq, 0:SWA_KV_WIDTH].reshape(1, 1, cw, SWA_KV_HEADS, SWA_HEAD_DIM)
    v_cache_p = kv[seq - cw:seq, SWA_KV_WIDTH:].reshape(1, 1, cw, SWA_KV_HEADS, SWA_HEAD_DIM)
    state_p = state_p.reshape(1, 1, GLA_HEADS, GLA_HEAD_K, GLA_HEAD_V)
    state_s = state_s.reshape(1, nseq, GLA_HEADS, GLA_HEAD_K, GLA_HEAD_V)

    return (y_prompt, y_sample, state_p, k_cache_p, v_cache_p, state_s, k_cache_s, v_cache_s)
```

```python
import functools
import itertools

import jax
import jax.numpy as jnp
from jax import lax
from jax.experimental import pallas as pl
from jax.experimental.pallas import tpu as pltpu

F32 = jnp.float32
BF16 = jnp.bfloat16

D_MODEL = 1024
D_FF = 2816
GLA_HEADS = 4
GLA_HEAD_K = 64
GLA_HEAD_V = 128
GLA_KEY_WIDTH = GLA_HEADS * GLA_HEAD_K
GLA_WIDTH = GLA_HEADS * GLA_HEAD_V
GLA_GATE_RANK = 16
GLA_GATE_TAU = 16.0
SWA_HEAD_DIM = 64
SWA_Q_HEADS = 8
SWA_KV_HEADS = 2
SWA_GROUP = SWA_Q_HEADS // SWA_KV_HEADS
SWA_WIDTH = SWA_Q_HEADS * SWA_HEAD_DIM
SWA_KV_WIDTH = SWA_KV_HEADS * SWA_HEAD_DIM
WINDOW = 128
DEC_SEQ = 4
NORM_EPS = 1e-6
HEAD_SCALE = 0.125

LANES = 128
SUBLANES = 8
VMEM_LIMIT_BYTES = 56 * 1024 * 1024

ROW_TILE = 512
FF_CHUNK = 256
DOWN_CHUNK = 256
GLA_CHUNK = 128
SAMPLE_SEQS = 32
SAMPLE_ROWS = SAMPLE_SEQS * DEC_SEQ
MIXER_STAGES = 2 * (ROW_TILE // GLA_CHUNK) + 3 * (ROW_TILE // WINDOW)
SIDE_STAGES_AT_NORM = 2
DECAY_CLAMP = 60.0

PROJ_Q_G = 0
PROJ_K_G = PROJ_Q_G + GLA_KEY_WIDTH
PROJ_V_G = PROJ_K_G + GLA_KEY_WIDTH
PROJ_R_G = PROJ_V_G + GLA_WIDTH
PROJ_Q_S = PROJ_R_G + GLA_WIDTH
PROJ_KV_S = PROJ_Q_S + SWA_WIDTH
PROJ_A = PROJ_KV_S + 2 * SWA_KV_WIDTH
PROJ_WIDTH = PROJ_A + LANES
D_IN = PROJ_A + GLA_GATE_RANK


def _dot(a, b):
    return jnp.dot(a, b, preferred_element_type=F32)


def _dot_tb(a, b):
    return lax.dot_general(a, b, (((1,), (1,)), ((), ())), preferred_element_type=F32)


def _dot_ta(a, b):
    return lax.dot_general(a, b, (((0,), (0,)), ((), ())), preferred_element_type=F32)


def _rms(x, g):
    return x * lax.rsqrt(jnp.mean(x * x, axis=-1, keepdims=True) + NORM_EPS) * g


def _gate_up(h, wg_ref, wu_ref, act_ref, after_dot=lambda: None):
    for c0 in range(0, D_FF, FF_CHUNK):
        g = _dot(h, wg_ref[:, c0:c0 + FF_CHUNK])
        after_dot()
        u = _dot(h, wu_ref[:, c0:c0 + FF_CHUNK])
        after_dot()
        act_ref[:, c0:c0 + FF_CHUNK] = (g * jax.nn.sigmoid(g) * u).astype(BF16)


def _down(act_ref, rows, wd_ref, after_dot=lambda: None):
    outs = []
    for n0 in range(0, D_MODEL, DOWN_CHUNK):
        outs.append(_dot(act_ref[rows, :], wd_ref[:, n0:n0 + DOWN_CHUNK]))
        after_dot()
    return jnp.concatenate(outs, axis=1)


def _head_norm_gate(o, r, gn):
    parts = []
    for h in range(GLA_HEADS):
        sl = slice(h * GLA_HEAD_V, (h + 1) * GLA_HEAD_V)
        parts.append(_rms(o[:, sl], gn[:, sl]))
    return (jnp.concatenate(parts, axis=1) * (r * jax.nn.sigmoid(r))).astype(BF16)


def _out_ffn(x1_ref, og, osw, wo_ref, n2_ref, wg_ref, wu_ref, wd_ref, nf_ref, y_ref, act_ref, side_stages=None):
    n_inside = 0 if side_stages is None else MIXER_STAGES - 2 * SIDE_STAGES_AT_NORM
    side_stages = iter(()) if side_stages is None else side_stages
    n_dots = 2 * (D_FF // FF_CHUNK) + D_MODEL // DOWN_CHUNK
    done = [0, 0]

    def after_dot():
        done[0] += 1
        while done[1] * n_dots < done[0] * n_inside:
            next(side_stages, None)
            done[1] += 1

    mixed = jnp.concatenate([og, osw], axis=1)
    x2 = x1_ref[...] + _dot(mixed, wo_ref[...])
    for _ in range(SIDE_STAGES_AT_NORM):
        next(side_stages, None)
    h = _rms(x2, n2_ref[...]).astype(BF16)
    _gate_up(h, wg_ref, wu_ref, act_ref, after_dot)
    x3 = x2 + 0.5 * _down(act_ref, slice(None), wd_ref, after_dot)
    for _ in side_stages:
        pass
    y_ref[...] = _rms(x3, nf_ref[...])


_HBM = pl.BlockSpec(memory_space=pl.ANY)
WIDE_CHUNK_ROWS = 128
NARROW_CHUNK_ROWS = 256
LOAD_SLOTS = 3


def _load_weights_bf16(loads, staging, sems):
    jobs = []
    for src, dst, segments in loads:
        cols = src.shape[1]
        assert cols in (D_FF, D_MODEL) and dst.shape[1] == cols
        kind, step = (0, WIDE_CHUNK_ROWS) if cols == D_FF else (1, NARROW_CHUNK_ROWS)
        for s0, d0, n in segments or [(0, 0, src.shape[0])]:
            jobs += [(src, dst, s0 + r, d0 + r, min(step, n - r), kind) for r in range(0, n, step)]

    used = [0, 0]
    copies = []
    for src, _, s0, _, nr, kind in jobs:
        slot = used[kind] % LOAD_SLOTS
        used[kind] += 1
        copies.append((pltpu.make_async_copy(src.at[pl.ds(s0, nr), :], staging[kind].at[slot, pl.ds(0, nr), :],
                                             sems.at[kind, slot]), slot))
    for copy, _ in copies[:LOAD_SLOTS - 1]:
        copy.start()
    for j, (_, dst, _, d0, nr, kind) in enumerate(jobs):
        ahead = j + LOAD_SLOTS - 1
        if ahead < len(jobs):
            copies[ahead][0].start()
        copy, slot = copies[j]
        copy.wait()
        dst[d0:d0 + nr, :] = staging[kind][slot, 0:nr, :].astype(BF16)


def _staging_scratch():
    return [pltpu.VMEM((LOAD_SLOTS, WIDE_CHUNK_ROWS, D_FF), F32),
            pltpu.VMEM((LOAD_SLOTS, NARROW_CHUNK_ROWS, D_MODEL), F32),
            pltpu.SemaphoreType.DMA((2, LOAD_SLOTS))]


def _chunk_plan(matrices):
    plan, first = [], 0
    for m, mat in enumerate(matrices):
        rows, cols = mat.shape
        assert cols in (D_FF, D_MODEL)
        kind, nr = (0, WIDE_CHUNK_ROWS) if cols == D_FF else (1, NARROW_CHUNK_ROWS)
        assert rows % nr == 0
        plan.append((m, first, rows // nr, nr, kind))
        first += rows // nr
    return plan, first


def _background_cast(step, srcs, dsts, staging, out_staging, sem_in, sem_out):
    plan, _ = _chunk_plan(srcs)
    slot = step % 2

    def rows_of(m, chunk):
        _, first, _, nr, _ = plan[m]
        return pl.ds(pl.multiple_of((chunk - first) * nr, nr), nr)

    def copy_in(m, chunk, slot_):
        kind = plan[m][4]
        return pltpu.make_async_copy(srcs[m].at[rows_of(m, chunk), :], staging[kind].at[slot_],
                                     sem_in.at[kind, slot_])

    def copy_out(m, chunk, slot_):
        kind = plan[m][4]
        return pltpu.make_async_copy(out_staging[kind].at[slot_], dsts[m].at[rows_of(m, chunk), :],
                                     sem_out.at[kind, slot_])

    def for_chunk(chunk, fn):
        for m, first, n, _, _ in plan:
            pl.when(jnp.logical_and(chunk >= first, chunk < first + n))(functools.partial(fn, m))

    def before():
        @pl.when(step == 0)
        def _():
            copy_in(0, 0, 0).start()
        for_chunk(step, lambda m: copy_in(m, step, slot).wait())
        for_chunk(step - 2, lambda m: copy_out(m, step - 2, slot).wait())
        for_chunk(step + 1, lambda m: copy_in(m, step + 1, 1 - slot).start())

    def cast():
        for kind in range(2):
            out_staging[kind][slot] = staging[kind][slot].astype(BF16)

    def after():
        for_chunk(step, lambda m: copy_out(m, step, slot).start())

    return before, cast, after


def _stage_a_body(xp_ref, xs_ref, n1_ref, wg_hbm, wu_hbm, wd_hbm, nm_ref, win_hbm, wgu_ref, bg_ref,
                  wo_hbm, wg2_hbm, wu2_hbm, wd2_hbm,
                  x1_ref, qk_ref, g_ref, v_ref, r_ref, qs_ref, kv_ref, wo_out, wg2_out, wu2_out, wd2_out,
                  act_ref, wg_ref, wu_ref, wd_ref, win_ref, wide_stage, narrow_stage, load_sems,
                  wide_out, narrow_out, sem_in, sem_out):
    i = pl.program_id(0)

    @pl.when(i == 0)
    def _():
        a0, a1 = PROJ_Q_S, PROJ_Q_S + GLA_GATE_RANK
        win_rows = [(0, 0, a0), (a1, a0, D_IN - a1), (a0, PROJ_A, GLA_GATE_RANK)]
        win_ref[PROJ_A + GLA_GATE_RANK:, :] = jnp.zeros((LANES - GLA_GATE_RANK, D_MODEL), BF16)
        _load_weights_bf16([(wg_hbm, wg_ref, None), (wu_hbm, wu_ref, None), (wd_hbm, wd_ref, None),
                            (win_hbm, win_ref, win_rows)], (wide_stage, narrow_stage), load_sems)

    bg_before, bg_cast, bg_after = _background_cast(
        i, (wo_hbm, wg2_hbm, wu2_hbm, wd2_hbm), (wo_out, wg2_out, wu2_out, wd2_out),
        (wide_stage, narrow_stage), (wide_out, narrow_out), sem_in, sem_out)
    bg_before()

    x = jnp.where(i < pl.num_programs(0) - 1, xp_ref[...], xs_ref[...])
    h = _rms(x, n1_ref[...]).astype(BF16)
    _gate_up(h, wg_ref, wu_ref, act_ref)
    bg_cast()

    halves = [slice(k * ROW_TILE // 2, (k + 1) * ROW_TILE // 2) for k in range(2)]
    x1 = []
    for rows in halves:
        x1.append(x[rows, :] + 0.5 * _down(act_ref, rows, wd_ref))
        x1_ref[rows, :] = x1[-1]
    gate_in = []
    for rows, x1_h in zip(halves, x1):
        h2 = _rms(x1_h, nm_ref[...]).astype(BF16)
        proj = _dot_tb(h2, win_ref[...])
        qk_ref[rows, 0:GLA_KEY_WIDTH] = proj[:, PROJ_Q_G:PROJ_K_G] * HEAD_SCALE
        qk_ref[rows, GLA_KEY_WIDTH:] = proj[:, PROJ_K_G:PROJ_V_G]
        v_ref[rows, :] = proj[:, PROJ_V_G:PROJ_R_G].astype(BF16)
        r_ref[rows, :] = proj[:, PROJ_R_G:PROJ_Q_S]
        qs_ref[rows, :] = (proj[:, PROJ_Q_S:PROJ_KV_S] * HEAD_SCALE).astype(BF16)
        kv_ref[rows, :] = proj[:, PROJ_KV_S:PROJ_A]
        gate_in.append(proj[:, PROJ_A:PROJ_WIDTH].astype(BF16))
    for rows, a in zip(halves, gate_in):
        z = _dot(a, wgu_ref[...]) + bg_ref[...]
        g_ref[rows, :] = jax.nn.log_sigmoid(z) * (1.0 / GLA_GATE_TAU)
    bg_after()


def _resident(shape):
    return pl.BlockSpec(shape, lambda i: (0,) * len(shape), pipeline_mode=pl.Buffered(1))


def _rows(tm, n):
    return pl.BlockSpec((tm, n), lambda i: (i, 0))


def _stage_a(xp, xs, w):
    t = xp.shape[0]
    assert t % ROW_TILE == 0 and xs.shape[0] == ROW_TILE
    nt = t // ROW_TILE
    out_widths = ((D_MODEL, F32), (2 * GLA_KEY_WIDTH, F32), (GLA_KEY_WIDTH, F32), (GLA_WIDTH, BF16),
                  (GLA_WIDTH, F32), (SWA_WIDTH, BF16), (2 * SWA_KV_WIDTH, F32))
    to_cast = (w["wo"], w["wg2"], w["wu2"], w["wd2"])
    assert _chunk_plan(to_cast)[1] + 2 <= nt + 1
    outs = pl.pallas_call(
        _stage_a_body,
        grid=(nt + 1,),
        in_specs=[pl.BlockSpec((ROW_TILE, D_MODEL), lambda i: (jnp.minimum(i, nt - 1), 0)),
                  _resident((ROW_TILE, D_MODEL)), _resident((1, D_MODEL)),
                  _HBM, _HBM, _HBM,
                  _resident((1, D_MODEL)), _HBM,
                  _resident((LANES, GLA_KEY_WIDTH)), _resident((1, GLA_KEY_WIDTH)),
                  _HBM, _HBM, _HBM, _HBM],
        out_specs=[_rows(ROW_TILE, n) for n, _ in out_widths] + [_HBM] * len(to_cast),
        out_shape=([jax.ShapeDtypeStruct((t + ROW_TILE, n), dt) for n, dt in out_widths]
                   + [jax.ShapeDtypeStruct(m.shape, BF16) for m in to_cast]),
        scratch_shapes=([pltpu.VMEM((ROW_TILE, D_FF), BF16), pltpu.VMEM((D_MODEL, D_FF), BF16),
                         pltpu.VMEM((D_MODEL, D_FF), BF16), pltpu.VMEM((D_FF, D_MODEL), BF16),
                         pltpu.VMEM((PROJ_WIDTH, D_MODEL), BF16)]
                        + _staging_scratch()
                        + [pltpu.VMEM((2, WIDE_CHUNK_ROWS, D_FF), BF16),
                           pltpu.VMEM((2, NARROW_CHUNK_ROWS, D_MODEL), BF16),
                           pltpu.SemaphoreType.DMA((2, 2)), pltpu.SemaphoreType.DMA((2, 2))]),
        compiler_params=pltpu.CompilerParams(dimension_semantics=("arbitrary",),
                                             vmem_limit_bytes=VMEM_LIMIT_BYTES),
        name="stage_a_ffn1_proj",
    )(xp, xs, w["n1"], w["wg1"], w["wu1"], w["wd1"], w["nm"], w["win"], w["wgu"], w["bg"], *to_cast)
    return outs[:len(out_widths)], outs[len(out_widths):]


def _cumsum_rows(x):
    n = x.shape[0]
    row = lax.broadcasted_iota(jnp.int32, (n, 1), 0)
    shift = 1
    while shift < n:
        x = x + jnp.where(row >= shift, pltpu.roll(x, shift, 0), 0.0)
        shift *= 2
    return x


def _alternate(a, b):
    pending = [iter(a), iter(b)]
    while pending:
        for it in list(pending):
            try:
                next(it)
            except StopIteration:
                pending.remove(it)
                continue
            yield


def _gla_stages(qk_ref, g_ref, v_ref, r_ref, gn_ref, s_ref, og_scr, cum_scr, inter_scr, flag_ref):
    c_len = GLA_CHUNK
    n_chunks = ROW_TILE // c_len
    causal_cat = (lax.broadcasted_iota(jnp.int32, (c_len, GLA_HEADS * c_len), 0)
                  >= lax.broadcasted_iota(jnp.int32, (c_len, GLA_HEADS * c_len), 1) % c_len)
    lane_head = lax.broadcasted_iota(jnp.int32, (1, GLA_KEY_WIDTH), 1) // GLA_HEAD_K
    row_head = lax.broadcasted_iota(jnp.int32, (GLA_KEY_WIDTH, 1), 0) // GLA_HEAD_K
    eye = (lax.broadcasted_iota(jnp.int32, (GLA_KEY_WIDTH, GLA_KEY_WIDTH), 0)
           == lax.broadcasted_iota(jnp.int32, (GLA_KEY_WIDTH, GLA_KEY_WIDTH), 1))
    zero_b = jnp.zeros((), BF16)
    zero_v = jnp.zeros((c_len, GLA_HEAD_V), BF16)
    worst = []

    def chunk(c):
        rows = slice(c * c_len, (c + 1) * c_len)
        cum = _cumsum_rows(g_ref[rows, :])
        cum_scr[rows, :] = cum
        last = cum[c_len - 1:c_len, :]
        q = qk_ref[rows, 0:GLA_KEY_WIDTH]
        k = qk_ref[rows, GLA_KEY_WIDTH:]
        vb = v_ref[rows, :]
        qe = (q * jnp.exp(cum)).astype(BF16)
        ke = (k * jnp.exp(jnp.minimum(-cum, DECAY_CLAMP))).astype(BF16)
        kl = (k * jnp.exp(last - cum)).astype(BF16)
        state = s_ref[...]
        sb = state.astype(BF16)
        s_bd = jnp.concatenate([jnp.where(row_head == h, sb, zero_b) for h in range(GLA_HEADS)], axis=1)
        o_inter = _dot(qe, s_bd)
        inter_scr[rows, :] = o_inter
        ke_bd = jnp.concatenate([jnp.where(lane_head == h, ke, zero_b) for h in range(GLA_HEADS)], axis=0)
        attn = _dot_tb(qe, ke_bd)
        upds = []
        for p in range(GLA_HEADS // 2):
            u = _dot_ta(kl[:, p * LANES:(p + 1) * LANES], vb[:, 2 * p * GLA_HEAD_V:(2 * p + 2) * GLA_HEAD_V])
            upds.append(u[0:GLA_HEAD_K, 0:GLA_HEAD_V])
            upds.append(u[GLA_HEAD_K:, GLA_HEAD_V:])
        yield
        attn = jnp.where(causal_cat, attn, 0.0).astype(BF16)
        o_pairs = []
        for p in range(GLA_HEADS // 2):
            v_a = vb[:, (2 * p) * GLA_HEAD_V:(2 * p + 1) * GLA_HEAD_V]
            v_b = vb[:, (2 * p + 1) * GLA_HEAD_V:(2 * p + 2) * GLA_HEAD_V]
            v_bd = jnp.concatenate([jnp.concatenate([v_a, zero_v], axis=1),
                                    jnp.concatenate([zero_v, v_b], axis=1)], axis=0)
            o_pairs.append(_dot(attn[:, 2 * p * c_len:(2 * p + 2) * c_len], v_bd))
        og_scr[rows, :] = _head_norm_gate(o_inter + jnp.concatenate(o_pairs, axis=1), r_ref[rows, :],
                                          gn_ref[...])
        last_col = jnp.sum(jnp.where(eye, last, 0.0), axis=1, keepdims=True)
        s_ref[...] = state * jnp.exp(last_col) + jnp.concatenate(upds, axis=0)
        worst.append(jnp.min(last, axis=1, keepdims=True))
        if c == n_chunks - 1:
            tile_min = functools.reduce(jnp.minimum, worst)
            flag_ref[0] = jnp.where(tile_min[0, 0] < -DECAY_CLAMP, 1, 0)
        yield

    return itertools.chain.from_iterable(chunk(c) for c in range(n_chunks))


def _gla_tile_exact(qk_ref, v_ref, r_ref, gn_ref, og_scr, o_scr, cum_scr, inter_scr):
    c_len = GLA_CHUNK
    ind = jnp.where(lax.broadcasted_iota(jnp.int32, (GLA_KEY_WIDTH, LANES), 0) // GLA_HEAD_K
                    == lax.broadcasted_iota(jnp.int32, (GLA_KEY_WIDTH, LANES), 1), 1.0, 0.0).astype(BF16)
    j_idx = lax.broadcasted_iota(jnp.int32, (c_len, 1), 0)
    for c in range(ROW_TILE // c_len):
        r0 = c * c_len
        rows = slice(r0, r0 + c_len)

        def one_row(i, carry, r0=r0, rows=rows):
            ci = cum_scr[pl.ds(r0 + i, 1), :]
            qi = qk_ref[pl.ds(r0 + i, 1), 0:GLA_KEY_WIDTH]
            kk = qk_ref[rows, GLA_KEY_WIDTH:]
            dec = jnp.exp(jnp.minimum(ci - cum_scr[rows, :], 0.0))
            a_cols = _dot(((qi * kk) * dec).astype(BF16), ind)
            outs = []
            for h in range(GLA_HEADS):
                w_col = jnp.where(j_idx <= i, a_cols[:, h:h + 1], 0.0)
                v_h = v_ref[rows, h * GLA_HEAD_V:(h + 1) * GLA_HEAD_V].astype(F32)
                outs.append(jnp.sum(w_col * v_h, axis=0, keepdims=True))
            o_scr[pl.ds(r0 + i, 1), :] = inter_scr[pl.ds(r0 + i, 1), :] + jnp.concatenate(outs, axis=1)
            return carry

        lax.fori_loop(0, c_len, one_row, 0)
    og_scr[...] = _head_norm_gate(o_scr[...], r_ref[...], gn_ref[...])


def _dup_halves(x):
    lo = lax.broadcasted_iota(jnp.int32, (1, LANES), 1) < SWA_HEAD_DIM
    sw = pltpu.roll(x, SWA_HEAD_DIM, 1)
    return jnp.where(lo, x, sw).astype(BF16), jnp.where(lo, sw, x).astype(BF16)


def _swa_stages(sink_ref, qs_ref, kv_ref, kvprev_scr, sn_ref, os_scr, no_prev_bias):
    w = WINDOW
    n_blocks = ROW_TILE // w
    lo = lax.broadcasted_iota(jnp.int32, (1, LANES), 1) < SWA_HEAD_DIM
    tri = (lax.broadcasted_iota(jnp.int32, (w, w), 1) <= lax.broadcasted_iota(jnp.int32, (w, w), 0))
    zero_b = jnp.zeros((), BF16)

    def block(b):
        rows = slice(b * w, (b + 1) * w)
        kv_prev = kvprev_scr[...] if b == 0 else kv_ref[(b - 1) * w:b * w, :]
        kv = jnp.concatenate([kv_prev, kv_ref[rows, :]], axis=0)
        kk = _dup_halves(kv[:, 0:SWA_KV_WIDTH])
        vv = _dup_halves(kv[:, SWA_KV_WIDTH:])
        q = qs_ref[rows, :]
        scores = []
        for grp in range(SWA_KV_HEADS):
            stacked = []
            for j in range(SWA_GROUP):
                hq = grp * SWA_GROUP + j
                tile = q[:, (hq // 2) * LANES:(hq // 2 + 1) * LANES]
                stacked.append(jnp.where(lo if hq % 2 == 0 else ~lo, tile, zero_b))
            scores.append(_dot_tb(jnp.concatenate(stacked, axis=0), kk[grp]))
        yield
        outs, dens = [], []
        for grp in range(SWA_KV_HEADS):
            s = scores[grp]
            probs = []
            for j in range(SWA_GROUP):
                sink = sink_ref[grp * SWA_GROUP + j]
                s_prev = s[j * w:(j + 1) * w, 0:w]
                if b == 0:
                    s_prev = s_prev + no_prev_bias
                sf = jnp.where(tri, s[j * w:(j + 1) * w, w:], s_prev)
                m = jnp.maximum(jnp.max(sf, axis=1, keepdims=True), sink)
                p = jnp.exp(sf - m)
                dens.append(jnp.sum(p, axis=1, keepdims=True) + jnp.exp(sink - m))
                pb = p.astype(BF16)
                probs.append(jnp.concatenate([jnp.where(tri, zero_b, pb), jnp.where(tri, pb, zero_b)], axis=1))
            outs.append(_dot(jnp.concatenate(probs, axis=0), vv[grp]))
        yield
        tiles = []
        for grp in range(SWA_KV_HEADS):
            res = [outs[grp][j * w:(j + 1) * w] / dens[grp * SWA_GROUP + j] for j in range(SWA_GROUP)]
            tiles.append(jnp.where(lo, res[0], res[1]))
            tiles.append(jnp.where(lo, res[2], res[3]))
        os_scr[rows, :] = _rms(jnp.concatenate(tiles, axis=1), sn_ref[...]).astype(BF16)
        if b == n_blocks - 1:
            kvprev_scr[...] = kv_ref[rows, :]
        yield

    return itertools.chain.from_iterable(block(b) for b in range(n_blocks))


def _mix_out_body(sink_ref, qk_ref, g_ref, v_ref, r_ref, qs_ref, kv_ref, x1_ref, ogs_ref, oss_ref,
                  gn_ref, sn_ref, wo_ref, n2_ref, wg_ref, wu_ref, wd_ref, nf_ref, yp_ref, ys_ref, s_ref,
                  og_scr, os_scr, kvprev_scr, o_scr, cum_scr, inter_scr, act_scr, flag_ref):
    i = pl.program_id(0)
    n_prompt = pl.num_programs(0) - 1

    def stage_d(og, osw, y_ref, side_stages=None):
        _out_ffn(x1_ref, og, osw, wo_ref, n2_ref, wg_ref, wu_ref, wd_ref, nf_ref, y_ref, act_scr,
                 side_stages)

    def mixer_stages():
        gla = _gla_stages(qk_ref, g_ref, v_ref, r_ref, gn_ref, s_ref, og_scr, cum_scr, inter_scr, flag_ref)
        swa = _swa_stages(sink_ref, qs_ref, kv_ref, kvprev_scr, sn_ref, os_scr,
                          jnp.where(i > 0, 0.0, -jnp.inf))
        return _alternate(gla, swa)

    @pl.when(i == 0)
    def _():
        s_ref[...] = jnp.zeros_like(s_ref)
        kvprev_scr[...] = jnp.zeros_like(kvprev_scr)
        stage_d(ogs_ref[...], oss_ref[...], ys_ref, mixer_stages())

    @pl.when(jnp.logical_and(i > 0, i < n_prompt))
    def _():
        stage_d(og_scr[...], os_scr[...], yp_ref, mixer_stages())

    @pl.when(i == n_prompt)
    def _():
        stage_d(og_scr[...], os_scr[...], yp_ref)
        flag_ref[0] = 0

    @pl.when(flag_ref[0] != 0)
    def _():
        _gla_tile_exact(qk_ref, v_ref, r_ref, gn_ref, og_scr, o_scr, cum_scr, inter_scr)


def _mix_out(qk, g, v, r, qs, kv, x1, og_s, os_s, w):
    t = qk.shape[0] - ROW_TILE
    assert t % ROW_TILE == 0 and og_s.shape[0] == ROW_TILE
    nt = t // ROW_TILE
    cur = lambda n: pl.BlockSpec((ROW_TILE, n), lambda i: (jnp.minimum(i, nt - 1), 0))
    return pl.pallas_call(
        _mix_out_body,
        grid=(nt + 1,),
        in_specs=[pl.BlockSpec(memory_space=pltpu.SMEM),
                  cur(2 * GLA_KEY_WIDTH), cur(GLA_KEY_WIDTH), cur(GLA_WIDTH), cur(GLA_WIDTH),
                  cur(SWA_WIDTH), cur(2 * SWA_KV_WIDTH),
                  pl.BlockSpec((ROW_TILE, D_MODEL), lambda i: (jnp.where(i == 0, nt, i - 1), 0)),
                  _resident((ROW_TILE, GLA_WIDTH)), _resident((ROW_TILE, SWA_WIDTH)),
                  _resident((1, GLA_WIDTH)), _resident((1, SWA_WIDTH)),
                  _resident((D_MODEL, D_MODEL)), _resident((1, D_MODEL)),
                  _resident((D_MODEL, D_FF)), _resident((D_MODEL, D_FF)), _resident((D_FF, D_MODEL)),
                  _resident((1, D_MODEL))],
        out_specs=[pl.BlockSpec((ROW_TILE, D_MODEL), lambda i: (jnp.clip(i - 1, 0, nt - 1), 0)),
                   pl.BlockSpec((ROW_TILE, D_MODEL), lambda i: (0, 0)),
                   pl.BlockSpec((GLA_KEY_WIDTH, GLA_HEAD_V), lambda i: (0, 0))],
        out_shape=[jax.ShapeDtypeStruct((t, D_MODEL), F32),
                   jax.ShapeDtypeStruct((ROW_TILE, D_MODEL), F32),
                   jax.ShapeDtypeStruct((GLA_KEY_WIDTH, GLA_HEAD_V), F32)],
        scratch_shapes=[pltpu.VMEM((ROW_TILE, GLA_WIDTH), BF16), pltpu.VMEM((ROW_TILE, SWA_WIDTH), BF16),
                        pltpu.VMEM((WINDOW, 2 * SWA_KV_WIDTH), F32),
                        pltpu.VMEM((ROW_TILE, GLA_WIDTH), F32), pltpu.VMEM((ROW_TILE, GLA_KEY_WIDTH), F32),
                        pltpu.VMEM((ROW_TILE, GLA_WIDTH), F32), pltpu.VMEM((ROW_TILE, D_FF), BF16),
                        pltpu.SMEM((1,), jnp.int32)],
        compiler_params=pltpu.CompilerParams(dimension_semantics=("arbitrary",),
                                             vmem_limit_bytes=VMEM_LIMIT_BYTES),
        name="prompt_mixers_stage_d",
    )(w["sinks"], qk, g, v, r, qs, kv, x1, og_s, os_s, w["gn"], w["sn"], w["wo"], w["n2"], w["wg2"],
      w["wu2"], w["wd2"], w["nf"])


def _gla_sample_body(qk_ref, g_ref, v_ref, r_ref, gn_ref, s_ref, og_ref, so_ref, o_scr):
    rows = SAMPLE_ROWS
    g = g_ref[...]
    q = qk_ref[:, 0:GLA_KEY_WIDTH]
    k = qk_ref[:, GLA_KEY_WIDTH:]
    vb = v_ref[...]
    vf = vb.astype(F32)
    tok = lax.broadcasted_iota(jnp.int32, (rows, 1), 0) % DEC_SEQ

    cum = g
    for d in range(1, DEC_SEQ):
        cum = cum + jnp.where(tok >= d, pltpu.roll(g, d, 0), 0.0)
    tot = jnp.where(tok == DEC_SEQ - 1, cum, 0.0)
    for d in range(1, DEC_SEQ):
        tot = tot + jnp.where(tok == DEC_SEQ - 1 - d, pltpu.roll(cum, rows - d, 0), 0.0)

    qe = q * jnp.exp(cum)
    kl = k * jnp.exp(tot - cum)
    decay_t = jnp.exp(tot).T

    ind = jnp.where(lax.broadcasted_iota(jnp.int32, (GLA_KEY_WIDTH, LANES), 0) // GLA_HEAD_K
                    == lax.broadcasted_iota(jnp.int32, (GLA_KEY_WIDTH, LANES), 1), 1.0, 0.0).astype(BF16)
    expand = jnp.where(lax.broadcasted_iota(jnp.int32, (LANES, GLA_WIDTH), 0)
                       == lax.broadcasted_iota(jnp.int32, (LANES, GLA_WIDTH), 1) // GLA_HEAD_V,
                       1.0, 0.0).astype(BF16)

    o_intra = jnp.zeros((rows, GLA_WIDTH), F32)
    for d in range(DEC_SEQ):
        k_d = k if d == 0 else pltpu.roll(k, d, 0)
        c_d = cum if d == 0 else pltpu.roll(cum, d, 0)
        v_d = vf if d == 0 else pltpu.roll(vf, d, 0)
        pair = jnp.where(tok >= d, q * k_d * jnp.exp(jnp.minimum(cum - c_d, 0.0)), 0.0)
        a = _dot(pair.astype(BF16), ind)
        o_intra = o_intra + _dot(a.astype(BF16), expand) * v_d

    lane_head = lax.broadcasted_iota(jnp.int32, (1, GLA_KEY_WIDTH), 1) // GLA_HEAD_K
    row8 = lax.broadcasted_iota(jnp.int32, (SUBLANES, 1), 0)
    row32 = lax.broadcasted_iota(jnp.int32, (GLA_HEADS * SUBLANES, 1), 0)
    for pair_idx in range(SAMPLE_SEQS // 2):
        r8 = slice(pair_idx * SUBLANES, (pair_idx + 1) * SUBLANES)
        q8 = qe[r8, :]
        lhs = jnp.concatenate([jnp.where(lane_head == h, q8, 0.0) for h in range(GLA_HEADS)],
                              axis=0).astype(BF16)
        kl8 = kl[r8, :]
        v8 = vb[r8, :]
        res = []
        for s in range(2):
            b = 2 * pair_idx + s
            state = s_ref[b]
            res.append(_dot(lhs, state.astype(BF16)))
            kl_b = jnp.where(row8 // DEC_SEQ == s, kl8, 0.0).astype(BF16)
            upd = _dot_ta(kl_b, v8)
            upd = jnp.concatenate(
                [upd[h * GLA_HEAD_K:(h + 1) * GLA_HEAD_K, h * GLA_HEAD_V:(h + 1) * GLA_HEAD_V]
                 for h in range(GLA_HEADS)], axis=0)
            so_ref[b] = state * decay_t[:, DEC_SEQ * b:DEC_SEQ * b + 1] + upd
        sel = jnp.where(row32 % SUBLANES < DEC_SEQ, res[0], res[1])
        o_scr[r8, :] = jnp.concatenate([sel[h * SUBLANES:(h + 1) * SUBLANES] for h in range(GLA_HEADS)], axis=1)

    og_ref[...] = _head_norm_gate(o_scr[...] + o_intra, r_ref[...], gn_ref[...])


def _gla_sample(qk, g, v, r, gn, state, row0):
    nseq = state.shape[0]
    rows = nseq * DEC_SEQ
    assert nseq % SAMPLE_SEQS == 0 and row0 % SAMPLE_ROWS == 0
    blk0 = row0 // SAMPLE_ROWS
    src = lambda n: pl.BlockSpec((SAMPLE_ROWS, n), lambda i: (i + blk0, 0))
    blk = lambda n: pl.BlockSpec((SAMPLE_ROWS, n), lambda i: (i, 0))
    sblk = pl.BlockSpec((SAMPLE_SEQS, GLA_KEY_WIDTH, GLA_HEAD_V), lambda i: (i, 0, 0))
    return pl.pallas_call(
        _gla_sample_body,
        grid=(nseq // SAMPLE_SEQS,),
        in_specs=[src(2 * GLA_KEY_WIDTH), src(GLA_KEY_WIDTH), src(GLA_WIDTH), src(GLA_WIDTH),
                  pl.BlockSpec((1, GLA_WIDTH), lambda i: (0, 0)), sblk],
        out_specs=[blk(GLA_WIDTH), sblk],
        out_shape=[jax.ShapeDtypeStruct((rows, GLA_WIDTH), BF16),
                   jax.ShapeDtypeStruct((nseq, GLA_KEY_WIDTH, GLA_HEAD_V), F32)],
        scratch_shapes=[pltpu.VMEM((SAMPLE_ROWS, GLA_WIDTH), F32)],
        compiler_params=pltpu.CompilerParams(dimension_semantics=("arbitrary",)),
        name="gla_sample",
    )(qk, g, v, r, gn, state)


def _swa_sample_body(sink_ref, q_ref, kvn_ref, kc_ref, vc_ref, nrm_ref, o_ref, kco_ref, vco_ref,
                     lhs_scr, sc_scr, pc_scr, oc_scr):
    rows = SAMPLE_ROWS
    q = q_ref[...].astype(F32)
    lo = lax.broadcasted_iota(jnp.int32, (1, LANES), 1) < SWA_HEAD_DIM
    for hq in range(SWA_Q_HEADS):
        grp = hq // SWA_GROUP
        tile = q[:, (hq // 2) * LANES:(hq // 2 + 1) * LANES]
        src = tile if hq % 2 == grp else pltpu.roll(tile, SWA_HEAD_DIM, 1)
        lhs_scr[hq * rows:(hq + 1) * rows, :] = jnp.where(lo if grp == 0 else ~lo, src, 0.0)

    kvn = kvn_ref[...]
    k_new = kvn[:, 0:SWA_KV_WIDTH]
    v_new = kvn[:, SWA_KV_WIDTH:]
    s_new = _dot_tb(lhs_scr[...].astype(BF16), k_new.astype(BF16))

    row64 = lax.broadcasted_iota(jnp.int32, (SWA_Q_HEADS * SUBLANES, 1), 0)
    first_of_pair = row64 % SUBLANES < DEC_SEQ

    def gather_pair(ref, pair_idx):
        return jnp.concatenate(
            [ref[hq * rows + pair_idx * SUBLANES:hq * rows + (pair_idx + 1) * SUBLANES, :]
             for hq in range(SWA_Q_HEADS)], axis=0).astype(BF16)

    def scatter_pair(ref, pair_idx, val):
        for hq in range(SWA_Q_HEADS):
            ref[hq * rows + pair_idx * SUBLANES:hq * rows + (pair_idx + 1) * SUBLANES, :] = (
                val[hq * SUBLANES:(hq + 1) * SUBLANES])

    for pair_idx in range(SAMPLE_SEQS // 2):
        l64 = gather_pair(lhs_scr, pair_idx)
        sa = _dot(l64, kc_ref[2 * pair_idx].astype(BF16))
        sb = _dot(l64, kc_ref[2 * pair_idx + 1].astype(BF16))
        scatter_pair(sc_scr, pair_idx, jnp.where(first_of_pair, sa, sb))

    rr = lax.broadcasted_iota(jnp.int32, (rows, rows), 0)
    cc = lax.broadcasted_iota(jnp.int32, (rows, rows), 1)
    tok = rr % DEC_SEQ
    mask_cache = cc > tok
    mask_new = (cc // DEC_SEQ == rr // DEC_SEQ) & (cc % DEC_SEQ <= tok)
    p_new, dens = [], []
    for hq in range(SWA_Q_HEADS):
        sl = slice(hq * rows, (hq + 1) * rows)
        sink = sink_ref[hq]
        s_c = jnp.where(mask_cache, sc_scr[sl, :], -jnp.inf)
        s_n = jnp.where(mask_new, s_new[sl, :], -jnp.inf)
        m = jnp.maximum(jnp.maximum(jnp.max(s_c, axis=1, keepdims=True),
                                    jnp.max(s_n, axis=1, keepdims=True)), sink)
        p_c = jnp.exp(s_c - m)
        p_n = jnp.exp(s_n - m)
        dens.append(jnp.sum(p_c, axis=1, keepdims=True) + jnp.sum(p_n, axis=1, keepdims=True)
                    + jnp.exp(sink - m))
        pc_scr[sl, :] = p_c
        p_new.append(p_n.astype(BF16))
    o_new = _dot(jnp.concatenate(p_new, axis=0), v_new.astype(BF16))

    for pair_idx in range(SAMPLE_SEQS // 2):
        p64 = gather_pair(pc_scr, pair_idx)
        oa = _dot_tb(p64, vc_ref[2 * pair_idx].astype(BF16))
        ob = _dot_tb(p64, vc_ref[2 * pair_idx + 1].astype(BF16))
        scatter_pair(oc_scr, pair_idx, jnp.where(first_of_pair, oa, ob))

    tiles = []
    for i in range(SWA_Q_HEADS // 2):
        halves = []
        for hq in (2 * i, 2 * i + 1):
            sl = slice(hq * rows, (hq + 1) * rows)
            oh = (oc_scr[sl, :] + o_new[sl, :]) / dens[hq]
            halves.append(oh if hq % 2 == hq // SWA_GROUP else pltpu.roll(oh, SWA_HEAD_DIM, 1))
        tiles.append(jnp.where(lo, halves[0], halves[1]))
    o_ref[...] = _rms(jnp.concatenate(tiles, axis=1), nrm_ref[...]).astype(BF16)

    keep = lax.broadcasted_iota(jnp.int32, (1, WINDOW), 1) < WINDOW - DEC_SEQ
    k_new_t = k_new.T
    v_new_t = v_new.T
    for b in range(SAMPLE_SEQS):
        to_tail = (WINDOW - DEC_SEQ - DEC_SEQ * b) % rows
        kco_ref[b] = jnp.where(keep, pltpu.roll(kc_ref[b], WINDOW - DEC_SEQ, 1),
                               pltpu.roll(k_new_t, to_tail, 1))
        vco_ref[b] = jnp.where(keep, pltpu.roll(vc_ref[b], WINDOW - DEC_SEQ, 1),
                               pltpu.roll(v_new_t, to_tail, 1))


def _swa_sample(sinks, qs, kvn, kc, vc, nrm, row0):
    nseq = kc.shape[0]
    rows = nseq * DEC_SEQ
    assert nseq % SAMPLE_SEQS == 0 and row0 % SAMPLE_ROWS == 0
    assert kc.shape[1:] == (SWA_KV_WIDTH, WINDOW) and WINDOW == SAMPLE_ROWS
    blk0 = row0 // SAMPLE_ROWS
    cblk = pl.BlockSpec((SAMPLE_SEQS, SWA_KV_WIDTH, WINDOW), lambda i: (i, 0, 0))
    big = pltpu.VMEM((SWA_Q_HEADS * SAMPLE_ROWS, LANES), F32)
    return pl.pallas_call(
        _swa_sample_body,
        grid=(nseq // SAMPLE_SEQS,),
        in_specs=[pl.BlockSpec(memory_space=pltpu.SMEM),
                  pl.BlockSpec((SAMPLE_ROWS, SWA_WIDTH), lambda i: (i + blk0, 0)),
                  pl.BlockSpec((SAMPLE_ROWS, 2 * SWA_KV_WIDTH), lambda i: (i + blk0, 0)),
                  cblk, cblk,
                  pl.BlockSpec((1, SWA_WIDTH), lambda i: (0, 0))],
        out_specs=[pl.BlockSpec((SAMPLE_ROWS, SWA_WIDTH), lambda i: (i, 0)), cblk, cblk],
        out_shape=[jax.ShapeDtypeStruct((rows, SWA_WIDTH), BF16),
                   jax.ShapeDtypeStruct(kc.shape, F32), jax.ShapeDtypeStruct(vc.shape, F32)],
        scratch_shapes=[big, big, big, big],
        compiler_params=pltpu.CompilerParams(dimension_semantics=("arbitrary",)),
        name="swa_sample",
    )(sinks, qs, kvn, kc, vc, nrm)


def _prepare_weights(ffn1_norm, ffn1_w_gate, ffn1_w_up, ffn1_w_down, mix_norm, w_in, w_gate_up, b_gate,
                     gla_head_norm, swa_out_norm, swa_sinks, w_out,
                     ffn2_norm, ffn2_w_gate, ffn2_w_up, ffn2_w_down, final_norm, layer):
    assert w_in.shape[1:] == (D_MODEL, D_IN) and PROJ_Q_S == 2 * GLA_KEY_WIDTH + 2 * GLA_WIDTH
    wgu = jnp.concatenate([w_gate_up[layer],
                           jnp.zeros((LANES - GLA_GATE_RANK, GLA_KEY_WIDTH), w_gate_up.dtype)], axis=0)
    row = lambda a: a.reshape(1, -1).astype(F32)
    return dict(
        n1=row(ffn1_norm[layer]), wg1=ffn1_w_gate[layer], wu1=ffn1_w_up[layer], wd1=ffn1_w_down[layer],
        nm=row(mix_norm[layer]), win=w_in[layer].T, wgu=wgu.astype(BF16), bg=row(b_gate[layer]),
        gn=row(gla_head_norm[layer]), sn=row(swa_out_norm[layer]), sinks=swa_sinks[layer].astype(F32),
        wo=w_out[layer], n2=row(ffn2_norm[layer]), wg2=ffn2_w_gate[layer], wu2=ffn2_w_up[layer],
        wd2=ffn2_w_down[layer], nf=row(final_norm))


def kernel(x_prompt, x_sample, state_gla, cache_swa_k, cache_swa_v, ffn1_norm, ffn1_w_gate, ffn1_w_up,
           ffn1_w_down, mix_norm, w_in, w_gate_up, b_gate, gla_head_norm, swa_out_norm, swa_sinks, w_out,
           ffn2_norm, ffn2_w_gate, ffn2_w_up, ffn2_w_down, final_norm):
    depth = state_gla.shape[0]
    assert depth == 1 and x_prompt.shape[0] == 1 and x_sample.shape[1] == DEC_SEQ
    seq = x_prompt.shape[1]
    nseq = x_sample.shape[0]
    w = _prepare_weights(ffn1_norm, ffn1_w_gate, ffn1_w_up, ffn1_w_down, mix_norm, w_in, w_gate_up, b_gate,
                         gla_head_norm, swa_out_norm, swa_sinks, w_out,
                         ffn2_norm, ffn2_w_gate, ffn2_w_up, ffn2_w_down, final_norm, 0)

    (x1, qk, g, v, r, qs, kv), (wo, wg2, wu2, wd2) = _stage_a(
        x_prompt.reshape(seq, D_MODEL), x_sample.reshape(nseq * DEC_SEQ, D_MODEL), w)
    w = dict(w, wo=wo, wg2=wg2, wu2=wu2, wd2=wd2)

    og_s, state_s = _gla_sample(qk, g, v, r, w["gn"],
                                state_gla[0].reshape(nseq, GLA_KEY_WIDTH, GLA_HEAD_V), seq)
    cache_w = cache_swa_k.shape[2]
    to_feature_major = lambda c: jnp.transpose(c[0], (0, 2, 3, 1)).reshape(nseq, SWA_KV_WIDTH, cache_w)
    from_feature_major = lambda c: jnp.transpose(
        c.reshape(nseq, SWA_KV_HEADS, SWA_HEAD_DIM, cache_w), (0, 3, 1, 2))[None]
    os_s, k_cache_s, v_cache_s = _swa_sample(
        w["sinks"], qs, kv, to_feature_major(cache_swa_k), to_feature_major(cache_swa_v), w["sn"], seq)
    k_cache_s = from_feature_major(k_cache_s)
    v_cache_s = from_feature_major(v_cache_s)

    y_prompt, y_sample, state_p = _mix_out(qk, g, v, r, qs, kv, x1, og_s, os_s, w)
    y_prompt = y_prompt.reshape(1, seq, D_MODEL)
    y_sample = y_sample.reshape(nseq, DEC_SEQ, D_MODEL)
    cw = min(WINDOW, seq)
    k_cache_p = kv[seq - cw:seq, 0:SWA_KV_WIDTH].reshape(1, 1, cw, SWA_KV_HEADS, SWA_HEAD_DIM)
    v_cache_p = kv[seq - cw:seq, SWA_KV_WIDTH:].reshape(1, 1, cw, SWA_KV_HEADS, SWA_HEAD_DIM)
    state_p = state_p.reshape(1, 1, GLA_HEADS, GLA_HEAD_K, GLA_HEAD_V)
    state_s = state_s.reshape(1, nseq, GLA_HEADS, GLA_HEAD_K, GLA_HEAD_V)

    return (y_prompt, y_sample, state_p, k_cache_p, v_cache_p, state_s, k_cache_s, v_cache_s)
```

```python
import functools
import itertools

import jax
import jax.numpy as jnp
from jax import lax
from jax.experimental import pallas as pl
from jax.experimental.pallas import tpu as pltpu

F32 = jnp.float32
BF16 = jnp.bfloat16

D_MODEL = 1024
D_FF = 2816
GLA_HEADS = 4
GLA_HEAD_K = 64
GLA_HEAD_V = 128
GLA_KEY_WIDTH = GLA_HEADS * GLA_HEAD_K
GLA_WIDTH = GLA_HEADS * GLA_HEAD_V
GLA_GATE_RANK = 16
GLA_GATE_TAU = 16.0
SWA_HEAD_DIM = 64
SWA_Q_HEADS = 8
SWA_KV_HEADS = 2
SWA_GROUP = SWA_Q_HEADS // SWA_KV_HEADS
SWA_WIDTH = SWA_Q_HEADS * SWA_HEAD_DIM
SWA_KV_WIDTH = SWA_KV_HEADS * SWA_HEAD_DIM
WINDOW = 128
DEC_SEQ = 4
NORM_EPS = 1e-6
HEAD_SCALE = 0.125

LANES = 128
SUBLANES = 8
VMEM_LIMIT_BYTES = 56 * 1024 * 1024

ROW_TILE = 512
FF_CHUNK = 256
DOWN_CHUNK = 256
GLA_CHUNK = 128
SAMPLE_SEQS = 32
SAMPLE_ROWS = SAMPLE_SEQS * DEC_SEQ
MIXER_STAGES = 2 * (ROW_TILE // GLA_CHUNK) + 3 * (ROW_TILE // WINDOW)
SIDE_STAGES_AT_NORM = 2
DECAY_CLAMP = 60.0

PROJ_Q_G = 0
PROJ_K_G = PROJ_Q_G + GLA_KEY_WIDTH
PROJ_V_G = PROJ_K_G + GLA_KEY_WIDTH
PROJ_R_G = PROJ_V_G + GLA_WIDTH
PROJ_Q_S = PROJ_R_G + GLA_WIDTH
PROJ_KV_S = PROJ_Q_S + SWA_WIDTH
PROJ_A = PROJ_KV_S + 2 * SWA_KV_WIDTH
PROJ_WIDTH = PROJ_A + LANES
D_IN = PROJ_A + GLA_GATE_RANK


def _dot(a, b):
    return jnp.dot(a, b, preferred_element_type=F32)


def _dot_tb(a, b):
    return lax.dot_general(a, b, (((1,), (1,)), ((), ())), preferred_element_type=F32)


def _dot_ta(a, b):
    return lax.dot_general(a, b, (((0,), (0,)), ((), ())), preferred_element_type=F32)


def _rms(x, g):
    return x * lax.rsqrt(jnp.mean(x * x, axis=-1, keepdims=True) + NORM_EPS) * g


def _gate_up(h, wg_ref, wu_ref, act_ref, after_dot=lambda: None):
    for c0 in range(0, D_FF, FF_CHUNK):
        g = _dot(h, wg_ref[:, c0:c0 + FF_CHUNK])
        after_dot()
        u = _dot(h, wu_ref[:, c0:c0 + FF_CHUNK])
        after_dot()
        act_ref[:, c0:c0 + FF_CHUNK] = (g * jax.nn.sigmoid(g) * u).astype(BF16)


def _down(act_ref, rows, wd_ref, after_dot=lambda: None):
    outs = []
    for n0 in range(0, D_MODEL, DOWN_CHUNK):
        outs.append(_dot(act_ref[rows, :], wd_ref[:, n0:n0 + DOWN_CHUNK]))
        after_dot()
    return jnp.concatenate(outs, axis=1)


def _head_norm_gate(o, r, gn):
    parts = []
    for h in range(GLA_HEADS):
        sl = slice(h * GLA_HEAD_V, (h + 1) * GLA_HEAD_V)
        parts.append(_rms(o[:, sl], gn[:, sl]))
    return (jnp.concatenate(parts, axis=1) * (r * jax.nn.sigmoid(r))).astype(BF16)


def _out_ffn(x1_ref, og, osw, wo_ref, n2_ref, wg_ref, wu_ref, wd_ref, nf_ref, y_ref, act_ref, side_stages=None):
    n_inside = 0 if side_stages is None else MIXER_STAGES - 2 * SIDE_STAGES_AT_NORM
    side_stages = iter(()) if side_stages is None else side_stages
    n_dots = 2 * (D_FF // FF_CHUNK) + D_MODEL // DOWN_CHUNK
    done = [0, 0]

    def after_dot():
        done[0] += 1
        while done[1] * n_dots < done[0] * n_inside:
            next(side_stages, None)
            done[1] += 1

    mixed = jnp.concatenate([og, osw], axis=1)
    x2 = x1_ref[...] + _dot(mixed, wo_ref[...])
    for _ in range(SIDE_STAGES_AT_NORM):
        next(side_stages, None)
    h = _rms(x2, n2_ref[...]).astype(BF16)
    _gate_up(h, wg_ref, wu_ref, act_ref, after_dot)
    x3 = x2 + 0.5 * _down(act_ref, slice(None), wd_ref, after_dot)
    for _ in side_stages:
        pass
    y_ref[...] = _rms(x3, nf_ref[...])


_HBM = pl.BlockSpec(memory_space=pl.ANY)
WIDE_CHUNK_ROWS = 128
NARROW_CHUNK_ROWS = 256
LOAD_SLOTS = 3


def _load_weights_bf16(loads, staging, sems):
    jobs = []
    for src, dst, segments in loads:
        cols = src.shape[1]
        assert cols in (D_FF, D_MODEL) and dst.shape[1] == cols
        kind, step = (0, WIDE_CHUNK_ROWS) if cols == D_FF else (1, NARROW_CHUNK_ROWS)
        for s0, d0, n in segments or [(0, 0, src.shape[0])]:
            jobs += [(src, dst, s0 + r, d0 + r, min(step, n - r), kind) for r in range(0, n, step)]

    used = [0, 0]
    copies = []
    for src, _, s0, _, nr, kind in jobs:
        slot = used[kind] % LOAD_SLOTS
        used[kind] += 1
        copies.append((pltpu.make_async_copy(src.at[pl.ds(s0, nr), :], staging[kind].at[slot, pl.ds(0, nr), :],
                                             sems.at[kind, slot]), slot))
    for copy, _ in copies[:LOAD_SLOTS - 1]:
        copy.start()
    for j, (_, dst, _, d0, nr, kind) in enumerate(jobs):
        ahead = j + LOAD_SLOTS - 1
        if ahead < len(jobs):
            copies[ahead][0].start()
        copy, slot = copies[j]
        copy.wait()
        dst[d0:d0 + nr, :] = staging[kind][slot, 0:nr, :].astype(BF16)


def _staging_scratch():
    return [pltpu.VMEM((LOAD_SLOTS, WIDE_CHUNK_ROWS, D_FF), F32),
            pltpu.VMEM((LOAD_SLOTS, NARROW_CHUNK_ROWS, D_MODEL), F32),
            pltpu.SemaphoreType.DMA((2, LOAD_SLOTS))]


def _chunk_plan(matrices):
    plan, first = [], 0
    for m, mat in enumerate(matrices):
        rows, cols = mat.shape
        assert cols in (D_FF, D_MODEL)
        kind, nr = (0, WIDE_CHUNK_ROWS) if cols == D_FF else (1, NARROW_CHUNK_ROWS)
        assert rows % nr == 0
        plan.append((m, first, rows // nr, nr, kind))
        first += rows // nr
    return plan, first


def _background_cast(step, srcs, dsts, staging, out_staging, sem_in, sem_out):
    plan, _ = _chunk_plan(srcs)
    slot = step % 2

    def rows_of(m, chunk):
        _, first, _, nr, _ = plan[m]
        return pl.ds(pl.multiple_of((chunk - first) * nr, nr), nr)

    def copy_in(m, chunk, slot_):
        kind = plan[m][4]
        return pltpu.make_async_copy(srcs[m].at[rows_of(m, chunk), :], staging[kind].at[slot_],
                                     sem_in.at[kind, slot_])

    def copy_out(m, chunk, slot_):
        kind = plan[m][4]
        return pltpu.make_async_copy(out_staging[kind].at[slot_], dsts[m].at[rows_of(m, chunk), :],
                                     sem_out.at[kind, slot_])

    def for_chunk(chunk, fn):
        for m, first, n, _, _ in plan:
            pl.when(jnp.logical_and(chunk >= first, chunk < first + n))(functools.partial(fn, m))

    def before():
        @pl.when(step == 0)
        def _():
            copy_in(0, 0, 0).start()
        for_chunk(step, lambda m: copy_in(m, step, slot).wait())
        for_chunk(step - 2, lambda m: copy_out(m, step - 2, slot).wait())
        for_chunk(step + 1, lambda m: copy_in(m, step + 1, 1 - slot).start())

    def cast():
        for kind in range(2):
            out_staging[kind][slot] = staging[kind][slot].astype(BF16)

    def after():
        for_chunk(step, lambda m: copy_out(m, step, slot).start())

    return before, cast, after


def _stage_a_body(xp_ref, xs_ref, n1_ref, wg_hbm, wu_hbm, wd_hbm, nm_ref, win_hbm, wgu_ref, bg_ref,
                  wo_hbm, wg2_hbm, wu2_hbm, wd2_hbm,
                  x1_ref, qk_ref, g_ref, v_ref, r_ref, qs_ref, kv_ref, wo_out, wg2_out, wu2_out, wd2_out,
                  act_ref, wg_ref, wu_ref, wd_ref, win_ref, wide_stage, narrow_stage, load_sems,
                  wide_out, narrow_out, sem_in, sem_out):
    i = pl.program_id(0)

    @pl.when(i == 0)
    def _():
        a0, a1 = PROJ_Q_S, PROJ_Q_S + GLA_GATE_RANK
        win_rows = [(0, 0, a0), (a1, a0, D_IN - a1), (a0, PROJ_A, GLA_GATE_RANK)]
        win_ref[PROJ_A + GLA_GATE_RANK:, :] = jnp.zeros((LANES - GLA_GATE_RANK, D_MODEL), BF16)
        _load_weights_bf16([(wg_hbm, wg_ref, None), (wu_hbm, wu_ref, None), (wd_hbm, wd_ref, None),
                            (win_hbm, win_ref, win_rows)], (wide_stage, narrow_stage), load_sems)

    bg_before, bg_cast, bg_after = _background_cast(
        i, (wo_hbm, wg2_hbm, wu2_hbm, wd2_hbm), (wo_out, wg2_out, wu2_out, wd2_out),
        (wide_stage, narrow_stage), (wide_out, narrow_out), sem_in, sem_out)
    bg_before()

    x = jnp.where(i < pl.num_programs(0) - 1, xp_ref[...], xs_ref[...])
    h = _rms(x, n1_ref[...]).astype(BF16)
    _gate_up(h, wg_ref, wu_ref, act_ref)
    bg_cast()

    halves = [slice(k * ROW_TILE // 2, (k + 1) * ROW_TILE // 2) for k in range(2)]
    x1 = []
    for rows in halves:
        x1.append(x[rows, :] + 0.5 * _down(act_ref, rows, wd_ref))
        x1_ref[rows, :] = x1[-1]
    h2 = [_rms(x1_h, nm_ref[...]).astype(BF16) for x1_h in x1]

    def gla_proj(k):
        proj = _dot_tb(h2[k], win_ref[PROJ_Q_G:PROJ_Q_S, :])
        qk_ref[halves[k], 0:GLA_KEY_WIDTH] = proj[:, PROJ_Q_G:PROJ_K_G] * HEAD_SCALE
        qk_ref[halves[k], GLA_KEY_WIDTH:] = proj[:, PROJ_K_G:PROJ_V_G]
        v_ref[halves[k], :] = proj[:, PROJ_V_G:PROJ_R_G].astype(BF16)
        r_ref[halves[k], :] = proj[:, PROJ_R_G:PROJ_Q_S]

    def swa_proj(k):
        proj = _dot_tb(h2[k], win_ref[PROJ_Q_S:PROJ_A, :])
        qs_ref[halves[k], :] = (proj[:, 0:SWA_WIDTH] * HEAD_SCALE).astype(BF16)
        kv_ref[halves[k], :] = proj[:, SWA_WIDTH:]

    def gate_in(k):
        return _dot_tb(h2[k], win_ref[PROJ_A:PROJ_WIDTH, :]).astype(BF16)

    def gate(k, a):
        z = _dot(a, wgu_ref[...]) + bg_ref[...]
        g_ref[halves[k], :] = jax.nn.log_sigmoid(z) * (1.0 / GLA_GATE_TAU)

    gla_proj(0)
    swa_proj(0)
    gate_ins = [gate_in(0), gate_in(1)]
    gla_proj(1)
    gate(0, gate_ins[0])
    gate(1, gate_ins[1])
    swa_proj(1)
    bg_after()


def _resident(shape):
    return pl.BlockSpec(shape, lambda i: (0,) * len(shape), pipeline_mode=pl.Buffered(1))


def _rows(tm, n):
    return pl.BlockSpec((tm, n), lambda i: (i, 0))


def _stage_a(xp, xs, w):
    t = xp.shape[0]
    assert t % ROW_TILE == 0 and xs.shape[0] == ROW_TILE
    nt = t // ROW_TILE
    out_widths = ((D_MODEL, F32), (2 * GLA_KEY_WIDTH, F32), (GLA_KEY_WIDTH, F32), (GLA_WIDTH, BF16),
                  (GLA_WIDTH, F32), (SWA_WIDTH, BF16), (2 * SWA_KV_WIDTH, F32))
    to_cast = (w["wo"], w["wg2"], w["wu2"], w["wd2"])
    assert _chunk_plan(to_cast)[1] + 2 <= nt + 1
    outs = pl.pallas_call(
        _stage_a_body,
        grid=(nt + 1,),
        in_specs=[pl.BlockSpec((ROW_TILE, D_MODEL), lambda i: (jnp.minimum(i, nt - 1), 0)),
                  _resident((ROW_TILE, D_MODEL)), _resident((1, D_MODEL)),
                  _HBM, _HBM, _HBM,
                  _resident((1, D_MODEL)), _HBM,
                  _resident((LANES, GLA_KEY_WIDTH)), _resident((1, GLA_KEY_WIDTH)),
                  _HBM, _HBM, _HBM, _HBM],
        out_specs=[_rows(ROW_TILE, n) for n, _ in out_widths] + [_HBM] * len(to_cast),
        out_shape=([jax.ShapeDtypeStruct((t + ROW_TILE, n), dt) for n, dt in out_widths]
                   + [jax.ShapeDtypeStruct(m.shape, BF16) for m in to_cast]),
        scratch_shapes=([pltpu.VMEM((ROW_TILE, D_FF), BF16), pltpu.VMEM((D_MODEL, D_FF), BF16),
                         pltpu.VMEM((D_MODEL, D_FF), BF16), pltpu.VMEM((D_FF, D_MODEL), BF16),
                         pltpu.VMEM((PROJ_WIDTH, D_MODEL), BF16)]
                        + _staging_scratch()
                        + [pltpu.VMEM((2, WIDE_CHUNK_ROWS, D_FF), BF16),
                           pltpu.VMEM((2, NARROW_CHUNK_ROWS, D_MODEL), BF16),
                           pltpu.SemaphoreType.DMA((2, 2)), pltpu.SemaphoreType.DMA((2, 2))]),
        compiler_params=pltpu.CompilerParams(dimension_semantics=("arbitrary",),
                                             vmem_limit_bytes=VMEM_LIMIT_BYTES),
        name="stage_a_ffn1_proj",
    )(xp, xs, w["n1"], w["wg1"], w["wu1"], w["wd1"], w["nm"], w["win"], w["wgu"], w["bg"], *to_cast)
    return outs[:len(out_widths)], outs[len(out_widths):]


def _cumsum_rows(x):
    n = x.shape[0]
    row = lax.broadcasted_iota(jnp.int32, (n, 1), 0)
    shift = 1
    while shift < n:
        x = x + jnp.where(row >= shift, pltpu.roll(x, shift, 0), 0.0)
        shift *= 2
    return x


def _alternate(a, b):
    pending = [iter(a), iter(b)]
    while pending:
        for it in list(pending):
            try:
                next(it)
            except StopIteration:
                pending.remove(it)
                continue
            yield


def _gla_stages(qk_ref, g_ref, v_ref, r_ref, gn_ref, s_ref, og_scr, cum_scr, inter_scr, flag_ref):
    c_len = GLA_CHUNK
    n_chunks = ROW_TILE // c_len
    causal_cat = (lax.broadcasted_iota(jnp.int32, (c_len, GLA_HEADS * c_len), 0)
                  >= lax.broadcasted_iota(jnp.int32, (c_len, GLA_HEADS * c_len), 1) % c_len)
    lane_head = lax.broadcasted_iota(jnp.int32, (1, GLA_KEY_WIDTH), 1) // GLA_HEAD_K
    row_head = lax.broadcasted_iota(jnp.int32, (GLA_KEY_WIDTH, 1), 0) // GLA_HEAD_K
    eye = (lax.broadcasted_iota(jnp.int32, (GLA_KEY_WIDTH, GLA_KEY_WIDTH), 0)
           == lax.broadcasted_iota(jnp.int32, (GLA_KEY_WIDTH, GLA_KEY_WIDTH), 1))
    zero_b = jnp.zeros((), BF16)
    zero_v = jnp.zeros((c_len, GLA_HEAD_V), BF16)
    worst = []

    def chunk(c):
        rows = slice(c * c_len, (c + 1) * c_len)
        cum = _cumsum_rows(g_ref[rows, :])
        cum_scr[rows, :] = cum
        last = cum[c_len - 1:c_len, :]
        q = qk_ref[rows, 0:GLA_KEY_WIDTH]
        k = qk_ref[rows, GLA_KEY_WIDTH:]
        vb = v_ref[rows, :]
        qe = (q * jnp.exp(cum)).astype(BF16)
        ke = (k * jnp.exp(jnp.minimum(-cum, DECAY_CLAMP))).astype(BF16)
        kl = (k * jnp.exp(last - cum)).astype(BF16)
        state = s_ref[...]
        sb = state.astype(BF16)
        s_bd = jnp.concatenate([jnp.where(row_head == h, sb, zero_b) for h in range(GLA_HEADS)], axis=1)
        o_inter = _dot(qe, s_bd)
        inter_scr[rows, :] = o_inter
        ke_bd = jnp.concatenate([jnp.where(lane_head == h, ke, zero_b) for h in range(GLA_HEADS)], axis=0)
        attn = _dot_tb(qe, ke_bd)
        upds = []
        for p in range(GLA_HEADS // 2):
            u = _dot_ta(kl[:, p * LANES:(p + 1) * LANES], vb[:, 2 * p * GLA_HEAD_V:(2 * p + 2) * GLA_HEAD_V])
            upds.append(u[0:GLA_HEAD_K, 0:GLA_HEAD_V])
            upds.append(u[GLA_HEAD_K:, GLA_HEAD_V:])
        yield
        attn = jnp.where(causal_cat, attn, 0.0).astype(BF16)
        o_pairs = []
        for p in range(GLA_HEADS // 2):
            v_a = vb[:, (2 * p) * GLA_HEAD_V:(2 * p + 1) * GLA_HEAD_V]
            v_b = vb[:, (2 * p + 1) * GLA_HEAD_V:(2 * p + 2) * GLA_HEAD_V]
            v_bd = jnp.concatenate([jnp.concatenate([v_a, zero_v], axis=1),
                                    jnp.concatenate([zero_v, v_b], axis=1)], axis=0)
            o_pairs.append(_dot(attn[:, 2 * p * c_len:(2 * p + 2) * c_len], v_bd))
        og_scr[rows, :] = _head_norm_gate(o_inter + jnp.concatenate(o_pairs, axis=1), r_ref[rows, :],
                                          gn_ref[...])
        last_col = jnp.sum(jnp.where(eye, last, 0.0), axis=1, keepdims=True)
        s_ref[...] = state * jnp.exp(last_col) + jnp.concatenate(upds, axis=0)
        worst.append(jnp.min(last, axis=1, keepdims=True))
        if c == n_chunks - 1:
            tile_min = functools.reduce(jnp.minimum, worst)
            flag_ref[0] = jnp.where(tile_min[0, 0] < -DECAY_CLAMP, 1, 0)
        yield

    return itertools.chain.from_iterable(chunk(c) for c in range(n_chunks))


def _gla_tile_exact(qk_ref, v_ref, r_ref, gn_ref, og_scr, o_scr, cum_scr, inter_scr):
    c_len = GLA_CHUNK
    ind = jnp.where(lax.broadcasted_iota(jnp.int32, (GLA_KEY_WIDTH, LANES), 0) // GLA_HEAD_K
                    == lax.broadcasted_iota(jnp.int32, (GLA_KEY_WIDTH, LANES), 1), 1.0, 0.0).astype(BF16)
    j_idx = lax.broadcasted_iota(jnp.int32, (c_len, 1), 0)
    for c in range(ROW_TILE // c_len):
        r0 = c * c_len
        rows = slice(r0, r0 + c_len)

        def one_row(i, carry, r0=r0, rows=rows):
            ci = cum_scr[pl.ds(r0 + i, 1), :]
            qi = qk_ref[pl.ds(r0 + i, 1), 0:GLA_KEY_WIDTH]
            kk = qk_ref[rows, GLA_KEY_WIDTH:]
            dec = jnp.exp(jnp.minimum(ci - cum_scr[rows, :], 0.0))
            a_cols = _dot(((qi * kk) * dec).astype(BF16), ind)
            outs = []
            for h in range(GLA_HEADS):
                w_col = jnp.where(j_idx <= i, a_cols[:, h:h + 1], 0.0)
                v_h = v_ref[rows, h * GLA_HEAD_V:(h + 1) * GLA_HEAD_V].astype(F32)
                outs.append(jnp.sum(w_col * v_h, axis=0, keepdims=True))
            o_scr[pl.ds(r0 + i, 1), :] = inter_scr[pl.ds(r0 + i, 1), :] + jnp.concatenate(outs, axis=1)
            return carry

        lax.fori_loop(0, c_len, one_row, 0)
    og_scr[...] = _head_norm_gate(o_scr[...], r_ref[...], gn_ref[...])


def _dup_halves(x):
    lo = lax.broadcasted_iota(jnp.int32, (1, LANES), 1) < SWA_HEAD_DIM
    sw = pltpu.roll(x, SWA_HEAD_DIM, 1)
    return jnp.where(lo, x, sw).astype(BF16), jnp.where(lo, sw, x).astype(BF16)


def _swa_stages(sink_ref, qs_ref, kv_ref, kvprev_scr, sn_ref, os_scr, no_prev_bias):
    w = WINDOW
    n_blocks = ROW_TILE // w
    lo = lax.broadcasted_iota(jnp.int32, (1, LANES), 1) < SWA_HEAD_DIM
    tri = (lax.broadcasted_iota(jnp.int32, (w, w), 1) <= lax.broadcasted_iota(jnp.int32, (w, w), 0))
    zero_b = jnp.zeros((), BF16)

    def block(b):
        rows = slice(b * w, (b + 1) * w)
        kv_prev = kvprev_scr[...] if b == 0 else kv_ref[(b - 1) * w:b * w, :]
        kv = jnp.concatenate([kv_prev, kv_ref[rows, :]], axis=0)
        kk = _dup_halves(kv[:, 0:SWA_KV_WIDTH])
        vv = _dup_halves(kv[:, SWA_KV_WIDTH:])
        q = qs_ref[rows, :]
        scores = []
        for grp in range(SWA_KV_HEADS):
            stacked = []
            for j in range(SWA_GROUP):
                hq = grp * SWA_GROUP + j
                tile = q[:, (hq // 2) * LANES:(hq // 2 + 1) * LANES]
                stacked.append(jnp.where(lo if hq % 2 == 0 else ~lo, tile, zero_b))
            scores.append(_dot_tb(jnp.concatenate(stacked, axis=0), kk[grp]))
        yield
        outs, dens = [], []
        for grp in range(SWA_KV_HEADS):
            s = scores[grp]
            probs = []
            for j in range(SWA_GROUP):
                sink = sink_ref[grp * SWA_GROUP + j]
                s_prev = s[j * w:(j + 1) * w, 0:w]
                if b == 0:
                    s_prev = s_prev + no_prev_bias
                sf = jnp.where(tri, s[j * w:(j + 1) * w, w:], s_prev)
                m = jnp.maximum(jnp.max(sf, axis=1, keepdims=True), sink)
                p = jnp.exp(sf - m)
                dens.append(jnp.sum(p, axis=1, keepdims=True) + jnp.exp(sink - m))
                pb = p.astype(BF16)
                probs.append(jnp.concatenate([jnp.where(tri, zero_b, pb), jnp.where(tri, pb, zero_b)], axis=1))
            outs.append(_dot(jnp.concatenate(probs, axis=0), vv[grp]))
        yield
        tiles = []
        for grp in range(SWA_KV_HEADS):
            res = [outs[grp][j * w:(j + 1) * w] / dens[grp * SWA_GROUP + j] for j in range(SWA_GROUP)]
            tiles.append(jnp.where(lo, res[0], res[1]))
            tiles.append(jnp.where(lo, res[2], res[3]))
        os_scr[rows, :] = _rms(jnp.concatenate(tiles, axis=1), sn_ref[...]).astype(BF16)
        if b == n_blocks - 1:
            kvprev_scr[...] = kv_ref[rows, :]
        yield

    return itertools.chain.from_iterable(block(b) for b in range(n_blocks))


def _mix_out_body(sink_ref, qk_ref, g_ref, v_ref, r_ref, qs_ref, kv_ref, x1_ref, ogs_ref, oss_ref,
                  gn_ref, sn_ref, wo_ref, n2_ref, wg_ref, wu_ref, wd_ref, nf_ref, yp_ref, ys_ref, s_ref,
                  og_scr, os_scr, kvprev_scr, o_scr, cum_scr, inter_scr, act_scr, flag_ref):
    i = pl.program_id(0)
    n_prompt = pl.num_programs(0) - 1

    def stage_d(og, osw, y_ref, side_stages=None):
        _out_ffn(x1_ref, og, osw, wo_ref, n2_ref, wg_ref, wu_ref, wd_ref, nf_ref, y_ref, act_scr,
                 side_stages)

    def mixer_stages():
        gla = _gla_stages(qk_ref, g_ref, v_ref, r_ref, gn_ref, s_ref, og_scr, cum_scr, inter_scr, flag_ref)
        swa = _swa_stages(sink_ref, qs_ref, kv_ref, kvprev_scr, sn_ref, os_scr,
                          jnp.where(i > 0, 0.0, -jnp.inf))
        return _alternate(gla, swa)

    @pl.when(i == 0)
    def _():
        s_ref[...] = jnp.zeros_like(s_ref)
        kvprev_scr[...] = jnp.zeros_like(kvprev_scr)
        stage_d(ogs_ref[...], oss_ref[...], ys_ref, mixer_stages())

    @pl.when(jnp.logical_and(i > 0, i < n_prompt))
    def _():
        stage_d(og_scr[...], os_scr[...], yp_ref, mixer_stages())

    @pl.when(i == n_prompt)
    def _():
        stage_d(og_scr[...], os_scr[...], yp_ref)
        flag_ref[0] = 0

    @pl.when(flag_ref[0] != 0)
    def _():
        _gla_tile_exact(qk_ref, v_ref, r_ref, gn_ref, og_scr, o_scr, cum_scr, inter_scr)


def _mix_out(qk, g, v, r, qs, kv, x1, og_s, os_s, w):
    t = qk.shape[0] - ROW_TILE
    assert t % ROW_TILE == 0 and og_s.shape[0] == ROW_TILE
    nt = t // ROW_TILE
    cur = lambda n: pl.BlockSpec((ROW_TILE, n), lambda i: (jnp.minimum(i, nt - 1), 0))
    return pl.pallas_call(
        _mix_out_body,
        grid=(nt + 1,),
        in_specs=[pl.BlockSpec(memory_space=pltpu.SMEM),
                  cur(2 * GLA_KEY_WIDTH), cur(GLA_KEY_WIDTH), cur(GLA_WIDTH), cur(GLA_WIDTH),
                  cur(SWA_WIDTH), cur(2 * SWA_KV_WIDTH),
                  pl.BlockSpec((ROW_TILE, D_MODEL), lambda i: (jnp.where(i == 0, nt, i - 1), 0)),
                  _resident((ROW_TILE, GLA_WIDTH)), _resident((ROW_TILE, SWA_WIDTH)),
                  _resident((1, GLA_WIDTH)), _resident((1, SWA_WIDTH)),
                  _resident((D_MODEL, D_MODEL)), _resident((1, D_MODEL)),
                  _resident((D_MODEL, D_FF)), _resident((D_MODEL, D_FF)), _resident((D_FF, D_MODEL)),
                  _resident((1, D_MODEL))],
        out_specs=[pl.BlockSpec((ROW_TILE, D_MODEL), lambda i: (jnp.clip(i - 1, 0, nt - 1), 0)),
                   pl.BlockSpec((ROW_TILE, D_MODEL), lambda i: (0, 0)),
                   pl.BlockSpec((GLA_KEY_WIDTH, GLA_HEAD_V), lambda i: (0, 0))],
        out_shape=[jax.ShapeDtypeStruct((t, D_MODEL), F32),
                   jax.ShapeDtypeStruct((ROW_TILE, D_MODEL), F32),
                   jax.ShapeDtypeStruct((GLA_KEY_WIDTH, GLA_HEAD_V), F32)],
        scratch_shapes=[pltpu.VMEM((ROW_TILE, GLA_WIDTH), BF16), pltpu.VMEM((ROW_TILE, SWA_WIDTH), BF16),
                        pltpu.VMEM((WINDOW, 2 * SWA_KV_WIDTH), F32),
                        pltpu.VMEM((ROW_TILE, GLA_WIDTH), F32), pltpu.VMEM((ROW_TILE, GLA_KEY_WIDTH), F32),
                        pltpu.VMEM((ROW_TILE, GLA_WIDTH), F32), pltpu.VMEM((ROW_TILE, D_FF), BF16),
                        pltpu.SMEM((1,), jnp.int32)],
        compiler_params=pltpu.CompilerParams(dimension_semantics=("arbitrary",),
                                             vmem_limit_bytes=VMEM_LIMIT_BYTES),
        name="prompt_mixers_stage_d",
    )(w["sinks"], qk, g, v, r, qs, kv, x1, og_s, os_s, w["gn"], w["sn"], w["wo"], w["n2"], w["wg2"],
      w["wu2"], w["wd2"], w["nf"])


def _gla_sample_body(qk_ref, g_ref, v_ref, r_ref, gn_ref, s_ref, og_ref, so_ref, o_scr):
    rows = SAMPLE_ROWS
    g = g_ref[...]
    q = qk_ref[:, 0:GLA_KEY_WIDTH]
    k = qk_ref[:, GLA_KEY_WIDTH:]
    vb = v_ref[...]
    vf = vb.astype(F32)
    tok = lax.broadcasted_iota(jnp.int32, (rows, 1), 0) % DEC_SEQ

    cum = g
    for d in range(1, DEC_SEQ):
        cum = cum + jnp.where(tok >= d, pltpu.roll(g, d, 0), 0.0)
    tot = jnp.where(tok == DEC_SEQ - 1, cum, 0.0)
    for d in range(1, DEC_SEQ):
        tot = tot + jnp.where(tok == DEC_SEQ - 1 - d, pltpu.roll(cum, rows - d, 0), 0.0)

    qe = q * jnp.exp(cum)
    kl = k * jnp.exp(tot - cum)
    decay_t = jnp.exp(tot).T

    ind = jnp.where(lax.broadcasted_iota(jnp.int32, (GLA_KEY_WIDTH, LANES), 0) // GLA_HEAD_K
                    == lax.broadcasted_iota(jnp.int32, (GLA_KEY_WIDTH, LANES), 1), 1.0, 0.0).astype(BF16)
    expand = jnp.where(lax.broadcasted_iota(jnp.int32, (LANES, GLA_WIDTH), 0)
                       == lax.broadcasted_iota(jnp.int32, (LANES, GLA_WIDTH), 1) // GLA_HEAD_V,
                       1.0, 0.0).astype(BF16)

    o_intra = jnp.zeros((rows, GLA_WIDTH), F32)
    for d in range(DEC_SEQ):
        k_d = k if d == 0 else pltpu.roll(k, d, 0)
        c_d = cum if d == 0 else pltpu.roll(cum, d, 0)
        v_d = vf if d == 0 else pltpu.roll(vf, d, 0)
        pair = jnp.where(tok >= d, q * k_d * jnp.exp(jnp.minimum(cum - c_d, 0.0)), 0.0)
        a = _dot(pair.astype(BF16), ind)
        o_intra = o_intra + _dot(a.astype(BF16), expand) * v_d

    lane_head = lax.broadcasted_iota(jnp.int32, (1, GLA_KEY_WIDTH), 1) // GLA_HEAD_K
    row8 = lax.broadcasted_iota(jnp.int32, (SUBLANES, 1), 0)
    row32 = lax.broadcasted_iota(jnp.int32, (GLA_HEADS * SUBLANES, 1), 0)
    for pair_idx in range(SAMPLE_SEQS // 2):
        r8 = slice(pair_idx * SUBLANES, (pair_idx + 1) * SUBLANES)
        q8 = qe[r8, :]
        lhs = jnp.concatenate([jnp.where(lane_head == h, q8, 0.0) for h in range(GLA_HEADS)],
                              axis=0).astype(BF16)
        kl8 = kl[r8, :]
        v8 = vb[r8, :]
        res = []
        for s in range(2):
            b = 2 * pair_idx + s
            state = s_ref[b]
            res.append(_dot(lhs, state.astype(BF16)))
            kl_b = jnp.where(row8 // DEC_SEQ == s, kl8, 0.0).astype(BF16)
            upd = _dot_ta(kl_b, v8)
            upd = jnp.concatenate(
                [upd[h * GLA_HEAD_K:(h + 1) * GLA_HEAD_K, h * GLA_HEAD_V:(h + 1) * GLA_HEAD_V]
                 for h in range(GLA_HEADS)], axis=0)
            so_ref[b] = state * decay_t[:, DEC_SEQ * b:DEC_SEQ * b + 1] + upd
        sel = jnp.where(row32 % SUBLANES < DEC_SEQ, res[0], res[1])
        o_scr[r8, :] = jnp.concatenate([sel[h * SUBLANES:(h + 1) * SUBLANES] for h in range(GLA_HEADS)], axis=1)

    og_ref[...] = _head_norm_gate(o_scr[...] + o_intra, r_ref[...], gn_ref[...])


def _gla_sample(qk, g, v, r, gn, state, row0):
    nseq = state.shape[0]
    rows = nseq * DEC_SEQ
    assert nseq % SAMPLE_SEQS == 0 and row0 % SAMPLE_ROWS == 0
    blk0 = row0 // SAMPLE_ROWS
    src = lambda n: pl.BlockSpec((SAMPLE_ROWS, n), lambda i: (i + blk0, 0))
    blk = lambda n: pl.BlockSpec((SAMPLE_ROWS, n), lambda i: (i, 0))
    sblk = pl.BlockSpec((SAMPLE_SEQS, GLA_KEY_WIDTH, GLA_HEAD_V), lambda i: (i, 0, 0))
    return pl.pallas_call(
        _gla_sample_body,
        grid=(nseq // SAMPLE_SEQS,),
        in_specs=[src(2 * GLA_KEY_WIDTH), src(GLA_KEY_WIDTH), src(GLA_WIDTH), src(GLA_WIDTH),
                  pl.BlockSpec((1, GLA_WIDTH), lambda i: (0, 0)), sblk],
        out_specs=[blk(GLA_WIDTH), sblk],
        out_shape=[jax.ShapeDtypeStruct((rows, GLA_WIDTH), BF16),
                   jax.ShapeDtypeStruct((nseq, GLA_KEY_WIDTH, GLA_HEAD_V), F32)],
        scratch_shapes=[pltpu.VMEM((SAMPLE_ROWS, GLA_WIDTH), F32)],
        compiler_params=pltpu.CompilerParams(dimension_semantics=("arbitrary",)),
        name="gla_sample",
    )(qk, g, v, r, gn, state)


def _swa_sample_body(sink_ref, q_ref, kvn_ref, kc_ref, vc_ref, nrm_ref, o_ref, kco_ref, vco_ref,
                     lhs_scr, sc_scr, pc_scr, oc_scr):
    rows = SAMPLE_ROWS
    q = q_ref[...].astype(F32)
    lo = lax.broadcasted_iota(jnp.int32, (1, LANES), 1) < SWA_HEAD_DIM
    for hq in range(SWA_Q_HEADS):
        grp = hq // SWA_GROUP
        tile = q[:, (hq // 2) * LANES:(hq // 2 + 1) * LANES]
        src = tile if hq % 2 == grp else pltpu.roll(tile, SWA_HEAD_DIM, 1)
        lhs_scr[hq * rows:(hq + 1) * rows, :] = jnp.where(lo if grp == 0 else ~lo, src, 0.0)

    kvn = kvn_ref[...]
    k_new = kvn[:, 0:SWA_KV_WIDTH]
    v_new = kvn[:, SWA_KV_WIDTH:]
    s_new = _dot_tb(lhs_scr[...].astype(BF16), k_new.astype(BF16))

    row64 = lax.broadcasted_iota(jnp.int32, (SWA_Q_HEADS * SUBLANES, 1), 0)
    first_of_pair = row64 % SUBLANES < DEC_SEQ

    def gather_pair(ref, pair_idx):
        return jnp.concatenate(
            [ref[hq * rows + pair_idx * SUBLANES:hq * rows + (pair_idx + 1) * SUBLANES, :]
             for hq in range(SWA_Q_HEADS)], axis=0).astype(BF16)

    def scatter_pair(ref, pair_idx, val):
        for hq in range(SWA_Q_HEADS):
            ref[hq * rows + pair_idx * SUBLANES:hq * rows + (pair_idx + 1) * SUBLANES, :] = (
                val[hq * SUBLANES:(hq + 1) * SUBLANES])

    for pair_idx in range(SAMPLE_SEQS // 2):
        l64 = gather_pair(lhs_scr, pair_idx)
        sa = _dot(l64, kc_ref[2 * pair_idx].astype(BF16))
        sb = _dot(l64, kc_ref[2 * pair_idx + 1].astype(BF16))
        scatter_pair(sc_scr, pair_idx, jnp.where(first_of_pair, sa, sb))

    rr = lax.broadcasted_iota(jnp.int32, (rows, rows), 0)
    cc = lax.broadcasted_iota(jnp.int32, (rows, rows), 1)
    tok = rr % DEC_SEQ
    mask_cache = cc > tok
    mask_new = (cc // DEC_SEQ == rr // DEC_SEQ) & (cc % DEC_SEQ <= tok)
    p_new, dens = [], []
    for hq in range(SWA_Q_HEADS):
        sl = slice(hq * rows, (hq + 1) * rows)
        sink = sink_ref[hq]
        s_c = jnp.where(mask_cache, sc_scr[sl, :], -jnp.inf)
        s_n = jnp.where(mask_new, s_new[sl, :], -jnp.inf)
        m = jnp.maximum(jnp.maximum(jnp.max(s_c, axis=1, keepdims=True),
                                    jnp.max(s_n, axis=1, keepdims=True)), sink)
        p_c = jnp.exp(s_c - m)
        p_n = jnp.exp(s_n - m)
        dens.append(jnp.sum(p_c, axis=1, keepdims=True) + jnp.sum(p_n, axis=1, keepdims=True)
                    + jnp.exp(sink - m))
        pc_scr[sl, :] = p_c
        p_new.append(p_n.astype(BF16))
    o_new = _dot(jnp.concatenate(p_new, axis=0), v_new.astype(BF16))

    for pair_idx in range(SAMPLE_SEQS // 2):
        p64 = gather_pair(pc_scr, pair_idx)
        oa = _dot_tb(p64, vc_ref[2 * pair_idx].astype(BF16))
        ob = _dot_tb(p64, vc_ref[2 * pair_idx + 1].astype(BF16))
        scatter_pair(oc_scr, pair_idx, jnp.where(first_of_pair, oa, ob))

    tiles = []
    for i in range(SWA_Q_HEADS // 2):
        halves = []
        for hq in (2 * i, 2 * i + 1):
            sl = slice(hq * rows, (hq + 1) * rows)
            oh = (oc_scr[sl, :] + o_new[sl, :]) / dens[hq]
            halves.append(oh if hq % 2 == hq // SWA_GROUP else pltpu.roll(oh, SWA_HEAD_DIM, 1))
        tiles.append(jnp.where(lo, halves[0], halves[1]))
    o_ref[...] = _rms(jnp.concatenate(tiles, axis=1), nrm_ref[...]).astype(BF16)

    keep = lax.broadcasted_iota(jnp.int32, (1, WINDOW), 1) < WINDOW - DEC_SEQ
    k_new_t = k_new.T
    v_new_t = v_new.T
    for b in range(SAMPLE_SEQS):
        to_tail = (WINDOW - DEC_SEQ - DEC_SEQ * b) % rows
        kco_ref[b] = jnp.where(keep, pltpu.roll(kc_ref[b], WINDOW - DEC_SEQ, 1),
                               pltpu.roll(k_new_t, to_tail, 1))
        vco_ref[b] = jnp.where(keep, pltpu.roll(vc_ref[b], WINDOW - DEC_SEQ, 1),
                               pltpu.roll(v_new_t, to_tail, 1))


def _swa_sample(sinks, qs, kvn, kc, vc, nrm, row0):
    nseq = kc.shape[0]
    rows = nseq * DEC_SEQ
    assert nseq % SAMPLE_SEQS == 0 and row0 % SAMPLE_ROWS == 0
    assert kc.shape[1:] == (SWA_KV_WIDTH, WINDOW) and WINDOW == SAMPLE_ROWS
    blk0 = row0 // SAMPLE_ROWS
    cblk = pl.BlockSpec((SAMPLE_SEQS, SWA_KV_WIDTH, WINDOW), lambda i: (i, 0, 0))
    big = pltpu.VMEM((SWA_Q_HEADS * SAMPLE_ROWS, LANES), F32)
    return pl.pallas_call(
        _swa_sample_body,
        grid=(nseq // SAMPLE_SEQS,),
        in_specs=[pl.BlockSpec(memory_space=pltpu.SMEM),
                  pl.BlockSpec((SAMPLE_ROWS, SWA_WIDTH), lambda i: (i + blk0, 0)),
                  pl.BlockSpec((SAMPLE_ROWS, 2 * SWA_KV_WIDTH), lambda i: (i + blk0, 0)),
                  cblk, cblk,
                  pl.BlockSpec((1, SWA_WIDTH), lambda i: (0, 0))],
        out_specs=[pl.BlockSpec((SAMPLE_ROWS, SWA_WIDTH), lambda i: (i, 0)), cblk, cblk],
        out_shape=[jax.ShapeDtypeStruct((rows, SWA_WIDTH), BF16),
                   jax.ShapeDtypeStruct(kc.shape, F32), jax.ShapeDtypeStruct(vc.shape, F32)],
        scratch_shapes=[big, big, big, big],
        compiler_params=pltpu.CompilerParams(dimension_semantics=("arbitrary",)),
        name="swa_sample",
    )(sinks, qs, kvn, kc, vc, nrm)


def _prepare_weights(ffn1_norm, ffn1_w_gate, ffn1_w_up, ffn1_w_down, mix_norm, w_in, w_gate_up, b_gate,
                     gla_head_norm, swa_out_norm, swa_sinks, w_out,
                     ffn2_norm, ffn2_w_gate, ffn2_w_up, ffn2_w_down, final_norm, layer):
    assert w_in.shape[1:] == (D_MODEL, D_IN) and PROJ_Q_S == 2 * GLA_KEY_WIDTH + 2 * GLA_WIDTH
    wgu = jnp.concatenate([w_gate_up[layer],
                           jnp.zeros((LANES - GLA_GATE_RANK, GLA_KEY_WIDTH), w_gate_up.dtype)], axis=0)
    row = lambda a: a.reshape(1, -1).astype(F32)
    return dict(
        n1=row(ffn1_norm[layer]), wg1=ffn1_w_gate[layer], wu1=ffn1_w_up[layer], wd1=ffn1_w_down[layer],
        nm=row(mix_norm[layer]), win=w_in[layer].T, wgu=wgu.astype(BF16), bg=row(b_gate[layer]),
        gn=row(gla_head_norm[layer]), sn=row(swa_out_norm[layer]), sinks=swa_sinks[layer].astype(F32),
        wo=w_out[layer], n2=row(ffn2_norm[layer]), wg2=ffn2_w_gate[layer], wu2=ffn2_w_up[layer],
        wd2=ffn2_w_down[layer], nf=row(final_norm))


def kernel(x_prompt, x_sample, state_gla, cache_swa_k, cache_swa_v, ffn1_norm, ffn1_w_gate, ffn1_w_up,
           ffn1_w_down, mix_norm, w_in, w_gate_up, b_gate, gla_head_norm, swa_out_norm, swa_sinks, w_out,
           ffn2_norm, ffn2_w_gate, ffn2_w_up, ffn2_w_down, final_norm):
    depth = state_gla.shape[0]
    assert depth == 1 and x_prompt.shape[0] == 1 and x_sample.shape[1] == DEC_SEQ
    seq = x_prompt.shape[1]
    nseq = x_sample.shape[0]
    w = _prepare_weights(ffn1_norm, ffn1_w_gate, ffn1_w_up, ffn1_w_down, mix_norm, w_in, w_gate_up, b_gate,
                         gla_head_norm, swa_out_norm, swa_sinks, w_out,
                         ffn2_norm, ffn2_w_gate, ffn2_w_up, ffn2_w_down, final_norm, 0)

    (x1, qk, g, v, r, qs, kv), (wo, wg2, wu2, wd2) = _stage_a(
        x_prompt.reshape(seq, D_MODEL), x_sample.reshape(nseq * DEC_SEQ, D_MODEL), w)
    w = dict(w, wo=wo, wg2=wg2, wu2=wu2, wd2=wd2)

    og_s, state_s = _gla_sample(qk, g, v, r, w["gn"],
                                state_gla[0].reshape(nseq, GLA_KEY_WIDTH, GLA_HEAD_V), seq)
    cache_w = cache_swa_k.shape[2]
    to_feature_major = lambda c: jnp.transpose(c[0], (0, 2, 3, 1)).reshape(nseq, SWA_KV_WIDTH, cache_w)
    from_feature_major = lambda c: jnp.transpose(
        c.reshape(nseq, SWA_KV_HEADS, SWA_HEAD_DIM, cache_w), (0, 3, 1, 2))[None]
    os_s, k_cache_s, v_cache_s = _swa_sample(
        w["sinks"], qs, kv, to_feature_major(cache_swa_k), to_feature_major(cache_swa_v), w["sn"], seq)
    k_cache_s = from_feature_major(k_cache_s)
    v_cache_s = from_feature_major(v_cache_s)

    y_prompt, y_sample, state_p = _mix_out(qk, g, v, r, qs, kv, x1, og_s, os_s, w)
    y_prompt = y_prompt.reshape(1, seq, D_MODEL)
    y_sample = y_sample.reshape(nseq, DEC_SEQ, D_MODEL)
    cw = min(WINDOW, seq)
    k_cache_p = kv[seq - cw:seq, 0:SWA_KV_WIDTH].reshape(1, 1, cw, SWA_KV_HEADS, SWA_HEAD_DIM)
    v_cache_p = kv[seq - cw:seq, SWA_KV_WIDTH:].reshape(1, 1, cw, SWA_KV_HEADS, SWA_HEAD_DIM)
    state_p = state_p.reshape(1, 1, GLA_HEADS, GLA_HEAD_K, GLA_HEAD_V)
    state_s = state_s.reshape(1, nseq, GLA_HEADS, GLA_HEAD_K, GLA_HEAD_V)

    return (y_prompt, y_sample, state_p, k_cache_p, v_cache_p, state_s, k_cache_s, v_cache_s)
```

```python
import functools
import itertools

import jax
import jax.numpy as jnp
from jax import lax
from jax.experimental import pallas as pl
from jax.experimental.pallas import tpu as pltpu

F32 = jnp.float32
BF16 = jnp.bfloat16

D_MODEL = 1024
D_FF = 2816
GLA_HEADS = 4
GLA_HEAD_K = 64
GLA_HEAD_V = 128
GLA_KEY_WIDTH = GLA_HEADS * GLA_HEAD_K
GLA_WIDTH = GLA_HEADS * GLA_HEAD_V
GLA_GATE_RANK = 16
GLA_GATE_TAU = 16.0
SWA_HEAD_DIM = 64
SWA_Q_HEADS = 8
SWA_KV_HEADS = 2
SWA_GROUP = SWA_Q_HEADS // SWA_KV_HEADS
SWA_WIDTH = SWA_Q_HEADS * SWA_HEAD_DIM
SWA_KV_WIDTH = SWA_KV_HEADS * SWA_HEAD_DIM
WINDOW = 128
DEC_SEQ = 4
NORM_EPS = 1e-6
HEAD_SCALE = 0.125

LANES = 128
SUBLANES = 8
VMEM_LIMIT_BYTES = 56 * 1024 * 1024

ROW_TILE = 512
FF_CHUNK = 256
DOWN_CHUNK = 256
GLA_CHUNK = 128
SAMPLE_SEQS = 32
SAMPLE_ROWS = SAMPLE_SEQS * DEC_SEQ
MIXER_STAGES = 2 * (ROW_TILE // GLA_CHUNK) + 3 * (ROW_TILE // WINDOW)
SIDE_STAGES_AT_NORM = 2
DECAY_CLAMP = 60.0

PROJ_Q_G = 0
PROJ_K_G = PROJ_Q_G + GLA_KEY_WIDTH
PROJ_V_G = PROJ_K_G + GLA_KEY_WIDTH
PROJ_R_G = PROJ_V_G + GLA_WIDTH
PROJ_Q_S = PROJ_R_G + GLA_WIDTH
PROJ_KV_S = PROJ_Q_S + SWA_WIDTH
PROJ_A = PROJ_KV_S + 2 * SWA_KV_WIDTH
PROJ_WIDTH = PROJ_A + LANES
D_IN = PROJ_A + GLA_GATE_RANK


def _dot(a, b):
    return jnp.dot(a, b, preferred_element_type=F32)


def _dot_tb(a, b):
    return lax.dot_general(a, b, (((1,), (1,)), ((), ())), preferred_element_type=F32)


def _dot_ta(a, b):
    return lax.dot_general(a, b, (((0,), (0,)), ((), ())), preferred_element_type=F32)


def _rms(x, g):
    return x * lax.rsqrt(jnp.mean(x * x, axis=-1, keepdims=True) + NORM_EPS) * g


def _gate_up(h, wg_ref, wu_ref, act_ref, after_dot=lambda: None):
    for c0 in range(0, D_FF, FF_CHUNK):
        g = _dot(h, wg_ref[:, c0:c0 + FF_CHUNK])
        after_dot()
        u = _dot(h, wu_ref[:, c0:c0 + FF_CHUNK])
        after_dot()
        act_ref[:, c0:c0 + FF_CHUNK] = (g * jax.nn.sigmoid(g) * u).astype(BF16)


def _down(act_ref, rows, wd_ref, after_dot=lambda: None):
    outs = []
    for n0 in range(0, D_MODEL, DOWN_CHUNK):
        outs.append(_dot(act_ref[rows, :], wd_ref[:, n0:n0 + DOWN_CHUNK]))
        after_dot()
    return jnp.concatenate(outs, axis=1)


def _head_norm_gate(o, r, gn):
    parts = []
    for h in range(GLA_HEADS):
        sl = slice(h * GLA_HEAD_V, (h + 1) * GLA_HEAD_V)
        parts.append(_rms(o[:, sl], gn[:, sl]))
    return (jnp.concatenate(parts, axis=1) * (r * jax.nn.sigmoid(r))).astype(BF16)


def _out_ffn(x1_ref, og, osw, wo_ref, n2_ref, wg_ref, wu_ref, wd_ref, nf_ref, y_ref, act_ref, side_stages=None):
    n_inside = 0 if side_stages is None else MIXER_STAGES - 2 * SIDE_STAGES_AT_NORM
    side_stages = iter(()) if side_stages is None else side_stages
    n_dots = 2 * (D_FF // FF_CHUNK) + D_MODEL // DOWN_CHUNK
    done = [0, 0]

    def after_dot():
        done[0] += 1
        while done[1] * n_dots < done[0] * n_inside:
            next(side_stages, None)
            done[1] += 1

    mixed = jnp.concatenate([og, osw], axis=1)
    x2 = x1_ref[...] + _dot(mixed, wo_ref[...])
    for _ in range(SIDE_STAGES_AT_NORM):
        next(side_stages, None)
    h = _rms(x2, n2_ref[...]).astype(BF16)
    _gate_up(h, wg_ref, wu_ref, act_ref, after_dot)
    x3 = x2 + 0.5 * _down(act_ref, slice(None), wd_ref, after_dot)
    for _ in side_stages:
        pass
    y_ref[...] = _rms(x3, nf_ref[...])


_HBM = pl.BlockSpec(memory_space=pl.ANY)
WIDE_CHUNK_ROWS = 128
NARROW_CHUNK_ROWS = 256
LOAD_SLOTS = 3


def _load_weights_bf16(loads, staging, sems):
    jobs = []
    for src, dst, segments, transposed in loads:
        cols = src.shape[1]
        assert cols in (D_FF, D_MODEL) and dst.shape[0 if transposed else 1] == cols
        kind, step = (0, WIDE_CHUNK_ROWS) if cols == D_FF else (1, NARROW_CHUNK_ROWS)
        for s0, d0, n in segments or [(0, 0, src.shape[0])]:
            jobs += [(src, dst, s0 + r, d0 + r, min(step, n - r), kind, transposed)
                     for r in range(0, n, step)]

    used = [0, 0]
    copies = []
    for src, _, s0, _, nr, kind, _ in jobs:
        slot = used[kind] % LOAD_SLOTS
        used[kind] += 1
        copies.append((pltpu.make_async_copy(src.at[pl.ds(s0, nr), :], staging[kind].at[slot, pl.ds(0, nr), :],
                                             sems.at[kind, slot]), slot))
    for copy, _ in copies[:LOAD_SLOTS - 1]:
        copy.start()
    for j, (_, dst, _, d0, nr, kind, transposed) in enumerate(jobs):
        ahead = j + LOAD_SLOTS - 1
        if ahead < len(jobs):
            copies[ahead][0].start()
        copy, slot = copies[j]
        copy.wait()
        if not transposed:
            dst[d0:d0 + nr, :] = staging[kind][slot, 0:nr, :].astype(BF16)
        elif nr % LANES == 0:
            dst[:, d0:d0 + nr] = staging[kind][slot, 0:nr, :].T.astype(BF16)
        else:
            tile = staging[kind][slot, 0:LANES, :].T
            live = lax.broadcasted_iota(jnp.int32, (1, LANES), 1) < nr
            dst[:, d0:d0 + LANES] = jnp.where(live, tile, 0.0).astype(BF16)


def _staging_scratch():
    return [pltpu.VMEM((LOAD_SLOTS, WIDE_CHUNK_ROWS, D_FF), F32),
            pltpu.VMEM((LOAD_SLOTS, NARROW_CHUNK_ROWS, D_MODEL), F32),
            pltpu.SemaphoreType.DMA((2, LOAD_SLOTS))]


def _chunk_plan(matrices):
    plan, first = [], 0
    for m, mat in enumerate(matrices):
        rows, cols = mat.shape
        assert cols in (D_FF, D_MODEL)
        kind, nr = (0, WIDE_CHUNK_ROWS) if cols == D_FF else (1, NARROW_CHUNK_ROWS)
        assert rows % nr == 0
        plan.append((m, first, rows // nr, nr, kind))
        first += rows // nr
    return plan, first


def _background_cast(step, srcs, dsts, staging, out_staging, sem_in, sem_out):
    plan, _ = _chunk_plan(srcs)
    slot = step % 2

    def rows_of(m, chunk):
        _, first, _, nr, _ = plan[m]
        return pl.ds(pl.multiple_of((chunk - first) * nr, nr), nr)

    def copy_in(m, chunk, slot_):
        kind = plan[m][4]
        return pltpu.make_async_copy(srcs[m].at[rows_of(m, chunk), :], staging[kind].at[slot_],
                                     sem_in.at[kind, slot_])

    def copy_out(m, chunk, slot_):
        kind = plan[m][4]
        return pltpu.make_async_copy(out_staging[kind].at[slot_], dsts[m].at[rows_of(m, chunk), :],
                                     sem_out.at[kind, slot_])

    def for_chunk(chunk, fn):
        for m, first, n, _, _ in plan:
            pl.when(jnp.logical_and(chunk >= first, chunk < first + n))(functools.partial(fn, m))

    def before():
        @pl.when(step == 0)
        def _():
            copy_in(0, 0, 0).start()
        for_chunk(step, lambda m: copy_in(m, step, slot).wait())
        for_chunk(step - 2, lambda m: copy_out(m, step - 2, slot).wait())
        for_chunk(step + 1, lambda m: copy_in(m, step + 1, 1 - slot).start())

    def cast():
        for kind in range(2):
            out_staging[kind][slot] = staging[kind][slot].astype(BF16)

    def after():
        for_chunk(step, lambda m: copy_out(m, step, slot).start())

    return before, cast, after


def _stage_a_body(xp_ref, xs_ref, n1_ref, wg_hbm, wu_hbm, wd_hbm, nm_ref, win_hbm, wgu_ref, bg_ref,
                  wo_hbm, wg2_hbm, wu2_hbm, wd2_hbm,
                  x1_ref, qk_ref, g_ref, v_ref, r_ref, qs_ref, kv_ref, wo_out, wg2_out, wu2_out, wd2_out,
                  act_ref, wg_ref, wu_ref, wd_ref, win_ref, wide_stage, narrow_stage, load_sems,
                  wide_out, narrow_out, sem_in, sem_out):
    i = pl.program_id(0)

    @pl.when(i == 0)
    def _():
        a0, a1 = PROJ_Q_S, PROJ_Q_S + GLA_GATE_RANK
        win_rows = [(0, 0, a0), (a1, a0, D_IN - a1), (a0, PROJ_A, GLA_GATE_RANK)]
        _load_weights_bf16([(wg_hbm, wg_ref, None, False), (wu_hbm, wu_ref, None, False),
                            (wd_hbm, wd_ref, None, False), (win_hbm, win_ref, win_rows, True)],
                           (wide_stage, narrow_stage), load_sems)

    bg_before, bg_cast, bg_after = _background_cast(
        i, (wo_hbm, wg2_hbm, wu2_hbm, wd2_hbm), (wo_out, wg2_out, wu2_out, wd2_out),
        (wide_stage, narrow_stage), (wide_out, narrow_out), sem_in, sem_out)
    bg_before()

    x = jnp.where(i < pl.num_programs(0) - 1, xp_ref[...], xs_ref[...])
    h = _rms(x, n1_ref[...]).astype(BF16)
    _gate_up(h, wg_ref, wu_ref, act_ref)
    bg_cast()

    halves = [slice(k * ROW_TILE // 2, (k + 1) * ROW_TILE // 2) for k in range(2)]
    x1 = []
    for rows in halves:
        x1.append(x[rows, :] + 0.5 * _down(act_ref, rows, wd_ref))
        x1_ref[rows, :] = x1[-1]
    gate_in = []
    for rows, x1_h in zip(halves, x1):
        h2 = _rms(x1_h, nm_ref[...]).astype(BF16)
        proj = _dot(h2, win_ref[...])
        qk_ref[rows, 0:GLA_KEY_WIDTH] = proj[:, PROJ_Q_G:PROJ_K_G] * HEAD_SCALE
        qk_ref[rows, GLA_KEY_WIDTH:] = proj[:, PROJ_K_G:PROJ_V_G]
        v_ref[rows, :] = proj[:, PROJ_V_G:PROJ_R_G].astype(BF16)
        r_ref[rows, :] = proj[:, PROJ_R_G:PROJ_Q_S]
        qs_ref[rows, :] = (proj[:, PROJ_Q_S:PROJ_KV_S] * HEAD_SCALE).astype(BF16)
        kv_ref[rows, :] = proj[:, PROJ_KV_S:PROJ_A]
        gate_in.append(proj[:, PROJ_A:PROJ_WIDTH].astype(BF16))
    for rows, a in zip(halves, gate_in):
        z = _dot(a, wgu_ref[...]) + bg_ref[...]
        g_ref[rows, :] = jax.nn.log_sigmoid(z) * (1.0 / GLA_GATE_TAU)
    bg_after()


def _resident(shape):
    return pl.BlockSpec(shape, lambda i: (0,) * len(shape), pipeline_mode=pl.Buffered(1))


def _rows(tm, n):
    return pl.BlockSpec((tm, n), lambda i: (i, 0))


def _stage_a(xp, xs, w):
    t = xp.shape[0]
    assert t % ROW_TILE == 0 and xs.shape[0] == ROW_TILE
    nt = t // ROW_TILE
    out_widths = ((D_MODEL, F32), (2 * GLA_KEY_WIDTH, F32), (GLA_KEY_WIDTH, F32), (GLA_WIDTH, BF16),
                  (GLA_WIDTH, F32), (SWA_WIDTH, BF16), (2 * SWA_KV_WIDTH, F32))
    to_cast = (w["wo"], w["wg2"], w["wu2"], w["wd2"])
    assert _chunk_plan(to_cast)[1] + 2 <= nt + 1
    outs = pl.pallas_call(
        _stage_a_body,
        grid=(nt + 1,),
        in_specs=[pl.BlockSpec((ROW_TILE, D_MODEL), lambda i: (jnp.minimum(i, nt - 1), 0)),
                  _resident((ROW_TILE, D_MODEL)), _resident((1, D_MODEL)),
                  _HBM, _HBM, _HBM,
                  _resident((1, D_MODEL)), _HBM,
                  _resident((LANES, GLA_KEY_WIDTH)), _resident((1, GLA_KEY_WIDTH)),
                  _HBM, _HBM, _HBM, _HBM],
        out_specs=[_rows(ROW_TILE, n) for n, _ in out_widths] + [_HBM] * len(to_cast),
        out_shape=([jax.ShapeDtypeStruct((t + ROW_TILE, n), dt) for n, dt in out_widths]
                   + [jax.ShapeDtypeStruct(m.shape, BF16) for m in to_cast]),
        scratch_shapes=([pltpu.VMEM((ROW_TILE, D_FF), BF16), pltpu.VMEM((D_MODEL, D_FF), BF16),
                         pltpu.VMEM((D_MODEL, D_FF), BF16), pltpu.VMEM((D_FF, D_MODEL), BF16),
                         pltpu.VMEM((D_MODEL, PROJ_WIDTH), BF16)]
                        + _staging_scratch()
                        + [pltpu.VMEM((2, WIDE_CHUNK_ROWS, D_FF), BF16),
                           pltpu.VMEM((2, NARROW_CHUNK_ROWS, D_MODEL), BF16),
                           pltpu.SemaphoreType.DMA((2, 2)), pltpu.SemaphoreType.DMA((2, 2))]),
        compiler_params=pltpu.CompilerParams(dimension_semantics=("arbitrary",),
                                             vmem_limit_bytes=VMEM_LIMIT_BYTES),
        name="stage_a_ffn1_proj",
    )(xp, xs, w["n1"], w["wg1"], w["wu1"], w["wd1"], w["nm"], w["win"], w["wgu"], w["bg"], *to_cast)
    return outs[:len(out_widths)], outs[len(out_widths):]


def _cumsum_rows(x):
    n = x.shape[0]
    row = lax.broadcasted_iota(jnp.int32, (n, 1), 0)
    shift = 1
    while shift < n:
        x = x + jnp.where(row >= shift, pltpu.roll(x, shift, 0), 0.0)
        shift *= 2
    return x


def _alternate(a, b):
    pending = [iter(a), iter(b)]
    while pending:
        for it in list(pending):
            try:
                next(it)
            except StopIteration:
                pending.remove(it)
                continue
            yield


def _gla_stages(qk_ref, g_ref, v_ref, r_ref, gn_ref, s_ref, og_scr, cum_scr, inter_scr, flag_ref):
    c_len = GLA_CHUNK
    n_chunks = ROW_TILE // c_len
    causal_cat = (lax.broadcasted_iota(jnp.int32, (c_len, GLA_HEADS * c_len), 0)
                  >= lax.broadcasted_iota(jnp.int32, (c_len, GLA_HEADS * c_len), 1) % c_len)
    lane_head = lax.broadcasted_iota(jnp.int32, (1, GLA_KEY_WIDTH), 1) // GLA_HEAD_K
    row_head = lax.broadcasted_iota(jnp.int32, (GLA_KEY_WIDTH, 1), 0) // GLA_HEAD_K
    eye = (lax.broadcasted_iota(jnp.int32, (GLA_KEY_WIDTH, GLA_KEY_WIDTH), 0)
           == lax.broadcasted_iota(jnp.int32, (GLA_KEY_WIDTH, GLA_KEY_WIDTH), 1))
    zero_b = jnp.zeros((), BF16)
    zero_v = jnp.zeros((c_len, GLA_HEAD_V), BF16)
    worst = []

    def chunk(c):
        rows = slice(c * c_len, (c + 1) * c_len)
        cum = _cumsum_rows(g_ref[rows, :])
        cum_scr[rows, :] = cum
        last = cum[c_len - 1:c_len, :]
        q = qk_ref[rows, 0:GLA_KEY_WIDTH]
        k = qk_ref[rows, GLA_KEY_WIDTH:]
        vb = v_ref[rows, :]
        qe = (q * jnp.exp(cum)).astype(BF16)
        ke = (k * jnp.exp(jnp.minimum(-cum, DECAY_CLAMP))).astype(BF16)
        kl = (k * jnp.exp(last - cum)).astype(BF16)
        state = s_ref[...]
        sb = state.astype(BF16)
        s_bd = jnp.concatenate([jnp.where(row_head == h, sb, zero_b) for h in range(GLA_HEADS)], axis=1)
        o_inter = _dot(qe, s_bd)
        inter_scr[rows, :] = o_inter
        ke_bd = jnp.concatenate([jnp.where(lane_head == h, ke, zero_b) for h in range(GLA_HEADS)], axis=0)
        attn = _dot_tb(qe, ke_bd)
        upds = []
        for p in range(GLA_HEADS // 2):
            u = _dot_ta(kl[:, p * LANES:(p + 1) * LANES], vb[:, 2 * p * GLA_HEAD_V:(2 * p + 2) * GLA_HEAD_V])
            upds.append(u[0:GLA_HEAD_K, 0:GLA_HEAD_V])
            upds.append(u[GLA_HEAD_K:, GLA_HEAD_V:])
        yield
        attn = jnp.where(causal_cat, attn, 0.0).astype(BF16)
        o_pairs = []
        for p in range(GLA_HEADS // 2):
            v_a = vb[:, (2 * p) * GLA_HEAD_V:(2 * p + 1) * GLA_HEAD_V]
            v_b = vb[:, (2 * p + 1) * GLA_HEAD_V:(2 * p + 2) * GLA_HEAD_V]
            v_bd = jnp.concatenate([jnp.concatenate([v_a, zero_v], axis=1),
                                    jnp.concatenate([zero_v, v_b], axis=1)], axis=0)
            o_pairs.append(_dot(attn[:, 2 * p * c_len:(2 * p + 2) * c_len], v_bd))
        og_scr[rows, :] = _head_norm_gate(o_inter + jnp.concatenate(o_pairs, axis=1), r_ref[rows, :],
                                          gn_ref[...])
        last_col = jnp.sum(jnp.where(eye, last, 0.0), axis=1, keepdims=True)
        s_ref[...] = state * jnp.exp(last_col) + jnp.concatenate(upds, axis=0)
        worst.append(jnp.min(last, axis=1, keepdims=True))
        if c == n_chunks - 1:
            tile_min = functools.reduce(jnp.minimum, worst)
            flag_ref[0] = jnp.where(tile_min[0, 0] < -DECAY_CLAMP, 1, 0)
        yield

    return itertools.chain.from_iterable(chunk(c) for c in range(n_chunks))


def _gla_tile_exact(qk_ref, v_ref, r_ref, gn_ref, og_scr, o_scr, cum_scr, inter_scr):
    c_len = GLA_CHUNK
    ind = jnp.where(lax.broadcasted_iota(jnp.int32, (GLA_KEY_WIDTH, LANES), 0) // GLA_HEAD_K
                    == lax.broadcasted_iota(jnp.int32, (GLA_KEY_WIDTH, LANES), 1), 1.0, 0.0).astype(BF16)
    j_idx = lax.broadcasted_iota(jnp.int32, (c_len, 1), 0)
    for c in range(ROW_TILE // c_len):
        r0 = c * c_len
        rows = slice(r0, r0 + c_len)

        def one_row(i, carry, r0=r0, rows=rows):
            ci = cum_scr[pl.ds(r0 + i, 1), :]
            qi = qk_ref[pl.ds(r0 + i, 1), 0:GLA_KEY_WIDTH]
            kk = qk_ref[rows, GLA_KEY_WIDTH:]
            dec = jnp.exp(jnp.minimum(ci - cum_scr[rows, :], 0.0))
            a_cols = _dot(((qi * kk) * dec).astype(BF16), ind)
            outs = []
            for h in range(GLA_HEADS):
                w_col = jnp.where(j_idx <= i, a_cols[:, h:h + 1], 0.0)
                v_h = v_ref[rows, h * GLA_HEAD_V:(h + 1) * GLA_HEAD_V].astype(F32)
                outs.append(jnp.sum(w_col * v_h, axis=0, keepdims=True))
            o_scr[pl.ds(r0 + i, 1), :] = inter_scr[pl.ds(r0 + i, 1), :] + jnp.concatenate(outs, axis=1)
            return carry

        lax.fori_loop(0, c_len, one_row, 0)
    og_scr[...] = _head_norm_gate(o_scr[...], r_ref[...], gn_ref[...])


def _dup_halves(x):
    lo = lax.broadcasted_iota(jnp.int32, (1, LANES), 1) < SWA_HEAD_DIM
    sw = pltpu.roll(x, SWA_HEAD_DIM, 1)
    return jnp.where(lo, x, sw).astype(BF16), jnp.where(lo, sw, x).astype(BF16)


def _swa_stages(sink_ref, qs_ref, kv_ref, kvprev_scr, sn_ref, os_scr, no_prev_bias):
    w = WINDOW
    n_blocks = ROW_TILE // w
    lo = lax.broadcasted_iota(jnp.int32, (1, LANES), 1) < SWA_HEAD_DIM
    tri = (lax.broadcasted_iota(jnp.int32, (w, w), 1) <= lax.broadcasted_iota(jnp.int32, (w, w), 0))
    zero_b = jnp.zeros((), BF16)

    def block(b):
        rows = slice(b * w, (b + 1) * w)
        kv_prev = kvprev_scr[...] if b == 0 else kv_ref[(b - 1) * w:b * w, :]
        kv = jnp.concatenate([kv_prev, kv_ref[rows, :]], axis=0)
        kk = _dup_halves(kv[:, 0:SWA_KV_WIDTH])
        vv = _dup_halves(kv[:, SWA_KV_WIDTH:])
        q = qs_ref[rows, :]
        scores = []
        for grp in range(SWA_KV_HEADS):
            stacked = []
            for j in range(SWA_GROUP):
                hq = grp * SWA_GROUP + j
                tile = q[:, (hq // 2) * LANES:(hq // 2 + 1) * LANES]
                stacked.append(jnp.where(lo if hq % 2 == 0 else ~lo, tile, zero_b))
            scores.append(_dot_tb(jnp.concatenate(stacked, axis=0), kk[grp]))
        yield
        outs, dens = [], []
        for grp in range(SWA_KV_HEADS):
            s = scores[grp]
            probs = []
            for j in range(SWA_GROUP):
                sink = sink_ref[grp * SWA_GROUP + j]
                s_prev = s[j * w:(j + 1) * w, 0:w]
                if b == 0:
                    s_prev = s_prev + no_prev_bias
                sf = jnp.where(tri, s[j * w:(j + 1) * w, w:], s_prev)
                m = jnp.maximum(jnp.max(sf, axis=1, keepdims=True), sink)
                p = jnp.exp(sf - m)
                dens.append(jnp.sum(p, axis=1, keepdims=True) + jnp.exp(sink - m))
                pb = p.astype(BF16)
                probs.append(jnp.concatenate([jnp.where(tri, zero_b, pb), jnp.where(tri, pb, zero_b)], axis=1))
            outs.append(_dot(jnp.concatenate(probs, axis=0), vv[grp]))
        yield
        tiles = []
        for grp in range(SWA_KV_HEADS):
            res = [outs[grp][j * w:(j + 1) * w] / dens[grp * SWA_GROUP + j] for j in range(SWA_GROUP)]
            tiles.append(jnp.where(lo, res[0], res[1]))
            tiles.append(jnp.where(lo, res[2], res[3]))
        os_scr[rows, :] = _rms(jnp.concatenate(tiles, axis=1), sn_ref[...]).astype(BF16)
        if b == n_blocks - 1:
            kvprev_scr[...] = kv_ref[rows, :]
        yield

    return itertools.chain.from_iterable(block(b) for b in range(n_blocks))


def _mix_out_body(sink_ref, qk_ref, g_ref, v_ref, r_ref, qs_ref, kv_ref, x1_ref, ogs_ref, oss_ref,
                  gn_ref, sn_ref, wo_ref, n2_ref, wg_ref, wu_ref, wd_ref, nf_ref, yp_ref, ys_ref, s_ref,
                  og_scr, os_scr, kvprev_scr, o_scr, cum_scr, inter_scr, act_scr, flag_ref):
    i = pl.program_id(0)
    n_prompt = pl.num_programs(0) - 1

    def stage_d(og, osw, y_ref, side_stages=None):
        _out_ffn(x1_ref, og, osw, wo_ref, n2_ref, wg_ref, wu_ref, wd_ref, nf_ref, y_ref, act_scr,
                 side_stages)

    def mixer_stages():
        gla = _gla_stages(qk_ref, g_ref, v_ref, r_ref, gn_ref, s_ref, og_scr, cum_scr, inter_scr, flag_ref)
        swa = _swa_stages(sink_ref, qs_ref, kv_ref, kvprev_scr, sn_ref, os_scr,
                          jnp.where(i > 0, 0.0, -jnp.inf))
        return _alternate(gla, swa)

    @pl.when(i == 0)
    def _():
        s_ref[...] = jnp.zeros_like(s_ref)
        kvprev_scr[...] = jnp.zeros_like(kvprev_scr)
        stage_d(ogs_ref[...], oss_ref[...], ys_ref, mixer_stages())

    @pl.when(jnp.logical_and(i > 0, i < n_prompt))
    def _():
        stage_d(og_scr[...], os_scr[...], yp_ref, mixer_stages())

    @pl.when(i == n_prompt)
    def _():
        stage_d(og_scr[...], os_scr[...], yp_ref)
        flag_ref[0] = 0

    @pl.when(flag_ref[0] != 0)
    def _():
        _gla_tile_exact(qk_ref, v_ref, r_ref, gn_ref, og_scr, o_scr, cum_scr, inter_scr)


def _mix_out(qk, g, v, r, qs, kv, x1, og_s, os_s, w):
    t = qk.shape[0] - ROW_TILE
    assert t % ROW_TILE == 0 and og_s.shape[0] == ROW_TILE
    nt = t // ROW_TILE
    cur = lambda n: pl.BlockSpec((ROW_TILE, n), lambda i: (jnp.minimum(i, nt - 1), 0))
    return pl.pallas_call(
        _mix_out_body,
        grid=(nt + 1,),
        in_specs=[pl.BlockSpec(memory_space=pltpu.SMEM),
                  cur(2 * GLA_KEY_WIDTH), cur(GLA_KEY_WIDTH), cur(GLA_WIDTH), cur(GLA_WIDTH),
                  cur(SWA_WIDTH), cur(2 * SWA_KV_WIDTH),
                  pl.BlockSpec((ROW_TILE, D_MODEL), lambda i: (jnp.where(i == 0, nt, i - 1), 0)),
                  _resident((ROW_TILE, GLA_WIDTH)), _resident((ROW_TILE, SWA_WIDTH)),
                  _resident((1, GLA_WIDTH)), _resident((1, SWA_WIDTH)),
                  _resident((D_MODEL, D_MODEL)), _resident((1, D_MODEL)),
                  _resident((D_MODEL, D_FF)), _resident((D_MODEL, D_FF)), _resident((D_FF, D_MODEL)),
                  _resident((1, D_MODEL))],
        out_specs=[pl.BlockSpec((ROW_TILE, D_MODEL), lambda i: (jnp.clip(i - 1, 0, nt - 1), 0)),
                   pl.BlockSpec((ROW_TILE, D_MODEL), lambda i: (0, 0)),
                   pl.BlockSpec((GLA_KEY_WIDTH, GLA_HEAD_V), lambda i: (0, 0))],
        out_shape=[jax.ShapeDtypeStruct((t, D_MODEL), F32),
                   jax.ShapeDtypeStruct((ROW_TILE, D_MODEL), F32),
                   jax.ShapeDtypeStruct((GLA_KEY_WIDTH, GLA_HEAD_V), F32)],
        scratch_shapes=[pltpu.VMEM((ROW_TILE, GLA_WIDTH), BF16), pltpu.VMEM((ROW_TILE, SWA_WIDTH), BF16),
                        pltpu.VMEM((WINDOW, 2 * SWA_KV_WIDTH), F32),
                        pltpu.VMEM((ROW_TILE, GLA_WIDTH), F32), pltpu.VMEM((ROW_TILE, GLA_KEY_WIDTH), F32),
                        pltpu.VMEM((ROW_TILE, GLA_WIDTH), F32), pltpu.VMEM((ROW_TILE, D_FF), BF16),
                        pltpu.SMEM((1,), jnp.int32)],
        compiler_params=pltpu.CompilerParams(dimension_semantics=("arbitrary",),
                                             vmem_limit_bytes=VMEM_LIMIT_BYTES),
        name="prompt_mixers_stage_d",
    )(w["sinks"], qk, g, v, r, qs, kv, x1, og_s, os_s, w["gn"], w["sn"], w["wo"], w["n2"], w["wg2"],
      w["wu2"], w["wd2"], w["nf"])


def _gla_sample_body(qk_ref, g_ref, v_ref, r_ref, gn_ref, s_ref, og_ref, so_ref, o_scr):
    rows = SAMPLE_ROWS
    g = g_ref[...]
    q = qk_ref[:, 0:GLA_KEY_WIDTH]
    k = qk_ref[:, GLA_KEY_WIDTH:]
    vb = v_ref[...]
    vf = vb.astype(F32)
    tok = lax.broadcasted_iota(jnp.int32, (rows, 1), 0) % DEC_SEQ

    cum = g
    for d in range(1, DEC_SEQ):
        cum = cum + jnp.where(tok >= d, pltpu.roll(g, d, 0), 0.0)
    tot = jnp.where(tok == DEC_SEQ - 1, cum, 0.0)
    for d in range(1, DEC_SEQ):
        tot = tot + jnp.where(tok == DEC_SEQ - 1 - d, pltpu.roll(cum, rows - d, 0), 0.0)

    qe = q * jnp.exp(cum)
    kl = k * jnp.exp(tot - cum)
    decay_t = jnp.exp(tot).T

    ind = jnp.where(lax.broadcasted_iota(jnp.int32, (GLA_KEY_WIDTH, LANES), 0) // GLA_HEAD_K
                    == lax.broadcasted_iota(jnp.int32, (GLA_KEY_WIDTH, LANES), 1), 1.0, 0.0).astype(BF16)
    expand = jnp.where(lax.broadcasted_iota(jnp.int32, (LANES, GLA_WIDTH), 0)
                       == lax.broadcasted_iota(jnp.int32, (LANES, GLA_WIDTH), 1) // GLA_HEAD_V,
                       1.0, 0.0).astype(BF16)

    o_intra = jnp.zeros((rows, GLA_WIDTH), F32)
    for d in range(DEC_SEQ):
        k_d = k if d == 0 else pltpu.roll(k, d, 0)
        c_d = cum if d == 0 else pltpu.roll(cum, d, 0)
        v_d = vf if d == 0 else pltpu.roll(vf, d, 0)
        pair = jnp.where(tok >= d, q * k_d * jnp.exp(jnp.minimum(cum - c_d, 0.0)), 0.0)
        a = _dot(pair.astype(BF16), ind)
        o_intra = o_intra + _dot(a.astype(BF16), expand) * v_d

    lane_head = lax.broadcasted_iota(jnp.int32, (1, GLA_KEY_WIDTH), 1) // GLA_HEAD_K
    row8 = lax.broadcasted_iota(jnp.int32, (SUBLANES, 1), 0)
    row32 = lax.broadcasted_iota(jnp.int32, (GLA_HEADS * SUBLANES, 1), 0)
    for pair_idx in range(SAMPLE_SEQS // 2):
        r8 = slice(pair_idx * SUBLANES, (pair_idx + 1) * SUBLANES)
        q8 = qe[r8, :]
        lhs = jnp.concatenate([jnp.where(lane_head == h, q8, 0.0) for h in range(GLA_HEADS)],
                              axis=0).astype(BF16)
        kl8 = kl[r8, :]
        v8 = vb[r8, :]
        res = []
        for s in range(2):
            b = 2 * pair_idx + s
            state = s_ref[b]
            res.append(_dot(lhs, state.astype(BF16)))
            kl_b = jnp.where(row8 // DEC_SEQ == s, kl8, 0.0).astype(BF16)
            upd = _dot_ta(kl_b, v8)
            upd = jnp.concatenate(
                [upd[h * GLA_HEAD_K:(h + 1) * GLA_HEAD_K, h * GLA_HEAD_V:(h + 1) * GLA_HEAD_V]
                 for h in range(GLA_HEADS)], axis=0)
            so_ref[b] = state * decay_t[:, DEC_SEQ * b:DEC_SEQ * b + 1] + upd
        sel = jnp.where(row32 % SUBLANES < DEC_SEQ, res[0], res[1])
        o_scr[r8, :] = jnp.concatenate([sel[h * SUBLANES:(h + 1) * SUBLANES] for h in range(GLA_HEADS)], axis=1)

    og_ref[...] = _head_norm_gate(o_scr[...] + o_intra, r_ref[...], gn_ref[...])


def _gla_sample(qk, g, v, r, gn, state, row0):
    nseq = state.shape[0]
    rows = nseq * DEC_SEQ
    assert nseq % SAMPLE_SEQS == 0 and row0 % SAMPLE_ROWS == 0
    blk0 = row0 // SAMPLE_ROWS
    src = lambda n: pl.BlockSpec((SAMPLE_ROWS, n), lambda i: (i + blk0, 0))
    blk = lambda n: pl.BlockSpec((SAMPLE_ROWS, n), lambda i: (i, 0))
    sblk = pl.BlockSpec((SAMPLE_SEQS, GLA_KEY_WIDTH, GLA_HEAD_V), lambda i: (i, 0, 0))
    return pl.pallas_call(
        _gla_sample_body,
        grid=(nseq // SAMPLE_SEQS,),
        in_specs=[src(2 * GLA_KEY_WIDTH), src(GLA_KEY_WIDTH), src(GLA_WIDTH), src(GLA_WIDTH),
                  pl.BlockSpec((1, GLA_WIDTH), lambda i: (0, 0)), sblk],
        out_specs=[blk(GLA_WIDTH), sblk],
        out_shape=[jax.ShapeDtypeStruct((rows, GLA_WIDTH), BF16),
                   jax.ShapeDtypeStruct((nseq, GLA_KEY_WIDTH, GLA_HEAD_V), F32)],
        scratch_shapes=[pltpu.VMEM((SAMPLE_ROWS, GLA_WIDTH), F32)],
        compiler_params=pltpu.CompilerParams(dimension_semantics=("arbitrary",)),
        name="gla_sample",
    )(qk, g, v, r, gn, state)


def _swa_sample_body(sink_ref, q_ref, kvn_ref, kc_ref, vc_ref, nrm_ref, o_ref, kco_ref, vco_ref,
                     lhs_scr, sc_scr, pc_scr, oc_scr):
    rows = SAMPLE_ROWS
    q = q_ref[...].astype(F32)
    lo = lax.broadcasted_iota(jnp.int32, (1, LANES), 1) < SWA_HEAD_DIM
    for hq in range(SWA_Q_HEADS):
        grp = hq // SWA_GROUP
        tile = q[:, (hq // 2) * LANES:(hq // 2 + 1) * LANES]
        src = tile if hq % 2 == grp else pltpu.roll(tile, SWA_HEAD_DIM, 1)
        lhs_scr[hq * rows:(hq + 1) * rows, :] = jnp.where(lo if grp == 0 else ~lo, src, 0.0)

    kvn = kvn_ref[...]
    k_new = kvn[:, 0:SWA_KV_WIDTH]
    v_new = kvn[:, SWA_KV_WIDTH:]
    s_new = _dot_tb(lhs_scr[...].astype(BF16), k_new.astype(BF16))

    row64 = lax.broadcasted_iota(jnp.int32, (SWA_Q_HEADS * SUBLANES, 1), 0)
    first_of_pair = row64 % SUBLANES < DEC_SEQ

    def gather_pair(ref, pair_idx):
        return jnp.concatenate(
            [ref[hq * rows + pair_idx * SUBLANES:hq * rows + (pair_idx + 1) * SUBLANES, :]
             for hq in range(SWA_Q_HEADS)], axis=0).astype(BF16)

    def scatter_pair(ref, pair_idx, val):
        for hq in range(SWA_Q_HEADS):
            ref[hq * rows + pair_idx * SUBLANES:hq * rows + (pair_idx + 1) * SUBLANES, :] = (
                val[hq * SUBLANES:(hq + 1) * SUBLANES])

    for pair_idx in range(SAMPLE_SEQS // 2):
        l64 = gather_pair(lhs_scr, pair_idx)
        sa = _dot(l64, kc_ref[2 * pair_idx].astype(BF16))
        sb = _dot(l64, kc_ref[2 * pair_idx + 1].astype(BF16))
        scatter_pair(sc_scr, pair_idx, jnp.where(first_of_pair, sa, sb))

    rr = lax.broadcasted_iota(jnp.int32, (rows, rows), 0)
    cc = lax.broadcasted_iota(jnp.int32, (rows, rows), 1)
    tok = rr % DEC_SEQ
    mask_cache = cc > tok
    mask_new = (cc // DEC_SEQ == rr // DEC_SEQ) & (cc % DEC_SEQ <= tok)
    p_new, dens = [], []
    for hq in range(SWA_Q_HEADS):
        sl = slice(hq * rows, (hq + 1) * rows)
        sink = sink_ref[hq]
        s_c = jnp.where(mask_cache, sc_scr[sl, :], -jnp.inf)
        s_n = jnp.where(mask_new, s_new[sl, :], -jnp.inf)
        m = jnp.maximum(jnp.maximum(jnp.max(s_c, axis=1, keepdims=True),
                                    jnp.max(s_n, axis=1, keepdims=True)), sink)
        p_c = jnp.exp(s_c - m)
        p_n = jnp.exp(s_n - m)
        dens.append(jnp.sum(p_c, axis=1, keepdims=True) + jnp.sum(p_n, axis=1, keepdims=True)
                    + jnp.exp(sink - m))
        pc_scr[sl, :] = p_c
        p_new.append(p_n.astype(BF16))
    o_new = _dot(jnp.concatenate(p_new, axis=0), v_new.astype(BF16))

    for pair_idx in range(SAMPLE_SEQS // 2):
        p64 = gather_pair(pc_scr, pair_idx)
        oa = _dot_tb(p64, vc_ref[2 * pair_idx].astype(BF16))
        ob = _dot_tb(p64, vc_ref[2 * pair_idx + 1].astype(BF16))
        scatter_pair(oc_scr, pair_idx, jnp.where(first_of_pair, oa, ob))

    tiles = []
    for i in range(SWA_Q_HEADS // 2):
        halves = []
        for hq in (2 * i, 2 * i + 1):
            sl = slice(hq * rows, (hq + 1) * rows)
            oh = (oc_scr[sl, :] + o_new[sl, :]) / dens[hq]
            halves.append(oh if hq % 2 == hq // SWA_GROUP else pltpu.roll(oh, SWA_HEAD_DIM, 1))
        tiles.append(jnp.where(lo, halves[0], halves[1]))
    o_ref[...] = _rms(jnp.concatenate(tiles, axis=1), nrm_ref[...]).astype(BF16)

    keep = lax.broadcasted_iota(jnp.int32, (1, WINDOW), 1) < WINDOW - DEC_SEQ
    k_new_t = k_new.T
    v_new_t = v_new.T
    for b in range(SAMPLE_SEQS):
        to_tail = (WINDOW - DEC_SEQ - DEC_SEQ * b) % rows
        kco_ref[b] = jnp.where(keep, pltpu.roll(kc_ref[b], WINDOW - DEC_SEQ, 1),
                               pltpu.roll(k_new_t, to_tail, 1))
        vco_ref[b] = jnp.where(keep, pltpu.roll(vc_ref[b], WINDOW - DEC_SEQ, 1),
                               pltpu.roll(v_new_t, to_tail, 1))


def _swa_sample(sinks, qs, kvn, kc, vc, nrm, row0):
    nseq = kc.shape[0]
    rows = nseq * DEC_SEQ
    assert nseq % SAMPLE_SEQS == 0 and row0 % SAMPLE_ROWS == 0
    assert kc.shape[1:] == (SWA_KV_WIDTH, WINDOW) and WINDOW == SAMPLE_ROWS
    blk0 = row0 // SAMPLE_ROWS
    cblk = pl.BlockSpec((SAMPLE_SEQS, SWA_KV_WIDTH, WINDOW), lambda i: (i, 0, 0))
    big = pltpu.VMEM((SWA_Q_HEADS * SAMPLE_ROWS, LANES), F32)
    return pl.pallas_call(
        _swa_sample_body,
        grid=(nseq // SAMPLE_SEQS,),
        in_specs=[pl.BlockSpec(memory_space=pltpu.SMEM),
                  pl.BlockSpec((SAMPLE_ROWS, SWA_WIDTH), lambda i: (i + blk0, 0)),
                  pl.BlockSpec((SAMPLE_ROWS, 2 * SWA_KV_WIDTH), lambda i: (i + blk0, 0)),
                  cblk, cblk,
                  pl.BlockSpec((1, SWA_WIDTH), lambda i: (0, 0))],
        out_specs=[pl.BlockSpec((SAMPLE_ROWS, SWA_WIDTH), lambda i: (i, 0)), cblk, cblk],
        out_shape=[jax.ShapeDtypeStruct((rows, SWA_WIDTH), BF16),
                   jax.ShapeDtypeStruct(kc.shape, F32), jax.ShapeDtypeStruct(vc.shape, F32)],
        scratch_shapes=[big, big, big, big],
        compiler_params=pltpu.CompilerParams(dimension_semantics=("arbitrary",)),
        name="swa_sample",
    )(sinks, qs, kvn, kc, vc, nrm)


def _prepare_weights(ffn1_norm, ffn1_w_gate, ffn1_w_up, ffn1_w_down, mix_norm, w_in, w_gate_up, b_gate,
                     gla_head_norm, swa_out_norm, swa_sinks, w_out,
                     ffn2_norm, ffn2_w_gate, ffn2_w_up, ffn2_w_down, final_norm, layer):
    assert w_in.shape[1:] == (D_MODEL, D_IN) and PROJ_Q_S == 2 * GLA_KEY_WIDTH + 2 * GLA_WIDTH
    wgu = jnp.concatenate([w_gate_up[layer],
                           jnp.zeros((LANES - GLA_GATE_RANK, GLA_KEY_WIDTH), w_gate_up.dtype)], axis=0)
    row = lambda a: a.reshape(1, -1).astype(F32)
    return dict(
        n1=row(ffn1_norm[layer]), wg1=ffn1_w_gate[layer], wu1=ffn1_w_up[layer], wd1=ffn1_w_down[layer],
        nm=row(mix_norm[layer]), win=w_in[layer].T, wgu=wgu.astype(BF16), bg=row(b_gate[layer]),
        gn=row(gla_head_norm[layer]), sn=row(swa_out_norm[layer]), sinks=swa_sinks[layer].astype(F32),
        wo=w_out[layer], n2=row(ffn2_norm[layer]), wg2=ffn2_w_gate[layer], wu2=ffn2_w_up[layer],
        wd2=ffn2_w_down[layer], nf=row(final_norm))


def kernel(x_prompt, x_sample, state_gla, cache_swa_k, cache_swa_v, ffn1_norm, ffn1_w_gate, ffn1_w_up,
           ffn1_w_down, mix_norm, w_in, w_gate_up, b_gate, gla_head_norm, swa_out_norm, swa_sinks, w_out,
           ffn2_norm, ffn2_w_gate, ffn2_w_up, ffn2_w_down, final_norm):
    depth = state_gla.shape[0]
    assert depth == 1 and x_prompt.shape[0] == 1 and x_sample.shape[1] == DEC_SEQ
    seq = x_prompt.shape[1]
    nseq = x_sample.shape[0]
    w = _prepare_weights(ffn1_norm, ffn1_w_gate, ffn1_w_up, ffn1_w_down, mix_norm, w_in, w_gate_up, b_gate,
                         gla_head_norm, swa_out_norm, swa_sinks, w_out,
                         ffn2_norm, ffn2_w_gate, ffn2_w_up, ffn2_w_down, final_norm, 0)

    (x1, qk, g, v, r, qs, kv), (wo, wg2, wu2, wd2) = _stage_a(
        x_prompt.reshape(seq, D_MODEL), x_sample.reshape(nseq * DEC_SEQ, D_MODEL), w)
    w = dict(w, wo=wo, wg2=wg2, wu2=wu2, wd2=wd2)

    og_s, state_s = _gla_sample(qk, g, v, r, w["gn"],
                                state_gla[0].reshape(nseq, GLA_KEY_WIDTH, GLA_HEAD_V), seq)
    cache_w = cache_swa_k.shape[2]
    to_feature_major = lambda c: jnp.transpose(c[0], (0, 2, 3, 1)).reshape(nseq, SWA_KV_WIDTH, cache_w)
    from_feature_major = lambda c: jnp.transpose(
        c.reshape(nseq, SWA_KV_HEADS, SWA_HEAD_DIM, cache_w), (0, 3, 1, 2))[None]
    os_s, k_cache_s, v_cache_s = _swa_sample(
        w["sinks"], qs, kv, to_feature_major(cache_swa_k), to_feature_major(cache_swa_v), w["sn"], seq)
    k_cache_s = from_feature_major(k_cache_s)
    v_cache_s = from_feature_major(v_cache_s)

    y_prompt, y_sample, state_p = _mix_out(qk, g, v, r, qs, kv, x1, og_s, os_s, w)
    y_prompt = y_prompt.reshape(1, seq, D_MODEL)
    y_sample = y_sample.reshape(nseq, DEC_SEQ, D_MODEL)
    cw = min(WINDOW, seq)
    k_cache_p = kv[seq - cw:seq, 0:SWA_KV_WIDTH].reshape(1, 1, cw, SWA_KV_HEADS, SWA_HEAD_DIM)
    v_cache_p = kv[seq - cw:seq, SWA_KV_WIDTH:].reshape(1, 1, cw, SWA_KV_HEADS, SWA_HEAD_DIM)
    state_p = state_p.reshape(1, 1, GLA_HEADS, GLA_HEAD_K, GLA_HEAD_V)
    state_s = state_s.reshape(1, nseq, GLA_HEADS, GLA_HEAD_K, GLA_HEAD_V)

    return (y_prompt, y_sample, state_p, k_cache_p, v_cache_p, state_s, k_cache_s, v_cache_s)
```

```python
import functools
import itertools

import jax
import jax.numpy as jnp
from jax import lax
from jax.experimental import pallas as pl
from jax.experimental.pallas import tpu as pltpu

F32 = jnp.float32
BF16 = jnp.bfloat16

D_MODEL = 1024
D_FF = 2816
GLA_HEADS = 4
GLA_HEAD_K = 64
GLA_HEAD_V = 128
GLA_KEY_WIDTH = GLA_HEADS * GLA_HEAD_K
GLA_WIDTH = GLA_HEADS * GLA_HEAD_V
GLA_GATE_RANK = 16
GLA_GATE_TAU = 16.0
SWA_HEAD_DIM = 64
SWA_Q_HEADS = 8
SWA_KV_HEADS = 2
SWA_GROUP = SWA_Q_HEADS // SWA_KV_HEADS
SWA_WIDTH = SWA_Q_HEADS * SWA_HEAD_DIM
SWA_KV_WIDTH = SWA_KV_HEADS * SWA_HEAD_DIM
WINDOW = 128
DEC_SEQ = 4
NORM_EPS = 1e-6
HEAD_SCALE = 0.125

LANES = 128
SUBLANES = 8
VMEM_LIMIT_BYTES = 58 * 1024 * 1024

ROW_TILE = 512
FF_CHUNK = 256
DOWN_CHUNK = 256
GLA_CHUNK = 128
SAMPLE_SEQS = 32
SAMPLE_ROWS = SAMPLE_SEQS * DEC_SEQ
MIXER_STAGES = 2 * (ROW_TILE // GLA_CHUNK) + 3 * (ROW_TILE // WINDOW)
SIDE_STAGES_AT_NORM = 2
DECAY_CLAMP = 60.0

PROJ_Q_G = 0
PROJ_K_G = PROJ_Q_G + GLA_KEY_WIDTH
PROJ_V_G = PROJ_K_G + GLA_KEY_WIDTH
PROJ_R_G = PROJ_V_G + GLA_WIDTH
PROJ_Q_S = PROJ_R_G + GLA_WIDTH
PROJ_KV_S = PROJ_Q_S + SWA_WIDTH
PROJ_A = PROJ_KV_S + 2 * SWA_KV_WIDTH
PROJ_WIDTH = PROJ_A + LANES
D_IN = PROJ_A + GLA_GATE_RANK


def _dot(a, b):
    return jnp.dot(a, b, preferred_element_type=F32)


def _dot_tb(a, b):
    return lax.dot_general(a, b, (((1,), (1,)), ((), ())), preferred_element_type=F32)


def _dot_ta(a, b):
    return lax.dot_general(a, b, (((0,), (0,)), ((), ())), preferred_element_type=F32)


def _rms(x, g):
    return x * lax.rsqrt(jnp.mean(x * x, axis=-1, keepdims=True) + NORM_EPS) * g


def _gate_up(h, wg_ref, wu_ref, act_ref, after_dot=lambda: None):
    for c0 in range(0, D_FF, FF_CHUNK):
        g = _dot(h, wg_ref[:, c0:c0 + FF_CHUNK])
        after_dot()
        u = _dot(h, wu_ref[:, c0:c0 + FF_CHUNK])
        after_dot()
        act_ref[:, c0:c0 + FF_CHUNK] = (g * jax.nn.sigmoid(g) * u).astype(BF16)


def _down(act_ref, rows, wd_ref, after_dot=lambda: None):
    outs = []
    for n0 in range(0, D_MODEL, DOWN_CHUNK):
        outs.append(_dot(act_ref[rows, :], wd_ref[:, n0:n0 + DOWN_CHUNK]))
        after_dot()
    return jnp.concatenate(outs, axis=1)


def _head_norm_gate(o, r, gn):
    parts = []
    for h in range(GLA_HEADS):
        sl = slice(h * GLA_HEAD_V, (h + 1) * GLA_HEAD_V)
        parts.append(_rms(o[:, sl], gn[:, sl]))
    return (jnp.concatenate(parts, axis=1) * (r * jax.nn.sigmoid(r))).astype(BF16)


def _out_ffn(x1_ref, og, osw, wo_ref, n2_ref, wg_ref, wu_ref, wd_ref, nf_ref, y_ref, act_ref, side_stages=None):
    n_inside = 0 if side_stages is None else MIXER_STAGES - 2 * SIDE_STAGES_AT_NORM
    side_stages = iter(()) if side_stages is None else side_stages
    n_dots = 2 * (D_FF // FF_CHUNK) + D_MODEL // DOWN_CHUNK
    done = [0, 0]

    def after_dot():
        done[0] += 1
        while done[1] * n_dots < done[0] * n_inside:
            next(side_stages, None)
            done[1] += 1

    mixed = jnp.concatenate([og, osw], axis=1)
    x2 = x1_ref[...] + _dot(mixed, wo_ref[...])
    for _ in range(SIDE_STAGES_AT_NORM):
        next(side_stages, None)
    h = _rms(x2, n2_ref[...]).astype(BF16)
    _gate_up(h, wg_ref, wu_ref, act_ref, after_dot)
    x3 = x2 + 0.5 * _down(act_ref, slice(None), wd_ref, after_dot)
    for _ in side_stages:
        pass
    y_ref[...] = _rms(x3, nf_ref[...])


_HBM = pl.BlockSpec(memory_space=pl.ANY)
WIDE_CHUNK_ROWS = 128
NARROW_CHUNK_ROWS = 256
LOAD_SLOTS = 4


def _load_weights_bf16(loads, staging, sems):
    jobs = []
    for src, dst, segments, transposed in loads:
        cols = src.shape[1]
        assert cols in (D_FF, D_MODEL) and dst.shape[0 if transposed else 1] == cols
        kind, step = (0, WIDE_CHUNK_ROWS) if cols == D_FF else (1, NARROW_CHUNK_ROWS)
        for s0, d0, n in segments or [(0, 0, src.shape[0])]:
            jobs += [(src, dst, s0 + r, d0 + r, min(step, n - r), kind, transposed)
                     for r in range(0, n, step)]

    used = [0, 0]
    copies = []
    for src, _, s0, _, nr, kind, _ in jobs:
        slot = used[kind] % LOAD_SLOTS
        used[kind] += 1
        copies.append((pltpu.make_async_copy(src.at[pl.ds(s0, nr), :], staging[kind].at[slot, pl.ds(0, nr), :],
                                             sems.at[kind, slot]), slot))
    for copy, _ in copies[:LOAD_SLOTS - 1]:
        copy.start()
    for j, (_, dst, _, d0, nr, kind, transposed) in enumerate(jobs):
        ahead = j + LOAD_SLOTS - 1
        if ahead < len(jobs):
            copies[ahead][0].start()
        copy, slot = copies[j]
        copy.wait()
        if not transposed:
            dst[d0:d0 + nr, :] = staging[kind][slot, 0:nr, :].astype(BF16)
        elif nr % LANES == 0:
            dst[:, d0:d0 + nr] = staging[kind][slot, 0:nr, :].T.astype(BF16)
        else:
            tile = staging[kind][slot, 0:LANES, :].T
            live = lax.broadcasted_iota(jnp.int32, (1, LANES), 1) < nr
            dst[:, d0:d0 + LANES] = jnp.where(live, tile, 0.0).astype(BF16)


def _staging_scratch():
    return [pltpu.VMEM((LOAD_SLOTS, WIDE_CHUNK_ROWS, D_FF), F32),
            pltpu.VMEM((LOAD_SLOTS, NARROW_CHUNK_ROWS, D_MODEL), F32),
            pltpu.SemaphoreType.DMA((2, LOAD_SLOTS))]


def _chunk_plan(matrices):
    plan, first = [], 0
    for m, mat in enumerate(matrices):
        rows, cols = mat.shape
        assert cols in (D_FF, D_MODEL)
        kind, nr = (0, WIDE_CHUNK_ROWS) if cols == D_FF else (1, NARROW_CHUNK_ROWS)
        assert rows % nr == 0
        plan.append((m, first, rows // nr, nr, kind))
        first += rows // nr
    return plan, first


def _background_cast(step, srcs, dsts, staging, out_staging, sem_in, sem_out):
    plan, _ = _chunk_plan(srcs)
    slot = step % 2

    def rows_of(m, chunk):
        _, first, _, nr, _ = plan[m]
        return pl.ds(pl.multiple_of((chunk - first) * nr, nr), nr)

    def copy_in(m, chunk, slot_):
        kind = plan[m][4]
        return pltpu.make_async_copy(srcs[m].at[rows_of(m, chunk), :], staging[kind].at[slot_],
                                     sem_in.at[kind, slot_])

    def copy_out(m, chunk, slot_):
        kind = plan[m][4]
        return pltpu.make_async_copy(out_staging[kind].at[slot_], dsts[m].at[rows_of(m, chunk), :],
                                     sem_out.at[kind, slot_])

    def for_chunk(chunk, fn):
        for m, first, n, _, _ in plan:
            pl.when(jnp.logical_and(chunk >= first, chunk < first + n))(functools.partial(fn, m))

    def before():
        @pl.when(step == 0)
        def _():
            copy_in(0, 0, 0).start()
        for_chunk(step, lambda m: copy_in(m, step, slot).wait())
        for_chunk(step - 2, lambda m: copy_out(m, step - 2, slot).wait())
        for_chunk(step + 1, lambda m: copy_in(m, step + 1, 1 - slot).start())

    def cast():
        for kind in range(2):
            out_staging[kind][slot] = staging[kind][slot].astype(BF16)

    def after():
        for_chunk(step, lambda m: copy_out(m, step, slot).start())

    return before, cast, after


def _stage_a_body(xp_ref, xs_ref, n1_ref, wg_hbm, wu_hbm, wd_hbm, nm_ref, win_hbm, wgu_ref, bg_ref,
                  wo_hbm, wg2_hbm, wu2_hbm, wd2_hbm,
                  x1_ref, qk_ref, g_ref, v_ref, r_ref, qs_ref, kv_ref, wo_out, wg2_out, wu2_out, wd2_out,
                  act_ref, wg_ref, wu_ref, wd_ref, win_ref, wide_stage, narrow_stage, load_sems,
                  wide_out, narrow_out, sem_in, sem_out):
    i = pl.program_id(0)

    @pl.when(i == 0)
    def _():
        a0, a1 = PROJ_Q_S, PROJ_Q_S + GLA_GATE_RANK
        win_rows = [(0, 0, a0), (a1, a0, D_IN - a1), (a0, PROJ_A, GLA_GATE_RANK)]
        _load_weights_bf16([(wg_hbm, wg_ref, None, False), (wu_hbm, wu_ref, None, False),
                            (wd_hbm, wd_ref, None, False), (win_hbm, win_ref, win_rows, True)],
                           (wide_stage, narrow_stage), load_sems)

    bg_before, bg_cast, bg_after = _background_cast(
        i, (wo_hbm, wg2_hbm, wu2_hbm, wd2_hbm), (wo_out, wg2_out, wu2_out, wd2_out),
        (wide_stage, narrow_stage), (wide_out, narrow_out), sem_in, sem_out)
    bg_before()

    x = jnp.where(i < pl.num_programs(0) - 1, xp_ref[...], xs_ref[...])
    h = _rms(x, n1_ref[...]).astype(BF16)
    _gate_up(h, wg_ref, wu_ref, act_ref)
    bg_cast()

    halves = [slice(k * ROW_TILE // 2, (k + 1) * ROW_TILE // 2) for k in range(2)]
    x1 = []
    for rows in halves:
        x1.append(x[rows, :] + 0.5 * _down(act_ref, rows, wd_ref))
        x1_ref[rows, :] = x1[-1]
    gate_in = []
    for rows, x1_h in zip(halves, x1):
        h2 = _rms(x1_h, nm_ref[...]).astype(BF16)
        proj = _dot(h2, win_ref[...])
        qk_ref[rows, 0:GLA_KEY_WIDTH] = proj[:, PROJ_Q_G:PROJ_K_G] * HEAD_SCALE
        qk_ref[rows, GLA_KEY_WIDTH:] = proj[:, PROJ_K_G:PROJ_V_G]
        v_ref[rows, :] = proj[:, PROJ_V_G:PROJ_R_G].astype(BF16)
        r_ref[rows, :] = proj[:, PROJ_R_G:PROJ_Q_S]
        qs_ref[rows, :] = (proj[:, PROJ_Q_S:PROJ_KV_S] * HEAD_SCALE).astype(BF16)
        kv_ref[rows, :] = proj[:, PROJ_KV_S:PROJ_A]
        gate_in.append(proj[:, PROJ_A:PROJ_WIDTH].astype(BF16))
    for rows, a in zip(halves, gate_in):
        z = _dot(a, wgu_ref[...]) + bg_ref[...]
        g_ref[rows, :] = jax.nn.log_sigmoid(z) * (1.0 / GLA_GATE_TAU)
    bg_after()


def _resident(shape):
    return pl.BlockSpec(shape, lambda i: (0,) * len(shape), pipeline_mode=pl.Buffered(1))


def _rows(tm, n):
    return pl.BlockSpec((tm, n), lambda i: (i, 0))


def _stage_a(xp, xs, w):
    t = xp.shape[0]
    assert t % ROW_TILE == 0 and xs.shape[0] == ROW_TILE
    nt = t // ROW_TILE
    out_widths = ((D_MODEL, F32), (2 * GLA_KEY_WIDTH, F32), (GLA_KEY_WIDTH, F32), (GLA_WIDTH, BF16),
                  (GLA_WIDTH, F32), (SWA_WIDTH, BF16), (2 * SWA_KV_WIDTH, F32))
    to_cast = (w["wo"], w["wg2"], w["wu2"], w["wd2"])
    assert _chunk_plan(to_cast)[1] + 2 <= nt + 1
    outs = pl.pallas_call(
        _stage_a_body,
        grid=(nt + 1,),
        in_specs=[pl.BlockSpec((ROW_TILE, D_MODEL), lambda i: (jnp.minimum(i, nt - 1), 0)),
                  _resident((ROW_TILE, D_MODEL)), _resident((1, D_MODEL)),
                  _HBM, _HBM, _HBM,
                  _resident((1, D_MODEL)), _HBM,
                  _resident((LANES, GLA_KEY_WIDTH)), _resident((1, GLA_KEY_WIDTH)),
                  _HBM, _HBM, _HBM, _HBM],
        out_specs=[_rows(ROW_TILE, n) for n, _ in out_widths] + [_HBM] * len(to_cast),
        out_shape=([jax.ShapeDtypeStruct((t + ROW_TILE, n), dt) for n, dt in out_widths]
                   + [jax.ShapeDtypeStruct(m.shape, BF16) for m in to_cast]),
        scratch_shapes=([pltpu.VMEM((ROW_TILE, D_FF), BF16), pltpu.VMEM((D_MODEL, D_FF), BF16),
                         pltpu.VMEM((D_MODEL, D_FF), BF16), pltpu.VMEM((D_FF, D_MODEL), BF16),
                         pltpu.VMEM((D_MODEL, PROJ_WIDTH), BF16)]
                        + _staging_scratch()
                        + [pltpu.VMEM((2, WIDE_CHUNK_ROWS, D_FF), BF16),
                           pltpu.VMEM((2, NARROW_CHUNK_ROWS, D_MODEL), BF16),
                           pltpu.SemaphoreType.DMA((2, 2)), pltpu.SemaphoreType.DMA((2, 2))]),
        compiler_params=pltpu.CompilerParams(dimension_semantics=("arbitrary",),
                                             vmem_limit_bytes=VMEM_LIMIT_BYTES),
        name="stage_a_ffn1_proj",
    )(xp, xs, w["n1"], w["wg1"], w["wu1"], w["wd1"], w["nm"], w["win"], w["wgu"], w["bg"], *to_cast)
    return outs[:len(out_widths)], outs[len(out_widths):]


def _cumsum_rows(x):
    n = x.shape[0]
    row = lax.broadcasted_iota(jnp.int32, (n, 1), 0)
    shift = 1
    while shift < n:
        x = x + jnp.where(row >= shift, pltpu.roll(x, shift, 0), 0.0)
        shift *= 2
    return x


def _alternate(a, b):
    pending = [iter(a), iter(b)]
    while pending:
        for it in list(pending):
            try:
                next(it)
            except StopIteration:
                pending.remove(it)
                continue
            yield


def _gla_stages(qk_ref, g_ref, v_ref, r_ref, gn_ref, s_ref, og_scr, cum_scr, inter_scr, flag_ref):
    c_len = GLA_CHUNK
    n_chunks = ROW_TILE // c_len
    causal_cat = (lax.broadcasted_iota(jnp.int32, (c_len, GLA_HEADS * c_len), 0)
                  >= lax.broadcasted_iota(jnp.int32, (c_len, GLA_HEADS * c_len), 1) % c_len)
    lane_head = lax.broadcasted_iota(jnp.int32, (1, GLA_KEY_WIDTH), 1) // GLA_HEAD_K
    row_head = lax.broadcasted_iota(jnp.int32, (GLA_KEY_WIDTH, 1), 0) // GLA_HEAD_K
    eye = (lax.broadcasted_iota(jnp.int32, (GLA_KEY_WIDTH, GLA_KEY_WIDTH), 0)
           == lax.broadcasted_iota(jnp.int32, (GLA_KEY_WIDTH, GLA_KEY_WIDTH), 1))
    zero_b = jnp.zeros((), BF16)
    zero_v = jnp.zeros((c_len, GLA_HEAD_V), BF16)
    worst = []

    def chunk(c):
        rows = slice(c * c_len, (c + 1) * c_len)
        cum = _cumsum_rows(g_ref[rows, :])
        cum_scr[rows, :] = cum
        last = cum[c_len - 1:c_len, :]
        q = qk_ref[rows, 0:GLA_KEY_WIDTH]
        k = qk_ref[rows, GLA_KEY_WIDTH:]
        vb = v_ref[rows, :]
        qe = (q * jnp.exp(cum)).astype(BF16)
        ke = (k * jnp.exp(jnp.minimum(-cum, DECAY_CLAMP))).astype(BF16)
        kl = (k * jnp.exp(last - cum)).astype(BF16)
        state = s_ref[...]
        sb = state.astype(BF16)
        s_bd = jnp.concatenate([jnp.where(row_head == h, sb, zero_b) for h in range(GLA_HEADS)], axis=1)
        o_inter = _dot(qe, s_bd)
        inter_scr[rows, :] = o_inter
        ke_bd = jnp.concatenate([jnp.where(lane_head == h, ke, zero_b) for h in range(GLA_HEADS)], axis=0)
        attn = _dot_tb(qe, ke_bd)
        upds = []
        for p in range(GLA_HEADS // 2):
            u = _dot_ta(kl[:, p * LANES:(p + 1) * LANES], vb[:, 2 * p * GLA_HEAD_V:(2 * p + 2) * GLA_HEAD_V])
            upds.append(u[0:GLA_HEAD_K, 0:GLA_HEAD_V])
            upds.append(u[GLA_HEAD_K:, GLA_HEAD_V:])
        yield
        attn = jnp.where(causal_cat, attn, 0.0).astype(BF16)
        o_pairs = []
        for p in range(GLA_HEADS // 2):
            v_a = vb[:, (2 * p) * GLA_HEAD_V:(2 * p + 1) * GLA_HEAD_V]
            v_b = vb[:, (2 * p + 1) * GLA_HEAD_V:(2 * p + 2) * GLA_HEAD_V]
            v_bd = jnp.concatenate([jnp.concatenate([v_a, zero_v], axis=1),
                                    jnp.concatenate([zero_v, v_b], axis=1)], axis=0)
            o_pairs.append(_dot(attn[:, 2 * p * c_len:(2 * p + 2) * c_len], v_bd))
        og_scr[rows, :] = _head_norm_gate(o_inter + jnp.concatenate(o_pairs, axis=1), r_ref[rows, :],
                                          gn_ref[...])
        last_col = jnp.sum(jnp.where(eye, last, 0.0), axis=1, keepdims=True)
        s_ref[...] = state * jnp.exp(last_col) + jnp.concatenate(upds, axis=0)
        worst.append(jnp.min(last, axis=1, keepdims=True))
        if c == n_chunks - 1:
            tile_min = functools.reduce(jnp.minimum, worst)
            flag_ref[0] = jnp.where(tile_min[0, 0] < -DECAY_CLAMP, 1, 0)
        yield

    return itertools.chain.from_iterable(chunk(c) for c in range(n_chunks))


def _gla_tile_exact(qk_ref, v_ref, r_ref, gn_ref, og_scr, o_scr, cum_scr, inter_scr):
    c_len = GLA_CHUNK
    ind = jnp.where(lax.broadcasted_iota(jnp.int32, (GLA_KEY_WIDTH, LANES), 0) // GLA_HEAD_K
                    == lax.broadcasted_iota(jnp.int32, (GLA_KEY_WIDTH, LANES), 1), 1.0, 0.0).astype(BF16)
    j_idx = lax.broadcasted_iota(jnp.int32, (c_len, 1), 0)
    for c in range(ROW_TILE // c_len):
        r0 = c * c_len
        rows = slice(r0, r0 + c_len)

        def one_row(i, carry, r0=r0, rows=rows):
            ci = cum_scr[pl.ds(r0 + i, 1), :]
            qi = qk_ref[pl.ds(r0 + i, 1), 0:GLA_KEY_WIDTH]
            kk = qk_ref[rows, GLA_KEY_WIDTH:]
            dec = jnp.exp(jnp.minimum(ci - cum_scr[rows, :], 0.0))
            a_cols = _dot(((qi * kk) * dec).astype(BF16), ind)
            outs = []
            for h in range(GLA_HEADS):
                w_col = jnp.where(j_idx <= i, a_cols[:, h:h + 1], 0.0)
                v_h = v_ref[rows, h * GLA_HEAD_V:(h + 1) * GLA_HEAD_V].astype(F32)
                outs.append(jnp.sum(w_col * v_h, axis=0, keepdims=True))
            o_scr[pl.ds(r0 + i, 1), :] = inter_scr[pl.ds(r0 + i, 1), :] + jnp.concatenate(outs, axis=1)
            return carry

        lax.fori_loop(0, c_len, one_row, 0)
    og_scr[...] = _head_norm_gate(o_scr[...], r_ref[...], gn_ref[...])


def _dup_halves(x):
    lo = lax.broadcasted_iota(jnp.int32, (1, LANES), 1) < SWA_HEAD_DIM
    sw = pltpu.roll(x, SWA_HEAD_DIM, 1)
    return jnp.where(lo, x, sw).astype(BF16), jnp.where(lo, sw, x).astype(BF16)


def _swa_stages(sink_ref, qs_ref, kv_ref, kvprev_scr, sn_ref, os_scr, no_prev_bias):
    w = WINDOW
    n_blocks = ROW_TILE // w
    lo = lax.broadcasted_iota(jnp.int32, (1, LANES), 1) < SWA_HEAD_DIM
    tri = (lax.broadcasted_iota(jnp.int32, (w, w), 1) <= lax.broadcasted_iota(jnp.int32, (w, w), 0))
    zero_b = jnp.zeros((), BF16)

    def block(b):
        rows = slice(b * w, (b + 1) * w)
        kv_prev = kvprev_scr[...] if b == 0 else kv_ref[(b - 1) * w:b * w, :]
        kv = jnp.concatenate([kv_prev, kv_ref[rows, :]], axis=0)
        kk = _dup_halves(kv[:, 0:SWA_KV_WIDTH])
        vv = _dup_halves(kv[:, SWA_KV_WIDTH:])
        q = qs_ref[rows, :]
        scores = []
        for grp in range(SWA_KV_HEADS):
            stacked = []
            for j in range(SWA_GROUP):
                hq = grp * SWA_GROUP + j
                tile = q[:, (hq // 2) * LANES:(hq // 2 + 1) * LANES]
                stacked.append(jnp.where(lo if hq % 2 == 0 else ~lo, tile, zero_b))
            scores.append(_dot_tb(jnp.concatenate(stacked, axis=0), kk[grp]))
        yield
        outs, dens = [], []
        for grp in range(SWA_KV_HEADS):
            s = scores[grp]
            probs = []
            for j in range(SWA_GROUP):
                sink = sink_ref[grp * SWA_GROUP + j]
                s_prev = s[j * w:(j + 1) * w, 0:w]
                if b == 0:
                    s_prev = s_prev + no_prev_bias
                sf = jnp.where(tri, s[j * w:(j + 1) * w, w:], s_prev)
                m = jnp.maximum(jnp.max(sf, axis=1, keepdims=True), sink)
                p = jnp.exp(sf - m)
                dens.append(jnp.sum(p, axis=1, keepdims=True) + jnp.exp(sink - m))
                pb = p.astype(BF16)
                probs.append(jnp.concatenate([jnp.where(tri, zero_b, pb), jnp.where(tri, pb, zero_b)], axis=1))
            outs.append(_dot(jnp.concatenate(probs, axis=0), vv[grp]))
        yield
        tiles = []
        for grp in range(SWA_KV_HEADS):
            res = [outs[grp][j * w:(j + 1) * w] / dens[grp * SWA_GROUP + j] for j in range(SWA_GROUP)]
            tiles.append(jnp.where(lo, res[0], res[1]))
            tiles.append(jnp.where(lo, res[2], res[3]))
        os_scr[rows, :] = _rms(jnp.concatenate(tiles, axis=1), sn_ref[...]).astype(BF16)
        if b == n_blocks - 1:
            kvprev_scr[...] = kv_ref[rows, :]
        yield

    return itertools.chain.from_iterable(block(b) for b in range(n_blocks))


def _mix_out_body(sink_ref, qk_ref, g_ref, v_ref, r_ref, qs_ref, kv_ref, x1_ref, ogs_ref, oss_ref,
                  gn_ref, sn_ref, wo_ref, n2_ref, wg_ref, wu_ref, wd_ref, nf_ref, yp_ref, ys_ref, s_ref,
                  og_scr, os_scr, kvprev_scr, o_scr, cum_scr, inter_scr, act_scr, flag_ref):
    i = pl.program_id(0)
    n_prompt = pl.num_programs(0) - 1

    def stage_d(og, osw, y_ref, side_stages=None):
        _out_ffn(x1_ref, og, osw, wo_ref, n2_ref, wg_ref, wu_ref, wd_ref, nf_ref, y_ref, act_scr,
                 side_stages)

    def mixer_stages():
        gla = _gla_stages(qk_ref, g_ref, v_ref, r_ref, gn_ref, s_ref, og_scr, cum_scr, inter_scr, flag_ref)
        swa = _swa_stages(sink_ref, qs_ref, kv_ref, kvprev_scr, sn_ref, os_scr,
                          jnp.where(i > 0, 0.0, -jnp.inf))
        return _alternate(gla, swa)

    @pl.when(i == 0)
    def _():
        s_ref[...] = jnp.zeros_like(s_ref)
        kvprev_scr[...] = jnp.zeros_like(kvprev_scr)
        stage_d(ogs_ref[...], oss_ref[...], ys_ref, mixer_stages())

    @pl.when(jnp.logical_and(i > 0, i < n_prompt))
    def _():
        stage_d(og_scr[...], os_scr[...], yp_ref, mixer_stages())

    @pl.when(i == n_prompt)
    def _():
        stage_d(og_scr[...], os_scr[...], yp_ref)
        flag_ref[0] = 0

    @pl.when(flag_ref[0] != 0)
    def _():
        _gla_tile_exact(qk_ref, v_ref, r_ref, gn_ref, og_scr, o_scr, cum_scr, inter_scr)


def _mix_out(qk, g, v, r, qs, kv, x1, og_s, os_s, w):
    t = qk.shape[0] - ROW_TILE
    assert t % ROW_TILE == 0 and og_s.shape[0] == ROW_TILE
    nt = t // ROW_TILE
    cur = lambda n: pl.BlockSpec((ROW_TILE, n), lambda i: (jnp.minimum(i, nt - 1), 0))
    return pl.pallas_call(
        _mix_out_body,
        grid=(nt + 1,),
        in_specs=[pl.BlockSpec(memory_space=pltpu.SMEM),
                  cur(2 * GLA_KEY_WIDTH), cur(GLA_KEY_WIDTH), cur(GLA_WIDTH), cur(GLA_WIDTH),
                  cur(SWA_WIDTH), cur(2 * SWA_KV_WIDTH),
                  pl.BlockSpec((ROW_TILE, D_MODEL), lambda i: (jnp.where(i == 0, nt, i - 1), 0)),
                  _resident((ROW_TILE, GLA_WIDTH)), _resident((ROW_TILE, SWA_WIDTH)),
                  _resident((1, GLA_WIDTH)), _resident((1, SWA_WIDTH)),
                  _resident((D_MODEL, D_MODEL)), _resident((1, D_MODEL)),
                  _resident((D_MODEL, D_FF)), _resident((D_MODEL, D_FF)), _resident((D_FF, D_MODEL)),
                  _resident((1, D_MODEL))],
        out_specs=[pl.BlockSpec((ROW_TILE, D_MODEL), lambda i: (jnp.clip(i - 1, 0, nt - 1), 0)),
                   pl.BlockSpec((ROW_TILE, D_MODEL), lambda i: (0, 0)),
                   pl.BlockSpec((GLA_KEY_WIDTH, GLA_HEAD_V), lambda i: (0, 0))],
        out_shape=[jax.ShapeDtypeStruct((t, D_MODEL), F32),
                   jax.ShapeDtypeStruct((ROW_TILE, D_MODEL), F32),
                   jax.ShapeDtypeStruct((GLA_KEY_WIDTH, GLA_HEAD_V), F32)],
        scratch_shapes=[pltpu.VMEM((ROW_TILE, GLA_WIDTH), BF16), pltpu.VMEM((ROW_TILE, SWA_WIDTH), BF16),
                        pltpu.VMEM((WINDOW, 2 * SWA_KV_WIDTH), F32),
                        pltpu.VMEM((ROW_TILE, GLA_WIDTH), F32), pltpu.VMEM((ROW_TILE, GLA_KEY_WIDTH), F32),
                        pltpu.VMEM((ROW_TILE, GLA_WIDTH), F32), pltpu.VMEM((ROW_TILE, D_FF), BF16),
                        pltpu.SMEM((1,), jnp.int32)],
        compiler_params=pltpu.CompilerParams(dimension_semantics=("arbitrary",),
                                             vmem_limit_bytes=VMEM_LIMIT_BYTES),
        name="prompt_mixers_stage_d",
    )(w["sinks"], qk, g, v, r, qs, kv, x1, og_s, os_s, w["gn"], w["sn"], w["wo"], w["n2"], w["wg2"],
      w["wu2"], w["wd2"], w["nf"])


def _gla_sample_body(qk_ref, g_ref, v_ref, r_ref, gn_ref, s_ref, og_ref, so_ref, o_scr):
    rows = SAMPLE_ROWS
    g = g_ref[...]
    q = qk_ref[:, 0:GLA_KEY_WIDTH]
    k = qk_ref[:, GLA_KEY_WIDTH:]
    vb = v_ref[...]
    vf = vb.astype(F32)
    tok = lax.broadcasted_iota(jnp.int32, (rows, 1), 0) % DEC_SEQ

    cum = g
    for d in range(1, DEC_SEQ):
        cum = cum + jnp.where(tok >= d, pltpu.roll(g, d, 0), 0.0)
    tot = jnp.where(tok == DEC_SEQ - 1, cum, 0.0)
    for d in range(1, DEC_SEQ):
        tot = tot + jnp.where(tok == DEC_SEQ - 1 - d, pltpu.roll(cum, rows - d, 0), 0.0)

    qe = q * jnp.exp(cum)
    kl = k * jnp.exp(tot - cum)
    decay_t = jnp.exp(tot).T

    ind = jnp.where(lax.broadcasted_iota(jnp.int32, (GLA_KEY_WIDTH, LANES), 0) // GLA_HEAD_K
                    == lax.broadcasted_iota(jnp.int32, (GLA_KEY_WIDTH, LANES), 1), 1.0, 0.0).astype(BF16)
    expand = jnp.where(lax.broadcasted_iota(jnp.int32, (LANES, GLA_WIDTH), 0)
                       == lax.broadcasted_iota(jnp.int32, (LANES, GLA_WIDTH), 1) // GLA_HEAD_V,
                       1.0, 0.0).astype(BF16)

    o_intra = jnp.zeros((rows, GLA_WIDTH), F32)
    for d in range(DEC_SEQ):
        k_d = k if d == 0 else pltpu.roll(k, d, 0)
        c_d = cum if d == 0 else pltpu.roll(cum, d, 0)
        v_d = vf if d == 0 else pltpu.roll(vf, d, 0)
        pair = jnp.where(tok >= d, q * k_d * jnp.exp(jnp.minimum(cum - c_d, 0.0)), 0.0)
        a = _dot(pair.astype(BF16), ind)
        o_intra = o_intra + _dot(a.astype(BF16), expand) * v_d

    lane_head = lax.broadcasted_iota(jnp.int32, (1, GLA_KEY_WIDTH), 1) // GLA_HEAD_K
    row8 = lax.broadcasted_iota(jnp.int32, (SUBLANES, 1), 0)
    row32 = lax.broadcasted_iota(jnp.int32, (GLA_HEADS * SUBLANES, 1), 0)
    for pair_idx in range(SAMPLE_SEQS // 2):
        r8 = slice(pair_idx * SUBLANES, (pair_idx + 1) * SUBLANES)
        q8 = qe[r8, :]
        lhs = jnp.concatenate([jnp.where(lane_head == h, q8, 0.0) for h in range(GLA_HEADS)],
                              axis=0).astype(BF16)
        kl8 = kl[r8, :]
        v8 = vb[r8, :]
        res = []
        for s in range(2):
            b = 2 * pair_idx + s
            state = s_ref[b]
            res.append(_dot(lhs, state.astype(BF16)))
            kl_b = jnp.where(row8 // DEC_SEQ == s, kl8, 0.0).astype(BF16)
            upd = _dot_ta(kl_b, v8)
            upd = jnp.concatenate(
                [upd[h * GLA_HEAD_K:(h + 1) * GLA_HEAD_K, h * GLA_HEAD_V:(h + 1) * GLA_HEAD_V]
                 for h in range(GLA_HEADS)], axis=0)
            so_ref[b] = state * decay_t[:, DEC_SEQ * b:DEC_SEQ * b + 1] + upd
        sel = jnp.where(row32 % SUBLANES < DEC_SEQ, res[0], res[1])
        o_scr[r8, :] = jnp.concatenate([sel[h * SUBLANES:(h + 1) * SUBLANES] for h in range(GLA_HEADS)], axis=1)

    og_ref[...] = _head_norm_gate(o_scr[...] + o_intra, r_ref[...], gn_ref[...])


def _gla_sample(qk, g, v, r, gn, state, row0):
    nseq = state.shape[0]
    rows = nseq * DEC_SEQ
    assert nseq % SAMPLE_SEQS == 0 and row0 % SAMPLE_ROWS == 0
    blk0 = row0 // SAMPLE_ROWS
    src = lambda n: pl.BlockSpec((SAMPLE_ROWS, n), lambda i: (i + blk0, 0))
    blk = lambda n: pl.BlockSpec((SAMPLE_ROWS, n), lambda i: (i, 0))
    sblk = pl.BlockSpec((SAMPLE_SEQS, GLA_KEY_WIDTH, GLA_HEAD_V), lambda i: (i, 0, 0))
    return pl.pallas_call(
        _gla_sample_body,
        grid=(nseq // SAMPLE_SEQS,),
        in_specs=[src(2 * GLA_KEY_WIDTH), src(GLA_KEY_WIDTH), src(GLA_WIDTH), src(GLA_WIDTH),
                  pl.BlockSpec((1, GLA_WIDTH), lambda i: (0, 0)), sblk],
        out_specs=[blk(GLA_WIDTH), sblk],
        out_shape=[jax.ShapeDtypeStruct((rows, GLA_WIDTH), BF16),
                   jax.ShapeDtypeStruct((nseq, GLA_KEY_WIDTH, GLA_HEAD_V), F32)],
        scratch_shapes=[pltpu.VMEM((SAMPLE_ROWS, GLA_WIDTH), F32)],
        compiler_params=pltpu.CompilerParams(dimension_semantics=("arbitrary",)),
        name="gla_sample",
    )(qk, g, v, r, gn, state)


def _swa_sample_body(sink_ref, q_ref, kvn_ref, kc_ref, vc_ref, nrm_ref, o_ref, kco_ref, vco_ref,
                     lhs_scr, sc_scr, pc_scr, oc_scr):
    rows = SAMPLE_ROWS
    q = q_ref[...].astype(F32)
    lo = lax.broadcasted_iota(jnp.int32, (1, LANES), 1) < SWA_HEAD_DIM
    for hq in range(SWA_Q_HEADS):
        grp = hq // SWA_GROUP
        tile = q[:, (hq // 2) * LANES:(hq // 2 + 1) * LANES]
        src = tile if hq % 2 == grp else pltpu.roll(tile, SWA_HEAD_DIM, 1)
        lhs_scr[hq * rows:(hq + 1) * rows, :] = jnp.where(lo if grp == 0 else ~lo, src, 0.0)

    kvn = kvn_ref[...]
    k_new = kvn[:, 0:SWA_KV_WIDTH]
    v_new = kvn[:, SWA_KV_WIDTH:]
    s_new = _dot_tb(lhs_scr[...].astype(BF16), k_new.astype(BF16))

    row64 = lax.broadcasted_iota(jnp.int32, (SWA_Q_HEADS * SUBLANES, 1), 0)
    first_of_pair = row64 % SUBLANES < DEC_SEQ

    def gather_pair(ref, pair_idx):
        return jnp.concatenate(
            [ref[hq * rows + pair_idx * SUBLANES:hq * rows + (pair_idx + 1) * SUBLANES, :]
             for hq in range(SWA_Q_HEADS)], axis=0).astype(BF16)

    def scatter_pair(ref, pair_idx, val):
        for hq in range(SWA_Q_HEADS):
            ref[hq * rows + pair_idx * SUBLANES:hq * rows + (pair_idx + 1) * SUBLANES, :] = (
                val[hq * SUBLANES:(hq + 1) * SUBLANES])

    for pair_idx in range(SAMPLE_SEQS // 2):
        l64 = gather_pair(lhs_scr, pair_idx)
        sa = _dot(l64, kc_ref[2 * pair_idx].astype(BF16))
        sb = _dot(l64, kc_ref[2 * pair_idx + 1].astype(BF16))
        scatter_pair(sc_scr, pair_idx, jnp.where(first_of_pair, sa, sb))

    rr = lax.broadcasted_iota(jnp.int32, (rows, rows), 0)
    cc = lax.broadcasted_iota(jnp.int32, (rows, rows), 1)
    tok = rr % DEC_SEQ
    mask_cache = cc > tok
    mask_new = (cc // DEC_SEQ == rr // DEC_SEQ) & (cc % DEC_SEQ <= tok)
    p_new, dens = [], []
    for hq in range(SWA_Q_HEADS):
        sl = slice(hq * rows, (hq + 1) * rows)
        sink = sink_ref[hq]
        s_c = jnp.where(mask_cache, sc_scr[sl, :], -jnp.inf)
        s_n = jnp.where(mask_new, s_new[sl, :], -jnp.inf)
        m = jnp.maximum(jnp.maximum(jnp.max(s_c, axis=1, keepdims=True),
                                    jnp.max(s_n, axis=1, keepdims=True)), sink)
        p_c = jnp.exp(s_c - m)
        p_n = jnp.exp(s_n - m)
        dens.append(jnp.sum(p_c, axis=1, keepdims=True) + jnp.sum(p_n, axis=1, keepdims=True)
                    + jnp.exp(sink - m))
        pc_scr[sl, :] = p_c
        p_new.append(p_n.astype(BF16))
    o_new = _dot(jnp.concatenate(p_new, axis=0), v_new.astype(BF16))

    for pair_idx in range(SAMPLE_SEQS // 2):
        p64 = gather_pair(pc_scr, pair_idx)
        oa = _dot_tb(p64, vc_ref[2 * pair_idx].astype(BF16))
        ob = _dot_tb(p64, vc_ref[2 * pair_idx + 1].astype(BF16))
        scatter_pair(oc_scr, pair_idx, jnp.where(first_of_pair, oa, ob))

    tiles = []
    for i in range(SWA_Q_HEADS // 2):
        halves = []
        for hq in (2 * i, 2 * i + 1):
            sl = slice(hq * rows, (hq + 1) * rows)
            oh = (oc_scr[sl, :] + o_new[sl, :]) / dens[hq]
            halves.append(oh if hq % 2 == hq // SWA_GROUP else pltpu.roll(oh, SWA_HEAD_DIM, 1))
        tiles.append(jnp.where(lo, halves[0], halves[1]))
    o_ref[...] = _rms(jnp.concatenate(tiles, axis=1), nrm_ref[...]).astype(BF16)

    keep = lax.broadcasted_iota(jnp.int32, (1, WINDOW), 1) < WINDOW - DEC_SEQ
    k_new_t = k_new.T
    v_new_t = v_new.T
    for b in range(SAMPLE_SEQS):
        to_tail = (WINDOW - DEC_SEQ - DEC_SEQ * b) % rows
        kco_ref[b] = jnp.where(keep, pltpu.roll(kc_ref[b], WINDOW - DEC_SEQ, 1),
                               pltpu.roll(k_new_t, to_tail, 1))
        vco_ref[b] = jnp.where(keep, pltpu.roll(vc_ref[b], WINDOW - DEC_SEQ, 1),
                               pltpu.roll(v_new_t, to_tail, 1))


def _swa_sample(sinks, qs, kvn, kc, vc, nrm, row0):
    nseq = kc.shape[0]
    rows = nseq * DEC_SEQ
    assert nseq % SAMPLE_SEQS == 0 and row0 % SAMPLE_ROWS == 0
    assert kc.shape[1:] == (SWA_KV_WIDTH, WINDOW) and WINDOW == SAMPLE_ROWS
    blk0 = row0 // SAMPLE_ROWS
    cblk = pl.BlockSpec((SAMPLE_SEQS, SWA_KV_WIDTH, WINDOW), lambda i: (i, 0, 0))
    big = pltpu.VMEM((SWA_Q_HEADS * SAMPLE_ROWS, LANES), F32)
    return pl.pallas_call(
        _swa_sample_body,
        grid=(nseq // SAMPLE_SEQS,),
        in_specs=[pl.BlockSpec(memory_space=pltpu.SMEM),
                  pl.BlockSpec((SAMPLE_ROWS, SWA_WIDTH), lambda i: (i + blk0, 0)),
                  pl.BlockSpec((SAMPLE_ROWS, 2 * SWA_KV_WIDTH), lambda i: (i + blk0, 0)),
                  cblk, cblk,
                  pl.BlockSpec((1, SWA_WIDTH), lambda i: (0, 0))],
        out_specs=[pl.BlockSpec((SAMPLE_ROWS, SWA_WIDTH), lambda i: (i, 0)), cblk, cblk],
        out_shape=[jax.ShapeDtypeStruct((rows, SWA_WIDTH), BF16),
                   jax.ShapeDtypeStruct(kc.shape, F32), jax.ShapeDtypeStruct(vc.shape, F32)],
        scratch_shapes=[big, big, big, big],
        compiler_params=pltpu.CompilerParams(dimension_semantics=("arbitrary",)),
        name="swa_sample",
    )(sinks, qs, kvn, kc, vc, nrm)


def _prepare_weights(ffn1_norm, ffn1_w_gate, ffn1_w_up, ffn1_w_down, mix_norm, w_in, w_gate_up, b_gate,
                     gla_head_norm, swa_out_norm, swa_sinks, w_out,
                     ffn2_norm, ffn2_w_gate, ffn2_w_up, ffn2_w_down, final_norm, layer):
    assert w_in.shape[1:] == (D_MODEL, D_IN) and PROJ_Q_S == 2 * GLA_KEY_WIDTH + 2 * GLA_WIDTH
    wgu = jnp.concatenate([w_gate_up[layer],
                           jnp.zeros((LANES - GLA_GATE_RANK, GLA_KEY_WIDTH), w_gate_up.dtype)], axis=0)
    row = lambda a: a.reshape(1, -1).astype(F32)
    return dict(
        n1=row(ffn1_norm[layer]), wg1=ffn1_w_gate[layer], wu1=ffn1_w_up[layer], wd1=ffn1_w_down[layer],
        nm=row(mix_norm[layer]), win=w_in[layer].T, wgu=wgu.astype(BF16), bg=row(b_gate[layer]),
        gn=row(gla_head_norm[layer]), sn=row(swa_out_norm[layer]), sinks=swa_sinks[layer].astype(F32),
        wo=w_out[layer], n2=row(ffn2_norm[layer]), wg2=ffn2_w_gate[layer], wu2=ffn2_w_up[layer],
        wd2=ffn2_w_down[layer], nf=row(final_norm))


def kernel(x_prompt, x_sample, state_gla, cache_swa_k, cache_swa_v, ffn1_norm, ffn1_w_gate, ffn1_w_up,
           ffn1_w_down, mix_norm, w_in, w_gate_up, b_gate, gla_head_norm, swa_out_norm, swa_sinks, w_out,
           ffn2_norm, ffn2_w_gate, ffn2_w_up, ffn2_w_down, final_norm):
    depth = state_gla.shape[0]
    assert depth == 1 and x_prompt.shape[0] == 1 and x_sample.shape[1] == DEC_SEQ
    seq = x_prompt.shape[1]
    nseq = x_sample.shape[0]
    w = _prepare_weights(ffn1_norm, ffn1_w_gate, ffn1_w_up, ffn1_w_down, mix_norm, w_in, w_gate_up, b_gate,
                         gla_head_norm, swa_out_norm, swa_sinks, w_out,
                         ffn2_norm, ffn2_w_gate, ffn2_w_up, ffn2_w_down, final_norm, 0)

    (x1, qk, g, v, r, qs, kv), (wo, wg2, wu2, wd2) = _stage_a(
        x_prompt.reshape(seq, D_MODEL), x_sample.reshape(nseq * DEC_SEQ, D_MODEL), w)
    w = dict(w, wo=wo, wg2=wg2, wu2=wu2, wd2=wd2)

    og_s, state_s = _gla_sample(qk, g, v, r, w["gn"],
                                state_gla[0].reshape(nseq, GLA_KEY_WIDTH, GLA_HEAD_V), seq)
    cache_w = cache_swa_k.shape[2]
    to_feature_major = lambda c: jnp.transpose(c[0], (0, 2, 3, 1)).reshape(nseq, SWA_KV_WIDTH, cache_w)
    from_feature_major = lambda c: jnp.transpose(
        c.reshape(nseq, SWA_KV_HEADS, SWA_HEAD_DIM, cache_w), (0, 3, 1, 2))[None]
    os_s, k_cache_s, v_cache_s = _swa_sample(
        w["sinks"], qs, kv, to_feature_major(cache_swa_k), to_feature_major(cache_swa_v), w["sn"], seq)
    k_cache_s = from_feature_major(k_cache_s)
    v_cache_s = from_feature_major(v_cache_s)

    y_prompt, y_sample, state_p = _mix_out(qk, g, v, r, qs, kv, x1, og_s, os_s, w)
    y_prompt = y_prompt.reshape(1, seq, D_MODEL)
    y_sample = y_sample.reshape(nseq, DEC_SEQ, D_MODEL)
    cw = min(WINDOW, seq)
    k_cache_p = kv[seq - cw:seq, 0:SWA_KV_WIDTH].reshape(1, 1, cw, SWA_KV_HEADS, SWA_HEAD_DIM)
    v_cache_p = kv[seq - cw:seq, SWA_KV_WIDTH:].reshape(1, 1, cw, SWA_KV_HEADS, SWA_HEAD_DIM)
    state_p = state_p.reshape(1, 1, GLA_HEADS, GLA_HEAD_K, GLA_HEAD_V)
    state_s = state_s.reshape(1, nseq, GLA_HEADS, GLA_HEAD_K, GLA_HEAD_V)

    return (y_prompt, y_sample, state_p, k_cache_p, v_cache_p, state_s, k_cache_s, v_cache_s)
```

```python
import functools
import itertools

import jax
import jax.numpy as jnp
from jax import lax
from jax.experimental import pallas as pl
from jax.experimental.pallas import tpu as pltpu

F32 = jnp.float32
BF16 = jnp.bfloat16

D_MODEL = 1024
D_FF = 2816
GLA_HEADS = 4
GLA_HEAD_K = 64
GLA_HEAD_V = 128
GLA_KEY_WIDTH = GLA_HEADS * GLA_HEAD_K
GLA_WIDTH = GLA_HEADS * GLA_HEAD_V
GLA_GATE_RANK = 16
GLA_GATE_TAU = 16.0
SWA_HEAD_DIM = 64
SWA_Q_HEADS = 8
SWA_KV_HEADS = 2
SWA_GROUP = SWA_Q_HEADS // SWA_KV_HEADS
SWA_WIDTH = SWA_Q_HEADS * SWA_HEAD_DIM
SWA_KV_WIDTH = SWA_KV_HEADS * SWA_HEAD_DIM
WINDOW = 128
DEC_SEQ = 4
NORM_EPS = 1e-6
HEAD_SCALE = 0.125

LANES = 128
SUBLANES = 8
VMEM_CAPACITY_BYTES = 64 * 1024 * 1024
STAGE_A_VMEM_LIMIT = VMEM_CAPACITY_BYTES - 6 * 1024 * 1024
STAGE_D_VMEM_LIMIT = VMEM_CAPACITY_BYTES - 8 * 1024 * 1024

ROW_TILE = 512
FF_CHUNK = 256
DOWN_CHUNK = 256
GLA_CHUNK = 128
SAMPLE_SEQS = 32
SAMPLE_ROWS = SAMPLE_SEQS * DEC_SEQ
MIXER_STAGES = 2 * (ROW_TILE // GLA_CHUNK) + 3 * (ROW_TILE // WINDOW)
SIDE_STAGES_AT_NORM = 2
DECAY_CLAMP = 60.0

PROJ_Q_G = 0
PROJ_K_G = PROJ_Q_G + GLA_KEY_WIDTH
PROJ_V_G = PROJ_K_G + GLA_KEY_WIDTH
PROJ_R_G = PROJ_V_G + GLA_WIDTH
PROJ_Q_S = PROJ_R_G + GLA_WIDTH
PROJ_KV_S = PROJ_Q_S + SWA_WIDTH
PROJ_A = PROJ_KV_S + 2 * SWA_KV_WIDTH
PROJ_WIDTH = PROJ_A + LANES
D_IN = PROJ_A + GLA_GATE_RANK


def _dot(a, b):
    return jnp.dot(a, b, preferred_element_type=F32)


def _dot_tb(a, b):
    return lax.dot_general(a, b, (((1,), (1,)), ((), ())), preferred_element_type=F32)


def _dot_ta(a, b):
    return lax.dot_general(a, b, (((0,), (0,)), ((), ())), preferred_element_type=F32)


def _rms(x, g):
    return x * lax.rsqrt(jnp.mean(x * x, axis=-1, keepdims=True) + NORM_EPS) * g


def _gate_up(h, wg_ref, wu_ref, act_ref, after_dot=lambda: None):
    for c0 in range(0, D_FF, FF_CHUNK):
        g = _dot(h, wg_ref[:, c0:c0 + FF_CHUNK])
        after_dot()
        u = _dot(h, wu_ref[:, c0:c0 + FF_CHUNK])
        after_dot()
        act_ref[:, c0:c0 + FF_CHUNK] = (g * jax.nn.sigmoid(g) * u).astype(BF16)


def _down(act_ref, rows, wd_ref, after_dot=lambda: None):
    outs = []
    for n0 in range(0, D_MODEL, DOWN_CHUNK):
        outs.append(_dot(act_ref[rows, :], wd_ref[:, n0:n0 + DOWN_CHUNK]))
        after_dot()
    return jnp.concatenate(outs, axis=1)


def _head_norm_gate(o, r, gn):
    parts = []
    for h in range(GLA_HEADS):
        sl = slice(h * GLA_HEAD_V, (h + 1) * GLA_HEAD_V)
        parts.append(_rms(o[:, sl], gn[:, sl]))
    return (jnp.concatenate(parts, axis=1) * (r * jax.nn.sigmoid(r))).astype(BF16)


def _out_ffn(x1_ref, og, osw, wo_ref, n2_ref, wg_ref, wu_ref, wd_ref, nf_ref, y_ref, act_ref, side_stages=None):
    n_inside = 0 if side_stages is None else MIXER_STAGES - 2 * SIDE_STAGES_AT_NORM
    side_stages = iter(()) if side_stages is None else side_stages
    n_dots = 2 * (D_FF // FF_CHUNK) + D_MODEL // DOWN_CHUNK
    done = [0, 0]

    def after_dot():
        done[0] += 1
        while done[1] * n_dots < done[0] * n_inside:
            next(side_stages, None)
            done[1] += 1

    mixed = jnp.concatenate([og, osw], axis=1)
    x2 = x1_ref[...] + _dot(mixed, wo_ref[...])
    for _ in range(SIDE_STAGES_AT_NORM):
        next(side_stages, None)
    h = _rms(x2, n2_ref[...]).astype(BF16)
    _gate_up(h, wg_ref, wu_ref, act_ref, after_dot)
    x3 = x2 + 0.5 * _down(act_ref, slice(None), wd_ref, after_dot)
    for _ in side_stages:
        pass
    y_ref[...] = _rms(x3, nf_ref[...])


_HBM = pl.BlockSpec(memory_space=pl.ANY)
WIDE_CHUNK_ROWS = 128
NARROW_CHUNK_ROWS = 256
LOAD_SLOTS = 4


def _load_weights_bf16(loads, staging, sems):
    jobs = []
    for src, dst, segments, transposed in loads:
        cols = src.shape[1]
        assert cols in (D_FF, D_MODEL) and dst.shape[0 if transposed else 1] == cols
        kind, step = (0, WIDE_CHUNK_ROWS) if cols == D_FF else (1, NARROW_CHUNK_ROWS)
        for s0, d0, n in segments or [(0, 0, src.shape[0])]:
            jobs += [(src, dst, s0 + r, d0 + r, min(step, n - r), kind, transposed)
                     for r in range(0, n, step)]

    used = [0, 0]
    copies = []
    for src, _, s0, _, nr, kind, _ in jobs:
        slot = used[kind] % LOAD_SLOTS
        used[kind] += 1
        copies.append((pltpu.make_async_copy(src.at[pl.ds(s0, nr), :], staging[kind].at[slot, pl.ds(0, nr), :],
                                             sems.at[kind, slot]), slot))
    for copy, _ in copies[:LOAD_SLOTS - 1]:
        copy.start()
    for j, (_, dst, _, d0, nr, kind, transposed) in enumerate(jobs):
        ahead = j + LOAD_SLOTS - 1
        if ahead < len(jobs):
            copies[ahead][0].start()
        copy, slot = copies[j]
        copy.wait()
        if not transposed:
            dst[d0:d0 + nr, :] = staging[kind][slot, 0:nr, :].astype(BF16)
        elif nr % LANES == 0:
            dst[:, d0:d0 + nr] = staging[kind][slot, 0:nr, :].T.astype(BF16)
        else:
            tile = staging[kind][slot, 0:LANES, :].T
            live = lax.broadcasted_iota(jnp.int32, (1, LANES), 1) < nr
            dst[:, d0:d0 + LANES] = jnp.where(live, tile, 0.0).astype(BF16)


def _staging_scratch():
    return [pltpu.VMEM((LOAD_SLOTS, WIDE_CHUNK_ROWS, D_FF), F32),
            pltpu.VMEM((LOAD_SLOTS, NARROW_CHUNK_ROWS, D_MODEL), F32),
            pltpu.SemaphoreType.DMA((2, LOAD_SLOTS))]


def _chunk_plan(matrices):
    plan, first = [], 0
    for m, mat in enumerate(matrices):
        rows, cols = mat.shape
        assert cols in (D_FF, D_MODEL)
        kind, nr = (0, WIDE_CHUNK_ROWS) if cols == D_FF else (1, NARROW_CHUNK_ROWS)
        assert rows % nr == 0
        plan.append((m, first, rows // nr, nr, kind))
        first += rows // nr
    return plan, first


def _background_cast(step, srcs, dsts, staging, out_staging, sem_in, sem_out):
    plan, _ = _chunk_plan(srcs)
    slot = step % 2

    def rows_of(m, chunk):
        _, first, _, nr, _ = plan[m]
        return pl.ds(pl.multiple_of((chunk - first) * nr, nr), nr)

    def copy_in(m, chunk, slot_):
        kind = plan[m][4]
        return pltpu.make_async_copy(srcs[m].at[rows_of(m, chunk), :], staging[kind].at[slot_],
                                     sem_in.at[kind, slot_])

    def copy_out(m, chunk, slot_):
        kind = plan[m][4]
        return pltpu.make_async_copy(out_staging[kind].at[slot_], dsts[m].at[rows_of(m, chunk), :],
                                     sem_out.at[kind, slot_])

    def for_chunk(chunk, fn):
        for m, first, n, _, _ in plan:
            pl.when(jnp.logical_and(chunk >= first, chunk < first + n))(functools.partial(fn, m))

    def before():
        @pl.when(step == 0)
        def _():
            copy_in(0, 0, 0).start()
        for_chunk(step, lambda m: copy_in(m, step, slot).wait())
        for_chunk(step - 2, lambda m: copy_out(m, step - 2, slot).wait())
        for_chunk(step + 1, lambda m: copy_in(m, step + 1, 1 - slot).start())

    def cast():
        for kind in range(2):
            out_staging[kind][slot] = staging[kind][slot].astype(BF16)

    def after():
        for_chunk(step, lambda m: copy_out(m, step, slot).start())

    return before, cast, after


def _stage_a_body(xp_ref, xs_ref, n1_ref, wg_hbm, wu_hbm, wd_hbm, nm_ref, win_hbm, wgu_ref, bg_ref,
                  wo_hbm, wg2_hbm, wu2_hbm, wd2_hbm,
                  x1_ref, qk_ref, g_ref, v_ref, r_ref, qs_ref, kv_ref, wo_out, wg2_out, wu2_out, wd2_out,
                  act_ref, wg_ref, wu_ref, wd_ref, win_ref, wide_stage, narrow_stage, load_sems,
                  wide_out, narrow_out, sem_in, sem_out):
    i = pl.program_id(0)

    @pl.when(i == 0)
    def _():
        a0, a1 = PROJ_Q_S, PROJ_Q_S + GLA_GATE_RANK
        win_rows = [(0, 0, a0), (a1, a0, D_IN - a1), (a0, PROJ_A, GLA_GATE_RANK)]
        _load_weights_bf16([(wg_hbm, wg_ref, None, False), (wu_hbm, wu_ref, None, False),
                            (wd_hbm, wd_ref, None, False), (win_hbm, win_ref, win_rows, True)],
                           (wide_stage, narrow_stage), load_sems)

    bg_before, bg_cast, bg_after = _background_cast(
        i, (wo_hbm, wg2_hbm, wu2_hbm, wd2_hbm), (wo_out, wg2_out, wu2_out, wd2_out),
        (wide_stage, narrow_stage), (wide_out, narrow_out), sem_in, sem_out)
    bg_before()

    x = jnp.where(i < pl.num_programs(0) - 1, xp_ref[...], xs_ref[...])
    h = _rms(x, n1_ref[...]).astype(BF16)
    _gate_up(h, wg_ref, wu_ref, act_ref)
    bg_cast()

    halves = [slice(k * ROW_TILE // 2, (k + 1) * ROW_TILE // 2) for k in range(2)]
    x1 = []
    for rows in halves:
        x1.append(x[rows, :] + 0.5 * _down(act_ref, rows, wd_ref))
        x1_ref[rows, :] = x1[-1]
    gate_in = []
    for rows, x1_h in zip(halves, x1):
        h2 = _rms(x1_h, nm_ref[...]).astype(BF16)
        proj = _dot(h2, win_ref[...])
        qk_ref[rows, 0:GLA_KEY_WIDTH] = proj[:, PROJ_Q_G:PROJ_K_G] * HEAD_SCALE
        qk_ref[rows, GLA_KEY_WIDTH:] = proj[:, PROJ_K_G:PROJ_V_G]
        v_ref[rows, :] = proj[:, PROJ_V_G:PROJ_R_G].astype(BF16)
        r_ref[rows, :] = proj[:, PROJ_R_G:PROJ_Q_S]
        qs_ref[rows, :] = (proj[:, PROJ_Q_S:PROJ_KV_S] * HEAD_SCALE).astype(BF16)
        kv_ref[rows, :] = proj[:, PROJ_KV_S:PROJ_A]
        gate_in.append(proj[:, PROJ_A:PROJ_WIDTH].astype(BF16))
    for rows, a in zip(halves, gate_in):
        z = _dot(a, wgu_ref[...]) + bg_ref[...]
        g_ref[rows, :] = jax.nn.log_sigmoid(z) * (1.0 / GLA_GATE_TAU)
    bg_after()


def _resident(shape):
    return pl.BlockSpec(shape, lambda i: (0,) * len(shape), pipeline_mode=pl.Buffered(1))


def _rows(tm, n):
    return pl.BlockSpec((tm, n), lambda i: (i, 0))


def _stage_a(xp, xs, w):
    t = xp.shape[0]
    assert t % ROW_TILE == 0 and xs.shape[0] == ROW_TILE
    nt = t // ROW_TILE
    out_widths = ((D_MODEL, F32), (2 * GLA_KEY_WIDTH, F32), (GLA_KEY_WIDTH, F32), (GLA_WIDTH, BF16),
                  (GLA_WIDTH, F32), (SWA_WIDTH, BF16), (2 * SWA_KV_WIDTH, F32))
    to_cast = (w["wo"], w["wg2"], w["wu2"], w["wd2"])
    assert _chunk_plan(to_cast)[1] + 2 <= nt + 1
    outs = pl.pallas_call(
        _stage_a_body,
        grid=(nt + 1,),
        in_specs=[pl.BlockSpec((ROW_TILE, D_MODEL), lambda i: (jnp.minimum(i, nt - 1), 0)),
                  _resident((ROW_TILE, D_MODEL)), _resident((1, D_MODEL)),
                  _HBM, _HBM, _HBM,
                  _resident((1, D_MODEL)), _HBM,
                  _resident((LANES, GLA_KEY_WIDTH)), _resident((1, GLA_KEY_WIDTH)),
                  _HBM, _HBM, _HBM, _HBM],
        out_specs=[_rows(ROW_TILE, n) for n, _ in out_widths] + [_HBM] * len(to_cast),
        out_shape=([jax.ShapeDtypeStruct((t + ROW_TILE, n), dt) for n, dt in out_widths]
                   + [jax.ShapeDtypeStruct(m.shape, BF16) for m in to_cast]),
        scratch_shapes=([pltpu.VMEM((ROW_TILE, D_FF), BF16), pltpu.VMEM((D_MODEL, D_FF), BF16),
                         pltpu.VMEM((D_MODEL, D_FF), BF16), pltpu.VMEM((D_FF, D_MODEL), BF16),
                         pltpu.VMEM((D_MODEL, PROJ_WIDTH), BF16)]
                        + _staging_scratch()
                        + [pltpu.VMEM((2, WIDE_CHUNK_ROWS, D_FF), BF16),
                           pltpu.VMEM((2, NARROW_CHUNK_ROWS, D_MODEL), BF16),
                           pltpu.SemaphoreType.DMA((2, 2)), pltpu.SemaphoreType.DMA((2, 2))]),
        compiler_params=pltpu.CompilerParams(dimension_semantics=("arbitrary",),
                                             vmem_limit_bytes=STAGE_A_VMEM_LIMIT),
        name="stage_a_ffn1_proj",
    )(xp, xs, w["n1"], w["wg1"], w["wu1"], w["wd1"], w["nm"], w["win"], w["wgu"], w["bg"], *to_cast)
    return outs[:len(out_widths)], outs[len(out_widths):]


def _cumsum_rows(x):
    n = x.shape[0]
    row = lax.broadcasted_iota(jnp.int32, (n, 1), 0)
    shift = 1
    while shift < n:
        x = x + jnp.where(row >= shift, pltpu.roll(x, shift, 0), 0.0)
        shift *= 2
    return x


def _alternate(a, b):
    pending = [iter(a), iter(b)]
    while pending:
        for it in list(pending):
            try:
                next(it)
            except StopIteration:
                pending.remove(it)
                continue
            yield


def _gla_stages(qk_ref, g_ref, v_ref, r_ref, gn_ref, s_ref, og_scr, cum_scr, inter_scr, flag_ref):
    c_len = GLA_CHUNK
    n_chunks = ROW_TILE // c_len
    causal_cat = (lax.broadcasted_iota(jnp.int32, (c_len, GLA_HEADS * c_len), 0)
                  >= lax.broadcasted_iota(jnp.int32, (c_len, GLA_HEADS * c_len), 1) % c_len)
    lane_head = lax.broadcasted_iota(jnp.int32, (1, GLA_KEY_WIDTH), 1) // GLA_HEAD_K
    row_head = lax.broadcasted_iota(jnp.int32, (GLA_KEY_WIDTH, 1), 0) // GLA_HEAD_K
    eye = (lax.broadcasted_iota(jnp.int32, (GLA_KEY_WIDTH, GLA_KEY_WIDTH), 0)
           == lax.broadcasted_iota(jnp.int32, (GLA_KEY_WIDTH, GLA_KEY_WIDTH), 1))
    zero_b = jnp.zeros((), BF16)
    zero_v = jnp.zeros((c_len, GLA_HEAD_V), BF16)
    worst = []

    def chunk(c):
        rows = slice(c * c_len, (c + 1) * c_len)
        cum = _cumsum_rows(g_ref[rows, :])
        cum_scr[rows, :] = cum
        last = cum[c_len - 1:c_len, :]
        q = qk_ref[rows, 0:GLA_KEY_WIDTH]
        k = qk_ref[rows, GLA_KEY_WIDTH:]
        vb = v_ref[rows, :]
        qe = (q * jnp.exp(cum)).astype(BF16)
        ke = (k * jnp.exp(jnp.minimum(-cum, DECAY_CLAMP))).astype(BF16)
        kl = (k * jnp.exp(last - cum)).astype(BF16)
        state = s_ref[...]
        sb = state.astype(BF16)
        s_bd = jnp.concatenate([jnp.where(row_head == h, sb, zero_b) for h in range(GLA_HEADS)], axis=1)
        o_inter = _dot(qe, s_bd)
        inter_scr[rows, :] = o_inter
        ke_bd = jnp.concatenate([jnp.where(lane_head == h, ke, zero_b) for h in range(GLA_HEADS)], axis=0)
        attn = _dot_tb(qe, ke_bd)
        upds = []
        for p in range(GLA_HEADS // 2):
            u = _dot_ta(kl[:, p * LANES:(p + 1) * LANES], vb[:, 2 * p * GLA_HEAD_V:(2 * p + 2) * GLA_HEAD_V])
            upds.append(u[0:GLA_HEAD_K, 0:GLA_HEAD_V])
            upds.append(u[GLA_HEAD_K:, GLA_HEAD_V:])
        yield
        attn = jnp.where(causal_cat, attn, 0.0).astype(BF16)
        o_pairs = []
        for p in range(GLA_HEADS // 2):
            v_a = vb[:, (2 * p) * GLA_HEAD_V:(2 * p + 1) * GLA_HEAD_V]
            v_b = vb[:, (2 * p + 1) * GLA_HEAD_V:(2 * p + 2) * GLA_HEAD_V]
            v_bd = jnp.concatenate([jnp.concatenate([v_a, zero_v], axis=1),
                                    jnp.concatenate([zero_v, v_b], axis=1)], axis=0)
            o_pairs.append(_dot(attn[:, 2 * p * c_len:(2 * p + 2) * c_len], v_bd))
        og_scr[rows, :] = _head_norm_gate(o_inter + jnp.concatenate(o_pairs, axis=1), r_ref[rows, :],
                                          gn_ref[...])
        last_col = jnp.sum(jnp.where(eye, last, 0.0), axis=1, keepdims=True)
        s_ref[...] = state * jnp.exp(last_col) + jnp.concatenate(upds, axis=0)
        worst.append(jnp.min(last, axis=1, keepdims=True))
        if c == n_chunks - 1:
            tile_min = functools.reduce(jnp.minimum, worst)
            flag_ref[0] = jnp.where(tile_min[0, 0] < -DECAY_CLAMP, 1, 0)
        yield

    return itertools.chain.from_iterable(chunk(c) for c in range(n_chunks))


def _gla_tile_exact(qk_ref, v_ref, r_ref, gn_ref, og_scr, o_scr, cum_scr, inter_scr):
    c_len = GLA_CHUNK
    ind = jnp.where(lax.broadcasted_iota(jnp.int32, (GLA_KEY_WIDTH, LANES), 0) // GLA_HEAD_K
                    == lax.broadcasted_iota(jnp.int32, (GLA_KEY_WIDTH, LANES), 1), 1.0, 0.0).astype(BF16)
    j_idx = lax.broadcasted_iota(jnp.int32, (c_len, 1), 0)
    for c in range(ROW_TILE // c_len):
        r0 = c * c_len
        rows = slice(r0, r0 + c_len)

        def one_row(i, carry, r0=r0, rows=rows):
            ci = cum_scr[pl.ds(r0 + i, 1), :]
            qi = qk_ref[pl.ds(r0 + i, 1), 0:GLA_KEY_WIDTH]
            kk = qk_ref[rows, GLA_KEY_WIDTH:]
            dec = jnp.exp(jnp.minimum(ci - cum_scr[rows, :], 0.0))
            a_cols = _dot(((qi * kk) * dec).astype(BF16), ind)
            outs = []
            for h in range(GLA_HEADS):
                w_col = jnp.where(j_idx <= i, a_cols[:, h:h + 1], 0.0)
                v_h = v_ref[rows, h * GLA_HEAD_V:(h + 1) * GLA_HEAD_V].astype(F32)
                outs.append(jnp.sum(w_col * v_h, axis=0, keepdims=True))
            o_scr[pl.ds(r0 + i, 1), :] = inter_scr[pl.ds(r0 + i, 1), :] + jnp.concatenate(outs, axis=1)
            return carry

        lax.fori_loop(0, c_len, one_row, 0)
    og_scr[...] = _head_norm_gate(o_scr[...], r_ref[...], gn_ref[...])


def _dup_halves(x):
    lo = lax.broadcasted_iota(jnp.int32, (1, LANES), 1) < SWA_HEAD_DIM
    sw = pltpu.roll(x, SWA_HEAD_DIM, 1)
    return jnp.where(lo, x, sw).astype(BF16), jnp.where(lo, sw, x).astype(BF16)


def _swa_stages(sink_ref, qs_ref, kv_ref, kvprev_scr, sn_ref, os_scr, no_prev_bias):
    w = WINDOW
    n_blocks = ROW_TILE // w
    lo = lax.broadcasted_iota(jnp.int32, (1, LANES), 1) < SWA_HEAD_DIM
    tri = (lax.broadcasted_iota(jnp.int32, (w, w), 1) <= lax.broadcasted_iota(jnp.int32, (w, w), 0))
    zero_b = jnp.zeros((), BF16)

    def block(b):
        rows = slice(b * w, (b + 1) * w)
        kv_prev = kvprev_scr[...] if b == 0 else kv_ref[(b - 1) * w:b * w, :]
        kv = jnp.concatenate([kv_prev, kv_ref[rows, :]], axis=0)
        kk = _dup_halves(kv[:, 0:SWA_KV_WIDTH])
        vv = _dup_halves(kv[:, SWA_KV_WIDTH:])
        q = qs_ref[rows, :]
        scores = []
        for grp in range(SWA_KV_HEADS):
            stacked = []
            for j in range(SWA_GROUP):
                hq = grp * SWA_GROUP + j
                tile = q[:, (hq // 2) * LANES:(hq // 2 + 1) * LANES]
                stacked.append(jnp.where(lo if hq % 2 == 0 else ~lo, tile, zero_b))
            scores.append(_dot_tb(jnp.concatenate(stacked, axis=0), kk[grp]))
        yield
        outs, dens = [], []
        for grp in range(SWA_KV_HEADS):
            s = scores[grp]
            probs = []
            for j in range(SWA_GROUP):
                sink = sink_ref[grp * SWA_GROUP + j]
                s_prev = s[j * w:(j + 1) * w, 0:w]
                if b == 0:
                    s_prev = s_prev + no_prev_bias
                sf = jnp.where(tri, s[j * w:(j + 1) * w, w:], s_prev)
                m = jnp.maximum(jnp.max(sf, axis=1, keepdims=True), sink)
                p = jnp.exp(sf - m)
                dens.append(jnp.sum(p, axis=1, keepdims=True) + jnp.exp(sink - m))
                pb = p.astype(BF16)
                probs.append(jnp.concatenate([jnp.where(tri, zero_b, pb), jnp.where(tri, pb, zero_b)], axis=1))
            outs.append(_dot(jnp.concatenate(probs, axis=0), vv[grp]))
        yield
        tiles = []
        for grp in range(SWA_KV_HEADS):
            res = [outs[grp][j * w:(j + 1) * w] / dens[grp * SWA_GROUP + j] for j in range(SWA_GROUP)]
            tiles.append(jnp.where(lo, res[0], res[1]))
            tiles.append(jnp.where(lo, res[2], res[3]))
        os_scr[rows, :] = _rms(jnp.concatenate(tiles, axis=1), sn_ref[...]).astype(BF16)
        if b == n_blocks - 1:
            kvprev_scr[...] = kv_ref[rows, :]
        yield

    return itertools.chain.from_iterable(block(b) for b in range(n_blocks))


def _mix_out_body(sink_ref, qk_ref, g_ref, v_ref, r_ref, qs_ref, kv_ref, x1_ref, ogs_ref, oss_ref,
                  gn_ref, sn_ref, wo_ref, n2_ref, wg_ref, wu_ref, wd_ref, nf_ref, yp_ref, ys_ref, s_ref,
                  og_scr, os_scr, kvprev_scr, o_scr, cum_scr, inter_scr, act_scr, flag_ref):
    i = pl.program_id(0)
    n_prompt = pl.num_programs(0) - 1

    def stage_d(og, osw, y_ref, side_stages=None):
        _out_ffn(x1_ref, og, osw, wo_ref, n2_ref, wg_ref, wu_ref, wd_ref, nf_ref, y_ref, act_scr,
                 side_stages)

    def mixer_stages():
        gla = _gla_stages(qk_ref, g_ref, v_ref, r_ref, gn_ref, s_ref, og_scr, cum_scr, inter_scr, flag_ref)
        swa = _swa_stages(sink_ref, qs_ref, kv_ref, kvprev_scr, sn_ref, os_scr,
                          jnp.where(i > 0, 0.0, -jnp.inf))
        return _alternate(gla, swa)

    @pl.when(i == 0)
    def _():
        s_ref[...] = jnp.zeros_like(s_ref)
        kvprev_scr[...] = jnp.zeros_like(kvprev_scr)
        stage_d(ogs_ref[...], oss_ref[...], ys_ref, mixer_stages())

    @pl.when(jnp.logical_and(i > 0, i < n_prompt))
    def _():
        stage_d(og_scr[...], os_scr[...], yp_ref, mixer_stages())

    @pl.when(i == n_prompt)
    def _():
        stage_d(og_scr[...], os_scr[...], yp_ref)
        flag_ref[0] = 0

    @pl.when(flag_ref[0] != 0)
    def _():
        _gla_tile_exact(qk_ref, v_ref, r_ref, gn_ref, og_scr, o_scr, cum_scr, inter_scr)


def _mix_out(qk, g, v, r, qs, kv, x1, og_s, os_s, w):
    t = qk.shape[0] - ROW_TILE
    assert t % ROW_TILE == 0 and og_s.shape[0] == ROW_TILE
    nt = t // ROW_TILE
    cur = lambda n: pl.BlockSpec((ROW_TILE, n), lambda i: (jnp.minimum(i, nt - 1), 0))
    return pl.pallas_call(
        _mix_out_body,
        grid=(nt + 1,),
        in_specs=[pl.BlockSpec(memory_space=pltpu.SMEM),
                  cur(2 * GLA_KEY_WIDTH), cur(GLA_KEY_WIDTH), cur(GLA_WIDTH), cur(GLA_WIDTH),
                  cur(SWA_WIDTH), cur(2 * SWA_KV_WIDTH),
                  pl.BlockSpec((ROW_TILE, D_MODEL), lambda i: (jnp.where(i == 0, nt, i - 1), 0)),
                  _resident((ROW_TILE, GLA_WIDTH)), _resident((ROW_TILE, SWA_WIDTH)),
                  _resident((1, GLA_WIDTH)), _resident((1, SWA_WIDTH)),
                  _resident((D_MODEL, D_MODEL)), _resident((1, D_MODEL)),
                  _resident((D_MODEL, D_FF)), _resident((D_MODEL, D_FF)), _resident((D_FF, D_MODEL)),
                  _resident((1, D_MODEL))],
        out_specs=[pl.BlockSpec((ROW_TILE, D_MODEL), lambda i: (jnp.clip(i - 1, 0, nt - 1), 0)),
                   pl.BlockSpec((ROW_TILE, D_MODEL), lambda i: (0, 0)),
                   pl.BlockSpec((GLA_KEY_WIDTH, GLA_HEAD_V), lambda i: (0, 0))],
        out_shape=[jax.ShapeDtypeStruct((t, D_MODEL), F32),
                   jax.ShapeDtypeStruct((ROW_TILE, D_MODEL), F32),
                   jax.ShapeDtypeStruct((GLA_KEY_WIDTH, GLA_HEAD_V), F32)],
        scratch_shapes=[pltpu.VMEM((ROW_TILE, GLA_WIDTH), BF16), pltpu.VMEM((ROW_TILE, SWA_WIDTH), BF16),
                        pltpu.VMEM((WINDOW, 2 * SWA_KV_WIDTH), F32),
                        pltpu.VMEM((ROW_TILE, GLA_WIDTH), F32), pltpu.VMEM((ROW_TILE, GLA_KEY_WIDTH), F32),
                        pltpu.VMEM((ROW_TILE, GLA_WIDTH), F32), pltpu.VMEM((ROW_TILE, D_FF), BF16),
                        pltpu.SMEM((1,), jnp.int32)],
        compiler_params=pltpu.CompilerParams(dimension_semantics=("arbitrary",),
                                             vmem_limit_bytes=STAGE_D_VMEM_LIMIT),
        name="prompt_mixers_stage_d",
    )(w["sinks"], qk, g, v, r, qs, kv, x1, og_s, os_s, w["gn"], w["sn"], w["wo"], w["n2"], w["wg2"],
      w["wu2"], w["wd2"], w["nf"])


def _gla_sample_body(qk_ref, g_ref, v_ref, r_ref, gn_ref, s_ref, og_ref, so_ref, o_scr):
    rows = SAMPLE_ROWS
    g = g_ref[...]
    q = qk_ref[:, 0:GLA_KEY_WIDTH]
    k = qk_ref[:, GLA_KEY_WIDTH:]
    vb = v_ref[...]
    vf = vb.astype(F32)
    tok = lax.broadcasted_iota(jnp.int32, (rows, 1), 0) % DEC_SEQ

    cum = g
    for d in range(1, DEC_SEQ):
        cum = cum + jnp.where(tok >= d, pltpu.roll(g, d, 0), 0.0)
    tot = jnp.where(tok == DEC_SEQ - 1, cum, 0.0)
    for d in range(1, DEC_SEQ):
        tot = tot + jnp.where(tok == DEC_SEQ - 1 - d, pltpu.roll(cum, rows - d, 0), 0.0)

    qe = q * jnp.exp(cum)
    kl = k * jnp.exp(tot - cum)
    decay_t = jnp.exp(tot).T

    ind = jnp.where(lax.broadcasted_iota(jnp.int32, (GLA_KEY_WIDTH, LANES), 0) // GLA_HEAD_K
                    == lax.broadcasted_iota(jnp.int32, (GLA_KEY_WIDTH, LANES), 1), 1.0, 0.0).astype(BF16)
    expand = jnp.where(lax.broadcasted_iota(jnp.int32, (LANES, GLA_WIDTH), 0)
                       == lax.broadcasted_iota(jnp.int32, (LANES, GLA_WIDTH), 1) // GLA_HEAD_V,
                       1.0, 0.0).astype(BF16)

    o_intra = jnp.zeros((rows, GLA_WIDTH), F32)
    for d in range(DEC_SEQ):
        k_d = k if d == 0 else pltpu.roll(k, d, 0)
        c_d = cum if d == 0 else pltpu.roll(cum, d, 0)
        v_d = vf if d == 0 else pltpu.roll(vf, d, 0)
        pair = jnp.where(tok >= d, q * k_d * jnp.exp(jnp.minimum(cum - c_d, 0.0)), 0.0)
        a = _dot(pair.astype(BF16), ind)
        o_intra = o_intra + _dot(a.astype(BF16), expand) * v_d

    lane_head = lax.broadcasted_iota(jnp.int32, (1, GLA_KEY_WIDTH), 1) // GLA_HEAD_K
    row8 = lax.broadcasted_iota(jnp.int32, (SUBLANES, 1), 0)
    row32 = lax.broadcasted_iota(jnp.int32, (GLA_HEADS * SUBLANES, 1), 0)
    for pair_idx in range(SAMPLE_SEQS // 2):
        r8 = slice(pair_idx * SUBLANES, (pair_idx + 1) * SUBLANES)
        q8 = qe[r8, :]
        lhs = jnp.concatenate([jnp.where(lane_head == h, q8, 0.0) for h in range(GLA_HEADS)],
                              axis=0).astype(BF16)
        kl8 = kl[r8, :]
        v8 = vb[r8, :]
        res = []
        for s in range(2):
            b = 2 * pair_idx + s
            state = s_ref[b]
            res.append(_dot(lhs, state.astype(BF16)))
            kl_b = jnp.where(row8 // DEC_SEQ == s, kl8, 0.0).astype(BF16)
            upd = _dot_ta(kl_b, v8)
            upd = jnp.concatenate(
                [upd[h * GLA_HEAD_K:(h + 1) * GLA_HEAD_K, h * GLA_HEAD_V:(h + 1) * GLA_HEAD_V]
                 for h in range(GLA_HEADS)], axis=0)
            so_ref[b] = state * decay_t[:, DEC_SEQ * b:DEC_SEQ * b + 1] + upd
        sel = jnp.where(row32 % SUBLANES < DEC_SEQ, res[0], res[1])
        o_scr[r8, :] = jnp.concatenate([sel[h * SUBLANES:(h + 1) * SUBLANES] for h in range(GLA_HEADS)], axis=1)

    og_ref[...] = _head_norm_gate(o_scr[...] + o_intra, r_ref[...], gn_ref[...])


def _gla_sample(qk, g, v, r, gn, state, row0):
    nseq = state.shape[0]
    rows = nseq * DEC_SEQ
    assert nseq % SAMPLE_SEQS == 0 and row0 % SAMPLE_ROWS == 0
    blk0 = row0 // SAMPLE_ROWS
    src = lambda n: pl.BlockSpec((SAMPLE_ROWS, n), lambda i: (i + blk0, 0))
    blk = lambda n: pl.BlockSpec((SAMPLE_ROWS, n), lambda i: (i, 0))
    sblk = pl.BlockSpec((SAMPLE_SEQS, GLA_KEY_WIDTH, GLA_HEAD_V), lambda i: (i, 0, 0))
    return pl.pallas_call(
        _gla_sample_body,
        grid=(nseq // SAMPLE_SEQS,),
        in_specs=[src(2 * GLA_KEY_WIDTH), src(GLA_KEY_WIDTH), src(GLA_WIDTH), src(GLA_WIDTH),
                  pl.BlockSpec((1, GLA_WIDTH), lambda i: (0, 0)), sblk],
        out_specs=[blk(GLA_WIDTH), sblk],
        out_shape=[jax.ShapeDtypeStruct((rows, GLA_WIDTH), BF16),
                   jax.ShapeDtypeStruct((nseq, GLA_KEY_WIDTH, GLA_HEAD_V), F32)],
        scratch_shapes=[pltpu.VMEM((SAMPLE_ROWS, GLA_WIDTH), F32)],
        compiler_params=pltpu.CompilerParams(dimension_semantics=("arbitrary",)),
        name="gla_sample",
    )(qk, g, v, r, gn, state)


def _swa_sample_body(sink_ref, q_ref, kvn_ref, kc_ref, vc_ref, nrm_ref, o_ref, kco_ref, vco_ref,
                     lhs_scr, sc_scr, pc_scr, oc_scr):
    rows = SAMPLE_ROWS
    q = q_ref[...].astype(F32)
    lo = lax.broadcasted_iota(jnp.int32, (1, LANES), 1) < SWA_HEAD_DIM
    for hq in range(SWA_Q_HEADS):
        grp = hq // SWA_GROUP
        tile = q[:, (hq // 2) * LANES:(hq // 2 + 1) * LANES]
        src = tile if hq % 2 == grp else pltpu.roll(tile, SWA_HEAD_DIM, 1)
        lhs_scr[hq * rows:(hq + 1) * rows, :] = jnp.where(lo if grp == 0 else ~lo, src, 0.0)

    kvn = kvn_ref[...]
    k_new = kvn[:, 0:SWA_KV_WIDTH]
    v_new = kvn[:, SWA_KV_WIDTH:]
    s_new = _dot_tb(lhs_scr[...].astype(BF16), k_new.astype(BF16))

    row64 = lax.broadcasted_iota(jnp.int32, (SWA_Q_HEADS * SUBLANES, 1), 0)
    first_of_pair = row64 % SUBLANES < DEC_SEQ

    def gather_pair(ref, pair_idx):
        return jnp.concatenate(
            [ref[hq * rows + pair_idx * SUBLANES:hq * rows + (pair_idx + 1) * SUBLANES, :]
             for hq in range(SWA_Q_HEADS)], axis=0).astype(BF16)

    def scatter_pair(ref, pair_idx, val):
        for hq in range(SWA_Q_HEADS):
            ref[hq * rows + pair_idx * SUBLANES:hq * rows + (pair_idx + 1) * SUBLANES, :] = (
                val[hq * SUBLANES:(hq + 1) * SUBLANES])

    for pair_idx in range(SAMPLE_SEQS // 2):
        l64 = gather_pair(lhs_scr, pair_idx)
        sa = _dot(l64, kc_ref[2 * pair_idx].astype(BF16))
        sb = _dot(l64, kc_ref[2 * pair_idx + 1].astype(BF16))
        scatter_pair(sc_scr, pair_idx, jnp.where(first_of_pair, sa, sb))

    rr = lax.broadcasted_iota(jnp.int32, (rows, rows), 0)
    cc = lax.broadcasted_iota(jnp.int32, (rows, rows), 1)
    tok = rr % DEC_SEQ
    mask_cache = cc > tok
    mask_new = (cc // DEC_SEQ == rr // DEC_SEQ) & (cc % DEC_SEQ <= tok)
    p_new, dens = [], []
    for hq in range(SWA_Q_HEADS):
        sl = slice(hq * rows, (hq + 1) * rows)
        sink = sink_ref[hq]
        s_c = jnp.where(mask_cache, sc_scr[sl, :], -jnp.inf)
        s_n = jnp.where(mask_new, s_new[sl, :], -jnp.inf)
        m = jnp.maximum(jnp.maximum(jnp.max(s_c, axis=1, keepdims=True),
                                    jnp.max(s_n, axis=1, keepdims=True)), sink)
        p_c = jnp.exp(s_c - m)
        p_n = jnp.exp(s_n - m)
        dens.append(jnp.sum(p_c, axis=1, keepdims=True) + jnp.sum(p_n, axis=1, keepdims=True)
                    + jnp.exp(sink - m))
        pc_scr[sl, :] = p_c
        p_new.append(p_n.astype(BF16))
    o_new = _dot(jnp.concatenate(p_new, axis=0), v_new.astype(BF16))

    for pair_idx in range(SAMPLE_SEQS // 2):
        p64 = gather_pair(pc_scr, pair_idx)
        oa = _dot_tb(p64, vc_ref[2 * pair_idx].astype(BF16))
        ob = _dot_tb(p64, vc_ref[2 * pair_idx + 1].astype(BF16))
        scatter_pair(oc_scr, pair_idx, jnp.where(first_of_pair, oa, ob))

    tiles = []
    for i in range(SWA_Q_HEADS // 2):
        halves = []
        for hq in (2 * i, 2 * i + 1):
            sl = slice(hq * rows, (hq + 1) * rows)
            oh = (oc_scr[sl, :] + o_new[sl, :]) / dens[hq]
            halves.append(oh if hq % 2 == hq // SWA_GROUP else pltpu.roll(oh, SWA_HEAD_DIM, 1))
        tiles.append(jnp.where(lo, halves[0], halves[1]))
    o_ref[...] = _rms(jnp.concatenate(tiles, axis=1), nrm_ref[...]).astype(BF16)

    keep = lax.broadcasted_iota(jnp.int32, (1, WINDOW), 1) < WINDOW - DEC_SEQ
    k_new_t = k_new.T
    v_new_t = v_new.T
    for b in range(SAMPLE_SEQS):
        to_tail = (WINDOW - DEC_SEQ - DEC_SEQ * b) % rows
        kco_ref[b] = jnp.where(keep, pltpu.roll(kc_ref[b], WINDOW - DEC_SEQ, 1),
                               pltpu.roll(k_new_t, to_tail, 1))
        vco_ref[b] = jnp.where(keep, pltpu.roll(vc_ref[b], WINDOW - DEC_SEQ, 1),
                               pltpu.roll(v_new_t, to_tail, 1))


def _swa_sample(sinks, qs, kvn, kc, vc, nrm, row0):
    nseq = kc.shape[0]
    rows = nseq * DEC_SEQ
    assert nseq % SAMPLE_SEQS == 0 and row0 % SAMPLE_ROWS == 0
    assert kc.shape[1:] == (SWA_KV_WIDTH, WINDOW) and WINDOW == SAMPLE_ROWS
    blk0 = row0 // SAMPLE_ROWS
    cblk = pl.BlockSpec((SAMPLE_SEQS, SWA_KV_WIDTH, WINDOW), lambda i: (i, 0, 0))
    big = pltpu.VMEM((SWA_Q_HEADS * SAMPLE_ROWS, LANES), F32)
    return pl.pallas_call(
        _swa_sample_body,
        grid=(nseq // SAMPLE_SEQS,),
        in_specs=[pl.BlockSpec(memory_space=pltpu.SMEM),
                  pl.BlockSpec((SAMPLE_ROWS, SWA_WIDTH), lambda i: (i + blk0, 0)),
                  pl.BlockSpec((SAMPLE_ROWS, 2 * SWA_KV_WIDTH), lambda i: (i + blk0, 0)),
                  cblk, cblk,
                  pl.BlockSpec((1, SWA_WIDTH), lambda i: (0, 0))],
        out_specs=[pl.BlockSpec((SAMPLE_ROWS, SWA_WIDTH), lambda i: (i, 0)), cblk, cblk],
        out_shape=[jax.ShapeDtypeStruct((rows, SWA_WIDTH), BF16),
                   jax.ShapeDtypeStruct(kc.shape, F32), jax.ShapeDtypeStruct(vc.shape, F32)],
        scratch_shapes=[big, big, big, big],
        compiler_params=pltpu.CompilerParams(dimension_semantics=("arbitrary",)),
        name="swa_sample",
    )(sinks, qs, kvn, kc, vc, nrm)


def _prepare_weights(ffn1_norm, ffn1_w_gate, ffn1_w_up, ffn1_w_down, mix_norm, w_in, w_gate_up, b_gate,
                     gla_head_norm, swa_out_norm, swa_sinks, w_out,
                     ffn2_norm, ffn2_w_gate, ffn2_w_up, ffn2_w_down, final_norm, layer):
    assert w_in.shape[1:] == (D_MODEL, D_IN) and PROJ_Q_S == 2 * GLA_KEY_WIDTH + 2 * GLA_WIDTH
    wgu = jnp.concatenate([w_gate_up[layer],
                           jnp.zeros((LANES - GLA_GATE_RANK, GLA_KEY_WIDTH), w_gate_up.dtype)], axis=0)
    row = lambda a: a.reshape(1, -1).astype(F32)
    return dict(
        n1=row(ffn1_norm[layer]), wg1=ffn1_w_gate[layer], wu1=ffn1_w_up[layer], wd1=ffn1_w_down[layer],
        nm=row(mix_norm[layer]), win=w_in[layer].T, wgu=wgu.astype(BF16), bg=row(b_gate[layer]),
        gn=row(gla_head_norm[layer]), sn=row(swa_out_norm[layer]), sinks=swa_sinks[layer].astype(F32),
        wo=w_out[layer], n2=row(ffn2_norm[layer]), wg2=ffn2_w_gate[layer], wu2=ffn2_w_up[layer],
        wd2=ffn2_w_down[layer], nf=row(final_norm))


def kernel(x_prompt, x_sample, state_gla, cache_swa_k, cache_swa_v, ffn1_norm, ffn1_w_gate, ffn1_w_up,
           ffn1_w_down, mix_norm, w_in, w_gate_up, b_gate, gla_head_norm, swa_out_norm, swa_sinks, w_out,
           ffn2_norm, ffn2_w_gate, ffn2_w_up, ffn2_w_down, final_norm):
    depth = state_gla.shape[0]
    assert depth == 1 and x_prompt.shape[0] == 1 and x_sample.shape[1] == DEC_SEQ
    seq = x_prompt.shape[1]
    nseq = x_sample.shape[0]
    w = _prepare_weights(ffn1_norm, ffn1_w_gate, ffn1_w_up, ffn1_w_down, mix_norm, w_in, w_gate_up, b_gate,
                         gla_head_norm, swa_out_norm, swa_sinks, w_out,
                         ffn2_norm, ffn2_w_gate, ffn2_w_up, ffn2_w_down, final_norm, 0)

    (x1, qk, g, v, r, qs, kv), (wo, wg2, wu2, wd2) = _stage_a(
        x_prompt.reshape(seq, D_MODEL), x_sample.reshape(nseq * DEC_SEQ, D_MODEL), w)
    w = dict(w, wo=wo, wg2=wg2, wu2=wu2, wd2=wd2)

    og_s, state_s = _gla_sample(qk, g, v, r, w["gn"],
                                state_gla[0].reshape(nseq, GLA_KEY_WIDTH, GLA_HEAD_V), seq)
    cache_w = cache_swa_k.shape[2]
    to_feature_major = lambda c: jnp.transpose(c[0], (0, 2, 3, 1)).reshape(nseq, SWA_KV_WIDTH, cache_w)
    from_feature_major = lambda c: jnp.transpose(
        c.reshape(nseq, SWA_KV_HEADS, SWA_HEAD_DIM, cache_w), (0, 3, 1, 2))[None]
    os_s, k_cache_s, v_cache_s = _swa_sample(
        w["sinks"], qs, kv, to_feature_major(cache_swa_k), to_feature_major(cache_swa_v), w["sn"], seq)
    k_cache_s = from_feature_major(k_cache_s)
    v_cache_s = from_feature_major(v_cache_s)

    y_prompt, y_sample, state_p = _mix_out(qk, g, v, r, qs, kv, x1, og_s, os_s, w)
    y_prompt = y_prompt.reshape(1, seq, D_MODEL)
    y_sample = y_sample.reshape(nseq, DEC_SEQ, D_MODEL)
    cw = min(WINDOW, seq)
    k_cache_p = kv[seq - cw:seq, 0:SWA_KV_WIDTH].reshape(1, 1, cw, SWA_KV_HEADS, SWA_HEAD_DIM)
    v_cache_p = kv[seq - cw:seq, SWA_KV_WIDTH:].reshape(1, 1, cw, SWA_KV_HEADS, SWA_HEAD_DIM)
    state_p = state_p.reshape(1, 1, GLA_HEADS, GLA_HEAD_K, GLA_HEAD_V)
    state_s = state_s.reshape(1, nseq, GLA_HEADS, GLA_HEAD_K, GLA_HEAD_V)

    return (y_prompt, y_sample, state_p, k_cache_p, v_cache_p, state_s, k_cache_s, v_cache_s)
```

```python
import functools
import itertools

import jax
import jax.numpy as jnp
from jax import lax
from jax.experimental import pallas as pl
from jax.experimental.pallas import tpu as pltpu

F32 = jnp.float32
BF16 = jnp.bfloat16

D_MODEL = 1024
D_FF = 2816
GLA_HEADS = 4
GLA_HEAD_K = 64
GLA_HEAD_V = 128
GLA_KEY_WIDTH = GLA_HEADS * GLA_HEAD_K
GLA_WIDTH = GLA_HEADS * GLA_HEAD_V
GLA_GATE_RANK = 16
GLA_GATE_TAU = 16.0
SWA_HEAD_DIM = 64
SWA_Q_HEADS = 8
SWA_KV_HEADS = 2
SWA_GROUP = SWA_Q_HEADS // SWA_KV_HEADS
SWA_WIDTH = SWA_Q_HEADS * SWA_HEAD_DIM
SWA_KV_WIDTH = SWA_KV_HEADS * SWA_HEAD_DIM
WINDOW = 128
DEC_SEQ = 4
NORM_EPS = 1e-6
HEAD_SCALE = 0.125

LANES = 128
SUBLANES = 8
VMEM_CAPACITY_BYTES = 64 * 1024 * 1024
STAGE_A_VMEM_LIMIT = VMEM_CAPACITY_BYTES - 6 * 1024 * 1024
STAGE_D_VMEM_LIMIT = VMEM_CAPACITY_BYTES - 8 * 1024 * 1024

ROW_TILE = 512
FF_CHUNK = 256
DOWN_CHUNK = 256
GLA_CHUNK = 128
SAMPLE_SEQS = 32
SAMPLE_ROWS = SAMPLE_SEQS * DEC_SEQ
SAMPLE_TILE_SHAPE = (ROW_TILE // DEC_SEQ, DEC_SEQ, D_MODEL)
MIXER_STAGES = 2 * (ROW_TILE // GLA_CHUNK) + 3 * (ROW_TILE // WINDOW)
SIDE_STAGES_AT_NORM = 2
DECAY_CLAMP = 60.0

PROJ_Q_G = 0
PROJ_K_G = PROJ_Q_G + GLA_KEY_WIDTH
PROJ_V_G = PROJ_K_G + GLA_KEY_WIDTH
PROJ_R_G = PROJ_V_G + GLA_WIDTH
PROJ_Q_S = PROJ_R_G + GLA_WIDTH
PROJ_KV_S = PROJ_Q_S + SWA_WIDTH
PROJ_A = PROJ_KV_S + 2 * SWA_KV_WIDTH
PROJ_WIDTH = PROJ_A + LANES
D_IN = PROJ_A + GLA_GATE_RANK


def _dot(a, b):
    return jnp.dot(a, b, preferred_element_type=F32)


def _dot_tb(a, b):
    return lax.dot_general(a, b, (((1,), (1,)), ((), ())), preferred_element_type=F32)


def _dot_ta(a, b):
    return lax.dot_general(a, b, (((0,), (0,)), ((), ())), preferred_element_type=F32)


def _rms(x, g):
    return x * lax.rsqrt(jnp.mean(x * x, axis=-1, keepdims=True) + NORM_EPS) * g


def _gate_up(h, wg_ref, wu_ref, act_ref, after_dot=lambda: None):
    for c0 in range(0, D_FF, FF_CHUNK):
        g = _dot(h, wg_ref[:, c0:c0 + FF_CHUNK])
        after_dot()
        u = _dot(h, wu_ref[:, c0:c0 + FF_CHUNK])
        after_dot()
        act_ref[:, c0:c0 + FF_CHUNK] = (g * jax.nn.sigmoid(g) * u).astype(BF16)


def _down(act_ref, rows, wd_ref, after_dot=lambda: None):
    outs = []
    for n0 in range(0, D_MODEL, DOWN_CHUNK):
        outs.append(_dot(act_ref[rows, :], wd_ref[:, n0:n0 + DOWN_CHUNK]))
        after_dot()
    return jnp.concatenate(outs, axis=1)


def _head_norm_gate(o, r, gn):
    parts = []
    for h in range(GLA_HEADS):
        sl = slice(h * GLA_HEAD_V, (h + 1) * GLA_HEAD_V)
        parts.append(_rms(o[:, sl], gn[:, sl]))
    return (jnp.concatenate(parts, axis=1) * (r * jax.nn.sigmoid(r))).astype(BF16)


def _out_ffn(x1_ref, og, osw, wo_ref, n2_ref, wg_ref, wu_ref, wd_ref, nf_ref, y_ref, act_ref, side_stages=None):
    n_inside = 0 if side_stages is None else MIXER_STAGES - 2 * SIDE_STAGES_AT_NORM
    side_stages = iter(()) if side_stages is None else side_stages
    n_dots = 2 * (D_FF // FF_CHUNK) + D_MODEL // DOWN_CHUNK
    done = [0, 0]

    def after_dot():
        done[0] += 1
        while done[1] * n_dots < done[0] * n_inside:
            next(side_stages, None)
            done[1] += 1

    mixed = jnp.concatenate([og, osw], axis=1)
    x2 = x1_ref[...] + _dot(mixed, wo_ref[...])
    for _ in range(SIDE_STAGES_AT_NORM):
        next(side_stages, None)
    h = _rms(x2, n2_ref[...]).astype(BF16)
    _gate_up(h, wg_ref, wu_ref, act_ref, after_dot)
    x3 = x2 + 0.5 * _down(act_ref, slice(None), wd_ref, after_dot)
    for _ in side_stages:
        pass
    y_ref[...] = _rms(x3, nf_ref[...]).reshape(y_ref.shape)


_HBM = pl.BlockSpec(memory_space=pl.ANY)
WIDE_CHUNK_ROWS = 128
NARROW_CHUNK_ROWS = 256
LOAD_SLOTS = 4


def _load_weights_bf16(loads, staging, sems):
    jobs = []
    for src, dst, segments, transposed in loads:
        cols = src.shape[1]
        assert cols in (D_FF, D_MODEL) and dst.shape[0 if transposed else 1] == cols
        kind, step = (0, WIDE_CHUNK_ROWS) if cols == D_FF else (1, NARROW_CHUNK_ROWS)
        for s0, d0, n in segments or [(0, 0, src.shape[0])]:
            jobs += [(src, dst, s0 + r, d0 + r, min(step, n - r), kind, transposed)
                     for r in range(0, n, step)]

    used = [0, 0]
    copies = []
    for src, _, s0, _, nr, kind, _ in jobs:
        slot = used[kind] % LOAD_SLOTS
        used[kind] += 1
        copies.append((pltpu.make_async_copy(src.at[pl.ds(s0, nr), :], staging[kind].at[slot, pl.ds(0, nr), :],
                                             sems.at[kind, slot]), slot))
    for copy, _ in copies[:LOAD_SLOTS - 1]:
        copy.start()
    for j, (_, dst, _, d0, nr, kind, transposed) in enumerate(jobs):
        ahead = j + LOAD_SLOTS - 1
        if ahead < len(jobs):
            copies[ahead][0].start()
        copy, slot = copies[j]
        copy.wait()
        if not transposed:
            dst[d0:d0 + nr, :] = staging[kind][slot, 0:nr, :].astype(BF16)
        elif nr % LANES == 0:
            dst[:, d0:d0 + nr] = staging[kind][slot, 0:nr, :].T.astype(BF16)
        else:
            tile = staging[kind][slot, 0:LANES, :].T
            live = lax.broadcasted_iota(jnp.int32, (1, LANES), 1) < nr
            dst[:, d0:d0 + LANES] = jnp.where(live, tile, 0.0).astype(BF16)


def _staging_scratch():
    return [pltpu.VMEM((LOAD_SLOTS, WIDE_CHUNK_ROWS, D_FF), F32),
            pltpu.VMEM((LOAD_SLOTS, NARROW_CHUNK_ROWS, D_MODEL), F32),
            pltpu.SemaphoreType.DMA((2, LOAD_SLOTS))]


def _chunk_plan(matrices):
    plan, first = [], 0
    for m, mat in enumerate(matrices):
        rows, cols = mat.shape
        assert cols in (D_FF, D_MODEL)
        kind, nr = (0, WIDE_CHUNK_ROWS) if cols == D_FF else (1, NARROW_CHUNK_ROWS)
        assert rows % nr == 0
        plan.append((m, first, rows // nr, nr, kind))
        first += rows // nr
    return plan, first


def _background_cast(step, srcs, dsts, staging, out_staging, sem_in, sem_out):
    plan, _ = _chunk_plan(srcs)
    slot = step % 2

    def rows_of(m, chunk):
        _, first, _, nr, _ = plan[m]
        return pl.ds(pl.multiple_of((chunk - first) * nr, nr), nr)

    def copy_in(m, chunk, slot_):
        kind = plan[m][4]
        return pltpu.make_async_copy(srcs[m].at[rows_of(m, chunk), :], staging[kind].at[slot_],
                                     sem_in.at[kind, slot_])

    def copy_out(m, chunk, slot_):
        kind = plan[m][4]
        return pltpu.make_async_copy(out_staging[kind].at[slot_], dsts[m].at[rows_of(m, chunk), :],
                                     sem_out.at[kind, slot_])

    def for_chunk(chunk, fn):
        for m, first, n, _, _ in plan:
            pl.when(jnp.logical_and(chunk >= first, chunk < first + n))(functools.partial(fn, m))

    def before():
        @pl.when(step == 0)
        def _():
            copy_in(0, 0, 0).start()
        for_chunk(step, lambda m: copy_in(m, step, slot).wait())
        for_chunk(step - 2, lambda m: copy_out(m, step - 2, slot).wait())
        for_chunk(step + 1, lambda m: copy_in(m, step + 1, 1 - slot).start())

    def cast():
        for kind in range(2):
            out_staging[kind][slot] = staging[kind][slot].astype(BF16)

    def after():
        for_chunk(step, lambda m: copy_out(m, step, slot).start())

    return before, cast, after


def _stage_a_body(xp_ref, xs_ref, n1_ref, wg_hbm, wu_hbm, wd_hbm, nm_ref, win_hbm, wgu_ref, bg_ref,
                  wo_hbm, wg2_hbm, wu2_hbm, wd2_hbm,
                  x1_ref, qk_ref, g_ref, v_ref, r_ref, qs_ref, kv_ref, wo_out, wg2_out, wu2_out, wd2_out,
                  act_ref, wg_ref, wu_ref, wd_ref, win_ref, wide_stage, narrow_stage, load_sems,
                  wide_out, narrow_out, sem_in, sem_out):
    i = pl.program_id(0)

    @pl.when(i == 0)
    def _():
        a0, a1 = PROJ_Q_S, PROJ_Q_S + GLA_GATE_RANK
        win_rows = [(0, 0, a0), (a1, a0, D_IN - a1), (a0, PROJ_A, GLA_GATE_RANK)]
        _load_weights_bf16([(wg_hbm, wg_ref, None, False), (wu_hbm, wu_ref, None, False),
                            (wd_hbm, wd_ref, None, False), (win_hbm, win_ref, win_rows, True)],
                           (wide_stage, narrow_stage), load_sems)

    bg_before, bg_cast, bg_after = _background_cast(
        i, (wo_hbm, wg2_hbm, wu2_hbm, wd2_hbm), (wo_out, wg2_out, wu2_out, wd2_out),
        (wide_stage, narrow_stage), (wide_out, narrow_out), sem_in, sem_out)
    bg_before()

    x = jnp.where(i < pl.num_programs(0) - 1, xp_ref[...], xs_ref[...].reshape(ROW_TILE, D_MODEL))
    h = _rms(x, n1_ref[...]).astype(BF16)
    _gate_up(h, wg_ref, wu_ref, act_ref)
    bg_cast()

    halves = [slice(k * ROW_TILE // 2, (k + 1) * ROW_TILE // 2) for k in range(2)]
    x1 = []
    for rows in halves:
        x1.append(x[rows, :] + 0.5 * _down(act_ref, rows, wd_ref))
        x1_ref[rows, :] = x1[-1]
    gate_in = []
    for rows, x1_h in zip(halves, x1):
        h2 = _rms(x1_h, nm_ref[...]).astype(BF16)
        proj = _dot(h2, win_ref[...])
        qk_ref[rows, 0:GLA_KEY_WIDTH] = proj[:, PROJ_Q_G:PROJ_K_G] * HEAD_SCALE
        qk_ref[rows, GLA_KEY_WIDTH:] = proj[:, PROJ_K_G:PROJ_V_G]
        v_ref[rows, :] = proj[:, PROJ_V_G:PROJ_R_G].astype(BF16)
        r_ref[rows, :] = proj[:, PROJ_R_G:PROJ_Q_S]
        qs_ref[rows, :] = (proj[:, PROJ_Q_S:PROJ_KV_S] * HEAD_SCALE).astype(BF16)
        kv_ref[rows, :] = proj[:, PROJ_KV_S:PROJ_A]
        gate_in.append(proj[:, PROJ_A:PROJ_WIDTH].astype(BF16))
    for rows, a in zip(halves, gate_in):
        z = _dot(a, wgu_ref[...]) + bg_ref[...]
        g_ref[rows, :] = jax.nn.log_sigmoid(z) * (1.0 / GLA_GATE_TAU)
    bg_after()


def _resident(shape):
    return pl.BlockSpec(shape, lambda i: (0,) * len(shape), pipeline_mode=pl.Buffered(1))


def _rows(tm, n):
    return pl.BlockSpec((tm, n), lambda i: (i, 0))


def _stage_a(xp, xs, w):
    t = xp.shape[0]
    assert t % ROW_TILE == 0 and xs.shape == SAMPLE_TILE_SHAPE
    nt = t // ROW_TILE
    out_widths = ((D_MODEL, F32), (2 * GLA_KEY_WIDTH, F32), (GLA_KEY_WIDTH, F32), (GLA_WIDTH, BF16),
                  (GLA_WIDTH, F32), (SWA_WIDTH, BF16), (2 * SWA_KV_WIDTH, F32))
    to_cast = (w["wo"], w["wg2"], w["wu2"], w["wd2"])
    assert _chunk_plan(to_cast)[1] + 2 <= nt + 1
    outs = pl.pallas_call(
        _stage_a_body,
        grid=(nt + 1,),
        in_specs=[pl.BlockSpec((ROW_TILE, D_MODEL), lambda i: (jnp.minimum(i, nt - 1), 0)),
                  _resident(SAMPLE_TILE_SHAPE), _resident((1, D_MODEL)),
                  _HBM, _HBM, _HBM,
                  _resident((1, D_MODEL)), _HBM,
                  _resident((LANES, GLA_KEY_WIDTH)), _resident((1, GLA_KEY_WIDTH)),
                  _HBM, _HBM, _HBM, _HBM],
        out_specs=[_rows(ROW_TILE, n) for n, _ in out_widths] + [_HBM] * len(to_cast),
        out_shape=([jax.ShapeDtypeStruct((t + ROW_TILE, n), dt) for n, dt in out_widths]
                   + [jax.ShapeDtypeStruct(m.shape, BF16) for m in to_cast]),
        scratch_shapes=([pltpu.VMEM((ROW_TILE, D_FF), BF16), pltpu.VMEM((D_MODEL, D_FF), BF16),
                         pltpu.VMEM((D_MODEL, D_FF), BF16), pltpu.VMEM((D_FF, D_MODEL), BF16),
                         pltpu.VMEM((D_MODEL, PROJ_WIDTH), BF16)]
                        + _staging_scratch()
                        + [pltpu.VMEM((2, WIDE_CHUNK_ROWS, D_FF), BF16),
                           pltpu.VMEM((2, NARROW_CHUNK_ROWS, D_MODEL), BF16),
                           pltpu.SemaphoreType.DMA((2, 2)), pltpu.SemaphoreType.DMA((2, 2))]),
        compiler_params=pltpu.CompilerParams(dimension_semantics=("arbitrary",),
                                             vmem_limit_bytes=STAGE_A_VMEM_LIMIT),
        name="stage_a_ffn1_proj",
    )(xp, xs, w["n1"], w["wg1"], w["wu1"], w["wd1"], w["nm"], w["win"], w["wgu"], w["bg"], *to_cast)
    return outs[:len(out_widths)], outs[len(out_widths):]


def _cumsum_rows(x):
    n = x.shape[0]
    row = lax.broadcasted_iota(jnp.int32, (n, 1), 0)
    shift = 1
    while shift < n:
        x = x + jnp.where(row >= shift, pltpu.roll(x, shift, 0), 0.0)
        shift *= 2
    return x


def _alternate(a, b):
    pending = [iter(a), iter(b)]
    while pending:
        for it in list(pending):
            try:
                next(it)
            except StopIteration:
                pending.remove(it)
                continue
            yield


def _gla_stages(qk_ref, g_ref, v_ref, r_ref, gn_ref, s_ref, og_scr, cum_scr, inter_scr, flag_ref):
    c_len = GLA_CHUNK
    n_chunks = ROW_TILE // c_len
    causal_cat = (lax.broadcasted_iota(jnp.int32, (c_len, GLA_HEADS * c_len), 0)
                  >= lax.broadcasted_iota(jnp.int32, (c_len, GLA_HEADS * c_len), 1) % c_len)
    lane_head = lax.broadcasted_iota(jnp.int32, (1, GLA_KEY_WIDTH), 1) // GLA_HEAD_K
    row_head = lax.broadcasted_iota(jnp.int32, (GLA_KEY_WIDTH, 1), 0) // GLA_HEAD_K
    eye = (lax.broadcasted_iota(jnp.int32, (GLA_KEY_WIDTH, GLA_KEY_WIDTH), 0)
           == lax.broadcasted_iota(jnp.int32, (GLA_KEY_WIDTH, GLA_KEY_WIDTH), 1))
    zero_b = jnp.zeros((), BF16)
    zero_v = jnp.zeros((c_len, GLA_HEAD_V), BF16)
    worst = []

    def chunk(c):
        rows = slice(c * c_len, (c + 1) * c_len)
        cum = _cumsum_rows(g_ref[rows, :])
        cum_scr[rows, :] = cum
        last = cum[c_len - 1:c_len, :]
        q = qk_ref[rows, 0:GLA_KEY_WIDTH]
        k = qk_ref[rows, GLA_KEY_WIDTH:]
        vb = v_ref[rows, :]
        qe = (q * jnp.exp(cum)).astype(BF16)
        ke = (k * jnp.exp(jnp.minimum(-cum, DECAY_CLAMP))).astype(BF16)
        kl = (k * jnp.exp(last - cum)).astype(BF16)
        state = s_ref[...]
        sb = state.astype(BF16)
        s_bd = jnp.concatenate([jnp.where(row_head == h, sb, zero_b) for h in range(GLA_HEADS)], axis=1)
        o_inter = _dot(qe, s_bd)
        inter_scr[rows, :] = o_inter
        ke_bd = jnp.concatenate([jnp.where(lane_head == h, ke, zero_b) for h in range(GLA_HEADS)], axis=0)
        attn = _dot_tb(qe, ke_bd)
        upds = []
        for p in range(GLA_HEADS // 2):
            u = _dot_ta(kl[:, p * LANES:(p + 1) * LANES], vb[:, 2 * p * GLA_HEAD_V:(2 * p + 2) * GLA_HEAD_V])
            upds.append(u[0:GLA_HEAD_K, 0:GLA_HEAD_V])
            upds.append(u[GLA_HEAD_K:, GLA_HEAD_V:])
        yield
        attn = jnp.where(causal_cat, attn, 0.0).astype(BF16)
        o_pairs = []
        for p in range(GLA_HEADS // 2):
            v_a = vb[:, (2 * p) * GLA_HEAD_V:(2 * p + 1) * GLA_HEAD_V]
            v_b = vb[:, (2 * p + 1) * GLA_HEAD_V:(2 * p + 2) * GLA_HEAD_V]
            v_bd = jnp.concatenate([jnp.concatenate([v_a, zero_v], axis=1),
                                    jnp.concatenate([zero_v, v_b], axis=1)], axis=0)
            o_pairs.append(_dot(attn[:, 2 * p * c_len:(2 * p + 2) * c_len], v_bd))
        og_scr[rows, :] = _head_norm_gate(o_inter + jnp.concatenate(o_pairs, axis=1), r_ref[rows, :],
                                          gn_ref[...])
        last_col = jnp.sum(jnp.where(eye, last, 0.0), axis=1, keepdims=True)
        s_ref[...] = state * jnp.exp(last_col) + jnp.concatenate(upds, axis=0)
        worst.append(jnp.min(last, axis=1, keepdims=True))
        if c == n_chunks - 1:
            tile_min = functools.reduce(jnp.minimum, worst)
            flag_ref[0] = jnp.where(tile_min[0, 0] < -DECAY_CLAMP, 1, 0)
        yield

    return itertools.chain.from_iterable(chunk(c) for c in range(n_chunks))


def _gla_tile_exact(qk_ref, v_ref, r_ref, gn_ref, og_scr, o_scr, cum_scr, inter_scr):
    c_len = GLA_CHUNK
    ind = jnp.where(lax.broadcasted_iota(jnp.int32, (GLA_KEY_WIDTH, LANES), 0) // GLA_HEAD_K
                    == lax.broadcasted_iota(jnp.int32, (GLA_KEY_WIDTH, LANES), 1), 1.0, 0.0).astype(BF16)
    j_idx = lax.broadcasted_iota(jnp.int32, (c_len, 1), 0)
    for c in range(ROW_TILE // c_len):
        r0 = c * c_len
        rows = slice(r0, r0 + c_len)

        def one_row(i, carry, r0=r0, rows=rows):
            ci = cum_scr[pl.ds(r0 + i, 1), :]
            qi = qk_ref[pl.ds(r0 + i, 1), 0:GLA_KEY_WIDTH]
            kk = qk_ref[rows, GLA_KEY_WIDTH:]
            dec = jnp.exp(jnp.minimum(ci - cum_scr[rows, :], 0.0))
            a_cols = _dot(((qi * kk) * dec).astype(BF16), ind)
            outs = []
            for h in range(GLA_HEADS):
                w_col = jnp.where(j_idx <= i, a_cols[:, h:h + 1], 0.0)
                v_h = v_ref[rows, h * GLA_HEAD_V:(h + 1) * GLA_HEAD_V].astype(F32)
                outs.append(jnp.sum(w_col * v_h, axis=0, keepdims=True))
            o_scr[pl.ds(r0 + i, 1), :] = inter_scr[pl.ds(r0 + i, 1), :] + jnp.concatenate(outs, axis=1)
            return carry

        lax.fori_loop(0, c_len, one_row, 0)
    og_scr[...] = _head_norm_gate(o_scr[...], r_ref[...], gn_ref[...])


def _dup_halves(x):
    lo = lax.broadcasted_iota(jnp.int32, (1, LANES), 1) < SWA_HEAD_DIM
    sw = pltpu.roll(x, SWA_HEAD_DIM, 1)
    return jnp.where(lo, x, sw).astype(BF16), jnp.where(lo, sw, x).astype(BF16)


def _swa_stages(sink_ref, qs_ref, kv_ref, kvprev_scr, sn_ref, os_scr, no_prev_bias):
    w = WINDOW
    n_blocks = ROW_TILE // w
    lo = lax.broadcasted_iota(jnp.int32, (1, LANES), 1) < SWA_HEAD_DIM
    tri = (lax.broadcasted_iota(jnp.int32, (w, w), 1) <= lax.broadcasted_iota(jnp.int32, (w, w), 0))
    zero_b = jnp.zeros((), BF16)

    def block(b):
        rows = slice(b * w, (b + 1) * w)
        kv_prev = kvprev_scr[...] if b == 0 else kv_ref[(b - 1) * w:b * w, :]
        kv = jnp.concatenate([kv_prev, kv_ref[rows, :]], axis=0)
        kk = _dup_halves(kv[:, 0:SWA_KV_WIDTH])
        vv = _dup_halves(kv[:, SWA_KV_WIDTH:])
        q = qs_ref[rows, :]
        scores = []
        for grp in range(SWA_KV_HEADS):
            stacked = []
            for j in range(SWA_GROUP):
                hq = grp * SWA_GROUP + j
                tile = q[:, (hq // 2) * LANES:(hq // 2 + 1) * LANES]
                stacked.append(jnp.where(lo if hq % 2 == 0 else ~lo, tile, zero_b))
            scores.append(_dot_tb(jnp.concatenate(stacked, axis=0), kk[grp]))
        yield
        outs, dens = [], []
        for grp in range(SWA_KV_HEADS):
            s = scores[grp]
            probs = []
            for j in range(SWA_GROUP):
                sink = sink_ref[grp * SWA_GROUP + j]
                s_prev = s[j * w:(j + 1) * w, 0:w]
                if b == 0:
                    s_prev = s_prev + no_prev_bias
                sf = jnp.where(tri, s[j * w:(j + 1) * w, w:], s_prev)
                m = jnp.maximum(jnp.max(sf, axis=1, keepdims=True), sink)
                p = jnp.exp(sf - m)
                dens.append(jnp.sum(p, axis=1, keepdims=True) + jnp.exp(sink - m))
                pb = p.astype(BF16)
                probs.append(jnp.concatenate([jnp.where(tri, zero_b, pb), jnp.where(tri, pb, zero_b)], axis=1))
            outs.append(_dot(jnp.concatenate(probs, axis=0), vv[grp]))
        yield
        tiles = []
        for grp in range(SWA_KV_HEADS):
            res = [outs[grp][j * w:(j + 1) * w] / dens[grp * SWA_GROUP + j] for j in range(SWA_GROUP)]
            tiles.append(jnp.where(lo, res[0], res[1]))
            tiles.append(jnp.where(lo, res[2], res[3]))
        os_scr[rows, :] = _rms(jnp.concatenate(tiles, axis=1), sn_ref[...]).astype(BF16)
        if b == n_blocks - 1:
            kvprev_scr[...] = kv_ref[rows, :]
        yield

    return itertools.chain.from_iterable(block(b) for b in range(n_blocks))


def _mix_out_body(sink_ref, qk_ref, g_ref, v_ref, r_ref, qs_ref, kv_ref, x1_ref, ogs_ref, oss_ref,
                  gn_ref, sn_ref, wo_ref, n2_ref, wg_ref, wu_ref, wd_ref, nf_ref, yp_ref, ys_ref, s_ref,
                  og_scr, os_scr, kvprev_scr, o_scr, cum_scr, inter_scr, act_scr, flag_ref):
    i = pl.program_id(0)
    n_prompt = pl.num_programs(0) - 1

    def stage_d(og, osw, y_ref, side_stages=None):
        _out_ffn(x1_ref, og, osw, wo_ref, n2_ref, wg_ref, wu_ref, wd_ref, nf_ref, y_ref, act_scr,
                 side_stages)

    def mixer_stages():
        gla = _gla_stages(qk_ref, g_ref, v_ref, r_ref, gn_ref, s_ref, og_scr, cum_scr, inter_scr, flag_ref)
        swa = _swa_stages(sink_ref, qs_ref, kv_ref, kvprev_scr, sn_ref, os_scr,
                          jnp.where(i > 0, 0.0, -jnp.inf))
        return _alternate(gla, swa)

    @pl.when(i == 0)
    def _():
        s_ref[...] = jnp.zeros_like(s_ref)
        kvprev_scr[...] = jnp.zeros_like(kvprev_scr)
        stage_d(ogs_ref[...], oss_ref[...], ys_ref, mixer_stages())

    @pl.when(jnp.logical_and(i > 0, i < n_prompt))
    def _():
        stage_d(og_scr[...], os_scr[...], yp_ref, mixer_stages())

    @pl.when(i == n_prompt)
    def _():
        stage_d(og_scr[...], os_scr[...], yp_ref)
        flag_ref[0] = 0

    @pl.when(flag_ref[0] != 0)
    def _():
        _gla_tile_exact(qk_ref, v_ref, r_ref, gn_ref, og_scr, o_scr, cum_scr, inter_scr)


def _mix_out(qk, g, v, r, qs, kv, x1, og_s, os_s, w):
    t = qk.shape[0] - ROW_TILE
    assert t % ROW_TILE == 0 and og_s.shape[0] == ROW_TILE
    nt = t // ROW_TILE
    cur = lambda n: pl.BlockSpec((ROW_TILE, n), lambda i: (jnp.minimum(i, nt - 1), 0))
    return pl.pallas_call(
        _mix_out_body,
        grid=(nt + 1,),
        in_specs=[pl.BlockSpec(memory_space=pltpu.SMEM),
                  cur(2 * GLA_KEY_WIDTH), cur(GLA_KEY_WIDTH), cur(GLA_WIDTH), cur(GLA_WIDTH),
                  cur(SWA_WIDTH), cur(2 * SWA_KV_WIDTH),
                  pl.BlockSpec((ROW_TILE, D_MODEL), lambda i: (jnp.where(i == 0, nt, i - 1), 0)),
                  _resident((ROW_TILE, GLA_WIDTH)), _resident((ROW_TILE, SWA_WIDTH)),
                  _resident((1, GLA_WIDTH)), _resident((1, SWA_WIDTH)),
                  _resident((D_MODEL, D_MODEL)), _resident((1, D_MODEL)),
                  _resident((D_MODEL, D_FF)), _resident((D_MODEL, D_FF)), _resident((D_FF, D_MODEL)),
                  _resident((1, D_MODEL))],
        out_specs=[pl.BlockSpec((ROW_TILE, D_MODEL), lambda i: (jnp.clip(i - 1, 0, nt - 1), 0)),
                   pl.BlockSpec(SAMPLE_TILE_SHAPE, lambda i: (0, 0, 0)),
                   pl.BlockSpec((GLA_KEY_WIDTH, GLA_HEAD_V), lambda i: (0, 0))],
        out_shape=[jax.ShapeDtypeStruct((t, D_MODEL), F32),
                   jax.ShapeDtypeStruct(SAMPLE_TILE_SHAPE, F32),
                   jax.ShapeDtypeStruct((GLA_KEY_WIDTH, GLA_HEAD_V), F32)],
        scratch_shapes=[pltpu.VMEM((ROW_TILE, GLA_WIDTH), BF16), pltpu.VMEM((ROW_TILE, SWA_WIDTH), BF16),
                        pltpu.VMEM((WINDOW, 2 * SWA_KV_WIDTH), F32),
                        pltpu.VMEM((ROW_TILE, GLA_WIDTH), F32), pltpu.VMEM((ROW_TILE, GLA_KEY_WIDTH), F32),
                        pltpu.VMEM((ROW_TILE, GLA_WIDTH), F32), pltpu.VMEM((ROW_TILE, D_FF), BF16),
                        pltpu.SMEM((1,), jnp.int32)],
        compiler_params=pltpu.CompilerParams(dimension_semantics=("arbitrary",),
                                             vmem_limit_bytes=STAGE_D_VMEM_LIMIT),
        name="prompt_mixers_stage_d",
    )(w["sinks"], qk, g, v, r, qs, kv, x1, og_s, os_s, w["gn"], w["sn"], w["wo"], w["n2"], w["wg2"],
      w["wu2"], w["wd2"], w["nf"])


def _gla_sample_body(qk_ref, g_ref, v_ref, r_ref, gn_ref, s_ref, og_ref, so_ref, o_scr):
    rows = SAMPLE_ROWS
    g = g_ref[...]
    q = qk_ref[:, 0:GLA_KEY_WIDTH]
    k = qk_ref[:, GLA_KEY_WIDTH:]
    vb = v_ref[...]
    vf = vb.astype(F32)
    tok = lax.broadcasted_iota(jnp.int32, (rows, 1), 0) % DEC_SEQ

    cum = g
    for d in range(1, DEC_SEQ):
        cum = cum + jnp.where(tok >= d, pltpu.roll(g, d, 0), 0.0)
    tot = jnp.where(tok == DEC_SEQ - 1, cum, 0.0)
    for d in range(1, DEC_SEQ):
        tot = tot + jnp.where(tok == DEC_SEQ - 1 - d, pltpu.roll(cum, rows - d, 0), 0.0)

    qe = q * jnp.exp(cum)
    kl = k * jnp.exp(tot - cum)
    decay_t = jnp.exp(tot).T

    ind = jnp.where(lax.broadcasted_iota(jnp.int32, (GLA_KEY_WIDTH, LANES), 0) // GLA_HEAD_K
                    == lax.broadcasted_iota(jnp.int32, (GLA_KEY_WIDTH, LANES), 1), 1.0, 0.0).astype(BF16)
    expand = jnp.where(lax.broadcasted_iota(jnp.int32, (LANES, GLA_WIDTH), 0)
                       == lax.broadcasted_iota(jnp.int32, (LANES, GLA_WIDTH), 1) // GLA_HEAD_V,
                       1.0, 0.0).astype(BF16)

    o_intra = jnp.zeros((rows, GLA_WIDTH), F32)
    for d in range(DEC_SEQ):
        k_d = k if d == 0 else pltpu.roll(k, d, 0)
        c_d = cum if d == 0 else pltpu.roll(cum, d, 0)
        v_d = vf if d == 0 else pltpu.roll(vf, d, 0)
        pair = jnp.where(tok >= d, q * k_d * jnp.exp(jnp.minimum(cum - c_d, 0.0)), 0.0)
        a = _dot(pair.astype(BF16), ind)
        o_intra = o_intra + _dot(a.astype(BF16), expand) * v_d

    lane_head = lax.broadcasted_iota(jnp.int32, (1, GLA_KEY_WIDTH), 1) // GLA_HEAD_K
    row8 = lax.broadcasted_iota(jnp.int32, (SUBLANES, 1), 0)
    row32 = lax.broadcasted_iota(jnp.int32, (GLA_HEADS * SUBLANES, 1), 0)
    for pair_idx in range(SAMPLE_SEQS // 2):
        r8 = slice(pair_idx * SUBLANES, (pair_idx + 1) * SUBLANES)
        q8 = qe[r8, :]
        lhs = jnp.concatenate([jnp.where(lane_head == h, q8, 0.0) for h in range(GLA_HEADS)],
                              axis=0).astype(BF16)
        kl8 = kl[r8, :]
        v8 = vb[r8, :]
        res = []
        for s in range(2):
            b = 2 * pair_idx + s
            state = s_ref[b]
            res.append(_dot(lhs, state.astype(BF16)))
            kl_b = jnp.where(row8 // DEC_SEQ == s, kl8, 0.0).astype(BF16)
            upd = _dot_ta(kl_b, v8)
            upd = jnp.concatenate(
                [upd[h * GLA_HEAD_K:(h + 1) * GLA_HEAD_K, h * GLA_HEAD_V:(h + 1) * GLA_HEAD_V]
                 for h in range(GLA_HEADS)], axis=0)
            so_ref[b] = state * decay_t[:, DEC_SEQ * b:DEC_SEQ * b + 1] + upd
        sel = jnp.where(row32 % SUBLANES < DEC_SEQ, res[0], res[1])
        o_scr[r8, :] = jnp.concatenate([sel[h * SUBLANES:(h + 1) * SUBLANES] for h in range(GLA_HEADS)], axis=1)

    og_ref[...] = _head_norm_gate(o_scr[...] + o_intra, r_ref[...], gn_ref[...])


def _gla_sample(qk, g, v, r, gn, state, row0):
    nseq = state.shape[0]
    rows = nseq * DEC_SEQ
    assert nseq % SAMPLE_SEQS == 0 and row0 % SAMPLE_ROWS == 0
    blk0 = row0 // SAMPLE_ROWS
    src = lambda n: pl.BlockSpec((SAMPLE_ROWS, n), lambda i: (i + blk0, 0))
    blk = lambda n: pl.BlockSpec((SAMPLE_ROWS, n), lambda i: (i, 0))
    sblk = pl.BlockSpec((SAMPLE_SEQS, GLA_KEY_WIDTH, GLA_HEAD_V), lambda i: (i, 0, 0))
    return pl.pallas_call(
        _gla_sample_body,
        grid=(nseq // SAMPLE_SEQS,),
        in_specs=[src(2 * GLA_KEY_WIDTH), src(GLA_KEY_WIDTH), src(GLA_WIDTH), src(GLA_WIDTH),
                  pl.BlockSpec((1, GLA_WIDTH), lambda i: (0, 0)), sblk],
        out_specs=[blk(GLA_WIDTH), sblk],
        out_shape=[jax.ShapeDtypeStruct((rows, GLA_WIDTH), BF16),
                   jax.ShapeDtypeStruct((nseq, GLA_KEY_WIDTH, GLA_HEAD_V), F32)],
        scratch_shapes=[pltpu.VMEM((SAMPLE_ROWS, GLA_WIDTH), F32)],
        compiler_params=pltpu.CompilerParams(dimension_semantics=("arbitrary",)),
        name="gla_sample",
    )(qk, g, v, r, gn, state)


def _swa_sample_body(sink_ref, q_ref, kvn_ref, kc_ref, vc_ref, nrm_ref, o_ref, kco_ref, vco_ref,
                     lhs_scr, sc_scr, pc_scr, oc_scr):
    rows = SAMPLE_ROWS
    q = q_ref[...].astype(F32)
    lo = lax.broadcasted_iota(jnp.int32, (1, LANES), 1) < SWA_HEAD_DIM
    for hq in range(SWA_Q_HEADS):
        grp = hq // SWA_GROUP
        tile = q[:, (hq // 2) * LANES:(hq // 2 + 1) * LANES]
        src = tile if hq % 2 == grp else pltpu.roll(tile, SWA_HEAD_DIM, 1)
        lhs_scr[hq * rows:(hq + 1) * rows, :] = jnp.where(lo if grp == 0 else ~lo, src, 0.0)

    kvn = kvn_ref[...]
    k_new = kvn[:, 0:SWA_KV_WIDTH]
    v_new = kvn[:, SWA_KV_WIDTH:]
    s_new = _dot_tb(lhs_scr[...].astype(BF16), k_new.astype(BF16))

    row64 = lax.broadcasted_iota(jnp.int32, (SWA_Q_HEADS * SUBLANES, 1), 0)
    first_of_pair = row64 % SUBLANES < DEC_SEQ

    def gather_pair(ref, pair_idx):
        return jnp.concatenate(
            [ref[hq * rows + pair_idx * SUBLANES:hq * rows + (pair_idx + 1) * SUBLANES, :]
             for hq in range(SWA_Q_HEADS)], axis=0).astype(BF16)

    def scatter_pair(ref, pair_idx, val):
        for hq in range(SWA_Q_HEADS):
            ref[hq * rows + pair_idx * SUBLANES:hq * rows + (pair_idx + 1) * SUBLANES, :] = (
                val[hq * SUBLANES:(hq + 1) * SUBLANES])

    for pair_idx in range(SAMPLE_SEQS // 2):
        l64 = gather_pair(lhs_scr, pair_idx)
        sa = _dot(l64, kc_ref[2 * pair_idx].astype(BF16))
        sb = _dot(l64, kc_ref[2 * pair_idx + 1].astype(BF16))
        scatter_pair(sc_scr, pair_idx, jnp.where(first_of_pair, sa, sb))

    rr = lax.broadcasted_iota(jnp.int32, (rows, rows), 0)
    cc = lax.broadcasted_iota(jnp.int32, (rows, rows), 1)
    tok = rr % DEC_SEQ
    mask_cache = cc > tok
    mask_new = (cc // DEC_SEQ == rr // DEC_SEQ) & (cc % DEC_SEQ <= tok)
    p_new, dens = [], []
    for hq in range(SWA_Q_HEADS):
        sl = slice(hq * rows, (hq + 1) * rows)
        sink = sink_ref[hq]
        s_c = jnp.where(mask_cache, sc_scr[sl, :], -jnp.inf)
        s_n = jnp.where(mask_new, s_new[sl, :], -jnp.inf)
        m = jnp.maximum(jnp.maximum(jnp.max(s_c, axis=1, keepdims=True),
                                    jnp.max(s_n, axis=1, keepdims=True)), sink)
        p_c = jnp.exp(s_c - m)
        p_n = jnp.exp(s_n - m)
        dens.append(jnp.sum(p_c, axis=1, keepdims=True) + jnp.sum(p_n, axis=1, keepdims=True)
                    + jnp.exp(sink - m))
        pc_scr[sl, :] = p_c
        p_new.append(p_n.astype(BF16))
    o_new = _dot(jnp.concatenate(p_new, axis=0), v_new.astype(BF16))

    for pair_idx in range(SAMPLE_SEQS // 2):
        p64 = gather_pair(pc_scr, pair_idx)
        oa = _dot_tb(p64, vc_ref[2 * pair_idx].astype(BF16))
        ob = _dot_tb(p64, vc_ref[2 * pair_idx + 1].astype(BF16))
        scatter_pair(oc_scr, pair_idx, jnp.where(first_of_pair, oa, ob))

    tiles = []
    for i in range(SWA_Q_HEADS // 2):
        halves = []
        for hq in (2 * i, 2 * i + 1):
            sl = slice(hq * rows, (hq + 1) * rows)
            oh = (oc_scr[sl, :] + o_new[sl, :]) / dens[hq]
            halves.append(oh if hq % 2 == hq // SWA_GROUP else pltpu.roll(oh, SWA_HEAD_DIM, 1))
        tiles.append(jnp.where(lo, halves[0], halves[1]))
    o_ref[...] = _rms(jnp.concatenate(tiles, axis=1), nrm_ref[...]).astype(BF16)

    keep = lax.broadcasted_iota(jnp.int32, (1, WINDOW), 1) < WINDOW - DEC_SEQ
    k_new_t = k_new.T
    v_new_t = v_new.T
    for b in range(SAMPLE_SEQS):
        to_tail = (WINDOW - DEC_SEQ - DEC_SEQ * b) % rows
        kco_ref[b] = jnp.where(keep, pltpu.roll(kc_ref[b], WINDOW - DEC_SEQ, 1),
                               pltpu.roll(k_new_t, to_tail, 1))
        vco_ref[b] = jnp.where(keep, pltpu.roll(vc_ref[b], WINDOW - DEC_SEQ, 1),
                               pltpu.roll(v_new_t, to_tail, 1))


def _swa_sample(sinks, qs, kvn, kc, vc, nrm, row0):
    nseq = kc.shape[0]
    rows = nseq * DEC_SEQ
    assert nseq % SAMPLE_SEQS == 0 and row0 % SAMPLE_ROWS == 0
    assert kc.shape[1:] == (SWA_KV_WIDTH, WINDOW) and WINDOW == SAMPLE_ROWS
    blk0 = row0 // SAMPLE_ROWS
    cblk = pl.BlockSpec((SAMPLE_SEQS, SWA_KV_WIDTH, WINDOW), lambda i: (i, 0, 0))
    big = pltpu.VMEM((SWA_Q_HEADS * SAMPLE_ROWS, LANES), F32)
    return pl.pallas_call(
        _swa_sample_body,
        grid=(nseq // SAMPLE_SEQS,),
        in_specs=[pl.BlockSpec(memory_space=pltpu.SMEM),
                  pl.BlockSpec((SAMPLE_ROWS, SWA_WIDTH), lambda i: (i + blk0, 0)),
                  pl.BlockSpec((SAMPLE_ROWS, 2 * SWA_KV_WIDTH), lambda i: (i + blk0, 0)),
                  cblk, cblk,
                  pl.BlockSpec((1, SWA_WIDTH), lambda i: (0, 0))],
        out_specs=[pl.BlockSpec((SAMPLE_ROWS, SWA_WIDTH), lambda i: (i, 0)), cblk, cblk],
        out_shape=[jax.ShapeDtypeStruct((rows, SWA_WIDTH), BF16),
                   jax.ShapeDtypeStruct(kc.shape, F32), jax.ShapeDtypeStruct(vc.shape, F32)],
        scratch_shapes=[big, big, big, big],
        compiler_params=pltpu.CompilerParams(dimension_semantics=("arbitrary",)),
        name="swa_sample",
    )(sinks, qs, kvn, kc, vc, nrm)


def _prepare_weights(ffn1_norm, ffn1_w_gate, ffn1_w_up, ffn1_w_down, mix_norm, w_in, w_gate_up, b_gate,
                     gla_head_norm, swa_out_norm, swa_sinks, w_out,
                     ffn2_norm, ffn2_w_gate, ffn2_w_up, ffn2_w_down, final_norm, layer):
    assert w_in.shape[1:] == (D_MODEL, D_IN) and PROJ_Q_S == 2 * GLA_KEY_WIDTH + 2 * GLA_WIDTH
    wgu = jnp.concatenate([w_gate_up[layer],
                           jnp.zeros((LANES - GLA_GATE_RANK, GLA_KEY_WIDTH), w_gate_up.dtype)], axis=0)
    row = lambda a: a.reshape(1, -1).astype(F32)
    return dict(
        n1=row(ffn1_norm[layer]), wg1=ffn1_w_gate[layer], wu1=ffn1_w_up[layer], wd1=ffn1_w_down[layer],
        nm=row(mix_norm[layer]), win=w_in[layer].T, wgu=wgu.astype(BF16), bg=row(b_gate[layer]),
        gn=row(gla_head_norm[layer]), sn=row(swa_out_norm[layer]), sinks=swa_sinks[layer].astype(F32),
        wo=w_out[layer], n2=row(ffn2_norm[layer]), wg2=ffn2_w_gate[layer], wu2=ffn2_w_up[layer],
        wd2=ffn2_w_down[layer], nf=row(final_norm))


def kernel(x_prompt, x_sample, state_gla, cache_swa_k, cache_swa_v, ffn1_norm, ffn1_w_gate, ffn1_w_up,
           ffn1_w_down, mix_norm, w_in, w_gate_up, b_gate, gla_head_norm, swa_out_norm, swa_sinks, w_out,
           ffn2_norm, ffn2_w_gate, ffn2_w_up, ffn2_w_down, final_norm):
    depth = state_gla.shape[0]
    assert depth == 1 and x_prompt.shape[0] == 1 and x_sample.shape[1] == DEC_SEQ
    seq = x_prompt.shape[1]
    nseq = x_sample.shape[0]
    w = _prepare_weights(ffn1_norm, ffn1_w_gate, ffn1_w_up, ffn1_w_down, mix_norm, w_in, w_gate_up, b_gate,
                         gla_head_norm, swa_out_norm, swa_sinks, w_out,
                         ffn2_norm, ffn2_w_gate, ffn2_w_up, ffn2_w_down, final_norm, 0)

    (x1, qk, g, v, r, qs, kv), (wo, wg2, wu2, wd2) = _stage_a(
        x_prompt.reshape(seq, D_MODEL), x_sample, w)
    w = dict(w, wo=wo, wg2=wg2, wu2=wu2, wd2=wd2)

    og_s, state_s = _gla_sample(qk, g, v, r, w["gn"],
                                state_gla[0].reshape(nseq, GLA_KEY_WIDTH, GLA_HEAD_V), seq)
    cache_w = cache_swa_k.shape[2]
    to_feature_major = lambda c: jnp.transpose(c[0], (0, 2, 3, 1)).reshape(nseq, SWA_KV_WIDTH, cache_w)
    from_feature_major = lambda c: jnp.transpose(
        c.reshape(nseq, SWA_KV_HEADS, SWA_HEAD_DIM, cache_w), (0, 3, 1, 2))[None]
    os_s, k_cache_s, v_cache_s = _swa_sample(
        w["sinks"], qs, kv, to_feature_major(cache_swa_k), to_feature_major(cache_swa_v), w["sn"], seq)
    k_cache_s = from_feature_major(k_cache_s)
    v_cache_s = from_feature_major(v_cache_s)

    y_prompt, y_sample, state_p = _mix_out(qk, g, v, r, qs, kv, x1, og_s, os_s, w)
    y_prompt = y_prompt.reshape(1, seq, D_MODEL)
    cw = min(WINDOW, seq)
    k_cache_p = kv[seq - cw:seq, 0:SWA_KV_WIDTH].reshape(1, 1, cw, SWA_KV_HEADS, SWA_HEAD_DIM)
    v_cache_p = kv[seq - cw:seq, SWA_KV_WIDTH:].reshape(1, 1, cw, SWA_KV_HEADS, SWA_HEAD_DIM)
    state_p = state_p.reshape(1, 1, GLA_HEADS, GLA_HEAD_K, GLA_HEAD_V)
    state_s = state_s.reshape(1, nseq, GLA_HEADS, GLA_HEAD_K, GLA_HEAD_V)

    return (y_prompt, y_sample, state_p, k_cache_p, v_cache_p, state_s, k_cache_s, v_cache_s)
```

```python
import functools
import itertools

import jax
import jax.numpy as jnp
from jax import lax
from jax.experimental import pallas as pl
from jax.experimental.pallas import tpu as pltpu

F32 = jnp.float32
BF16 = jnp.bfloat16

D_MODEL = 1024
D_FF = 2816
GLA_HEADS = 4
GLA_HEAD_K = 64
GLA_HEAD_V = 128
GLA_KEY_WIDTH = GLA_HEADS * GLA_HEAD_K
GLA_WIDTH = GLA_HEADS * GLA_HEAD_V
GLA_GATE_RANK = 16
GLA_GATE_TAU = 16.0
SWA_HEAD_DIM = 64
SWA_Q_HEADS = 8
SWA_KV_HEADS = 2
SWA_GROUP = SWA_Q_HEADS // SWA_KV_HEADS
SWA_WIDTH = SWA_Q_HEADS * SWA_HEAD_DIM
SWA_KV_WIDTH = SWA_KV_HEADS * SWA_HEAD_DIM
WINDOW = 128
DEC_SEQ = 4
NORM_EPS = 1e-6
HEAD_SCALE = 0.125

LANES = 128
SUBLANES = 8
VMEM_CAPACITY_BYTES = 64 * 1024 * 1024
STAGE_A_VMEM_LIMIT = VMEM_CAPACITY_BYTES - 6 * 1024 * 1024
STAGE_D_VMEM_LIMIT = VMEM_CAPACITY_BYTES - 8 * 1024 * 1024

ROW_TILE = 512
FF_CHUNK = 256
DOWN_CHUNK = 256
GLA_CHUNK = 128
SAMPLE_SEQS = 32
SAMPLE_ROWS = SAMPLE_SEQS * DEC_SEQ
SAMPLE_TILE_SHAPE = (ROW_TILE // DEC_SEQ, DEC_SEQ, D_MODEL)
MIXER_STAGES = 2 * (ROW_TILE // GLA_CHUNK) + 3 * (ROW_TILE // WINDOW)
SIDE_STAGES_AT_NORM = 2
SIDE_STAGES_AFTER_FFN = 0
DECAY_CLAMP = 60.0

PROJ_Q_G = 0
PROJ_K_G = PROJ_Q_G + GLA_KEY_WIDTH
PROJ_V_G = PROJ_K_G + GLA_KEY_WIDTH
PROJ_R_G = PROJ_V_G + GLA_WIDTH
PROJ_Q_S = PROJ_R_G + GLA_WIDTH
PROJ_KV_S = PROJ_Q_S + SWA_WIDTH
PROJ_A = PROJ_KV_S + 2 * SWA_KV_WIDTH
PROJ_WIDTH = PROJ_A + LANES
D_IN = PROJ_A + GLA_GATE_RANK


def _dot(a, b):
    return jnp.dot(a, b, preferred_element_type=F32)


def _dot_tb(a, b):
    return lax.dot_general(a, b, (((1,), (1,)), ((), ())), preferred_element_type=F32)


def _dot_ta(a, b):
    return lax.dot_general(a, b, (((0,), (0,)), ((), ())), preferred_element_type=F32)


def _rms(x, g):
    return x * lax.rsqrt(jnp.mean(x * x, axis=-1, keepdims=True) + NORM_EPS) * g


def _gate_up(h, wg_ref, wu_ref, act_ref, after_dot=lambda: None):
    for c0 in range(0, D_FF, FF_CHUNK):
        g = _dot(h, wg_ref[:, c0:c0 + FF_CHUNK])
        after_dot()
        u = _dot(h, wu_ref[:, c0:c0 + FF_CHUNK])
        after_dot()
        act_ref[:, c0:c0 + FF_CHUNK] = (g * jax.nn.sigmoid(g) * u).astype(BF16)


def _down(act_ref, rows, wd_ref, after_dot=lambda: None):
    outs = []
    for n0 in range(0, D_MODEL, DOWN_CHUNK):
        outs.append(_dot(act_ref[rows, :], wd_ref[:, n0:n0 + DOWN_CHUNK]))
        after_dot()
    return jnp.concatenate(outs, axis=1)


def _head_norm_gate(o, r, gn):
    parts = []
    for h in range(GLA_HEADS):
        sl = slice(h * GLA_HEAD_V, (h + 1) * GLA_HEAD_V)
        parts.append(_rms(o[:, sl], gn[:, sl]))
    return (jnp.concatenate(parts, axis=1) * (r * jax.nn.sigmoid(r))).astype(BF16)


def _out_ffn(x1_ref, og, osw, wo_ref, n2_ref, wg_ref, wu_ref, wd_ref, nf_ref, y_ref, act_ref, side_stages=None):
    n_inside = 0 if side_stages is None else MIXER_STAGES - SIDE_STAGES_AT_NORM - SIDE_STAGES_AFTER_FFN
    side_stages = iter(()) if side_stages is None else side_stages
    n_dots = 2 * (D_FF // FF_CHUNK) + D_MODEL // DOWN_CHUNK
    done = [0, 0]

    def after_dot():
        done[0] += 1
        while done[1] * n_dots < done[0] * n_inside:
            next(side_stages, None)
            done[1] += 1

    mixed = jnp.concatenate([og, osw], axis=1)
    x2 = x1_ref[...] + _dot(mixed, wo_ref[...])
    for _ in range(SIDE_STAGES_AT_NORM):
        next(side_stages, None)
    h = _rms(x2, n2_ref[...]).astype(BF16)
    _gate_up(h, wg_ref, wu_ref, act_ref, after_dot)
    x3 = x2 + 0.5 * _down(act_ref, slice(None), wd_ref, after_dot)
    for _ in side_stages:
        pass
    y_ref[...] = _rms(x3, nf_ref[...]).reshape(y_ref.shape)


_HBM = pl.BlockSpec(memory_space=pl.ANY)
WIDE_CHUNK_ROWS = 128
NARROW_CHUNK_ROWS = 256
LOAD_SLOTS = 4


def _load_weights_bf16(loads, staging, sems):
    jobs = []
    for src, dst, segments, transposed in loads:
        cols = src.shape[1]
        assert cols in (D_FF, D_MODEL) and dst.shape[0 if transposed else 1] == cols
        kind, step = (0, WIDE_CHUNK_ROWS) if cols == D_FF else (1, NARROW_CHUNK_ROWS)
        for s0, d0, n in segments or [(0, 0, src.shape[0])]:
            jobs += [(src, dst, s0 + r, d0 + r, min(step, n - r), kind, transposed)
                     for r in range(0, n, step)]

    used = [0, 0]
    copies = []
    for src, _, s0, _, nr, kind, _ in jobs:
        slot = used[kind] % LOAD_SLOTS
        used[kind] += 1
        copies.append((pltpu.make_async_copy(src.at[pl.ds(s0, nr), :], staging[kind].at[slot, pl.ds(0, nr), :],
                                             sems.at[kind, slot]), slot))
    for copy, _ in copies[:LOAD_SLOTS - 1]:
        copy.start()
    for j, (_, dst, _, d0, nr, kind, transposed) in enumerate(jobs):
        ahead = j + LOAD_SLOTS - 1
        if ahead < len(jobs):
            copies[ahead][0].start()
        copy, slot = copies[j]
        copy.wait()
        if not transposed:
            dst[d0:d0 + nr, :] = staging[kind][slot, 0:nr, :].astype(BF16)
        elif nr % LANES == 0:
            dst[:, d0:d0 + nr] = staging[kind][slot, 0:nr, :].T.astype(BF16)
        else:
            tile = staging[kind][slot, 0:LANES, :].T
            live = lax.broadcasted_iota(jnp.int32, (1, LANES), 1) < nr
            dst[:, d0:d0 + LANES] = jnp.where(live, tile, 0.0).astype(BF16)


def _staging_scratch():
    return [pltpu.VMEM((LOAD_SLOTS, WIDE_CHUNK_ROWS, D_FF), F32),
            pltpu.VMEM((LOAD_SLOTS, NARROW_CHUNK_ROWS, D_MODEL), F32),
            pltpu.SemaphoreType.DMA((2, LOAD_SLOTS))]


def _chunk_plan(matrices):
    plan, first = [], 0
    for m, mat in enumerate(matrices):
        rows, cols = mat.shape
        assert cols in (D_FF, D_MODEL)
        kind, nr = (0, WIDE_CHUNK_ROWS) if cols == D_FF else (1, NARROW_CHUNK_ROWS)
        assert rows % nr == 0
        plan.append((m, first, rows // nr, nr, kind))
        first += rows // nr
    return plan, first


def _background_cast(step, srcs, dsts, staging, out_staging, sem_in, sem_out):
    plan, _ = _chunk_plan(srcs)
    slot = step % 2

    def rows_of(m, chunk):
        _, first, _, nr, _ = plan[m]
        return pl.ds(pl.multiple_of((chunk - first) * nr, nr), nr)

    def copy_in(m, chunk, slot_):
        kind = plan[m][4]
        return pltpu.make_async_copy(srcs[m].at[rows_of(m, chunk), :], staging[kind].at[slot_],
                                     sem_in.at[kind, slot_])

    def copy_out(m, chunk, slot_):
        kind = plan[m][4]
        return pltpu.make_async_copy(out_staging[kind].at[slot_], dsts[m].at[rows_of(m, chunk), :],
                                     sem_out.at[kind, slot_])

    def for_chunk(chunk, fn):
        for m, first, n, _, _ in plan:
            pl.when(jnp.logical_and(chunk >= first, chunk < first + n))(functools.partial(fn, m))

    def before():
        @pl.when(step == 0)
        def _():
            copy_in(0, 0, 0).start()
        for_chunk(step, lambda m: copy_in(m, step, slot).wait())
        for_chunk(step - 2, lambda m: copy_out(m, step - 2, slot).wait())
        for_chunk(step + 1, lambda m: copy_in(m, step + 1, 1 - slot).start())

    def cast():
        for kind in range(2):
            out_staging[kind][slot] = staging[kind][slot].astype(BF16)

    def after():
        for_chunk(step, lambda m: copy_out(m, step, slot).start())

    return before, cast, after


def _stage_a_body(xp_ref, xs_ref, n1_ref, wg_hbm, wu_hbm, wd_hbm, nm_ref, win_hbm, wgu_ref, bg_ref,
                  wo_hbm, wg2_hbm, wu2_hbm, wd2_hbm,
                  x1_ref, qk_ref, g_ref, v_ref, r_ref, qs_ref, kv_ref, wo_out, wg2_out, wu2_out, wd2_out,
                  act_ref, wg_ref, wu_ref, wd_ref, win_ref, wide_stage, narrow_stage, load_sems,
                  wide_out, narrow_out, sem_in, sem_out):
    i = pl.program_id(0)

    @pl.when(i == 0)
    def _():
        a0, a1 = PROJ_Q_S, PROJ_Q_S + GLA_GATE_RANK
        win_rows = [(0, 0, a0), (a1, a0, D_IN - a1), (a0, PROJ_A, GLA_GATE_RANK)]
        _load_weights_bf16([(wg_hbm, wg_ref, None, False), (wu_hbm, wu_ref, None, False),
                            (wd_hbm, wd_ref, None, False), (win_hbm, win_ref, win_rows, True)],
                           (wide_stage, narrow_stage), load_sems)

    bg_before, bg_cast, bg_after = _background_cast(
        i, (wo_hbm, wg2_hbm, wu2_hbm, wd2_hbm), (wo_out, wg2_out, wu2_out, wd2_out),
        (wide_stage, narrow_stage), (wide_out, narrow_out), sem_in, sem_out)
    bg_before()

    x = jnp.where(i < pl.num_programs(0) - 1, xp_ref[...], xs_ref[...].reshape(ROW_TILE, D_MODEL))
    h = _rms(x, n1_ref[...]).astype(BF16)
    _gate_up(h, wg_ref, wu_ref, act_ref)
    bg_cast()

    halves = [slice(k * ROW_TILE // 2, (k + 1) * ROW_TILE // 2) for k in range(2)]
    x1 = []
    for rows in halves:
        x1.append(x[rows, :] + 0.5 * _down(act_ref, rows, wd_ref))
        x1_ref[rows, :] = x1[-1]
    gate_in = []
    for rows, x1_h in zip(halves, x1):
        h2 = _rms(x1_h, nm_ref[...]).astype(BF16)
        proj = _dot(h2, win_ref[...])
        qk_ref[rows, 0:GLA_KEY_WIDTH] = proj[:, PROJ_Q_G:PROJ_K_G] * HEAD_SCALE
        qk_ref[rows, GLA_KEY_WIDTH:] = proj[:, PROJ_K_G:PROJ_V_G]
        v_ref[rows, :] = proj[:, PROJ_V_G:PROJ_R_G].astype(BF16)
        r_ref[rows, :] = proj[:, PROJ_R_G:PROJ_Q_S]
        qs_ref[rows, :] = (proj[:, PROJ_Q_S:PROJ_KV_S] * HEAD_SCALE).astype(BF16)
        kv_ref[rows, :] = proj[:, PROJ_KV_S:PROJ_A]
        gate_in.append(proj[:, PROJ_A:PROJ_WIDTH].astype(BF16))
    for rows, a in zip(halves, gate_in):
        z = _dot(a, wgu_ref[...]) + bg_ref[...]
        g_ref[rows, :] = jax.nn.log_sigmoid(z) * (1.0 / GLA_GATE_TAU)
    bg_after()


def _resident(shape):
    return pl.BlockSpec(shape, lambda i: (0,) * len(shape), pipeline_mode=pl.Buffered(1))


def _rows(tm, n):
    return pl.BlockSpec((tm, n), lambda i: (i, 0))


def _stage_a(xp, xs, w):
    t = xp.shape[0]
    assert t % ROW_TILE == 0 and xs.shape == SAMPLE_TILE_SHAPE
    nt = t // ROW_TILE
    out_widths = ((D_MODEL, F32), (2 * GLA_KEY_WIDTH, F32), (GLA_KEY_WIDTH, F32), (GLA_WIDTH, BF16),
                  (GLA_WIDTH, F32), (SWA_WIDTH, BF16), (2 * SWA_KV_WIDTH, F32))
    to_cast = (w["wo"], w["wg2"], w["wu2"], w["wd2"])
    assert _chunk_plan(to_cast)[1] + 2 <= nt + 1
    outs = pl.pallas_call(
        _stage_a_body,
        grid=(nt + 1,),
        in_specs=[pl.BlockSpec((ROW_TILE, D_MODEL), lambda i: (jnp.minimum(i, nt - 1), 0)),
                  _resident(SAMPLE_TILE_SHAPE), _resident((1, D_MODEL)),
                  _HBM, _HBM, _HBM,
                  _resident((1, D_MODEL)), _HBM,
                  _resident((LANES, GLA_KEY_WIDTH)), _resident((1, GLA_KEY_WIDTH)),
                  _HBM, _HBM, _HBM, _HBM],
        out_specs=[_rows(ROW_TILE, n) for n, _ in out_widths] + [_HBM] * len(to_cast),
        out_shape=([jax.ShapeDtypeStruct((t + ROW_TILE, n), dt) for n, dt in out_widths]
                   + [jax.ShapeDtypeStruct(m.shape, BF16) for m in to_cast]),
        scratch_shapes=([pltpu.VMEM((ROW_TILE, D_FF), BF16), pltpu.VMEM((D_MODEL, D_FF), BF16),
                         pltpu.VMEM((D_MODEL, D_FF), BF16), pltpu.VMEM((D_FF, D_MODEL), BF16),
                         pltpu.VMEM((D_MODEL, PROJ_WIDTH), BF16)]
                        + _staging_scratch()
                        + [pltpu.VMEM((2, WIDE_CHUNK_ROWS, D_FF), BF16),
                           pltpu.VMEM((2, NARROW_CHUNK_ROWS, D_MODEL), BF16),
                           pltpu.SemaphoreType.DMA((2, 2)), pltpu.SemaphoreType.DMA((2, 2))]),
        compiler_params=pltpu.CompilerParams(dimension_semantics=("arbitrary",),
                                             vmem_limit_bytes=STAGE_A_VMEM_LIMIT),
        name="stage_a_ffn1_proj",
    )(xp, xs, w["n1"], w["wg1"], w["wu1"], w["wd1"], w["nm"], w["win"], w["wgu"], w["bg"], *to_cast)
    return outs[:len(out_widths)], outs[len(out_widths):]


def _cumsum_rows(x):
    n = x.shape[0]
    row = lax.broadcasted_iota(jnp.int32, (n, 1), 0)
    shift = 1
    while shift < n:
        x = x + jnp.where(row >= shift, pltpu.roll(x, shift, 0), 0.0)
        shift *= 2
    return x


def _alternate(a, b):
    pending = [iter(a), iter(b)]
    while pending:
        for it in list(pending):
            try:
                next(it)
            except StopIteration:
                pending.remove(it)
                continue
            yield


def _gla_stages(qk_ref, g_ref, v_ref, r_ref, gn_ref, s_ref, og_scr, cum_scr, inter_scr, flag_ref):
    c_len = GLA_CHUNK
    n_chunks = ROW_TILE // c_len
    causal_cat = (lax.broadcasted_iota(jnp.int32, (c_len, GLA_HEADS * c_len), 0)
                  >= lax.broadcasted_iota(jnp.int32, (c_len, GLA_HEADS * c_len), 1) % c_len)
    lane_head = lax.broadcasted_iota(jnp.int32, (1, GLA_KEY_WIDTH), 1) // GLA_HEAD_K
    row_head = lax.broadcasted_iota(jnp.int32, (GLA_KEY_WIDTH, 1), 0) // GLA_HEAD_K
    eye = (lax.broadcasted_iota(jnp.int32, (GLA_KEY_WIDTH, GLA_KEY_WIDTH), 0)
           == lax.broadcasted_iota(jnp.int32, (GLA_KEY_WIDTH, GLA_KEY_WIDTH), 1))
    zero_b = jnp.zeros((), BF16)
    zero_v = jnp.zeros((c_len, GLA_HEAD_V), BF16)
    worst = []

    def chunk(c):
        rows = slice(c * c_len, (c + 1) * c_len)
        cum = _cumsum_rows(g_ref[rows, :])
        cum_scr[rows, :] = cum
        last = cum[c_len - 1:c_len, :]
        q = qk_ref[rows, 0:GLA_KEY_WIDTH]
        k = qk_ref[rows, GLA_KEY_WIDTH:]
        vb = v_ref[rows, :]
        qe = (q * jnp.exp(cum)).astype(BF16)
        ke = (k * jnp.exp(jnp.minimum(-cum, DECAY_CLAMP))).astype(BF16)
        kl = (k * jnp.exp(last - cum)).astype(BF16)
        state = s_ref[...]
        sb = state.astype(BF16)
        s_bd = jnp.concatenate([jnp.where(row_head == h, sb, zero_b) for h in range(GLA_HEADS)], axis=1)
        o_inter = _dot(qe, s_bd)
        inter_scr[rows, :] = o_inter
        ke_bd = jnp.concatenate([jnp.where(lane_head == h, ke, zero_b) for h in range(GLA_HEADS)], axis=0)
        attn = _dot_tb(qe, ke_bd)
        upds = []
        for p in range(GLA_HEADS // 2):
            u = _dot_ta(kl[:, p * LANES:(p + 1) * LANES], vb[:, 2 * p * GLA_HEAD_V:(2 * p + 2) * GLA_HEAD_V])
            upds.append(u[0:GLA_HEAD_K, 0:GLA_HEAD_V])
            upds.append(u[GLA_HEAD_K:, GLA_HEAD_V:])
        yield
        attn = jnp.where(causal_cat, attn, 0.0).astype(BF16)
        o_pairs = []
        for p in range(GLA_HEADS // 2):
            v_a = vb[:, (2 * p) * GLA_HEAD_V:(2 * p + 1) * GLA_HEAD_V]
            v_b = vb[:, (2 * p + 1) * GLA_HEAD_V:(2 * p + 2) * GLA_HEAD_V]
            v_bd = jnp.concatenate([jnp.concatenate([v_a, zero_v], axis=1),
                                    jnp.concatenate([zero_v, v_b], axis=1)], axis=0)
            o_pairs.append(_dot(attn[:, 2 * p * c_len:(2 * p + 2) * c_len], v_bd))
        og_scr[rows, :] = _head_norm_gate(o_inter + jnp.concatenate(o_pairs, axis=1), r_ref[rows, :],
                                          gn_ref[...])
        last_col = jnp.sum(jnp.where(eye, last, 0.0), axis=1, keepdims=True)
        s_ref[...] = state * jnp.exp(last_col) + jnp.concatenate(upds, axis=0)
        worst.append(jnp.min(last, axis=1, keepdims=True))
        if c == n_chunks - 1:
            tile_min = functools.reduce(jnp.minimum, worst)
            flag_ref[0] = jnp.where(tile_min[0, 0] < -DECAY_CLAMP, 1, 0)
        yield

    return itertools.chain.from_iterable(chunk(c) for c in range(n_chunks))


def _gla_tile_exact(qk_ref, v_ref, r_ref, gn_ref, og_scr, o_scr, cum_scr, inter_scr):
    c_len = GLA_CHUNK
    ind = jnp.where(lax.broadcasted_iota(jnp.int32, (GLA_KEY_WIDTH, LANES), 0) // GLA_HEAD_K
                    == lax.broadcasted_iota(jnp.int32, (GLA_KEY_WIDTH, LANES), 1), 1.0, 0.0).astype(BF16)
    j_idx = lax.broadcasted_iota(jnp.int32, (c_len, 1), 0)
    for c in range(ROW_TILE // c_len):
        r0 = c * c_len
        rows = slice(r0, r0 + c_len)

        def one_row(i, carry, r0=r0, rows=rows):
            ci = cum_scr[pl.ds(r0 + i, 1), :]
            qi = qk_ref[pl.ds(r0 + i, 1), 0:GLA_KEY_WIDTH]
            kk = qk_ref[rows, GLA_KEY_WIDTH:]
            dec = jnp.exp(jnp.minimum(ci - cum_scr[rows, :], 0.0))
            a_cols = _dot(((qi * kk) * dec).astype(BF16), ind)
            outs = []
            for h in range(GLA_HEADS):
                w_col = jnp.where(j_idx <= i, a_cols[:, h:h + 1], 0.0)
                v_h = v_ref[rows, h * GLA_HEAD_V:(h + 1) * GLA_HEAD_V].astype(F32)
                outs.append(jnp.sum(w_col * v_h, axis=0, keepdims=True))
            o_scr[pl.ds(r0 + i, 1), :] = inter_scr[pl.ds(r0 + i, 1), :] + jnp.concatenate(outs, axis=1)
            return carry

        lax.fori_loop(0, c_len, one_row, 0)
    og_scr[...] = _head_norm_gate(o_scr[...], r_ref[...], gn_ref[...])


def _dup_halves(x):
    lo = lax.broadcasted_iota(jnp.int32, (1, LANES), 1) < SWA_HEAD_DIM
    sw = pltpu.roll(x, SWA_HEAD_DIM, 1)
    return jnp.where(lo, x, sw).astype(BF16), jnp.where(lo, sw, x).astype(BF16)


def _swa_stages(sink_ref, qs_ref, kv_ref, kvprev_scr, sn_ref, os_scr, no_prev_bias):
    w = WINDOW
    n_blocks = ROW_TILE // w
    lo = lax.broadcasted_iota(jnp.int32, (1, LANES), 1) < SWA_HEAD_DIM
    tri = (lax.broadcasted_iota(jnp.int32, (w, w), 1) <= lax.broadcasted_iota(jnp.int32, (w, w), 0))
    zero_b = jnp.zeros((), BF16)

    def block(b):
        rows = slice(b * w, (b + 1) * w)
        kv_prev = kvprev_scr[...] if b == 0 else kv_ref[(b - 1) * w:b * w, :]
        kv = jnp.concatenate([kv_prev, kv_ref[rows, :]], axis=0)
        kk = _dup_halves(kv[:, 0:SWA_KV_WIDTH])
        vv = _dup_halves(kv[:, SWA_KV_WIDTH:])
        q = qs_ref[rows, :]
        scores = []
        for grp in range(SWA_KV_HEADS):
            stacked = []
            for j in range(SWA_GROUP):
                hq = grp * SWA_GROUP + j
                tile = q[:, (hq // 2) * LANES:(hq // 2 + 1) * LANES]
                stacked.append(jnp.where(lo if hq % 2 == 0 else ~lo, tile, zero_b))
            scores.append(_dot_tb(jnp.concatenate(stacked, axis=0), kk[grp]))
        yield
        outs, dens = [], []
        for grp in range(SWA_KV_HEADS):
            s = scores[grp]
            probs = []
            for j in range(SWA_GROUP):
                sink = sink_ref[grp * SWA_GROUP + j]
                s_prev = s[j * w:(j + 1) * w, 0:w]
                if b == 0:
                    s_prev = s_prev + no_prev_bias
                sf = jnp.where(tri, s[j * w:(j + 1) * w, w:], s_prev)
                m = jnp.maximum(jnp.max(sf, axis=1, keepdims=True), sink)
                p = jnp.exp(sf - m)
                dens.append(jnp.sum(p, axis=1, keepdims=True) + jnp.exp(sink - m))
                pb = p.astype(BF16)
                probs.append(jnp.concatenate([jnp.where(tri, zero_b, pb), jnp.where(tri, pb, zero_b)], axis=1))
            outs.append(_dot(jnp.concatenate(probs, axis=0), vv[grp]))
        yield
        tiles = []
        for grp in range(SWA_KV_HEADS):
            res = [outs[grp][j * w:(j + 1) * w] / dens[grp * SWA_GROUP + j] for j in range(SWA_GROUP)]
            tiles.append(jnp.where(lo, res[0], res[1]))
            tiles.append(jnp.where(lo, res[2], res[3]))
        os_scr[rows, :] = _rms(jnp.concatenate(tiles, axis=1), sn_ref[...]).astype(BF16)
        if b == n_blocks - 1:
            kvprev_scr[...] = kv_ref[rows, :]
        yield

    return itertools.chain.from_iterable(block(b) for b in range(n_blocks))


def _mix_out_body(sink_ref, qk_ref, g_ref, v_ref, r_ref, qs_ref, kv_ref, x1_ref, ogs_ref, oss_ref,
                  gn_ref, sn_ref, wo_ref, n2_ref, wg_ref, wu_ref, wd_ref, nf_ref, yp_ref, ys_ref, s_ref,
                  og_scr, os_scr, kvprev_scr, o_scr, cum_scr, inter_scr, act_scr, flag_ref):
    i = pl.program_id(0)
    n_prompt = pl.num_programs(0) - 1

    def stage_d(og, osw, y_ref, side_stages=None):
        _out_ffn(x1_ref, og, osw, wo_ref, n2_ref, wg_ref, wu_ref, wd_ref, nf_ref, y_ref, act_scr,
                 side_stages)

    def mixer_stages():
        gla = _gla_stages(qk_ref, g_ref, v_ref, r_ref, gn_ref, s_ref, og_scr, cum_scr, inter_scr, flag_ref)
        swa = _swa_stages(sink_ref, qs_ref, kv_ref, kvprev_scr, sn_ref, os_scr,
                          jnp.where(i > 0, 0.0, -jnp.inf))
        return _alternate(gla, swa)

    @pl.when(i == 0)
    def _():
        s_ref[...] = jnp.zeros_like(s_ref)
        kvprev_scr[...] = jnp.zeros_like(kvprev_scr)
        stage_d(ogs_ref[...], oss_ref[...], ys_ref, mixer_stages())

    @pl.when(jnp.logical_and(i > 0, i < n_prompt))
    def _():
        stage_d(og_scr[...], os_scr[...], yp_ref, mixer_stages())

    @pl.when(i == n_prompt)
    def _():
        stage_d(og_scr[...], os_scr[...], yp_ref)
        flag_ref[0] = 0

    @pl.when(flag_ref[0] != 0)
    def _():
        _gla_tile_exact(qk_ref, v_ref, r_ref, gn_ref, og_scr, o_scr, cum_scr, inter_scr)


def _mix_out(qk, g, v, r, qs, kv, x1, og_s, os_s, w):
    t = qk.shape[0] - ROW_TILE
    assert t % ROW_TILE == 0 and og_s.shape[0] == ROW_TILE
    nt = t // ROW_TILE
    cur = lambda n: pl.BlockSpec((ROW_TILE, n), lambda i: (jnp.minimum(i, nt - 1), 0))
    return pl.pallas_call(
        _mix_out_body,
        grid=(nt + 1,),
        in_specs=[pl.BlockSpec(memory_space=pltpu.SMEM),
                  cur(2 * GLA_KEY_WIDTH), cur(GLA_KEY_WIDTH), cur(GLA_WIDTH), cur(GLA_WIDTH),
                  cur(SWA_WIDTH), cur(2 * SWA_KV_WIDTH),
                  pl.BlockSpec((ROW_TILE, D_MODEL), lambda i: (jnp.where(i == 0, nt, i - 1), 0)),
                  _resident((ROW_TILE, GLA_WIDTH)), _resident((ROW_TILE, SWA_WIDTH)),
                  _resident((1, GLA_WIDTH)), _resident((1, SWA_WIDTH)),
                  _resident((D_MODEL, D_MODEL)), _resident((1, D_MODEL)),
                  _resident((D_MODEL, D_FF)), _resident((D_MODEL, D_FF)), _resident((D_FF, D_MODEL)),
                  _resident((1, D_MODEL))],
        out_specs=[pl.BlockSpec((ROW_TILE, D_MODEL), lambda i: (jnp.clip(i - 1, 0, nt - 1), 0)),
                   pl.BlockSpec(SAMPLE_TILE_SHAPE, lambda i: (0, 0, 0)),
                   pl.BlockSpec((GLA_KEY_WIDTH, GLA_HEAD_V), lambda i: (0, 0))],
        out_shape=[jax.ShapeDtypeStruct((t, D_MODEL), F32),
                   jax.ShapeDtypeStruct(SAMPLE_TILE_SHAPE, F32),
                   jax.ShapeDtypeStruct((GLA_KEY_WIDTH, GLA_HEAD_V), F32)],
        scratch_shapes=[pltpu.VMEM((ROW_TILE, GLA_WIDTH), BF16), pltpu.VMEM((ROW_TILE, SWA_WIDTH), BF16),
                        pltpu.VMEM((WINDOW, 2 * SWA_KV_WIDTH), F32),
                        pltpu.VMEM((ROW_TILE, GLA_WIDTH), F32), pltpu.VMEM((ROW_TILE, GLA_KEY_WIDTH), F32),
                        pltpu.VMEM((ROW_TILE, GLA_WIDTH), F32), pltpu.VMEM((ROW_TILE, D_FF), BF16),
                        pltpu.SMEM((1,), jnp.int32)],
        compiler_params=pltpu.CompilerParams(dimension_semantics=("arbitrary",),
                                             vmem_limit_bytes=STAGE_D_VMEM_LIMIT),
        name="prompt_mixers_stage_d",
    )(w["sinks"], qk, g, v, r, qs, kv, x1, og_s, os_s, w["gn"], w["sn"], w["wo"], w["n2"], w["wg2"],
      w["wu2"], w["wd2"], w["nf"])


def _gla_sample_body(qk_ref, g_ref, v_ref, r_ref, gn_ref, s_ref, og_ref, so_ref, o_scr):
    rows = SAMPLE_ROWS
    g = g_ref[...]
    q = qk_ref[:, 0:GLA_KEY_WIDTH]
    k = qk_ref[:, GLA_KEY_WIDTH:]
    vb = v_ref[...]
    vf = vb.astype(F32)
    tok = lax.broadcasted_iota(jnp.int32, (rows, 1), 0) % DEC_SEQ

    cum = g
    for d in range(1, DEC_SEQ):
        cum = cum + jnp.where(tok >= d, pltpu.roll(g, d, 0), 0.0)
    tot = jnp.where(tok == DEC_SEQ - 1, cum, 0.0)
    for d in range(1, DEC_SEQ):
        tot = tot + jnp.where(tok == DEC_SEQ - 1 - d, pltpu.roll(cum, rows - d, 0), 0.0)

    qe = q * jnp.exp(cum)
    kl = k * jnp.exp(tot - cum)
    decay_t = jnp.exp(tot).T

    ind = jnp.where(lax.broadcasted_iota(jnp.int32, (GLA_KEY_WIDTH, LANES), 0) // GLA_HEAD_K
                    == lax.broadcasted_iota(jnp.int32, (GLA_KEY_WIDTH, LANES), 1), 1.0, 0.0).astype(BF16)
    expand = jnp.where(lax.broadcasted_iota(jnp.int32, (LANES, GLA_WIDTH), 0)
                       == lax.broadcasted_iota(jnp.int32, (LANES, GLA_WIDTH), 1) // GLA_HEAD_V,
                       1.0, 0.0).astype(BF16)

    o_intra = jnp.zeros((rows, GLA_WIDTH), F32)
    for d in range(DEC_SEQ):
        k_d = k if d == 0 else pltpu.roll(k, d, 0)
        c_d = cum if d == 0 else pltpu.roll(cum, d, 0)
        v_d = vf if d == 0 else pltpu.roll(vf, d, 0)
        pair = jnp.where(tok >= d, q * k_d * jnp.exp(jnp.minimum(cum - c_d, 0.0)), 0.0)
        a = _dot(pair.astype(BF16), ind)
        o_intra = o_intra + _dot(a.astype(BF16), expand) * v_d

    lane_head = lax.broadcasted_iota(jnp.int32, (1, GLA_KEY_WIDTH), 1) // GLA_HEAD_K
    row8 = lax.broadcasted_iota(jnp.int32, (SUBLANES, 1), 0)
    row32 = lax.broadcasted_iota(jnp.int32, (GLA_HEADS * SUBLANES, 1), 0)
    for pair_idx in range(SAMPLE_SEQS // 2):
        r8 = slice(pair_idx * SUBLANES, (pair_idx + 1) * SUBLANES)
        q8 = qe[r8, :]
        lhs = jnp.concatenate([jnp.where(lane_head == h, q8, 0.0) for h in range(GLA_HEADS)],
                              axis=0).astype(BF16)
        kl8 = kl[r8, :]
        v8 = vb[r8, :]
        res = []
        for s in range(2):
            b = 2 * pair_idx + s
            state = s_ref[b]
            res.append(_dot(lhs, state.astype(BF16)))
            kl_b = jnp.where(row8 // DEC_SEQ == s, kl8, 0.0).astype(BF16)
            upd = _dot_ta(kl_b, v8)
            upd = jnp.concatenate(
                [upd[h * GLA_HEAD_K:(h + 1) * GLA_HEAD_K, h * GLA_HEAD_V:(h + 1) * GLA_HEAD_V]
                 for h in range(GLA_HEADS)], axis=0)
            so_ref[b] = state * decay_t[:, DEC_SEQ * b:DEC_SEQ * b + 1] + upd
        sel = jnp.where(row32 % SUBLANES < DEC_SEQ, res[0], res[1])
        o_scr[r8, :] = jnp.concatenate([sel[h * SUBLANES:(h + 1) * SUBLANES] for h in range(GLA_HEADS)], axis=1)

    og_ref[...] = _head_norm_gate(o_scr[...] + o_intra, r_ref[...], gn_ref[...])


def _gla_sample(qk, g, v, r, gn, state, row0):
    nseq = state.shape[0]
    rows = nseq * DEC_SEQ
    assert nseq % SAMPLE_SEQS == 0 and row0 % SAMPLE_ROWS == 0
    blk0 = row0 // SAMPLE_ROWS
    src = lambda n: pl.BlockSpec((SAMPLE_ROWS, n), lambda i: (i + blk0, 0))
    blk = lambda n: pl.BlockSpec((SAMPLE_ROWS, n), lambda i: (i, 0))
    sblk = pl.BlockSpec((SAMPLE_SEQS, GLA_KEY_WIDTH, GLA_HEAD_V), lambda i: (i, 0, 0))
    return pl.pallas_call(
        _gla_sample_body,
        grid=(nseq // SAMPLE_SEQS,),
        in_specs=[src(2 * GLA_KEY_WIDTH), src(GLA_KEY_WIDTH), src(GLA_WIDTH), src(GLA_WIDTH),
                  pl.BlockSpec((1, GLA_WIDTH), lambda i: (0, 0)), sblk],
        out_specs=[blk(GLA_WIDTH), sblk],
        out_shape=[jax.ShapeDtypeStruct((rows, GLA_WIDTH), BF16),
                   jax.ShapeDtypeStruct((nseq, GLA_KEY_WIDTH, GLA_HEAD_V), F32)],
        scratch_shapes=[pltpu.VMEM((SAMPLE_ROWS, GLA_WIDTH), F32)],
        compiler_params=pltpu.CompilerParams(dimension_semantics=("arbitrary",)),
        name="gla_sample",
    )(qk, g, v, r, gn, state)


def _swa_sample_body(sink_ref, q_ref, kvn_ref, kc_ref, vc_ref, nrm_ref, o_ref, kco_ref, vco_ref,
                     lhs_scr, sc_scr, pc_scr, oc_scr):
    rows = SAMPLE_ROWS
    q = q_ref[...].astype(F32)
    lo = lax.broadcasted_iota(jnp.int32, (1, LANES), 1) < SWA_HEAD_DIM
    for hq in range(SWA_Q_HEADS):
        grp = hq // SWA_GROUP
        tile = q[:, (hq // 2) * LANES:(hq // 2 + 1) * LANES]
        src = tile if hq % 2 == grp else pltpu.roll(tile, SWA_HEAD_DIM, 1)
        lhs_scr[hq * rows:(hq + 1) * rows, :] = jnp.where(lo if grp == 0 else ~lo, src, 0.0)

    kvn = kvn_ref[...]
    k_new = kvn[:, 0:SWA_KV_WIDTH]
    v_new = kvn[:, SWA_KV_WIDTH:]
    s_new = _dot_tb(lhs_scr[...].astype(BF16), k_new.astype(BF16))

    row64 = lax.broadcasted_iota(jnp.int32, (SWA_Q_HEADS * SUBLANES, 1), 0)
    first_of_pair = row64 % SUBLANES < DEC_SEQ

    def gather_pair(ref, pair_idx):
        return jnp.concatenate(
            [ref[hq * rows + pair_idx * SUBLANES:hq * rows + (pair_idx + 1) * SUBLANES, :]
             for hq in range(SWA_Q_HEADS)], axis=0).astype(BF16)

    def scatter_pair(ref, pair_idx, val):
        for hq in range(SWA_Q_HEADS):
            ref[hq * rows + pair_idx * SUBLANES:hq * rows + (pair_idx + 1) * SUBLANES, :] = (
                val[hq * SUBLANES:(hq + 1) * SUBLANES])

    for pair_idx in range(SAMPLE_SEQS // 2):
        l64 = gather_pair(lhs_scr, pair_idx)
        sa = _dot(l64, kc_ref[2 * pair_idx].astype(BF16))
        sb = _dot(l64, kc_ref[2 * pair_idx + 1].astype(BF16))
        scatter_pair(sc_scr, pair_idx, jnp.where(first_of_pair, sa, sb))

    rr = lax.broadcasted_iota(jnp.int32, (rows, rows), 0)
    cc = lax.broadcasted_iota(jnp.int32, (rows, rows), 1)
    tok = rr % DEC_SEQ
    mask_cache = cc > tok
    mask_new = (cc // DEC_SEQ == rr // DEC_SEQ) & (cc % DEC_SEQ <= tok)
    p_new, dens = [], []
    for hq in range(SWA_Q_HEADS):
        sl = slice(hq * rows, (hq + 1) * rows)
        sink = sink_ref[hq]
        s_c = jnp.where(mask_cache, sc_scr[sl, :], -jnp.inf)
        s_n = jnp.where(mask_new, s_new[sl, :], -jnp.inf)
        m = jnp.maximum(jnp.maximum(jnp.max(s_c, axis=1, keepdims=True),
                                    jnp.max(s_n, axis=1, keepdims=True)), sink)
        p_c = jnp.exp(s_c - m)
        p_n = jnp.exp(s_n - m)
        dens.append(jnp.sum(p_c, axis=1, keepdims=True) + jnp.sum(p_n, axis=1, keepdims=True)
                    + jnp.exp(sink - m))
        pc_scr[sl, :] = p_c
        p_new.append(p_n.astype(BF16))
    o_new = _dot(jnp.concatenate(p_new, axis=0), v_new.astype(BF16))

    for pair_idx in range(SAMPLE_SEQS // 2):
        p64 = gather_pair(pc_scr, pair_idx)
        oa = _dot_tb(p64, vc_ref[2 * pair_idx].astype(BF16))
        ob = _dot_tb(p64, vc_ref[2 * pair_idx + 1].astype(BF16))
        scatter_pair(oc_scr, pair_idx, jnp.where(first_of_pair, oa, ob))

    tiles = []
    for i in range(SWA_Q_HEADS // 2):
        halves = []
        for hq in (2 * i, 2 * i + 1):
            sl = slice(hq * rows, (hq + 1) * rows)
            oh = (oc_scr[sl, :] + o_new[sl, :]) / dens[hq]
            halves.append(oh if hq % 2 == hq // SWA_GROUP else pltpu.roll(oh, SWA_HEAD_DIM, 1))
        tiles.append(jnp.where(lo, halves[0], halves[1]))
    o_ref[...] = _rms(jnp.concatenate(tiles, axis=1), nrm_ref[...]).astype(BF16)

    keep = lax.broadcasted_iota(jnp.int32, (1, WINDOW), 1) < WINDOW - DEC_SEQ
    k_new_t = k_new.T
    v_new_t = v_new.T
    for b in range(SAMPLE_SEQS):
        to_tail = (WINDOW - DEC_SEQ - DEC_SEQ * b) % rows
        kco_ref[b] = jnp.where(keep, pltpu.roll(kc_ref[b], WINDOW - DEC_SEQ, 1),
                               pltpu.roll(k_new_t, to_tail, 1))
        vco_ref[b] = jnp.where(keep, pltpu.roll(vc_ref[b], WINDOW - DEC_SEQ, 1),
                               pltpu.roll(v_new_t, to_tail, 1))


def _swa_sample(sinks, qs, kvn, kc, vc, nrm, row0):
    nseq = kc.shape[0]
    rows = nseq * DEC_SEQ
    assert nseq % SAMPLE_SEQS == 0 and row0 % SAMPLE_ROWS == 0
    assert kc.shape[1:] == (SWA_KV_WIDTH, WINDOW) and WINDOW == SAMPLE_ROWS
    blk0 = row0 // SAMPLE_ROWS
    cblk = pl.BlockSpec((SAMPLE_SEQS, SWA_KV_WIDTH, WINDOW), lambda i: (i, 0, 0))
    big = pltpu.VMEM((SWA_Q_HEADS * SAMPLE_ROWS, LANES), F32)
    return pl.pallas_call(
        _swa_sample_body,
        grid=(nseq // SAMPLE_SEQS,),
        in_specs=[pl.BlockSpec(memory_space=pltpu.SMEM),
                  pl.BlockSpec((SAMPLE_ROWS, SWA_WIDTH), lambda i: (i + blk0, 0)),
                  pl.BlockSpec((SAMPLE_ROWS, 2 * SWA_KV_WIDTH), lambda i: (i + blk0, 0)),
                  cblk, cblk,
                  pl.BlockSpec((1, SWA_WIDTH), lambda i: (0, 0))],
        out_specs=[pl.BlockSpec((SAMPLE_ROWS, SWA_WIDTH), lambda i: (i, 0)), cblk, cblk],
        out_shape=[jax.ShapeDtypeStruct((rows, SWA_WIDTH), BF16),
                   jax.ShapeDtypeStruct(kc.shape, F32), jax.ShapeDtypeStruct(vc.shape, F32)],
        scratch_shapes=[big, big, big, big],
        compiler_params=pltpu.CompilerParams(dimension_semantics=("arbitrary",)),
        name="swa_sample",
    )(sinks, qs, kvn, kc, vc, nrm)


def _prepare_weights(ffn1_norm, ffn1_w_gate, ffn1_w_up, ffn1_w_down, mix_norm, w_in, w_gate_up, b_gate,
                     gla_head_norm, swa_out_norm, swa_sinks, w_out,
                     ffn2_norm, ffn2_w_gate, ffn2_w_up, ffn2_w_down, final_norm, layer):
    assert w_in.shape[1:] == (D_MODEL, D_IN) and PROJ_Q_S == 2 * GLA_KEY_WIDTH + 2 * GLA_WIDTH
    wgu = jnp.concatenate([w_gate_up[layer],
                           jnp.zeros((LANES - GLA_GATE_RANK, GLA_KEY_WIDTH), w_gate_up.dtype)], axis=0)
    row = lambda a: a.reshape(1, -1).astype(F32)
    return dict(
        n1=row(ffn1_norm[layer]), wg1=ffn1_w_gate[layer], wu1=ffn1_w_up[layer], wd1=ffn1_w_down[layer],
        nm=row(mix_norm[layer]), win=w_in[layer].T, wgu=wgu.astype(BF16), bg=row(b_gate[layer]),
        gn=row(gla_head_norm[layer]), sn=row(swa_out_norm[layer]), sinks=swa_sinks[layer].astype(F32),
        wo=w_out[layer], n2=row(ffn2_norm[layer]), wg2=ffn2_w_gate[layer], wu2=ffn2_w_up[layer],
        wd2=ffn2_w_down[layer], nf=row(final_norm))


def kernel(x_prompt, x_sample, state_gla, cache_swa_k, cache_swa_v, ffn1_norm, ffn1_w_gate, ffn1_w_up,
           ffn1_w_down, mix_norm, w_in, w_gate_up, b_gate, gla_head_norm, swa_out_norm, swa_sinks, w_out,
           ffn2_norm, ffn2_w_gate, ffn2_w_up, ffn2_w_down, final_norm):
    depth = state_gla.shape[0]
    assert depth == 1 and x_prompt.shape[0] == 1 and x_sample.shape[1] == DEC_SEQ
    seq = x_prompt.shape[1]
    nseq = x_sample.shape[0]
    w = _prepare_weights(ffn1_norm, ffn1_w_gate, ffn1_w_up, ffn1_w_down, mix_norm, w_in, w_gate_up, b_gate,
                         gla_head_norm, swa_out_norm, swa_sinks, w_out,
                         ffn2_norm, ffn2_w_gate, ffn2_w_up, ffn2_w_down, final_norm, 0)

    (x1, qk, g, v, r, qs, kv), (wo, wg2, wu2, wd2) = _stage_a(
        x_prompt.reshape(seq, D_MODEL), x_sample, w)
    w = dict(w, wo=wo, wg2=wg2, wu2=wu2, wd2=wd2)

    og_s, state_s = _gla_sample(qk, g, v, r, w["gn"],
                                state_gla[0].reshape(nseq, GLA_KEY_WIDTH, GLA_HEAD_V), seq)
    cache_w = cache_swa_k.shape[2]
    to_feature_major = lambda c: jnp.transpose(c[0], (0, 2, 3, 1)).reshape(nseq, SWA_KV_WIDTH, cache_w)
    from_feature_major = lambda c: jnp.transpose(
        c.reshape(nseq, SWA_KV_HEADS, SWA_HEAD_DIM, cache_w), (0, 3, 1, 2))[None]
    os_s, k_cache_s, v_cache_s = _swa_sample(
        w["sinks"], qs, kv, to_feature_major(cache_swa_k), to_feature_major(cache_swa_v), w["sn"], seq)
    k_cache_s = from_feature_major(k_cache_s)
    v_cache_s = from_feature_major(v_cache_s)

    y_prompt, y_sample, state_p = _mix_out(qk, g, v, r, qs, kv, x1, og_s, os_s, w)
    y_prompt = y_prompt.reshape(1, seq, D_MODEL)
    cw = min(WINDOW, seq)
    k_cache_p = kv[seq - cw:seq, 0:SWA_KV_WIDTH].reshape(1, 1, cw, SWA_KV_HEADS, SWA_HEAD_DIM)
    v_cache_p = kv[seq - cw:seq, SWA_KV_WIDTH:].reshape(1, 1, cw, SWA_KV_HEADS, SWA_HEAD_DIM)
    state_p = state_p.reshape(1, 1, GLA_HEADS, GLA_HEAD_K, GLA_HEAD_V)
    state_s = state_s.reshape(1, nseq, GLA_HEADS, GLA_HEAD_K, GLA_HEAD_V)

    return (y_prompt, y_sample, state_p, k_cache_p, v_cache_p, state_s, k_cache_s, v_cache_s)
```

```python
import functools
import itertools

import jax
import jax.numpy as jnp
from jax import lax
from jax.experimental import pallas as pl
from jax.experimental.pallas import tpu as pltpu

F32 = jnp.float32
BF16 = jnp.bfloat16

D_MODEL = 1024
D_FF = 2816
GLA_HEADS = 4
GLA_HEAD_K = 64
GLA_HEAD_V = 128
GLA_KEY_WIDTH = GLA_HEADS * GLA_HEAD_K
GLA_WIDTH = GLA_HEADS * GLA_HEAD_V
GLA_GATE_RANK = 16
GLA_GATE_TAU = 16.0
SWA_HEAD_DIM = 64
SWA_Q_HEADS = 8
SWA_KV_HEADS = 2
SWA_GROUP = SWA_Q_HEADS // SWA_KV_HEADS
SWA_WIDTH = SWA_Q_HEADS * SWA_HEAD_DIM
SWA_KV_WIDTH = SWA_KV_HEADS * SWA_HEAD_DIM
WINDOW = 128
DEC_SEQ = 4
NORM_EPS = 1e-6
HEAD_SCALE = 0.125

LANES = 128
SUBLANES = 8
VMEM_CAPACITY_BYTES = 64 * 1024 * 1024
STAGE_A_VMEM_LIMIT = VMEM_CAPACITY_BYTES - 6 * 1024 * 1024
STAGE_D_VMEM_LIMIT = VMEM_CAPACITY_BYTES - 8 * 1024 * 1024

ROW_TILE = 512
FF_CHUNK = 256
DOWN_CHUNK = 256
GLA_CHUNK = 128
SAMPLE_SEQS = 32
SAMPLE_ROWS = SAMPLE_SEQS * DEC_SEQ
SAMPLE_TILE_SHAPE = (ROW_TILE // DEC_SEQ, DEC_SEQ, D_MODEL)
MIXER_STAGES = 2 * (ROW_TILE // GLA_CHUNK) + 3 * (ROW_TILE // WINDOW)
SIDE_STAGES_AT_NORM = 1
SIDE_STAGES_AFTER_FFN = 0
DECAY_CLAMP = 60.0

PROJ_Q_G = 0
PROJ_K_G = PROJ_Q_G + GLA_KEY_WIDTH
PROJ_V_G = PROJ_K_G + GLA_KEY_WIDTH
PROJ_R_G = PROJ_V_G + GLA_WIDTH
PROJ_Q_S = PROJ_R_G + GLA_WIDTH
PROJ_KV_S = PROJ_Q_S + SWA_WIDTH
PROJ_A = PROJ_KV_S + 2 * SWA_KV_WIDTH
PROJ_WIDTH = PROJ_A + LANES
D_IN = PROJ_A + GLA_GATE_RANK


def _dot(a, b):
    return jnp.dot(a, b, preferred_element_type=F32)


def _dot_tb(a, b):
    return lax.dot_general(a, b, (((1,), (1,)), ((), ())), preferred_element_type=F32)


def _dot_ta(a, b):
    return lax.dot_general(a, b, (((0,), (0,)), ((), ())), preferred_element_type=F32)


def _rms(x, g):
    return x * lax.rsqrt(jnp.mean(x * x, axis=-1, keepdims=True) + NORM_EPS) * g


def _gate_up(h, wg_ref, wu_ref, act_ref, after_dot=lambda: None):
    for c0 in range(0, D_FF, FF_CHUNK):
        g = _dot(h, wg_ref[:, c0:c0 + FF_CHUNK])
        after_dot()
        u = _dot(h, wu_ref[:, c0:c0 + FF_CHUNK])
        after_dot()
        act_ref[:, c0:c0 + FF_CHUNK] = (g * jax.nn.sigmoid(g) * u).astype(BF16)


def _down(act_ref, rows, wd_ref, after_dot=lambda: None):
    outs = []
    for n0 in range(0, D_MODEL, DOWN_CHUNK):
        outs.append(_dot(act_ref[rows, :], wd_ref[:, n0:n0 + DOWN_CHUNK]))
        after_dot()
    return jnp.concatenate(outs, axis=1)


def _head_norm_gate(o, r, gn):
    parts = []
    for h in range(GLA_HEADS):
        sl = slice(h * GLA_HEAD_V, (h + 1) * GLA_HEAD_V)
        parts.append(_rms(o[:, sl], gn[:, sl]))
    return (jnp.concatenate(parts, axis=1) * (r * jax.nn.sigmoid(r))).astype(BF16)


def _out_ffn(x1_ref, og, osw, wo_ref, n2_ref, wg_ref, wu_ref, wd_ref, nf_ref, y_ref, act_ref, side_stages=None):
    n_inside = 0 if side_stages is None else MIXER_STAGES - SIDE_STAGES_AT_NORM - SIDE_STAGES_AFTER_FFN
    side_stages = iter(()) if side_stages is None else side_stages
    n_dots = 2 * (D_FF // FF_CHUNK) + D_MODEL // DOWN_CHUNK
    done = [0, 0]

    def after_dot():
        done[0] += 1
        while done[1] * n_dots < done[0] * n_inside:
            next(side_stages, None)
            done[1] += 1

    mixed = jnp.concatenate([og, osw], axis=1)
    x2 = x1_ref[...] + _dot(mixed, wo_ref[...])
    for _ in range(SIDE_STAGES_AT_NORM):
        next(side_stages, None)
    h = _rms(x2, n2_ref[...]).astype(BF16)
    _gate_up(h, wg_ref, wu_ref, act_ref, after_dot)
    x3 = x2 + 0.5 * _down(act_ref, slice(None), wd_ref, after_dot)
    for _ in side_stages:
        pass
    y_ref[...] = _rms(x3, nf_ref[...]).reshape(y_ref.shape)


_HBM = pl.BlockSpec(memory_space=pl.ANY)
WIDE_CHUNK_ROWS = 128
NARROW_CHUNK_ROWS = 256
LOAD_SLOTS = 4


def _load_weights_bf16(loads, staging, sems):
    jobs = []
    for src, dst, segments, transposed in loads:
        cols = src.shape[1]
        assert cols in (D_FF, D_MODEL) and dst.shape[0 if transposed else 1] == cols
        kind, step = (0, WIDE_CHUNK_ROWS) if cols == D_FF else (1, NARROW_CHUNK_ROWS)
        for s0, d0, n in segments or [(0, 0, src.shape[0])]:
            jobs += [(src, dst, s0 + r, d0 + r, min(step, n - r), kind, transposed)
                     for r in range(0, n, step)]

    used = [0, 0]
    copies = []
    for src, _, s0, _, nr, kind, _ in jobs:
        slot = used[kind] % LOAD_SLOTS
        used[kind] += 1
        copies.append((pltpu.make_async_copy(src.at[pl.ds(s0, nr), :], staging[kind].at[slot, pl.ds(0, nr), :],
                                             sems.at[kind, slot]), slot))
    for copy, _ in copies[:LOAD_SLOTS - 1]:
        copy.start()
    for j, (_, dst, _, d0, nr, kind, transposed) in enumerate(jobs):
        ahead = j + LOAD_SLOTS - 1
        if ahead < len(jobs):
            copies[ahead][0].start()
        copy, slot = copies[j]
        copy.wait()
        if not transposed:
            dst[d0:d0 + nr, :] = staging[kind][slot, 0:nr, :].astype(BF16)
        elif nr % LANES == 0:
            dst[:, d0:d0 + nr] = staging[kind][slot, 0:nr, :].T.astype(BF16)
        else:
            tile = staging[kind][slot, 0:LANES, :].T
            live = lax.broadcasted_iota(jnp.int32, (1, LANES), 1) < nr
            dst[:, d0:d0 + LANES] = jnp.where(live, tile, 0.0).astype(BF16)


def _staging_scratch():
    return [pltpu.VMEM((LOAD_SLOTS, WIDE_CHUNK_ROWS, D_FF), F32),
            pltpu.VMEM((LOAD_SLOTS, NARROW_CHUNK_ROWS, D_MODEL), F32),
            pltpu.SemaphoreType.DMA((2, LOAD_SLOTS))]


def _chunk_plan(matrices):
    plan, first = [], 0
    for m, mat in enumerate(matrices):
        rows, cols = mat.shape
        assert cols in (D_FF, D_MODEL)
        kind, nr = (0, WIDE_CHUNK_ROWS) if cols == D_FF else (1, NARROW_CHUNK_ROWS)
        assert rows % nr == 0
        plan.append((m, first, rows // nr, nr, kind))
        first += rows // nr
    return plan, first


def _background_cast(step, srcs, dsts, staging, out_staging, sem_in, sem_out):
    plan, _ = _chunk_plan(srcs)
    slot = step % 2

    def rows_of(m, chunk):
        _, first, _, nr, _ = plan[m]
        return pl.ds(pl.multiple_of((chunk - first) * nr, nr), nr)

    def copy_in(m, chunk, slot_):
        kind = plan[m][4]
        return pltpu.make_async_copy(srcs[m].at[rows_of(m, chunk), :], staging[kind].at[slot_],
                                     sem_in.at[kind, slot_])

    def copy_out(m, chunk, slot_):
        kind = plan[m][4]
        return pltpu.make_async_copy(out_staging[kind].at[slot_], dsts[m].at[rows_of(m, chunk), :],
                                     sem_out.at[kind, slot_])

    def for_chunk(chunk, fn):
        for m, first, n, _, _ in plan:
            pl.when(jnp.logical_and(chunk >= first, chunk < first + n))(functools.partial(fn, m))

    def before():
        @pl.when(step == 0)
        def _():
            copy_in(0, 0, 0).start()
        for_chunk(step, lambda m: copy_in(m, step, slot).wait())
        for_chunk(step - 2, lambda m: copy_out(m, step - 2, slot).wait())
        for_chunk(step + 1, lambda m: copy_in(m, step + 1, 1 - slot).start())

    def cast():
        for kind in range(2):
            out_staging[kind][slot] = staging[kind][slot].astype(BF16)

    def after():
        for_chunk(step, lambda m: copy_out(m, step, slot).start())

    return before, cast, after


def _stage_a_body(xp_ref, xs_ref, n1_ref, wg_hbm, wu_hbm, wd_hbm, nm_ref, win_hbm, wgu_ref, bg_ref,
                  wo_hbm, wg2_hbm, wu2_hbm, wd2_hbm,
                  x1_ref, qk_ref, g_ref, v_ref, r_ref, qs_ref, kv_ref, wo_out, wg2_out, wu2_out, wd2_out,
                  act_ref, wg_ref, wu_ref, wd_ref, win_ref, wide_stage, narrow_stage, load_sems,
                  wide_out, narrow_out, sem_in, sem_out):
    i = pl.program_id(0)

    @pl.when(i == 0)
    def _():
        a0, a1 = PROJ_Q_S, PROJ_Q_S + GLA_GATE_RANK
        win_rows = [(0, 0, a0), (a1, a0, D_IN - a1), (a0, PROJ_A, GLA_GATE_RANK)]
        _load_weights_bf16([(wg_hbm, wg_ref, None, False), (wu_hbm, wu_ref, None, False),
                            (wd_hbm, wd_ref, None, False), (win_hbm, win_ref, win_rows, True)],
                           (wide_stage, narrow_stage), load_sems)

    bg_before, bg_cast, bg_after = _background_cast(
        i, (wo_hbm, wg2_hbm, wu2_hbm, wd2_hbm), (wo_out, wg2_out, wu2_out, wd2_out),
        (wide_stage, narrow_stage), (wide_out, narrow_out), sem_in, sem_out)
    bg_before()

    x = jnp.where(i < pl.num_programs(0) - 1, xp_ref[...], xs_ref[...].reshape(ROW_TILE, D_MODEL))
    h = _rms(x, n1_ref[...]).astype(BF16)
    _gate_up(h, wg_ref, wu_ref, act_ref)
    bg_cast()

    halves = [slice(k * ROW_TILE // 2, (k + 1) * ROW_TILE // 2) for k in range(2)]
    x1 = []
    for rows in halves:
        x1.append(x[rows, :] + 0.5 * _down(act_ref, rows, wd_ref))
        x1_ref[rows, :] = x1[-1]
    gate_in = []
    for rows, x1_h in zip(halves, x1):
        h2 = _rms(x1_h, nm_ref[...]).astype(BF16)
        proj = _dot(h2, win_ref[...])
        qk_ref[rows, 0:GLA_KEY_WIDTH] = proj[:, PROJ_Q_G:PROJ_K_G] * HEAD_SCALE
        qk_ref[rows, GLA_KEY_WIDTH:] = proj[:, PROJ_K_G:PROJ_V_G]
        v_ref[rows, :] = proj[:, PROJ_V_G:PROJ_R_G].astype(BF16)
        r_ref[rows, :] = proj[:, PROJ_R_G:PROJ_Q_S]
        qs_ref[rows, :] = (proj[:, PROJ_Q_S:PROJ_KV_S] * HEAD_SCALE).astype(BF16)
        kv_ref[rows, :] = proj[:, PROJ_KV_S:PROJ_A]
        gate_in.append(proj[:, PROJ_A:PROJ_WIDTH].astype(BF16))
    for rows, a in zip(halves, gate_in):
        z = _dot(a, wgu_ref[...]) + bg_ref[...]
        g_ref[rows, :] = jax.nn.log_sigmoid(z) * (1.0 / GLA_GATE_TAU)
    bg_after()


def _resident(shape):
    return pl.BlockSpec(shape, lambda i: (0,) * len(shape), pipeline_mode=pl.Buffered(1))


def _rows(tm, n):
    return pl.BlockSpec((tm, n), lambda i: (i, 0))


def _stage_a(xp, xs, w):
    t = xp.shape[0]
    assert t % ROW_TILE == 0 and xs.shape == SAMPLE_TILE_SHAPE
    nt = t // ROW_TILE
    out_widths = ((D_MODEL, F32), (2 * GLA_KEY_WIDTH, F32), (GLA_KEY_WIDTH, F32), (GLA_WIDTH, BF16),
                  (GLA_WIDTH, F32), (SWA_WIDTH, BF16), (2 * SWA_KV_WIDTH, F32))
    to_cast = (w["wo"], w["wg2"], w["wu2"], w["wd2"])
    assert _chunk_plan(to_cast)[1] + 2 <= nt + 1
    outs = pl.pallas_call(
        _stage_a_body,
        grid=(nt + 1,),
        in_specs=[pl.BlockSpec((ROW_TILE, D_MODEL), lambda i: (jnp.minimum(i, nt - 1), 0)),
                  _resident(SAMPLE_TILE_SHAPE), _resident((1, D_MODEL)),
                  _HBM, _HBM, _HBM,
                  _resident((1, D_MODEL)), _HBM,
                  _resident((LANES, GLA_KEY_WIDTH)), _resident((1, GLA_KEY_WIDTH)),
                  _HBM, _HBM, _HBM, _HBM],
        out_specs=[_rows(ROW_TILE, n) for n, _ in out_widths] + [_HBM] * len(to_cast),
        out_shape=([jax.ShapeDtypeStruct((t + ROW_TILE, n), dt) for n, dt in out_widths]
                   + [jax.ShapeDtypeStruct(m.shape, BF16) for m in to_cast]),
        scratch_shapes=([pltpu.VMEM((ROW_TILE, D_FF), BF16), pltpu.VMEM((D_MODEL, D_FF), BF16),
                         pltpu.VMEM((D_MODEL, D_FF), BF16), pltpu.VMEM((D_FF, D_MODEL), BF16),
                         pltpu.VMEM((D_MODEL, PROJ_WIDTH), BF16)]
                        + _staging_scratch()
                        + [pltpu.VMEM((2, WIDE_CHUNK_ROWS, D_FF), BF16),
                           pltpu.VMEM((2, NARROW_CHUNK_ROWS, D_MODEL), BF16),
                           pltpu.SemaphoreType.DMA((2, 2)), pltpu.SemaphoreType.DMA((2, 2))]),
        compiler_params=pltpu.CompilerParams(dimension_semantics=("arbitrary",),
                                             vmem_limit_bytes=STAGE_A_VMEM_LIMIT),
        name="stage_a_ffn1_proj",
    )(xp, xs, w["n1"], w["wg1"], w["wu1"], w["wd1"], w["nm"], w["win"], w["wgu"], w["bg"], *to_cast)
    return outs[:len(out_widths)], outs[len(out_widths):]


def _cumsum_rows(x):
    n = x.shape[0]
    row = lax.broadcasted_iota(jnp.int32, (n, 1), 0)
    shift = 1
    while shift < n:
        x = x + jnp.where(row >= shift, pltpu.roll(x, shift, 0), 0.0)
        shift *= 2
    return x


def _alternate(a, b):
    pending = [iter(a), iter(b)]
    while pending:
        for it in list(pending):
            try:
                next(it)
            except StopIteration:
                pending.remove(it)
                continue
            yield


def _gla_stages(qk_ref, g_ref, v_ref, r_ref, gn_ref, s_ref, og_scr, cum_scr, inter_scr, flag_ref):
    c_len = GLA_CHUNK
    n_chunks = ROW_TILE // c_len
    causal_cat = (lax.broadcasted_iota(jnp.int32, (c_len, GLA_HEADS * c_len), 0)
                  >= lax.broadcasted_iota(jnp.int32, (c_len, GLA_HEADS * c_len), 1) % c_len)
    lane_head = lax.broadcasted_iota(jnp.int32, (1, GLA_KEY_WIDTH), 1) // GLA_HEAD_K
    row_head = lax.broadcasted_iota(jnp.int32, (GLA_KEY_WIDTH, 1), 0) // GLA_HEAD_K
    eye = (lax.broadcasted_iota(jnp.int32, (GLA_KEY_WIDTH, GLA_KEY_WIDTH), 0)
           == lax.broadcasted_iota(jnp.int32, (GLA_KEY_WIDTH, GLA_KEY_WIDTH), 1))
    zero_b = jnp.zeros((), BF16)
    zero_v = jnp.zeros((c_len, GLA_HEAD_V), BF16)
    worst = []

    def chunk(c):
        rows = slice(c * c_len, (c + 1) * c_len)
        cum = _cumsum_rows(g_ref[rows, :])
        cum_scr[rows, :] = cum
        last = cum[c_len - 1:c_len, :]
        q = qk_ref[rows, 0:GLA_KEY_WIDTH]
        k = qk_ref[rows, GLA_KEY_WIDTH:]
        vb = v_ref[rows, :]
        qe = (q * jnp.exp(cum)).astype(BF16)
        ke = (k * jnp.exp(jnp.minimum(-cum, DECAY_CLAMP))).astype(BF16)
        kl = (k * jnp.exp(last - cum)).astype(BF16)
        state = s_ref[...]
        sb = state.astype(BF16)
        s_bd = jnp.concatenate([jnp.where(row_head == h, sb, zero_b) for h in range(GLA_HEADS)], axis=1)
        o_inter = _dot(qe, s_bd)
        inter_scr[rows, :] = o_inter
        ke_bd = jnp.concatenate([jnp.where(lane_head == h, ke, zero_b) for h in range(GLA_HEADS)], axis=0)
        attn = _dot_tb(qe, ke_bd)
        upds = []
        for p in range(GLA_HEADS // 2):
            u = _dot_ta(kl[:, p * LANES:(p + 1) * LANES], vb[:, 2 * p * GLA_HEAD_V:(2 * p + 2) * GLA_HEAD_V])
            upds.append(u[0:GLA_HEAD_K, 0:GLA_HEAD_V])
            upds.append(u[GLA_HEAD_K:, GLA_HEAD_V:])
        yield
        attn = jnp.where(causal_cat, attn, 0.0).astype(BF16)
        o_pairs = []
        for p in range(GLA_HEADS // 2):
            v_a = vb[:, (2 * p) * GLA_HEAD_V:(2 * p + 1) * GLA_HEAD_V]
            v_b = vb[:, (2 * p + 1) * GLA_HEAD_V:(2 * p + 2) * GLA_HEAD_V]
            v_bd = jnp.concatenate([jnp.concatenate([v_a, zero_v], axis=1),
                                    jnp.concatenate([zero_v, v_b], axis=1)], axis=0)
            o_pairs.append(_dot(attn[:, 2 * p * c_len:(2 * p + 2) * c_len], v_bd))
        og_scr[rows, :] = _head_norm_gate(o_inter + jnp.concatenate(o_pairs, axis=1), r_ref[rows, :],
                                          gn_ref[...])
        last_col = jnp.sum(jnp.where(eye, last, 0.0), axis=1, keepdims=True)
        s_ref[...] = state * jnp.exp(last_col) + jnp.concatenate(upds, axis=0)
        worst.append(jnp.min(last, axis=1, keepdims=True))
        if c == n_chunks - 1:
            tile_min = functools.reduce(jnp.minimum, worst)
            flag_ref[0] = jnp.where(tile_min[0, 0] < -DECAY_CLAMP, 1, 0)
        yield

    return itertools.chain.from_iterable(chunk(c) for c in range(n_chunks))


def _gla_tile_exact(qk_ref, v_ref, r_ref, gn_ref, og_scr, o_scr, cum_scr, inter_scr):
    c_len = GLA_CHUNK
    ind = jnp.where(lax.broadcasted_iota(jnp.int32, (GLA_KEY_WIDTH, LANES), 0) // GLA_HEAD_K
                    == lax.broadcasted_iota(jnp.int32, (GLA_KEY_WIDTH, LANES), 1), 1.0, 0.0).astype(BF16)
    j_idx = lax.broadcasted_iota(jnp.int32, (c_len, 1), 0)
    for c in range(ROW_TILE // c_len):
        r0 = c * c_len
        rows = slice(r0, r0 + c_len)

        def one_row(i, carry, r0=r0, rows=rows):
            ci = cum_scr[pl.ds(r0 + i, 1), :]
            qi = qk_ref[pl.ds(r0 + i, 1), 0:GLA_KEY_WIDTH]
            kk = qk_ref[rows, GLA_KEY_WIDTH:]
            dec = jnp.exp(jnp.minimum(ci - cum_scr[rows, :], 0.0))
            a_cols = _dot(((qi * kk) * dec).astype(BF16), ind)
            outs = []
            for h in range(GLA_HEADS):
                w_col = jnp.where(j_idx <= i, a_cols[:, h:h + 1], 0.0)
                v_h = v_ref[rows, h * GLA_HEAD_V:(h + 1) * GLA_HEAD_V].astype(F32)
                outs.append(jnp.sum(w_col * v_h, axis=0, keepdims=True))
            o_scr[pl.ds(r0 + i, 1), :] = inter_scr[pl.ds(r0 + i, 1), :] + jnp.concatenate(outs, axis=1)
            return carry

        lax.fori_loop(0, c_len, one_row, 0)
    og_scr[...] = _head_norm_gate(o_scr[...], r_ref[...], gn_ref[...])


def _dup_halves(x):
    lo = lax.broadcasted_iota(jnp.int32, (1, LANES), 1) < SWA_HEAD_DIM
    sw = pltpu.roll(x, SWA_HEAD_DIM, 1)
    return jnp.where(lo, x, sw).astype(BF16), jnp.where(lo, sw, x).astype(BF16)


def _swa_stages(sink_ref, qs_ref, kv_ref, kvprev_scr, sn_ref, os_scr, no_prev_bias):
    w = WINDOW
    n_blocks = ROW_TILE // w
    lo = lax.broadcasted_iota(jnp.int32, (1, LANES), 1) < SWA_HEAD_DIM
    tri = (lax.broadcasted_iota(jnp.int32, (w, w), 1) <= lax.broadcasted_iota(jnp.int32, (w, w), 0))
    zero_b = jnp.zeros((), BF16)

    def block(b):
        rows = slice(b * w, (b + 1) * w)
        kv_prev = kvprev_scr[...] if b == 0 else kv_ref[(b - 1) * w:b * w, :]
        kv = jnp.concatenate([kv_prev, kv_ref[rows, :]], axis=0)
        kk = _dup_halves(kv[:, 0:SWA_KV_WIDTH])
        vv = _dup_halves(kv[:, SWA_KV_WIDTH:])
        q = qs_ref[rows, :]
        scores = []
        for grp in range(SWA_KV_HEADS):
            stacked = []
            for j in range(SWA_GROUP):
                hq = grp * SWA_GROUP + j
                tile = q[:, (hq // 2) * LANES:(hq // 2 + 1) * LANES]
                stacked.append(jnp.where(lo if hq % 2 == 0 else ~lo, tile, zero_b))
            scores.append(_dot_tb(jnp.concatenate(stacked, axis=0), kk[grp]))
        yield
        outs, dens = [], []
        for grp in range(SWA_KV_HEADS):
            s = scores[grp]
            probs = []
            for j in range(SWA_GROUP):
                sink = sink_ref[grp * SWA_GROUP + j]
                s_prev = s[j * w:(j + 1) * w, 0:w]
                if b == 0:
                    s_prev = s_prev + no_prev_bias
                sf = jnp.where(tri, s[j * w:(j + 1) * w, w:], s_prev)
                m = jnp.maximum(jnp.max(sf, axis=1, keepdims=True), sink)
                p = jnp.exp(sf - m)
                dens.append(jnp.sum(p, axis=1, keepdims=True) + jnp.exp(sink - m))
                pb = p.astype(BF16)
                probs.append(jnp.concatenate([jnp.where(tri, zero_b, pb), jnp.where(tri, pb, zero_b)], axis=1))
            outs.append(_dot(jnp.concatenate(probs, axis=0), vv[grp]))
        yield
        tiles = []
        for grp in range(SWA_KV_HEADS):
            res = [outs[grp][j * w:(j + 1) * w] / dens[grp * SWA_GROUP + j] for j in range(SWA_GROUP)]
            tiles.append(jnp.where(lo, res[0], res[1]))
            tiles.append(jnp.where(lo, res[2], res[3]))
        os_scr[rows, :] = _rms(jnp.concatenate(tiles, axis=1), sn_ref[...]).astype(BF16)
        if b == n_blocks - 1:
            kvprev_scr[...] = kv_ref[rows, :]
        yield

    return itertools.chain.from_iterable(block(b) for b in range(n_blocks))


def _mix_out_body(sink_ref, qk_ref, g_ref, v_ref, r_ref, qs_ref, kv_ref, x1_ref, ogs_ref, oss_ref,
                  gn_ref, sn_ref, wo_ref, n2_ref, wg_ref, wu_ref, wd_ref, nf_ref, yp_ref, ys_ref, s_ref,
                  og_scr, os_scr, kvprev_scr, o_scr, cum_scr, inter_scr, act_scr, flag_ref):
    i = pl.program_id(0)
    n_prompt = pl.num_programs(0) - 1

    def stage_d(og, osw, y_ref, side_stages=None):
        _out_ffn(x1_ref, og, osw, wo_ref, n2_ref, wg_ref, wu_ref, wd_ref, nf_ref, y_ref, act_scr,
                 side_stages)

    def mixer_stages():
        gla = _gla_stages(qk_ref, g_ref, v_ref, r_ref, gn_ref, s_ref, og_scr, cum_scr, inter_scr, flag_ref)
        swa = _swa_stages(sink_ref, qs_ref, kv_ref, kvprev_scr, sn_ref, os_scr,
                          jnp.where(i > 0, 0.0, -jnp.inf))
        return _alternate(gla, swa)

    @pl.when(i == 0)
    def _():
        s_ref[...] = jnp.zeros_like(s_ref)
        kvprev_scr[...] = jnp.zeros_like(kvprev_scr)
        stage_d(ogs_ref[...], oss_ref[...], ys_ref, mixer_stages())

    @pl.when(jnp.logical_and(i > 0, i < n_prompt))
    def _():
        stage_d(og_scr[...], os_scr[...], yp_ref, mixer_stages())

    @pl.when(i == n_prompt)
    def _():
        stage_d(og_scr[...], os_scr[...], yp_ref)
        flag_ref[0] = 0

    @pl.when(flag_ref[0] != 0)
    def _():
        _gla_tile_exact(qk_ref, v_ref, r_ref, gn_ref, og_scr, o_scr, cum_scr, inter_scr)


def _mix_out(qk, g, v, r, qs, kv, x1, og_s, os_s, w):
    t = qk.shape[0] - ROW_TILE
    assert t % ROW_TILE == 0 and og_s.shape[0] == ROW_TILE
    nt = t // ROW_TILE
    cur = lambda n: pl.BlockSpec((ROW_TILE, n), lambda i: (jnp.minimum(i, nt - 1), 0))
    return pl.pallas_call(
        _mix_out_body,
        grid=(nt + 1,),
        in_specs=[pl.BlockSpec(memory_space=pltpu.SMEM),
                  cur(2 * GLA_KEY_WIDTH), cur(GLA_KEY_WIDTH), cur(GLA_WIDTH), cur(GLA_WIDTH),
                  cur(SWA_WIDTH), cur(2 * SWA_KV_WIDTH),
                  pl.BlockSpec((ROW_TILE, D_MODEL), lambda i: (jnp.where(i == 0, nt, i - 1), 0)),
                  _resident((ROW_TILE, GLA_WIDTH)), _resident((ROW_TILE, SWA_WIDTH)),
                  _resident((1, GLA_WIDTH)), _resident((1, SWA_WIDTH)),
                  _resident((D_MODEL, D_MODEL)), _resident((1, D_MODEL)),
                  _resident((D_MODEL, D_FF)), _resident((D_MODEL, D_FF)), _resident((D_FF, D_MODEL)),
                  _resident((1, D_MODEL))],
        out_specs=[pl.BlockSpec((ROW_TILE, D_MODEL), lambda i: (jnp.clip(i - 1, 0, nt - 1), 0)),
                   pl.BlockSpec(SAMPLE_TILE_SHAPE, lambda i: (0, 0, 0)),
                   pl.BlockSpec((GLA_KEY_WIDTH, GLA_HEAD_V), lambda i: (0, 0))],
        out_shape=[jax.ShapeDtypeStruct((t, D_MODEL), F32),
                   jax.ShapeDtypeStruct(SAMPLE_TILE_SHAPE, F32),
                   jax.ShapeDtypeStruct((GLA_KEY_WIDTH, GLA_HEAD_V), F32)],
        scratch_shapes=[pltpu.VMEM((ROW_TILE, GLA_WIDTH), BF16), pltpu.VMEM((ROW_TILE, SWA_WIDTH), BF16),
                        pltpu.VMEM((WINDOW, 2 * SWA_KV_WIDTH), F32),
                        pltpu.VMEM((ROW_TILE, GLA_WIDTH), F32), pltpu.VMEM((ROW_TILE, GLA_KEY_WIDTH), F32),
                        pltpu.VMEM((ROW_TILE, GLA_WIDTH), F32), pltpu.VMEM((ROW_TILE, D_FF), BF16),
                        pltpu.SMEM((1,), jnp.int32)],
        compiler_params=pltpu.CompilerParams(dimension_semantics=("arbitrary",),
                                             vmem_limit_bytes=STAGE_D_VMEM_LIMIT),
        name="prompt_mixers_stage_d",
    )(w["sinks"], qk, g, v, r, qs, kv, x1, og_s, os_s, w["gn"], w["sn"], w["wo"], w["n2"], w["wg2"],
      w["wu2"], w["wd2"], w["nf"])


def _gla_sample_body(qk_ref, g_ref, v_ref, r_ref, gn_ref, s_ref, og_ref, so_ref, o_scr):
    rows = SAMPLE_ROWS
    g = g_ref[...]
    q = qk_ref[:, 0:GLA_KEY_WIDTH]
    k = qk_ref[:, GLA_KEY_WIDTH:]
    vb = v_ref[...]
    vf = vb.astype(F32)
    tok = lax.broadcasted_iota(jnp.int32, (rows, 1), 0) % DEC_SEQ

    cum = g
    for d in range(1, DEC_SEQ):
        cum = cum + jnp.where(tok >= d, pltpu.roll(g, d, 0), 0.0)
    tot = jnp.where(tok == DEC_SEQ - 1, cum, 0.0)
    for d in range(1, DEC_SEQ):
        tot = tot + jnp.where(tok == DEC_SEQ - 1 - d, pltpu.roll(cum, rows - d, 0), 0.0)

    qe = q * jnp.exp(cum)
    kl = k * jnp.exp(tot - cum)
    decay_t = jnp.exp(tot).T

    ind = jnp.where(lax.broadcasted_iota(jnp.int32, (GLA_KEY_WIDTH, LANES), 0) // GLA_HEAD_K
                    == lax.broadcasted_iota(jnp.int32, (GLA_KEY_WIDTH, LANES), 1), 1.0, 0.0).astype(BF16)
    expand = jnp.where(lax.broadcasted_iota(jnp.int32, (LANES, GLA_WIDTH), 0)
                       == lax.broadcasted_iota(jnp.int32, (LANES, GLA_WIDTH), 1) // GLA_HEAD_V,
                       1.0, 0.0).astype(BF16)

    o_intra = jnp.zeros((rows, GLA_WIDTH), F32)
    for d in range(DEC_SEQ):
        k_d = k if d == 0 else pltpu.roll(k, d, 0)
        c_d = cum if d == 0 else pltpu.roll(cum, d, 0)
        v_d = vf if d == 0 else pltpu.roll(vf, d, 0)
        pair = jnp.where(tok >= d, q * k_d * jnp.exp(jnp.minimum(cum - c_d, 0.0)), 0.0)
        a = _dot(pair.astype(BF16), ind)
        o_intra = o_intra + _dot(a.astype(BF16), expand) * v_d

    lane_head = lax.broadcasted_iota(jnp.int32, (1, GLA_KEY_WIDTH), 1) // GLA_HEAD_K
    row8 = lax.broadcasted_iota(jnp.int32, (SUBLANES, 1), 0)
    row32 = lax.broadcasted_iota(jnp.int32, (GLA_HEADS * SUBLANES, 1), 0)
    for pair_idx in range(SAMPLE_SEQS // 2):
        r8 = slice(pair_idx * SUBLANES, (pair_idx + 1) * SUBLANES)
        q8 = qe[r8, :]
        lhs = jnp.concatenate([jnp.where(lane_head == h, q8, 0.0) for h in range(GLA_HEADS)],
                              axis=0).astype(BF16)
        kl8 = kl[r8, :]
        v8 = vb[r8, :]
        res = []
        for s in range(2):
            b = 2 * pair_idx + s
            state = s_ref[b]
            res.append(_dot(lhs, state.astype(BF16)))
            kl_b = jnp.where(row8 // DEC_SEQ == s, kl8, 0.0).astype(BF16)
            upd = _dot_ta(kl_b, v8)
            upd = jnp.concatenate(
                [upd[h * GLA_HEAD_K:(h + 1) * GLA_HEAD_K, h * GLA_HEAD_V:(h + 1) * GLA_HEAD_V]
                 for h in range(GLA_HEADS)], axis=0)
            so_ref[b] = state * decay_t[:, DEC_SEQ * b:DEC_SEQ * b + 1] + upd
        sel = jnp.where(row32 % SUBLANES < DEC_SEQ, res[0], res[1])
        o_scr[r8, :] = jnp.concatenate([sel[h * SUBLANES:(h + 1) * SUBLANES] for h in range(GLA_HEADS)], axis=1)

    og_ref[...] = _head_norm_gate(o_scr[...] + o_intra, r_ref[...], gn_ref[...])


def _gla_sample(qk, g, v, r, gn, state, row0):
    nseq = state.shape[0]
    rows = nseq * DEC_SEQ
    assert nseq % SAMPLE_SEQS == 0 and row0 % SAMPLE_ROWS == 0
    blk0 = row0 // SAMPLE_ROWS
    src = lambda n: pl.BlockSpec((SAMPLE_ROWS, n), lambda i: (i + blk0, 0))
    blk = lambda n: pl.BlockSpec((SAMPLE_ROWS, n), lambda i: (i, 0))
    sblk = pl.BlockSpec((SAMPLE_SEQS, GLA_KEY_WIDTH, GLA_HEAD_V), lambda i: (i, 0, 0))
    return pl.pallas_call(
        _gla_sample_body,
        grid=(nseq // SAMPLE_SEQS,),
        in_specs=[src(2 * GLA_KEY_WIDTH), src(GLA_KEY_WIDTH), src(GLA_WIDTH), src(GLA_WIDTH),
                  pl.BlockSpec((1, GLA_WIDTH), lambda i: (0, 0)), sblk],
        out_specs=[blk(GLA_WIDTH), sblk],
        out_shape=[jax.ShapeDtypeStruct((rows, GLA_WIDTH), BF16),
                   jax.ShapeDtypeStruct((nseq, GLA_KEY_WIDTH, GLA_HEAD_V), F32)],
        scratch_shapes=[pltpu.VMEM((SAMPLE_ROWS, GLA_WIDTH), F32)],
        compiler_params=pltpu.CompilerParams(dimension_semantics=("arbitrary",)),
        name="gla_sample",
    )(qk, g, v, r, gn, state)


def _swa_sample_body(sink_ref, q_ref, kvn_ref, kc_ref, vc_ref, nrm_ref, o_ref, kco_ref, vco_ref,
                     lhs_scr, sc_scr, pc_scr, oc_scr):
    rows = SAMPLE_ROWS
    q = q_ref[...].astype(F32)
    lo = lax.broadcasted_iota(jnp.int32, (1, LANES), 1) < SWA_HEAD_DIM
    for hq in range(SWA_Q_HEADS):
        grp = hq // SWA_GROUP
        tile = q[:, (hq // 2) * LANES:(hq // 2 + 1) * LANES]
        src = tile if hq % 2 == grp else pltpu.roll(tile, SWA_HEAD_DIM, 1)
        lhs_scr[hq * rows:(hq + 1) * rows, :] = jnp.where(lo if grp == 0 else ~lo, src, 0.0)

    kvn = kvn_ref[...]
    k_new = kvn[:, 0:SWA_KV_WIDTH]
    v_new = kvn[:, SWA_KV_WIDTH:]
    s_new = _dot_tb(lhs_scr[...].astype(BF16), k_new.astype(BF16))

    row64 = lax.broadcasted_iota(jnp.int32, (SWA_Q_HEADS * SUBLANES, 1), 0)
    first_of_pair = row64 % SUBLANES < DEC_SEQ

    def gather_pair(ref, pair_idx):
        return jnp.concatenate(
            [ref[hq * rows + pair_idx * SUBLANES:hq * rows + (pair_idx + 1) * SUBLANES, :]
             for hq in range(SWA_Q_HEADS)], axis=0).astype(BF16)

    def scatter_pair(ref, pair_idx, val):
        for hq in range(SWA_Q_HEADS):
            ref[hq * rows + pair_idx * SUBLANES:hq * rows + (pair_idx + 1) * SUBLANES, :] = (
                val[hq * SUBLANES:(hq + 1) * SUBLANES])

    for pair_idx in range(SAMPLE_SEQS // 2):
        l64 = gather_pair(lhs_scr, pair_idx)
        sa = _dot(l64, kc_ref[2 * pair_idx].astype(BF16))
        sb = _dot(l64, kc_ref[2 * pair_idx + 1].astype(BF16))
        scatter_pair(sc_scr, pair_idx, jnp.where(first_of_pair, sa, sb))

    rr = lax.broadcasted_iota(jnp.int32, (rows, rows), 0)
    cc = lax.broadcasted_iota(jnp.int32, (rows, rows), 1)
    tok = rr % DEC_SEQ
    mask_cache = cc > tok
    mask_new = (cc // DEC_SEQ == rr // DEC_SEQ) & (cc % DEC_SEQ <= tok)
    p_new, dens = [], []
    for hq in range(SWA_Q_HEADS):
        sl = slice(hq * rows, (hq + 1) * rows)
        sink = sink_ref[hq]
        s_c = jnp.where(mask_cache, sc_scr[sl, :], -jnp.inf)
        s_n = jnp.where(mask_new, s_new[sl, :], -jnp.inf)
        m = jnp.maximum(jnp.maximum(jnp.max(s_c, axis=1, keepdims=True),
                                    jnp.max(s_n, axis=1, keepdims=True)), sink)
        p_c = jnp.exp(s_c - m)
        p_n = jnp.exp(s_n - m)
        dens.append(jnp.sum(p_c, axis=1, keepdims=True) + jnp.sum(p_n, axis=1, keepdims=True)
                    + jnp.exp(sink - m))
        pc_scr[sl, :] = p_c
        p_new.append(p_n.astype(BF16))
    o_new = _dot(jnp.concatenate(p_new, axis=0), v_new.astype(BF16))

    for pair_idx in range(SAMPLE_SEQS // 2):
        p64 = gather_pair(pc_scr, pair_idx)
        oa = _dot_tb(p64, vc_ref[2 * pair_idx].astype(BF16))
        ob = _dot_tb(p64, vc_ref[2 * pair_idx + 1].astype(BF16))
        scatter_pair(oc_scr, pair_idx, jnp.where(first_of_pair, oa, ob))

    tiles = []
    for i in range(SWA_Q_HEADS // 2):
        halves = []
        for hq in (2 * i, 2 * i + 1):
            sl = slice(hq * rows, (hq + 1) * rows)
            oh = (oc_scr[sl, :] + o_new[sl, :]) / dens[hq]
            halves.append(oh if hq % 2 == hq // SWA_GROUP else pltpu.roll(oh, SWA_HEAD_DIM, 1))
        tiles.append(jnp.where(lo, halves[0], halves[1]))
    o_ref[...] = _rms(jnp.concatenate(tiles, axis=1), nrm_ref[...]).astype(BF16)

    keep = lax.broadcasted_iota(jnp.int32, (1, WINDOW), 1) < WINDOW - DEC_SEQ
    k_new_t = k_new.T
    v_new_t = v_new.T
    for b in range(SAMPLE_SEQS):
        to_tail = (WINDOW - DEC_SEQ - DEC_SEQ * b) % rows
        kco_ref[b] = jnp.where(keep, pltpu.roll(kc_ref[b], WINDOW - DEC_SEQ, 1),
                               pltpu.roll(k_new_t, to_tail, 1))
        vco_ref[b] = jnp.where(keep, pltpu.roll(vc_ref[b], WINDOW - DEC_SEQ, 1),
                               pltpu.roll(v_new_t, to_tail, 1))


def _swa_sample(sinks, qs, kvn, kc, vc, nrm, row0):
    nseq = kc.shape[0]
    rows = nseq * DEC_SEQ
    assert nseq % SAMPLE_SEQS == 0 and row0 % SAMPLE_ROWS == 0
    assert kc.shape[1:] == (SWA_KV_WIDTH, WINDOW) and WINDOW == SAMPLE_ROWS
    blk0 = row0 // SAMPLE_ROWS
    cblk = pl.BlockSpec((SAMPLE_SEQS, SWA_KV_WIDTH, WINDOW), lambda i: (i, 0, 0))
    big = pltpu.VMEM((SWA_Q_HEADS * SAMPLE_ROWS, LANES), F32)
    return pl.pallas_call(
        _swa_sample_body,
        grid=(nseq // SAMPLE_SEQS,),
        in_specs=[pl.BlockSpec(memory_space=pltpu.SMEM),
                  pl.BlockSpec((SAMPLE_ROWS, SWA_WIDTH), lambda i: (i + blk0, 0)),
                  pl.BlockSpec((SAMPLE_ROWS, 2 * SWA_KV_WIDTH), lambda i: (i + blk0, 0)),
                  cblk, cblk,
                  pl.BlockSpec((1, SWA_WIDTH), lambda i: (0, 0))],
        out_specs=[pl.BlockSpec((SAMPLE_ROWS, SWA_WIDTH), lambda i: (i, 0)), cblk, cblk],
        out_shape=[jax.ShapeDtypeStruct((rows, SWA_WIDTH), BF16),
                   jax.ShapeDtypeStruct(kc.shape, F32), jax.ShapeDtypeStruct(vc.shape, F32)],
        scratch_shapes=[big, big, big, big],
        compiler_params=pltpu.CompilerParams(dimension_semantics=("arbitrary",)),
        name="swa_sample",
    )(sinks, qs, kvn, kc, vc, nrm)


def _prepare_weights(ffn1_norm, ffn1_w_gate, ffn1_w_up, ffn1_w_down, mix_norm, w_in, w_gate_up, b_gate,
                     gla_head_norm, swa_out_norm, swa_sinks, w_out,
                     ffn2_norm, ffn2_w_gate, ffn2_w_up, ffn2_w_down, final_norm, layer):
    assert w_in.shape[1:] == (D_MODEL, D_IN) and PROJ_Q_S == 2 * GLA_KEY_WIDTH + 2 * GLA_WIDTH
    wgu = jnp.concatenate([w_gate_up[layer],
                           jnp.zeros((LANES - GLA_GATE_RANK, GLA_KEY_WIDTH), w_gate_up.dtype)], axis=0)
    row = lambda a: a.reshape(1, -1).astype(F32)
    return dict(
        n1=row(ffn1_norm[layer]), wg1=ffn1_w_gate[layer], wu1=ffn1_w_up[layer], wd1=ffn1_w_down[layer],
        nm=row(mix_norm[layer]), win=w_in[layer].T, wgu=wgu.astype(BF16), bg=row(b_gate[layer]),
        gn=row(gla_head_norm[layer]), sn=row(swa_out_norm[layer]), sinks=swa_sinks[layer].astype(F32),
        wo=w_out[layer], n2=row(ffn2_norm[layer]), wg2=ffn2_w_gate[layer], wu2=ffn2_w_up[layer],
        wd2=ffn2_w_down[layer], nf=row(final_norm))


def kernel(x_prompt, x_sample, state_gla, cache_swa_k, cache_swa_v, ffn1_norm, ffn1_w_gate, ffn1_w_up,
           ffn1_w_down, mix_norm, w_in, w_gate_up, b_gate, gla_head_norm, swa_out_norm, swa_sinks, w_out,
           ffn2_norm, ffn2_w_gate, ffn2_w_up, ffn2_w_down, final_norm):
    depth = state_gla.shape[0]
    assert depth == 1 and x_prompt.shape[0] == 1 and x_sample.shape[1] == DEC_SEQ
    seq = x_prompt.shape[1]
    nseq = x_sample.shape[0]
    w = _prepare_weights(ffn1_norm, ffn1_w_gate, ffn1_w_up, ffn1_w_down, mix_norm, w_in, w_gate_up, b_gate,
                         gla_head_norm, swa_out_norm, swa_sinks, w_out,
                         ffn2_norm, ffn2_w_gate, ffn2_w_up, ffn2_w_down, final_norm, 0)

    (x1, qk, g, v, r, qs, kv), (wo, wg2, wu2, wd2) = _stage_a(
        x_prompt.reshape(seq, D_MODEL), x_sample, w)
    w = dict(w, wo=wo, wg2=wg2, wu2=wu2, wd2=wd2)

    og_s, state_s = _gla_sample(qk, g, v, r, w["gn"],
                                state_gla[0].reshape(nseq, GLA_KEY_WIDTH, GLA_HEAD_V), seq)
    cache_w = cache_swa_k.shape[2]
    to_feature_major = lambda c: jnp.transpose(c[0], (0, 2, 3, 1)).reshape(nseq, SWA_KV_WIDTH, cache_w)
    from_feature_major = lambda c: jnp.transpose(
        c.reshape(nseq, SWA_KV_HEADS, SWA_HEAD_DIM, cache_w), (0, 3, 1, 2))[None]
    os_s, k_cache_s, v_cache_s = _swa_sample(
        w["sinks"], qs, kv, to_feature_major(cache_swa_k), to_feature_major(cache_swa_v), w["sn"], seq)
    k_cache_s = from_feature_major(k_cache_s)
    v_cache_s = from_feature_major(v_cache_s)

    y_prompt, y_sample, state_p = _mix_out(qk, g, v, r, qs, kv, x1, og_s, os_s, w)
    y_prompt = y_prompt.reshape(1, seq, D_MODEL)
    cw = min(WINDOW, seq)
    k_cache_p = kv[seq - cw:seq, 0:SWA_KV_WIDTH].reshape(1, 1, cw, SWA_KV_HEADS, SWA_HEAD_DIM)
    v_cache_p = kv[seq - cw:seq, SWA_KV_WIDTH:].reshape(1, 1, cw, SWA_KV_HEADS, SWA_HEAD_DIM)
    state_p = state_p.reshape(1, 1, GLA_HEADS, GLA_HEAD_K, GLA_HEAD_V)
    state_s = state_s.reshape(1, nseq, GLA_HEADS, GLA_HEAD_K, GLA_HEAD_V)

    return (y_prompt, y_sample, state_p, k_cache_p, v_cache_p, state_s, k_cache_s, v_cache_s)
```

```python
import functools
import itertools

import jax
import jax.numpy as jnp
from jax import lax
from jax.experimental import pallas as pl
from jax.experimental.pallas import tpu as pltpu

F32 = jnp.float32
BF16 = jnp.bfloat16

D_MODEL = 1024
D_FF = 2816
GLA_HEADS = 4
GLA_HEAD_K = 64
GLA_HEAD_V = 128
GLA_KEY_WIDTH = GLA_HEADS * GLA_HEAD_K
GLA_WIDTH = GLA_HEADS * GLA_HEAD_V
GLA_GATE_RANK = 16
GLA_GATE_TAU = 16.0
SWA_HEAD_DIM = 64
SWA_Q_HEADS = 8
SWA_KV_HEADS = 2
SWA_GROUP = SWA_Q_HEADS // SWA_KV_HEADS
SWA_WIDTH = SWA_Q_HEADS * SWA_HEAD_DIM
SWA_KV_WIDTH = SWA_KV_HEADS * SWA_HEAD_DIM
WINDOW = 128
DEC_SEQ = 4
NORM_EPS = 1e-6
HEAD_SCALE = 0.125

LANES = 128
SUBLANES = 8
VMEM_CAPACITY_BYTES = 64 * 1024 * 1024
STAGE_A_VMEM_LIMIT = VMEM_CAPACITY_BYTES - 6 * 1024 * 1024
STAGE_D_VMEM_LIMIT = VMEM_CAPACITY_BYTES - 8 * 1024 * 1024

ROW_TILE = 512
FF_CHUNK = 256
DOWN_CHUNK = 256
GLA_CHUNK = 128
SAMPLE_SEQS = 32
SAMPLE_ROWS = SAMPLE_SEQS * DEC_SEQ
SAMPLE_TILE_SHAPE = (ROW_TILE // DEC_SEQ, DEC_SEQ, D_MODEL)
MIXER_STAGES = 2 * (ROW_TILE // GLA_CHUNK) + 3 * (ROW_TILE // WINDOW)
SIDE_STAGES_AT_NORM = 0
SIDE_STAGES_AFTER_FFN = 0
DECAY_CLAMP = 60.0

PROJ_Q_G = 0
PROJ_K_G = PROJ_Q_G + GLA_KEY_WIDTH
PROJ_V_G = PROJ_K_G + GLA_KEY_WIDTH
PROJ_R_G = PROJ_V_G + GLA_WIDTH
PROJ_Q_S = PROJ_R_G + GLA_WIDTH
PROJ_KV_S = PROJ_Q_S + SWA_WIDTH
PROJ_A = PROJ_KV_S + 2 * SWA_KV_WIDTH
PROJ_WIDTH = PROJ_A + LANES
D_IN = PROJ_A + GLA_GATE_RANK


def _dot(a, b):
    return jnp.dot(a, b, preferred_element_type=F32)


def _dot_tb(a, b):
    return lax.dot_general(a, b, (((1,), (1,)), ((), ())), preferred_element_type=F32)


def _dot_ta(a, b):
    return lax.dot_general(a, b, (((0,), (0,)), ((), ())), preferred_element_type=F32)


def _rms(x, g):
    return x * lax.rsqrt(jnp.mean(x * x, axis=-1, keepdims=True) + NORM_EPS) * g


def _gate_up(h, wg_ref, wu_ref, act_ref, after_dot=lambda: None):
    for c0 in range(0, D_FF, FF_CHUNK):
        g = _dot(h, wg_ref[:, c0:c0 + FF_CHUNK])
        after_dot()
        u = _dot(h, wu_ref[:, c0:c0 + FF_CHUNK])
        after_dot()
        act_ref[:, c0:c0 + FF_CHUNK] = (g * jax.nn.sigmoid(g) * u).astype(BF16)


def _down(act_ref, rows, wd_ref, after_dot=lambda: None):
    outs = []
    for n0 in range(0, D_MODEL, DOWN_CHUNK):
        outs.append(_dot(act_ref[rows, :], wd_ref[:, n0:n0 + DOWN_CHUNK]))
        after_dot()
    return jnp.concatenate(outs, axis=1)


def _head_norm_gate(o, r, gn):
    parts = []
    for h in range(GLA_HEADS):
        sl = slice(h * GLA_HEAD_V, (h + 1) * GLA_HEAD_V)
        parts.append(_rms(o[:, sl], gn[:, sl]))
    return (jnp.concatenate(parts, axis=1) * (r * jax.nn.sigmoid(r))).astype(BF16)


def _out_ffn(x1_ref, og, osw, wo_ref, n2_ref, wg_ref, wu_ref, wd_ref, nf_ref, y_ref, act_ref, side_stages=None):
    n_inside = 0 if side_stages is None else MIXER_STAGES - SIDE_STAGES_AT_NORM - SIDE_STAGES_AFTER_FFN
    side_stages = iter(()) if side_stages is None else side_stages
    n_dots = 2 * (D_FF // FF_CHUNK) + D_MODEL // DOWN_CHUNK
    done = [0, 0]

    def after_dot():
        done[0] += 1
        while done[1] * n_dots < done[0] * n_inside:
            next(side_stages, None)
            done[1] += 1

    mixed = jnp.concatenate([og, osw], axis=1)
    x2 = x1_ref[...] + _dot(mixed, wo_ref[...])
    for _ in range(SIDE_STAGES_AT_NORM):
        next(side_stages, None)
    h = _rms(x2, n2_ref[...]).astype(BF16)
    _gate_up(h, wg_ref, wu_ref, act_ref, after_dot)
    x3 = x2 + 0.5 * _down(act_ref, slice(None), wd_ref, after_dot)
    for _ in side_stages:
        pass
    y_ref[...] = _rms(x3, nf_ref[...]).reshape(y_ref.shape)


_HBM = pl.BlockSpec(memory_space=pl.ANY)
WIDE_CHUNK_ROWS = 128
NARROW_CHUNK_ROWS = 256
LOAD_SLOTS = 4


def _load_weights_bf16(loads, staging, sems):
    jobs = []
    for src, dst, segments, transposed in loads:
        cols = src.shape[1]
        assert cols in (D_FF, D_MODEL) and dst.shape[0 if transposed else 1] == cols
        kind, step = (0, WIDE_CHUNK_ROWS) if cols == D_FF else (1, NARROW_CHUNK_ROWS)
        for s0, d0, n in segments or [(0, 0, src.shape[0])]:
            jobs += [(src, dst, s0 + r, d0 + r, min(step, n - r), kind, transposed)
                     for r in range(0, n, step)]

    used = [0, 0]
    copies = []
    for src, _, s0, _, nr, kind, _ in jobs:
        slot = used[kind] % LOAD_SLOTS
        used[kind] += 1
        copies.append((pltpu.make_async_copy(src.at[pl.ds(s0, nr), :], staging[kind].at[slot, pl.ds(0, nr), :],
                                             sems.at[kind, slot]), slot))
    for copy, _ in copies[:LOAD_SLOTS - 1]:
        copy.start()
    for j, (_, dst, _, d0, nr, kind, transposed) in enumerate(jobs):
        ahead = j + LOAD_SLOTS - 1
        if ahead < len(jobs):
            copies[ahead][0].start()
        copy, slot = copies[j]
        copy.wait()
        if not transposed:
            dst[d0:d0 + nr, :] = staging[kind][slot, 0:nr, :].astype(BF16)
        elif nr % LANES == 0:
            dst[:, d0:d0 + nr] = staging[kind][slot, 0:nr, :].T.astype(BF16)
        else:
            tile = staging[kind][slot, 0:LANES, :].T
            live = lax.broadcasted_iota(jnp.int32, (1, LANES), 1) < nr
            dst[:, d0:d0 + LANES] = jnp.where(live, tile, 0.0).astype(BF16)


def _staging_scratch():
    return [pltpu.VMEM((LOAD_SLOTS, WIDE_CHUNK_ROWS, D_FF), F32),
            pltpu.VMEM((LOAD_SLOTS, NARROW_CHUNK_ROWS, D_MODEL), F32),
            pltpu.SemaphoreType.DMA((2, LOAD_SLOTS))]


def _chunk_plan(matrices):
    plan, first = [], 0
    for m, mat in enumerate(matrices):
        rows, cols = mat.shape
        assert cols in (D_FF, D_MODEL)
        kind, nr = (0, WIDE_CHUNK_ROWS) if cols == D_FF else (1, NARROW_CHUNK_ROWS)
        assert rows % nr == 0
        plan.append((m, first, rows // nr, nr, kind))
        first += rows // nr
    return plan, first


def _background_cast(step, srcs, dsts, staging, out_staging, sem_in, sem_out):
    plan, _ = _chunk_plan(srcs)
    slot = step % 2

    def rows_of(m, chunk):
        _, first, _, nr, _ = plan[m]
        return pl.ds(pl.multiple_of((chunk - first) * nr, nr), nr)

    def copy_in(m, chunk, slot_):
        kind = plan[m][4]
        return pltpu.make_async_copy(srcs[m].at[rows_of(m, chunk), :], staging[kind].at[slot_],
                                     sem_in.at[kind, slot_])

    def copy_out(m, chunk, slot_):
        kind = plan[m][4]
        return pltpu.make_async_copy(out_staging[kind].at[slot_], dsts[m].at[rows_of(m, chunk), :],
                                     sem_out.at[kind, slot_])

    def for_chunk(chunk, fn):
        for m, first, n, _, _ in plan:
            pl.when(jnp.logical_and(chunk >= first, chunk < first + n))(functools.partial(fn, m))

    def before():
        @pl.when(step == 0)
        def _():
            copy_in(0, 0, 0).start()
        for_chunk(step, lambda m: copy_in(m, step, slot).wait())
        for_chunk(step - 2, lambda m: copy_out(m, step - 2, slot).wait())
        for_chunk(step + 1, lambda m: copy_in(m, step + 1, 1 - slot).start())

    def cast():
        for kind in range(2):
            out_staging[kind][slot] = staging[kind][slot].astype(BF16)

    def after():
        for_chunk(step, lambda m: copy_out(m, step, slot).start())

    return before, cast, after


def _stage_a_body(xp_ref, xs_ref, n1_ref, wg_hbm, wu_hbm, wd_hbm, nm_ref, win_hbm, wgu_ref, bg_ref,
                  wo_hbm, wg2_hbm, wu2_hbm, wd2_hbm,
                  x1_ref, qk_ref, g_ref, v_ref, r_ref, qs_ref, kv_ref, wo_out, wg2_out, wu2_out, wd2_out,
                  act_ref, wg_ref, wu_ref, wd_ref, win_ref, wide_stage, narrow_stage, load_sems,
                  wide_out, narrow_out, sem_in, sem_out):
    i = pl.program_id(0)

    @pl.when(i == 0)
    def _():
        a0, a1 = PROJ_Q_S, PROJ_Q_S + GLA_GATE_RANK
        win_rows = [(0, 0, a0), (a1, a0, D_IN - a1), (a0, PROJ_A, GLA_GATE_RANK)]
        _load_weights_bf16([(wg_hbm, wg_ref, None, False), (wu_hbm, wu_ref, None, False),
                            (wd_hbm, wd_ref, None, False), (win_hbm, win_ref, win_rows, True)],
                           (wide_stage, narrow_stage), load_sems)

    bg_before, bg_cast, bg_after = _background_cast(
        i, (wo_hbm, wg2_hbm, wu2_hbm, wd2_hbm), (wo_out, wg2_out, wu2_out, wd2_out),
        (wide_stage, narrow_stage), (wide_out, narrow_out), sem_in, sem_out)
    bg_before()

    x = jnp.where(i < pl.num_programs(0) - 1, xp_ref[...], xs_ref[...].reshape(ROW_TILE, D_MODEL))
    h = _rms(x, n1_ref[...]).astype(BF16)
    _gate_up(h, wg_ref, wu_ref, act_ref)
    bg_cast()

    halves = [slice(k * ROW_TILE // 2, (k + 1) * ROW_TILE // 2) for k in range(2)]
    x1 = []
    for rows in halves:
        x1.append(x[rows, :] + 0.5 * _down(act_ref, rows, wd_ref))
        x1_ref[rows, :] = x1[-1]
    gate_in = []
    for rows, x1_h in zip(halves, x1):
        h2 = _rms(x1_h, nm_ref[...]).astype(BF16)
        proj = _dot(h2, win_ref[...])
        qk_ref[rows, 0:GLA_KEY_WIDTH] = proj[:, PROJ_Q_G:PROJ_K_G] * HEAD_SCALE
        qk_ref[rows, GLA_KEY_WIDTH:] = proj[:, PROJ_K_G:PROJ_V_G]
        v_ref[rows, :] = proj[:, PROJ_V_G:PROJ_R_G].astype(BF16)
        r_ref[rows, :] = proj[:, PROJ_R_G:PROJ_Q_S]
        qs_ref[rows, :] = (proj[:, PROJ_Q_S:PROJ_KV_S] * HEAD_SCALE).astype(BF16)
        kv_ref[rows, :] = proj[:, PROJ_KV_S:PROJ_A]
        gate_in.append(proj[:, PROJ_A:PROJ_WIDTH].astype(BF16))
    for rows, a in zip(halves, gate_in):
        z = _dot(a, wgu_ref[...]) + bg_ref[...]
        g_ref[rows, :] = jax.nn.log_sigmoid(z) * (1.0 / GLA_GATE_TAU)
    bg_after()


def _resident(shape):
    return pl.BlockSpec(shape, lambda i: (0,) * len(shape), pipeline_mode=pl.Buffered(1))


def _rows(tm, n):
    return pl.BlockSpec((tm, n), lambda i: (i, 0))


def _stage_a(xp, xs, w):
    t = xp.shape[0]
    assert t % ROW_TILE == 0 and xs.shape == SAMPLE_TILE_SHAPE
    nt = t // ROW_TILE
    out_widths = ((D_MODEL, F32), (2 * GLA_KEY_WIDTH, F32), (GLA_KEY_WIDTH, F32), (GLA_WIDTH, BF16),
                  (GLA_WIDTH, F32), (SWA_WIDTH, BF16), (2 * SWA_KV_WIDTH, F32))
    to_cast = (w["wo"], w["wg2"], w["wu2"], w["wd2"])
    assert _chunk_plan(to_cast)[1] + 2 <= nt + 1
    outs = pl.pallas_call(
        _stage_a_body,
        grid=(nt + 1,),
        in_specs=[pl.BlockSpec((ROW_TILE, D_MODEL), lambda i: (jnp.minimum(i, nt - 1), 0)),
                  _resident(SAMPLE_TILE_SHAPE), _resident((1, D_MODEL)),
                  _HBM, _HBM, _HBM,
                  _resident((1, D_MODEL)), _HBM,
                  _resident((LANES, GLA_KEY_WIDTH)), _resident((1, GLA_KEY_WIDTH)),
                  _HBM, _HBM, _HBM, _HBM],
        out_specs=[_rows(ROW_TILE, n) for n, _ in out_widths] + [_HBM] * len(to_cast),
        out_shape=([jax.ShapeDtypeStruct((t + ROW_TILE, n), dt) for n, dt in out_widths]
                   + [jax.ShapeDtypeStruct(m.shape, BF16) for m in to_cast]),
        scratch_shapes=([pltpu.VMEM((ROW_TILE, D_FF), BF16), pltpu.VMEM((D_MODEL, D_FF), BF16),
                         pltpu.VMEM((D_MODEL, D_FF), BF16), pltpu.VMEM((D_FF, D_MODEL), BF16),
                         pltpu.VMEM((D_MODEL, PROJ_WIDTH), BF16)]
                        + _staging_scratch()
                        + [pltpu.VMEM((2, WIDE_CHUNK_ROWS, D_FF), BF16),
                           pltpu.VMEM((2, NARROW_CHUNK_ROWS, D_MODEL), BF16),
                           pltpu.SemaphoreType.DMA((2, 2)), pltpu.SemaphoreType.DMA((2, 2))]),
        compiler_params=pltpu.CompilerParams(dimension_semantics=("arbitrary",),
                                             vmem_limit_bytes=STAGE_A_VMEM_LIMIT),
        name="stage_a_ffn1_proj",
    )(xp, xs, w["n1"], w["wg1"], w["wu1"], w["wd1"], w["nm"], w["win"], w["wgu"], w["bg"], *to_cast)
    return outs[:len(out_widths)], outs[len(out_widths):]


def _cumsum_rows(x):
    n = x.shape[0]
    row = lax.broadcasted_iota(jnp.int32, (n, 1), 0)
    shift = 1
    while shift < n:
        x = x + jnp.where(row >= shift, pltpu.roll(x, shift, 0), 0.0)
        shift *= 2
    return x


def _alternate(a, b):
    pending = [iter(a), iter(b)]
    while pending:
        for it in list(pending):
            try:
                next(it)
            except StopIteration:
                pending.remove(it)
                continue
            yield


def _gla_stages(qk_ref, g_ref, v_ref, r_ref, gn_ref, s_ref, og_scr, cum_scr, inter_scr, flag_ref):
    c_len = GLA_CHUNK
    n_chunks = ROW_TILE // c_len
    causal_cat = (lax.broadcasted_iota(jnp.int32, (c_len, GLA_HEADS * c_len), 0)
                  >= lax.broadcasted_iota(jnp.int32, (c_len, GLA_HEADS * c_len), 1) % c_len)
    lane_head = lax.broadcasted_iota(jnp.int32, (1, GLA_KEY_WIDTH), 1) // GLA_HEAD_K
    row_head = lax.broadcasted_iota(jnp.int32, (GLA_KEY_WIDTH, 1), 0) // GLA_HEAD_K
    eye = (lax.broadcasted_iota(jnp.int32, (GLA_KEY_WIDTH, GLA_KEY_WIDTH), 0)
           == lax.broadcasted_iota(jnp.int32, (GLA_KEY_WIDTH, GLA_KEY_WIDTH), 1))
    zero_b = jnp.zeros((), BF16)
    zero_v = jnp.zeros((c_len, GLA_HEAD_V), BF16)
    worst = []

    def chunk(c):
        rows = slice(c * c_len, (c + 1) * c_len)
        cum = _cumsum_rows(g_ref[rows, :])
        cum_scr[rows, :] = cum
        last = cum[c_len - 1:c_len, :]
        q = qk_ref[rows, 0:GLA_KEY_WIDTH]
        k = qk_ref[rows, GLA_KEY_WIDTH:]
        vb = v_ref[rows, :]
        qe = (q * jnp.exp(cum)).astype(BF16)
        ke = (k * jnp.exp(jnp.minimum(-cum, DECAY_CLAMP))).astype(BF16)
        kl = (k * jnp.exp(last - cum)).astype(BF16)
        state = s_ref[...]
        sb = state.astype(BF16)
        s_bd = jnp.concatenate([jnp.where(row_head == h, sb, zero_b) for h in range(GLA_HEADS)], axis=1)
        o_inter = _dot(qe, s_bd)
        inter_scr[rows, :] = o_inter
        ke_bd = jnp.concatenate([jnp.where(lane_head == h, ke, zero_b) for h in range(GLA_HEADS)], axis=0)
        attn = _dot_tb(qe, ke_bd)
        upds = []
        for p in range(GLA_HEADS // 2):
            u = _dot_ta(kl[:, p * LANES:(p + 1) * LANES], vb[:, 2 * p * GLA_HEAD_V:(2 * p + 2) * GLA_HEAD_V])
            upds.append(u[0:GLA_HEAD_K, 0:GLA_HEAD_V])
            upds.append(u[GLA_HEAD_K:, GLA_HEAD_V:])
        yield
        attn = jnp.where(causal_cat, attn, 0.0).astype(BF16)
        o_pairs = []
        for p in range(GLA_HEADS // 2):
            v_a = vb[:, (2 * p) * GLA_HEAD_V:(2 * p + 1) * GLA_HEAD_V]
            v_b = vb[:, (2 * p + 1) * GLA_HEAD_V:(2 * p + 2) * GLA_HEAD_V]
            v_bd = jnp.concatenate([jnp.concatenate([v_a, zero_v], axis=1),
                                    jnp.concatenate([zero_v, v_b], axis=1)], axis=0)
            o_pairs.append(_dot(attn[:, 2 * p * c_len:(2 * p + 2) * c_len], v_bd))
        og_scr[rows, :] = _head_norm_gate(o_inter + jnp.concatenate(o_pairs, axis=1), r_ref[rows, :],
                                          gn_ref[...])
        last_col = jnp.sum(jnp.where(eye, last, 0.0), axis=1, keepdims=True)
        s_ref[...] = state * jnp.exp(last_col) + jnp.concatenate(upds, axis=0)
        worst.append(jnp.min(last, axis=1, keepdims=True))
        if c == n_chunks - 1:
            tile_min = functools.reduce(jnp.minimum, worst)
            flag_ref[0] = jnp.where(tile_min[0, 0] < -DECAY_CLAMP, 1, 0)
        yield

    return itertools.chain.from_iterable(chunk(c) for c in range(n_chunks))


def _gla_tile_exact(qk_ref, v_ref, r_ref, gn_ref, og_scr, o_scr, cum_scr, inter_scr):
    c_len = GLA_CHUNK
    ind = jnp.where(lax.broadcasted_iota(jnp.int32, (GLA_KEY_WIDTH, LANES), 0) // GLA_HEAD_K
                    == lax.broadcasted_iota(jnp.int32, (GLA_KEY_WIDTH, LANES), 1), 1.0, 0.0).astype(BF16)
    j_idx = lax.broadcasted_iota(jnp.int32, (c_len, 1), 0)
    for c in range(ROW_TILE // c_len):
        r0 = c * c_len
        rows = slice(r0, r0 + c_len)

        def one_row(i, carry, r0=r0, rows=rows):
            ci = cum_scr[pl.ds(r0 + i, 1), :]
            qi = qk_ref[pl.ds(r0 + i, 1), 0:GLA_KEY_WIDTH]
            kk = qk_ref[rows, GLA_KEY_WIDTH:]
            dec = jnp.exp(jnp.minimum(ci - cum_scr[rows, :], 0.0))
            a_cols = _dot(((qi * kk) * dec).astype(BF16), ind)
            outs = []
            for h in range(GLA_HEADS):
                w_col = jnp.where(j_idx <= i, a_cols[:, h:h + 1], 0.0)
                v_h = v_ref[rows, h * GLA_HEAD_V:(h + 1) * GLA_HEAD_V].astype(F32)
                outs.append(jnp.sum(w_col * v_h, axis=0, keepdims=True))
            o_scr[pl.ds(r0 + i, 1), :] = inter_scr[pl.ds(r0 + i, 1), :] + jnp.concatenate(outs, axis=1)
            return carry

        lax.fori_loop(0, c_len, one_row, 0)
    og_scr[...] = _head_norm_gate(o_scr[...], r_ref[...], gn_ref[...])


def _dup_halves(x):
    lo = lax.broadcasted_iota(jnp.int32, (1, LANES), 1) < SWA_HEAD_DIM
    sw = pltpu.roll(x, SWA_HEAD_DIM, 1)
    return jnp.where(lo, x, sw).astype(BF16), jnp.where(lo, sw, x).astype(BF16)


def _swa_stages(sink_ref, qs_ref, kv_ref, kvprev_scr, sn_ref, os_scr, no_prev_bias):
    w = WINDOW
    n_blocks = ROW_TILE // w
    lo = lax.broadcasted_iota(jnp.int32, (1, LANES), 1) < SWA_HEAD_DIM
    tri = (lax.broadcasted_iota(jnp.int32, (w, w), 1) <= lax.broadcasted_iota(jnp.int32, (w, w), 0))
    zero_b = jnp.zeros((), BF16)

    def block(b):
        rows = slice(b * w, (b + 1) * w)
        kv_prev = kvprev_scr[...] if b == 0 else kv_ref[(b - 1) * w:b * w, :]
        kv = jnp.concatenate([kv_prev, kv_ref[rows, :]], axis=0)
        kk = _dup_halves(kv[:, 0:SWA_KV_WIDTH])
        vv = _dup_halves(kv[:, SWA_KV_WIDTH:])
        q = qs_ref[rows, :]
        scores = []
        for grp in range(SWA_KV_HEADS):
            stacked = []
            for j in range(SWA_GROUP):
                hq = grp * SWA_GROUP + j
                tile = q[:, (hq // 2) * LANES:(hq // 2 + 1) * LANES]
                stacked.append(jnp.where(lo if hq % 2 == 0 else ~lo, tile, zero_b))
            scores.append(_dot_tb(jnp.concatenate(stacked, axis=0), kk[grp]))
        yield
        outs, dens = [], []
        for grp in range(SWA_KV_HEADS):
            s = scores[grp]
            probs = []
            for j in range(SWA_GROUP):
                sink = sink_ref[grp * SWA_GROUP + j]
                s_prev = s[j * w:(j + 1) * w, 0:w]
                if b == 0:
                    s_prev = s_prev + no_prev_bias
                sf = jnp.where(tri, s[j * w:(j + 1) * w, w:], s_prev)
                m = jnp.maximum(jnp.max(sf, axis=1, keepdims=True), sink)
                p = jnp.exp(sf - m)
                dens.append(jnp.sum(p, axis=1, keepdims=True) + jnp.exp(sink - m))
                pb = p.astype(BF16)
                probs.append(jnp.concatenate([jnp.where(tri, zero_b, pb), jnp.where(tri, pb, zero_b)], axis=1))
            outs.append(_dot(jnp.concatenate(probs, axis=0), vv[grp]))
        yield
        tiles = []
        for grp in range(SWA_KV_HEADS):
            res = [outs[grp][j * w:(j + 1) * w] / dens[grp * SWA_GROUP + j] for j in range(SWA_GROUP)]
            tiles.append(jnp.where(lo, res[0], res[1]))
            tiles.append(jnp.where(lo, res[2], res[3]))
        os_scr[rows, :] = _rms(jnp.concatenate(tiles, axis=1), sn_ref[...]).astype(BF16)
        if b == n_blocks - 1:
            kvprev_scr[...] = kv_ref[rows, :]
        yield

    return itertools.chain.from_iterable(block(b) for b in range(n_blocks))


def _mix_out_body(sink_ref, qk_ref, g_ref, v_ref, r_ref, qs_ref, kv_ref, x1_ref, ogs_ref, oss_ref,
                  gn_ref, sn_ref, wo_ref, n2_ref, wg_ref, wu_ref, wd_ref, nf_ref, yp_ref, ys_ref, s_ref,
                  og_scr, os_scr, kvprev_scr, o_scr, cum_scr, inter_scr, act_scr, flag_ref):
    i = pl.program_id(0)
    n_prompt = pl.num_programs(0) - 1

    def stage_d(og, osw, y_ref, side_stages=None):
        _out_ffn(x1_ref, og, osw, wo_ref, n2_ref, wg_ref, wu_ref, wd_ref, nf_ref, y_ref, act_scr,
                 side_stages)

    def mixer_stages():
        gla = _gla_stages(qk_ref, g_ref, v_ref, r_ref, gn_ref, s_ref, og_scr, cum_scr, inter_scr, flag_ref)
        swa = _swa_stages(sink_ref, qs_ref, kv_ref, kvprev_scr, sn_ref, os_scr,
                          jnp.where(i > 0, 0.0, -jnp.inf))
        return _alternate(gla, swa)

    @pl.when(i == 0)
    def _():
        s_ref[...] = jnp.zeros_like(s_ref)
        kvprev_scr[...] = jnp.zeros_like(kvprev_scr)
        stage_d(ogs_ref[...], oss_ref[...], ys_ref, mixer_stages())

    @pl.when(jnp.logical_and(i > 0, i < n_prompt))
    def _():
        stage_d(og_scr[...], os_scr[...], yp_ref, mixer_stages())

    @pl.when(i == n_prompt)
    def _():
        stage_d(og_scr[...], os_scr[...], yp_ref)
        flag_ref[0] = 0

    @pl.when(flag_ref[0] != 0)
    def _():
        _gla_tile_exact(qk_ref, v_ref, r_ref, gn_ref, og_scr, o_scr, cum_scr, inter_scr)


def _mix_out(qk, g, v, r, qs, kv, x1, og_s, os_s, w):
    t = qk.shape[0] - ROW_TILE
    assert t % ROW_TILE == 0 and og_s.shape[0] == ROW_TILE
    nt = t // ROW_TILE
    cur = lambda n: pl.BlockSpec((ROW_TILE, n), lambda i: (jnp.minimum(i, nt - 1), 0))
    return pl.pallas_call(
        _mix_out_body,
        grid=(nt + 1,),
        in_specs=[pl.BlockSpec(memory_space=pltpu.SMEM),
                  cur(2 * GLA_KEY_WIDTH), cur(GLA_KEY_WIDTH), cur(GLA_WIDTH), cur(GLA_WIDTH),
                  cur(SWA_WIDTH), cur(2 * SWA_KV_WIDTH),
                  pl.BlockSpec((ROW_TILE, D_MODEL), lambda i: (jnp.where(i == 0, nt, i - 1), 0)),
                  _resident((ROW_TILE, GLA_WIDTH)), _resident((ROW_TILE, SWA_WIDTH)),
                  _resident((1, GLA_WIDTH)), _resident((1, SWA_WIDTH)),
                  _resident((D_MODEL, D_MODEL)), _resident((1, D_MODEL)),
                  _resident((D_MODEL, D_FF)), _resident((D_MODEL, D_FF)), _resident((D_FF, D_MODEL)),
                  _resident((1, D_MODEL))],
        out_specs=[pl.BlockSpec((ROW_TILE, D_MODEL), lambda i: (jnp.clip(i - 1, 0, nt - 1), 0)),
                   pl.BlockSpec(SAMPLE_TILE_SHAPE, lambda i: (0, 0, 0)),
                   pl.BlockSpec((GLA_KEY_WIDTH, GLA_HEAD_V), lambda i: (0, 0))],
        out_shape=[jax.ShapeDtypeStruct((t, D_MODEL), F32),
                   jax.ShapeDtypeStruct(SAMPLE_TILE_SHAPE, F32),
                   jax.ShapeDtypeStruct((GLA_KEY_WIDTH, GLA_HEAD_V), F32)],
        scratch_shapes=[pltpu.VMEM((ROW_TILE, GLA_WIDTH), BF16), pltpu.VMEM((ROW_TILE, SWA_WIDTH), BF16),
                        pltpu.VMEM((WINDOW, 2 * SWA_KV_WIDTH), F32),
                        pltpu.VMEM((ROW_TILE, GLA_WIDTH), F32), pltpu.VMEM((ROW_TILE, GLA_KEY_WIDTH), F32),
                        pltpu.VMEM((ROW_TILE, GLA_WIDTH), F32), pltpu.VMEM((ROW_TILE, D_FF), BF16),
                        pltpu.SMEM((1,), jnp.int32)],
        compiler_params=pltpu.CompilerParams(dimension_semantics=("arbitrary",),
                                             vmem_limit_bytes=STAGE_D_VMEM_LIMIT),
        name="prompt_mixers_stage_d",
    )(w["sinks"], qk, g, v, r, qs, kv, x1, og_s, os_s, w["gn"], w["sn"], w["wo"], w["n2"], w["wg2"],
      w["wu2"], w["wd2"], w["nf"])


def _gla_sample_body(qk_ref, g_ref, v_ref, r_ref, gn_ref, s_ref, og_ref, so_ref, o_scr):
    rows = SAMPLE_ROWS
    g = g_ref[...]
    q = qk_ref[:, 0:GLA_KEY_WIDTH]
    k = qk_ref[:, GLA_KEY_WIDTH:]
    vb = v_ref[...]
    vf = vb.astype(F32)
    tok = lax.broadcasted_iota(jnp.int32, (rows, 1), 0) % DEC_SEQ

    cum = g
    for d in range(1, DEC_SEQ):
        cum = cum + jnp.where(tok >= d, pltpu.roll(g, d, 0), 0.0)
    tot = jnp.where(tok == DEC_SEQ - 1, cum, 0.0)
    for d in range(1, DEC_SEQ):
        tot = tot + jnp.where(tok == DEC_SEQ - 1 - d, pltpu.roll(cum, rows - d, 0), 0.0)

    qe = q * jnp.exp(cum)
    kl = k * jnp.exp(tot - cum)
    decay_t = jnp.exp(tot).T

    ind = jnp.where(lax.broadcasted_iota(jnp.int32, (GLA_KEY_WIDTH, LANES), 0) // GLA_HEAD_K
                    == lax.broadcasted_iota(jnp.int32, (GLA_KEY_WIDTH, LANES), 1), 1.0, 0.0).astype(BF16)
    expand = jnp.where(lax.broadcasted_iota(jnp.int32, (LANES, GLA_WIDTH), 0)
                       == lax.broadcasted_iota(jnp.int32, (LANES, GLA_WIDTH), 1) // GLA_HEAD_V,
                       1.0, 0.0).astype(BF16)

    o_intra = jnp.zeros((rows, GLA_WIDTH), F32)
    for d in range(DEC_SEQ):
        k_d = k if d == 0 else pltpu.roll(k, d, 0)
        c_d = cum if d == 0 else pltpu.roll(cum, d, 0)
        v_d = vf if d == 0 else pltpu.roll(vf, d, 0)
        pair = jnp.where(tok >= d, q * k_d * jnp.exp(jnp.minimum(cum - c_d, 0.0)), 0.0)
        a = _dot(pair.astype(BF16), ind)
        o_intra = o_intra + _dot(a.astype(BF16), expand) * v_d

    lane_head = lax.broadcasted_iota(jnp.int32, (1, GLA_KEY_WIDTH), 1) // GLA_HEAD_K
    row8 = lax.broadcasted_iota(jnp.int32, (SUBLANES, 1), 0)
    row32 = lax.broadcasted_iota(jnp.int32, (GLA_HEADS * SUBLANES, 1), 0)
    for pair_idx in range(SAMPLE_SEQS // 2):
        r8 = slice(pair_idx * SUBLANES, (pair_idx + 1) * SUBLANES)
        q8 = qe[r8, :]
        lhs = jnp.concatenate([jnp.where(lane_head == h, q8, 0.0) for h in range(GLA_HEADS)],
                              axis=0).astype(BF16)
        kl8 = kl[r8, :]
        v8 = vb[r8, :]
        res = []
        for s in range(2):
            b = 2 * pair_idx + s
            state = s_ref[b]
            res.append(_dot(lhs, state.astype(BF16)))
            kl_b = jnp.where(row8 // DEC_SEQ == s, kl8, 0.0).astype(BF16)
            upd = _dot_ta(kl_b, v8)
            upd = jnp.concatenate(
                [upd[h * GLA_HEAD_K:(h + 1) * GLA_HEAD_K, h * GLA_HEAD_V:(h + 1) * GLA_HEAD_V]
                 for h in range(GLA_HEADS)], axis=0)
            so_ref[b] = state * decay_t[:, DEC_SEQ * b:DEC_SEQ * b + 1] + upd
        sel = jnp.where(row32 % SUBLANES < DEC_SEQ, res[0], res[1])
        o_scr[r8, :] = jnp.concatenate([sel[h * SUBLANES:(h + 1) * SUBLANES] for h in range(GLA_HEADS)], axis=1)

    og_ref[...] = _head_norm_gate(o_scr[...] + o_intra, r_ref[...], gn_ref[...])


def _gla_sample(qk, g, v, r, gn, state, row0):
    nseq = state.shape[0]
    rows = nseq * DEC_SEQ
    assert nseq % SAMPLE_SEQS == 0 and row0 % SAMPLE_ROWS == 0
    blk0 = row0 // SAMPLE_ROWS
    src = lambda n: pl.BlockSpec((SAMPLE_ROWS, n), lambda i: (i + blk0, 0))
    blk = lambda n: pl.BlockSpec((SAMPLE_ROWS, n), lambda i: (i, 0))
    sblk = pl.BlockSpec((SAMPLE_SEQS, GLA_KEY_WIDTH, GLA_HEAD_V), lambda i: (i, 0, 0))
    return pl.pallas_call(
        _gla_sample_body,
        grid=(nseq // SAMPLE_SEQS,),
        in_specs=[src(2 * GLA_KEY_WIDTH), src(GLA_KEY_WIDTH), src(GLA_WIDTH), src(GLA_WIDTH),
                  pl.BlockSpec((1, GLA_WIDTH), lambda i: (0, 0)), sblk],
        out_specs=[blk(GLA_WIDTH), sblk],
        out_shape=[jax.ShapeDtypeStruct((rows, GLA_WIDTH), BF16),
                   jax.ShapeDtypeStruct((nseq, GLA_KEY_WIDTH, GLA_HEAD_V), F32)],
        scratch_shapes=[pltpu.VMEM((SAMPLE_ROWS, GLA_WIDTH), F32)],
        compiler_params=pltpu.CompilerParams(dimension_semantics=("arbitrary",)),
        name="gla_sample",
    )(qk, g, v, r, gn, state)


def _swa_sample_body(sink_ref, q_ref, kvn_ref, kc_ref, vc_ref, nrm_ref, o_ref, kco_ref, vco_ref,
                     lhs_scr, sc_scr, pc_scr, oc_scr):
    rows = SAMPLE_ROWS
    q = q_ref[...].astype(F32)
    lo = lax.broadcasted_iota(jnp.int32, (1, LANES), 1) < SWA_HEAD_DIM
    for hq in range(SWA_Q_HEADS):
        grp = hq // SWA_GROUP
        tile = q[:, (hq // 2) * LANES:(hq // 2 + 1) * LANES]
        src = tile if hq % 2 == grp else pltpu.roll(tile, SWA_HEAD_DIM, 1)
        lhs_scr[hq * rows:(hq + 1) * rows, :] = jnp.where(lo if grp == 0 else ~lo, src, 0.0)

    kvn = kvn_ref[...]
    k_new = kvn[:, 0:SWA_KV_WIDTH]
    v_new = kvn[:, SWA_KV_WIDTH:]
    s_new = _dot_tb(lhs_scr[...].astype(BF16), k_new.astype(BF16))

    row64 = lax.broadcasted_iota(jnp.int32, (SWA_Q_HEADS * SUBLANES, 1), 0)
    first_of_pair = row64 % SUBLANES < DEC_SEQ

    def gather_pair(ref, pair_idx):
        return jnp.concatenate(
            [ref[hq * rows + pair_idx * SUBLANES:hq * rows + (pair_idx + 1) * SUBLANES, :]
             for hq in range(SWA_Q_HEADS)], axis=0).astype(BF16)

    def scatter_pair(ref, pair_idx, val):
        for hq in range(SWA_Q_HEADS):
            ref[hq * rows + pair_idx * SUBLANES:hq * rows + (pair_idx + 1) * SUBLANES, :] = (
                val[hq * SUBLANES:(hq + 1) * SUBLANES])

    for pair_idx in range(SAMPLE_SEQS // 2):
        l64 = gather_pair(lhs_scr, pair_idx)
        sa = _dot(l64, kc_ref[2 * pair_idx].astype(BF16))
        sb = _dot(l64, kc_ref[2 * pair_idx + 1].astype(BF16))
        scatter_pair(sc_scr, pair_idx, jnp.where(first_of_pair, sa, sb))

    rr = lax.broadcasted_iota(jnp.int32, (rows, rows), 0)
    cc = lax.broadcasted_iota(jnp.int32, (rows, rows), 1)
    tok = rr % DEC_SEQ
    mask_cache = cc > tok
    mask_new = (cc // DEC_SEQ == rr // DEC_SEQ) & (cc % DEC_SEQ <= tok)
    p_new, dens = [], []
    for hq in range(SWA_Q_HEADS):
        sl = slice(hq * rows, (hq + 1) * rows)
        sink = sink_ref[hq]
        s_c = jnp.where(mask_cache, sc_scr[sl, :], -jnp.inf)
        s_n = jnp.where(mask_new, s_new[sl, :], -jnp.inf)
        m = jnp.maximum(jnp.maximum(jnp.max(s_c, axis=1, keepdims=True),
                                    jnp.max(s_n, axis=1, keepdims=True)), sink)
        p_c = jnp.exp(s_c - m)
        p_n = jnp.exp(s_n - m)
        dens.append(jnp.sum(p_c, axis=1, keepdims=True) + jnp.sum(p_n, axis=1, keepdims=True)
                    + jnp.exp(sink - m))
        pc_scr[sl, :] = p_c
        p_new.append(p_n.astype(BF16))
    o_new = _dot(jnp.concatenate(p_new, axis=0), v_new.astype(BF16))

    for pair_idx in range(SAMPLE_SEQS // 2):
        p64 = gather_pair(pc_scr, pair_idx)
        oa = _dot_tb(p64, vc_ref[2 * pair_idx].astype(BF16))
        ob = _dot_tb(p64, vc_ref[2 * pair_idx + 1].astype(BF16))
        scatter_pair(oc_scr, pair_idx, jnp.where(first_of_pair, oa, ob))

    tiles = []
    for i in range(SWA_Q_HEADS // 2):
        halves = []
        for hq in (2 * i, 2 * i + 1):
            sl = slice(hq * rows, (hq + 1) * rows)
            oh = (oc_scr[sl, :] + o_new[sl, :]) / dens[hq]
            halves.append(oh if hq % 2 == hq // SWA_GROUP else pltpu.roll(oh, SWA_HEAD_DIM, 1))
        tiles.append(jnp.where(lo, halves[0], halves[1]))
    o_ref[...] = _rms(jnp.concatenate(tiles, axis=1), nrm_ref[...]).astype(BF16)

    keep = lax.broadcasted_iota(jnp.int32, (1, WINDOW), 1) < WINDOW - DEC_SEQ
    k_new_t = k_new.T
    v_new_t = v_new.T
    for b in range(SAMPLE_SEQS):
        to_tail = (WINDOW - DEC_SEQ - DEC_SEQ * b) % rows
        kco_ref[b] = jnp.where(keep, pltpu.roll(kc_ref[b], WINDOW - DEC_SEQ, 1),
                               pltpu.roll(k_new_t, to_tail, 1))
        vco_ref[b] = jnp.where(keep, pltpu.roll(vc_ref[b], WINDOW - DEC_SEQ, 1),
                               pltpu.roll(v_new_t, to_tail, 1))


def _swa_sample(sinks, qs, kvn, kc, vc, nrm, row0):
    nseq = kc.shape[0]
    rows = nseq * DEC_SEQ
    assert nseq % SAMPLE_SEQS == 0 and row0 % SAMPLE_ROWS == 0
    assert kc.shape[1:] == (SWA_KV_WIDTH, WINDOW) and WINDOW == SAMPLE_ROWS
    blk0 = row0 // SAMPLE_ROWS
    cblk = pl.BlockSpec((SAMPLE_SEQS, SWA_KV_WIDTH, WINDOW), lambda i: (i, 0, 0))
    big = pltpu.VMEM((SWA_Q_HEADS * SAMPLE_ROWS, LANES), F32)
    return pl.pallas_call(
        _swa_sample_body,
        grid=(nseq // SAMPLE_SEQS,),
        in_specs=[pl.BlockSpec(memory_space=pltpu.SMEM),
                  pl.BlockSpec((SAMPLE_ROWS, SWA_WIDTH), lambda i: (i + blk0, 0)),
                  pl.BlockSpec((SAMPLE_ROWS, 2 * SWA_KV_WIDTH), lambda i: (i + blk0, 0)),
                  cblk, cblk,
                  pl.BlockSpec((1, SWA_WIDTH), lambda i: (0, 0))],
        out_specs=[pl.BlockSpec((SAMPLE_ROWS, SWA_WIDTH), lambda i: (i, 0)), cblk, cblk],
        out_shape=[jax.ShapeDtypeStruct((rows, SWA_WIDTH), BF16),
                   jax.ShapeDtypeStruct(kc.shape, F32), jax.ShapeDtypeStruct(vc.shape, F32)],
        scratch_shapes=[big, big, big, big],
        compiler_params=pltpu.CompilerParams(dimension_semantics=("arbitrary",)),
        name="swa_sample",
    )(sinks, qs, kvn, kc, vc, nrm)


def _prepare_weights(ffn1_norm, ffn1_w_gate, ffn1_w_up, ffn1_w_down, mix_norm, w_in, w_gate_up, b_gate,
                     gla_head_norm, swa_out_norm, swa_sinks, w_out,
                     ffn2_norm, ffn2_w_gate, ffn2_w_up, ffn2_w_down, final_norm, layer):
    assert w_in.shape[1:] == (D_MODEL, D_IN) and PROJ_Q_S == 2 * GLA_KEY_WIDTH + 2 * GLA_WIDTH
    wgu = jnp.concatenate([w_gate_up[layer],
                           jnp.zeros((LANES - GLA_GATE_RANK, GLA_KEY_WIDTH), w_gate_up.dtype)], axis=0)
    row = lambda a: a.reshape(1, -1).astype(F32)
    return dict(
        n1=row(ffn1_norm[layer]), wg1=ffn1_w_gate[layer], wu1=ffn1_w_up[layer], wd1=ffn1_w_down[layer],
        nm=row(mix_norm[layer]), win=w_in[layer].T, wgu=wgu.astype(BF16), bg=row(b_gate[layer]),
        gn=row(gla_head_norm[layer]), sn=row(swa_out_norm[layer]), sinks=swa_sinks[layer].astype(F32),
        wo=w_out[layer], n2=row(ffn2_norm[layer]), wg2=ffn2_w_gate[layer], wu2=ffn2_w_up[layer],
        wd2=ffn2_w_down[layer], nf=row(final_norm))


def kernel(x_prompt, x_sample, state_gla, cache_swa_k, cache_swa_v, ffn1_norm, ffn1_w_gate, ffn1_w_up,
           ffn1_w_down, mix_norm, w_in, w_gate_up, b_gate, gla_head_norm, swa_out_norm, swa_sinks, w_out,
           ffn2_norm, ffn2_w_gate, ffn2_w_up, ffn2_w_down, final_norm):
    depth = state_gla.shape[0]
    assert depth == 1 and x_prompt.shape[0] == 1 and x_sample.shape[1] == DEC_SEQ
    seq = x_prompt.shape[1]
    nseq = x_sample.shape[0]
    w = _prepare_weights(ffn1_norm, ffn1_w_gate, ffn1_w_up, ffn1_w_down, mix_norm, w_in, w_gate_up, b_gate,
                         gla_head_norm, swa_out_norm, swa_sinks, w_out,
                         ffn2_norm, ffn2_w_gate, ffn2_w_up, ffn2_w_down, final_norm, 0)

    (x1, qk, g, v, r, qs, kv), (wo, wg2, wu2, wd2) = _stage_a(
        x_prompt.reshape(seq, D_MODEL), x_sample, w)
    w = dict(w, wo=wo, wg2=wg2, wu2=wu2, wd2=wd2)

    og_s, state_s = _gla_sample(qk, g, v, r, w["gn"],
                                state_gla[0].reshape(nseq, GLA_KEY_WIDTH, GLA_HEAD_V), seq)
    cache_w = cache_swa_k.shape[2]
    to_feature_major = lambda c: jnp.transpose(c[0], (0, 2, 3, 1)).reshape(nseq, SWA_KV_WIDTH, cache_w)
    from_feature_major = lambda c: jnp.transpose(
        c.reshape(nseq, SWA_KV_HEADS, SWA_HEAD_DIM, cache_w), (0, 3, 1, 2))[None]
    os_s, k_cache_s, v_cache_s = _swa_sample(
        w["sinks"], qs, kv, to_feature_major(cache_swa_k), to_feature_major(cache_swa_v), w["sn"], seq)
    k_cache_s = from_feature_major(k_cache_s)
    v_cache_s = from_feature_major(v_cache_s)

    y_prompt, y_sample, state_p = _mix_out(qk, g, v, r, qs, kv, x1, og_s, os_s, w)
    y_prompt = y_prompt.reshape(1, seq, D_MODEL)
    cw = min(WINDOW, seq)
    k_cache_p = kv[seq - cw:seq, 0:SWA_KV_WIDTH].reshape(1, 1, cw, SWA_KV_HEADS, SWA_HEAD_DIM)
    v_cache_p = kv[seq - cw:seq, SWA_KV_WIDTH:].reshape(1, 1, cw, SWA_KV_HEADS, SWA_HEAD_DIM)
    state_p = state_p.reshape(1, 1, GLA_HEADS, GLA_HEAD_K, GLA_HEAD_V)
    state_s = state_s.reshape(1, nseq, GLA_HEADS, GLA_HEAD_K, GLA_HEAD_V)

    return (y_prompt, y_sample, state_p, k_cache_p, v_cache_p, state_s, k_cache_s, v_cache_s)
```
